```python
import jax, jax.numpy as jnp
from jax import lax
import numpy as np

D_MODEL = 1024
BATCH = 2
SEQ = 8192
DEPTH = 2

HGRN_HEADS = 4
HGRN_KEY = 128
HGRN_VAL = 128
HGRN_KW = HGRN_HEADS * HGRN_KEY
HGRN_VW = HGRN_HEADS * HGRN_VAL
HGRN_CHUNK = 64
POOL_WINDOWS = (2, 4, 8, 16)
POOL_GROUP = 128
POOL_WIDTH = POOL_GROUP * len(POOL_WINDOWS)
FOX_HEADS = 16
FOX_HEAD_DIM = 64
FOX_WIDTH = FOX_HEADS * FOX_HEAD_DIM
FOX_BLOCK = 128
EVEN_IN = 2 * HGRN_KW + 2 * HGRN_VW + 2 * POOL_WIDTH
EVEN_MIX = HGRN_VW + POOL_WIDTH
ODD_IN = 4 * FOX_WIDTH + FOX_HEADS
N_EVEN = (DEPTH + 1) // 2
N_ODD = DEPTH // 2
EPS = 1e-6

kernel_name = 'hybrid_hgrn2_pool_fox_adaln'


def rms_norm(x, g):
    xf = x.astype(jnp.float32)
    y = xf * lax.rsqrt(jnp.mean(xf * xf, axis=-1, keepdims=True) + EPS)
    return (y * g.astype(jnp.float32)).astype(x.dtype)


def hgrn2_chunked(q, k, v, logf):
    b_, s_, h_, kd = q.shape
    vd = v.shape[-1]
    nc = s_ // HGRN_CHUNK

    def to_chunks(t):
        return t.reshape(b_, nc, HGRN_CHUNK, h_, t.shape[-1]).transpose(1, 0, 3, 2, 4)

    causal = jnp.tril(jnp.ones((HGRN_CHUNK, HGRN_CHUNK), dtype=bool))

    def step(state, inp):
        qc, kc, vc, gc = inp
        cum = jnp.cumsum(gc, axis=2)
        diff = cum[:, :, :, None, :] - cum[:, :, None, :, :]
        decay = jnp.where(causal[:, :, None], jnp.exp(jnp.minimum(diff, 0.0)), 0.0)
        scores = jnp.einsum('bhtk,bhsk,bhtsk->bhts', qc, kc, decay)
        out = (jnp.einsum('bhts,bhsv->bhtv', scores, vc)
               + jnp.einsum('bhtk,bhkv->bhtv', qc * jnp.exp(cum), state))
        last = cum[:, :, -1:, :]
        state = (jnp.exp(last[:, :, 0, :])[..., None] * state
                 + jnp.einsum('bhsk,bhsv->bhkv', kc * jnp.exp(last - cum), vc))
        return state, out

    s0 = jnp.zeros((b_, h_, kd, vd), jnp.float32)
    _, out = lax.scan(step, s0, (to_chunks(q), to_chunks(k), to_chunks(v), to_chunks(logf)))
    return out.transpose(1, 0, 3, 2, 4).reshape(b_, s_, h_, vd)


def multiscale_pool(u):
    b_, s_, _ = u.shape
    groups = u.reshape(b_, s_, len(POOL_WINDOWS), POOL_GROUP)
    csum = jnp.cumsum(groups, axis=1)
    csum = jnp.concatenate([jnp.zeros_like(csum[:, :1]), csum], axis=1)
    pos = jnp.arange(s_)
    outs = []
    for gi, w in enumerate(POOL_WINDOWS):
        cg = csum[:, :, gi]
        lagged = jnp.concatenate([jnp.zeros((b_, w - 1, POOL_GROUP), cg.dtype), cg[:, :s_ - w + 1]], axis=1)
        count = jnp.minimum(pos + 1, w).astype(cg.dtype)[None, :, None]
        outs.append((cg[:, 1:] - lagged) / count - groups[:, :, gi])
    return jnp.stack(outs, axis=2)


def even_mixer(h, w_in, lower, onorm_g, pool_w, pool_scale, w_out):
    b_, s_, _ = h.shape
    proj = (h @ w_in).astype(jnp.float32)
    cuts = [HGRN_KW, 2 * HGRN_KW, 2 * HGRN_KW + HGRN_VW, 2 * HGRN_KW + 2 * HGRN_VW,
            2 * HGRN_KW + 2 * HGRN_VW + POOL_WIDTH]
    q, f, i, g_a, u, g_b = jnp.split(proj, cuts, axis=-1)
    lb = lower.astype(jnp.float32)
    forget = lb + (1.0 - lb) * jax.nn.sigmoid(f)
    logf = jnp.log(forget)
    key = 1.0 - forget
    kh = lambda t: t.reshape(b_, s_, HGRN_HEADS, HGRN_KEY)
    o_a = hgrn2_chunked(kh(q), kh(key), i.reshape(b_, s_, HGRN_HEADS, HGRN_VAL), kh(logf))
    o_a = rms_norm(o_a, onorm_g.reshape(HGRN_HEADS, HGRN_VAL)).reshape(b_, s_, HGRN_VW) * jax.nn.silu(g_a)
    pooled = multiscale_pool(u)
    o_b = jnp.einsum('bsgc,gcd->bsgd', pooled, pool_w.astype(jnp.float32)).reshape(b_, s_, POOL_WIDTH)
    o_b = o_b * pool_scale.astype(jnp.float32) * jax.nn.silu(g_b)
    mixed = jnp.concatenate([o_a, o_b], axis=-1).astype(h.dtype)
    return mixed @ w_out


def odd_mixer(h, w_in, b_f, qnorm_g, knorm_g, w_out):
    b_, s_, _ = h.shape
    proj = (h @ w_in).astype(jnp.float32)
    q, k, v, g, fl = jnp.split(proj, [FOX_WIDTH, 2 * FOX_WIDTH, 3 * FOX_WIDTH, 4 * FOX_WIDTH], axis=-1)
    hd = lambda t: t.reshape(b_, s_, FOX_HEADS, FOX_HEAD_DIM)
    q = rms_norm(hd(q), qnorm_g).transpose(0, 2, 1, 3)
    k = rms_norm(hd(k), knorm_g).transpose(0, 2, 1, 3)
    v = hd(v).transpose(0, 2, 1, 3)
    logf = jax.nn.log_sigmoid(fl + b_f.astype(jnp.float32))
    cumf = jnp.cumsum(logf, axis=1).transpose(0, 2, 1)
    scale = FOX_HEAD_DIM ** -0.5
    outs = []
    for blk in range(s_ // FOX_BLOCK):
        q0 = blk * FOX_BLOCK
        q1 = q0 + FOX_BLOCK
        logits = (jnp.einsum('bhqd,bhkd->bhqk', q[:, :, q0:q1], k[:, :, :q1]) * scale
                  + cumf[:, :, q0:q1, None] - cumf[:, :, None, :q1])
        mask = jnp.arange(q0, q1)[:, None] >= jnp.arange(q1)[None, :]
        probs = jax.nn.softmax(jnp.where(mask, logits, -jnp.inf), axis=-1)
        outs.append(jnp.einsum('bhqk,bhkd->bhqd', probs, v[:, :, :q1]))
    o = jnp.concatenate(outs, axis=2).transpose(0, 2, 1, 3).reshape(b_, s_, FOX_WIDTH)
    o = o * jax.nn.silu(g)
    return o.astype(h.dtype) @ w_out


def setup_inputs(seed: int = 0) -> dict:
    key = jax.random.key(seed)
    ks = jax.random.split(key, 16)
    nrm = jax.random.normal
    f32 = jnp.float32
    return {
        'x': nrm(ks[0], (BATCH, SEQ, D_MODEL), f32),
        'c': nrm(ks[1], (BATCH, D_MODEL), f32),
        'norm_g': 1.0 + 0.05 * nrm(ks[2], (DEPTH, D_MODEL), f32),
        'ada_w': 0.5 * D_MODEL ** -0.5 * nrm(ks[3], (DEPTH, D_MODEL, 3 * D_MODEL), f32),
        'ada_b': 0.02 * nrm(ks[4], (DEPTH, 3 * D_MODEL), f32),
        'hgrn_lb': 0.1 * nrm(ks[5], (DEPTH + 1, HGRN_KW), f32),
        'even_w_in': D_MODEL ** -0.5 * nrm(ks[6], (N_EVEN, D_MODEL, EVEN_IN), f32),
        'hgrn_onorm_g': 1.0 + 0.05 * nrm(ks[7], (N_EVEN, HGRN_VW), f32),
        'pool_w': POOL_GROUP ** -0.5 * nrm(ks[8], (N_EVEN, len(POOL_WINDOWS), POOL_GROUP, POOL_GROUP), f32),
        'pool_scale': 1.0 + 0.05 * nrm(ks[9], (N_EVEN, POOL_WIDTH), f32),
        'even_w_out': EVEN_MIX ** -0.5 * nrm(ks[10], (N_EVEN, EVEN_MIX, D_MODEL), f32),
        'odd_w_in': D_MODEL ** -0.5 * nrm(ks[11], (N_ODD, D_MODEL, ODD_IN), f32),
        'fox_b_f': jax.random.uniform(ks[12], (N_ODD, FOX_HEADS), f32, 1.0, 4.0),
        'fox_qnorm_g': 1.0 + 0.05 * nrm(ks[13], (N_ODD, FOX_HEAD_DIM), f32),
        'fox_knorm_g': 1.0 + 0.05 * nrm(ks[14], (N_ODD, FOX_HEAD_DIM), f32),
        'odd_w_out': FOX_WIDTH ** -0.5 * nrm(ks[15], (N_ODD, FOX_WIDTH, D_MODEL), f32),
    }


def reference(x, c, norm_g, ada_w, ada_b, hgrn_lb, even_w_in, hgrn_onorm_g, pool_w, pool_scale,
              even_w_out, odd_w_in, fox_b_f, fox_qnorm_g, fox_knorm_g, odd_w_out):
    lower = jnp.cumsum(jax.nn.softmax(hgrn_lb.astype(jnp.float32), axis=0), axis=0)
    cond = jax.nn.silu(c)
    for l in range(DEPTH):
        mod = cond @ ada_w[l] + ada_b[l]
        shift, scale, gate = jnp.split(mod, 3, axis=-1)
        h = rms_norm(x, norm_g[l]) * (1.0 + scale[:, None, :]) + shift[:, None, :]
        j = l // 2
        if l % 2 == 0:
            y = even_mixer(h, even_w_in[j], lower[l], hgrn_onorm_g[j], pool_w[j], pool_scale[j], even_w_out[j])
        else:
            y = odd_mixer(h, odd_w_in[j], fox_b_f[j], fox_qnorm_g[j], fox_knorm_g[j], odd_w_out[j])
        x = x + gate[:, None, :] * y
    return x
```

```python
import functools

import jax
import jax.numpy as jnp
from jax import lax
from jax.experimental import pallas as pl
from jax.experimental.pallas import tpu as pltpu

F32 = jnp.float32
BF16 = jnp.bfloat16
EPS = 1e-6

HGRN_HEADS = 4
HGRN_KEY = 128
HGRN_VAL = 128
HGRN_KW = HGRN_HEADS * HGRN_KEY
HGRN_VW = HGRN_HEADS * HGRN_VAL
POOL_WINDOWS = (2, 4, 8, 16)
POOL_GROUP = 128
POOL_WIDTH = POOL_GROUP * len(POOL_WINDOWS)
POOL_HISTORY = 16
FOX_HEADS = 16
FOX_HEAD_DIM = 64
FOX_WIDTH = FOX_HEADS * FOX_HEAD_DIM

LANES = 128
SUBLANES = 8
VMEM_LIMIT_BYTES = 56 * 1024 * 1024

SEQ_TILE = 256
HGRN_CHUNK = 128
ATTN_BLOCK = SEQ_TILE
MASK_VALUE = -1e30
AUG_LANE = FOX_HEAD_DIM


def _silu(x):
    return x * jax.nn.sigmoid(x)


def _params(*semantics):
    return pltpu.CompilerParams(dimension_semantics=semantics, vmem_limit_bytes=VMEM_LIMIT_BYTES)


def _mod_kernel(c_ref, w_ref, b_ref, o_ref):
    c = c_ref[...]
    cond = _silu(c)
    o_ref[0] = jnp.dot(cond, w_ref[0], preferred_element_type=F32,
                       precision=lax.Precision.HIGHEST) + b_ref[0]


def _adaln_mods(c, ada_w, ada_b):
    depth, d, n = ada_w.shape
    b = c.shape[0]
    rows = SUBLANES
    c_rows = jnp.zeros((rows, d), F32).at[:b].set(c)
    tn = 1024
    out = pl.pallas_call(
        _mod_kernel,
        grid=(depth, n // tn),
        in_specs=[pl.BlockSpec((rows, d), lambda l, j: (0, 0)),
                  pl.BlockSpec((1, d, tn), lambda l, j: (l, 0, j)),
                  pl.BlockSpec((1, 1, tn), lambda l, j: (l, 0, j))],
        out_specs=pl.BlockSpec((1, rows, tn), lambda l, j: (l, 0, j)),
        out_shape=jax.ShapeDtypeStruct((depth, rows, n), F32),
        compiler_params=_params("arbitrary", "arbitrary"),
        name="adaln_mods",
    )(c_rows, ada_w, ada_b.reshape(depth, 1, n))
    return out[:, :b].reshape(depth, b, 3, d)


def _modulated_norm(x, mod_ref, ng_ref):
    shift = mod_ref[0, 0:1, :]
    scale = mod_ref[0, 1:2, :]
    ms = jnp.mean(x * x, axis=-1, keepdims=True)
    return x * lax.rsqrt(ms + EPS) * ng_ref[...] * (1.0 + scale) + shift


def _layer0_kernel(x_ref, mod_ref, ng_ref, win_ref, lb_ref, og_ref, pw_ref, ps_ref, wout_ref,
                   o_ref, state_ref, carry_ref, *, tile, chunk, layer_slot):
    T, C, H = tile, chunk, HGRN_HEADS
    t_idx = pl.program_id(1)

    @pl.when(t_idx == 0)
    def _():
        state_ref[...] = jnp.zeros_like(state_ref)
        carry_ref[...] = jnp.zeros_like(carry_ref)

    x = x_ref[0]
    h = _modulated_norm(x, mod_ref, ng_ref)
    proj = jnp.dot(h.astype(BF16), win_ref[...], preferred_element_type=F32)
    o0 = 0
    q = proj[:, o0:o0 + HGRN_KW]; o0 += HGRN_KW
    f = proj[:, o0:o0 + HGRN_KW]; o0 += HGRN_KW
    val = proj[:, o0:o0 + HGRN_VW]; o0 += HGRN_VW
    g_a = proj[:, o0:o0 + HGRN_VW]; o0 += HGRN_VW
    u = proj[:, o0:o0 + POOL_WIDTH]; o0 += POOL_WIDTH
    g_b = proj[:, o0:o0 + POOL_WIDTH]

    lbv = lb_ref[...]
    e = jnp.exp(lbv - jnp.max(lbv, axis=0, keepdims=True))
    lower = (jnp.sum(e[0:layer_slot + 1], axis=0, keepdims=True)
             / jnp.sum(e, axis=0, keepdims=True))
    forget = lower + (1.0 - lower) * jax.nn.sigmoid(f)
    logf = jnp.log(forget)
    key = 1.0 - forget

    row = lax.broadcasted_iota(jnp.int32, (T, HGRN_KW), 0)
    ti = lax.broadcasted_iota(jnp.int32, (C, C), 0)
    si = lax.broadcasted_iota(jnp.int32, (C, C), 1)
    n_chunks = T // C
    nt_dims = (((1,), (1,)), ((), ()))

    def level_scores(qe, ke, mask, scores):
        qb = qe.astype(BF16)
        kb = ke.astype(BF16)
        out = []
        for c in range(n_chunks):
            for hh in range(H):
                rs = slice(c * C, (c + 1) * C)
                cs = slice(hh * HGRN_KEY, (hh + 1) * HGRN_KEY)
                d = lax.dot_general(qb[rs, cs], kb[rs, cs], nt_dims, preferred_element_type=F32)
                prev = scores[c * H + hh]
                out.append(jnp.where(mask, d, 0.0 if prev is None else prev))
        return out

    scores = level_scores(q, key, ti == si, [None] * (n_chunks * H))
    p_sum = logf
    q_sum = jnp.zeros_like(logf)
    total = logf
    m = 1
    while m < C:
        mask = ((ti ^ si) < 2 * m) & ((ti & m) != 0) & ((si & m) == 0)
        scores = level_scores(q * jnp.exp(p_sum), key * jnp.exp(q_sum), mask, scores)
        upper = (row & m) != 0
        t_dn = pltpu.roll(total, m, 0)
        t_up = pltpu.roll(total, T - m, 0)
        p_sum = p_sum + jnp.where(upper, t_dn, 0.0)
        q_sum = q_sum + jnp.where(upper, 0.0, t_up)
        total = total + jnp.where(upper, t_dn, t_up)
        m *= 2

    q_in = (q * jnp.exp(p_sum)).astype(BF16)
    k_out = (key * jnp.exp(q_sum)).astype(BF16)
    decay = jnp.exp(total)
    val_b = val.astype(BF16)
    tn_dims = (((0,), (0,)), ((), ()))
    oa_rows = []
    for c in range(n_chunks):
        rs = slice(c * C, (c + 1) * C)
        heads_out = []
        for hh in range(H):
            cs = slice(hh * HGRN_KEY, (hh + 1) * HGRN_KEY)
            st = state_ref[hh]
            vb = val_b[rs, cs]
            o = jnp.dot(scores[c * H + hh].astype(BF16), vb, preferred_element_type=F32)
            o = o + lax.dot_general(q_in[rs, cs], st.astype(BF16), nt_dims,
                                    preferred_element_type=F32)
            state_ref[hh] = (st * decay[c * C:c * C + 1, cs]
                             + lax.dot_general(vb, k_out[rs, cs], tn_dims,
                                               preferred_element_type=F32))
            ms_o = jnp.mean(o * o, axis=-1, keepdims=True)
            heads_out.append(o * lax.rsqrt(ms_o + EPS))
        oa_rows.append(jnp.concatenate(heads_out, axis=1))
    o_a = jnp.concatenate(oa_rows, axis=0) * og_ref[...] * _silu(g_a)

    ub = jnp.concatenate([carry_ref[...], u], axis=0)
    carry_ref[...] = u[T - POOL_HISTORY:T, :]
    wins = []
    acc = ub
    sh = 1
    while sh < max(POOL_WINDOWS):
        acc = acc + pltpu.roll(acc, sh, 0)
        sh *= 2
        wins.append(acc)
    pos = lax.broadcasted_iota(jnp.int32, (T, POOL_GROUP), 0) + t_idx * T + 1
    ob = []
    for gi, w in enumerate(POOL_WINDOWS):
        cs = slice(gi * POOL_GROUP, (gi + 1) * POOL_GROUP)
        win = wins[w.bit_length() - 2][POOL_HISTORY:, cs]
        cnt = jnp.minimum(pos, w).astype(F32)
        pooled = win / cnt - u[:, cs]
        ob.append(jnp.dot(pooled.astype(BF16), pw_ref[gi], preferred_element_type=F32))
    o_b = jnp.concatenate(ob, axis=1) * ps_ref[...] * _silu(g_b)

    mixed = jnp.concatenate([o_a, o_b], axis=1).astype(BF16)
    y = jnp.dot(mixed, wout_ref[...], preferred_element_type=F32)
    o_ref[0] = x + mod_ref[0, 2:3, :] * y


def _layer0(x, mod, norm_g, w_in, hgrn_lb, onorm_g, pool_w, pool_scale, w_out, layer_slot):
    b, s, d = x.shape
    tile = SEQ_TILE
    n_in = w_in.shape[1]
    mix = w_out.shape[0]
    const2 = lambda bi, ti: (0, 0)
    return pl.pallas_call(
        functools.partial(_layer0_kernel, tile=tile, chunk=HGRN_CHUNK, layer_slot=layer_slot),
        grid=(b, s // tile),
        in_specs=[pl.BlockSpec((1, tile, d), lambda bi, ti: (bi, ti, 0)),
                  pl.BlockSpec((1, 3, d), lambda bi, ti: (bi, 0, 0)),
                  pl.BlockSpec((1, d), const2),
                  pl.BlockSpec((d, n_in), const2),
                  pl.BlockSpec(hgrn_lb.shape, const2),
                  pl.BlockSpec((1, HGRN_VW), const2),
                  pl.BlockSpec(pool_w.shape, lambda bi, ti: (0, 0, 0)),
                  pl.BlockSpec((1, POOL_WIDTH), const2),
                  pl.BlockSpec((mix, d), const2)],
        out_specs=pl.BlockSpec((1, tile, d), lambda bi, ti: (bi, ti, 0)),
        out_shape=jax.ShapeDtypeStruct((b, s, d), F32),
        scratch_shapes=[pltpu.VMEM((HGRN_HEADS, HGRN_VAL, HGRN_KEY), F32),
                        pltpu.VMEM((POOL_HISTORY, POOL_WIDTH), F32)],
        compiler_params=_params("arbitrary", "arbitrary"),
        name="layer0_hgrn_pool",
    )(x, mod, norm_g.reshape(1, d), w_in.astype(BF16), hgrn_lb, onorm_g.reshape(1, HGRN_VW),
      pool_w.astype(BF16), pool_scale.reshape(1, POOL_WIDTH), w_out.astype(BF16))


def _split3(c):
    hi = c.astype(BF16).astype(F32)
    r = c - hi
    mid = r.astype(BF16).astype(F32)
    lo = (r - mid).astype(BF16).astype(F32)
    return hi, mid, lo


def _layer1_proj_kernel(x_ref, mod_ref, ng_ref, win_ref, bf_ref, qg_ref, kg_ref, bd_ref,
                        qa_ref, ka_ref, vt_ref, g_ref, base_ref, carry_ref, *, tile):
    T, W = tile, FOX_WIDTH
    t_idx = pl.program_id(1)

    @pl.when(t_idx == 0)
    def _():
        carry_ref[...] = jnp.zeros_like(carry_ref)

    x = x_ref[0]
    h = _modulated_norm(x, mod_ref, ng_ref)
    proj = jnp.dot(h.astype(BF16), win_ref[...], preferred_element_type=F32)
    q = proj[:, 0:W]
    k = proj[:, W:2 * W]
    v = proj[:, 2 * W:3 * W]
    g_ref[0] = proj[:, 3 * W:4 * W]
    fl = proj[:, 4 * W:4 * W + LANES]
    vt_ref[0, 0] = v.T.astype(BF16)

    z = fl + bf_ref[...]
    logf = jnp.minimum(z, 0.0) - jnp.log1p(jnp.exp(-jnp.abs(z)))
    row = lax.broadcasted_iota(jnp.int32, (T, LANES), 0)
    cum = logf
    sh = 1
    while sh < T:
        cum = cum + jnp.where(row >= sh, pltpu.roll(cum, sh, 0), 0.0)
        sh *= 2
    base = carry_ref[...]
    base_ref[0, 0] = base
    carry_ref[...] = base + cum[T - 1:T, :]
    c_hi, c_mid, c_lo = _split3(cum)

    lane = lax.broadcasted_iota(jnp.int32, (T, LANES), 1)
    ones_q = (lane >= AUG_LANE + 3) & (lane < AUG_LANE + 6)
    ones_k = (lane >= AUG_LANE) & (lane < AUG_LANE + 3)
    bd = bd_ref[...]
    qk_scale = FOX_HEAD_DIM ** -0.5
    for pair in range(FOX_HEADS // 2):
        cs = slice(pair * LANES, (pair + 1) * LANES)
        qp = q[:, cs]
        kp = k[:, cs]
        q_ms = jnp.dot((qp * qp).astype(BF16), bd, preferred_element_type=F32)
        k_ms = jnp.dot((kp * kp).astype(BF16), bd, preferred_element_type=F32)
        qn = qp * lax.rsqrt(q_ms + EPS) * (qg_ref[...] * qk_scale)
        kn = kp * lax.rsqrt(k_ms + EPS) * kg_ref[...]
        for e in range(2):
            hd = 2 * pair + e
            hi = jnp.broadcast_to(c_hi[:, hd:hd + 1], (T, LANES))
            mid = jnp.broadcast_to(c_mid[:, hd:hd + 1], (T, LANES))
            lo = jnp.broadcast_to(c_lo[:, hd:hd + 1], (T, LANES))
            aug_q = jnp.where(lane == AUG_LANE, hi,
                              jnp.where(lane == AUG_LANE + 1, mid,
                                        jnp.where(lane == AUG_LANE + 2, lo,
                                                  jnp.where(ones_q, 1.0, 0.0))))
            aug_k = jnp.where(lane == AUG_LANE + 3, -hi,
                              jnp.where(lane == AUG_LANE + 4, -mid,
                                        jnp.where(lane == AUG_LANE + 5, -lo,
                                                  jnp.where(ones_k, 1.0, 0.0))))
            q_main = qn if e == 0 else pltpu.roll(qn, FOX_HEAD_DIM, 1)
            k_main = kn if e == 0 else pltpu.roll(kn, FOX_HEAD_DIM, 1)
            qa_ref[0, hd] = jnp.where(lane < FOX_HEAD_DIM, q_main, aug_q).astype(BF16)
            ka_ref[0, hd] = jnp.where(lane < FOX_HEAD_DIM, k_main, aug_k).astype(BF16)


def _layer1_proj(x, mod, norm_g, w_in, b_f, qnorm_g, knorm_g):
    b, s, d = x.shape
    tile = SEQ_TILE
    nt = s // tile
    w = FOX_WIDTH
    n_pad = 4 * w + LANES
    w_pad = jnp.zeros((d, n_pad), BF16).at[:, :w_in.shape[1]].set(w_in.astype(BF16))
    bf_pad = jnp.zeros((1, LANES), F32).at[0, :FOX_HEADS].set(b_f)
    blk = jnp.arange(LANES) // FOX_HEAD_DIM
    bd = jnp.where(blk[:, None] == blk[None, :], 1.0 / FOX_HEAD_DIM, 0.0).astype(BF16)
    qg = jnp.tile(qnorm_g, LANES // FOX_HEAD_DIM).reshape(1, LANES)
    kg = jnp.tile(knorm_g, LANES // FOX_HEAD_DIM).reshape(1, LANES)
    const2 = lambda bi, ti: (0, 0)
    return pl.pallas_call(
        functools.partial(_layer1_proj_kernel, tile=tile),
        grid=(b, nt),
        in_specs=[pl.BlockSpec((1, tile, d), lambda bi, ti: (bi, ti, 0)),
                  pl.BlockSpec((1, 3, d), lambda bi, ti: (bi, 0, 0)),
                  pl.BlockSpec((1, d), const2),
                  pl.BlockSpec((d, n_pad), const2),
                  pl.BlockSpec((1, LANES), const2),
                  pl.BlockSpec((1, LANES), const2),
                  pl.BlockSpec((1, LANES), const2),
                  pl.BlockSpec((LANES, LANES), const2)],
        out_specs=[pl.BlockSpec((1, FOX_HEADS, tile, LANES), lambda bi, ti: (bi, 0, ti, 0)),
                   pl.BlockSpec((1, FOX_HEADS, tile, LANES), lambda bi, ti: (bi, 0, ti, 0)),
                   pl.BlockSpec((1, 1, w, tile), lambda bi, ti: (bi, ti, 0, 0)),
                   pl.BlockSpec((1, tile, w), lambda bi, ti: (bi, ti, 0)),
                   pl.BlockSpec((1, 1, 1, LANES), lambda bi, ti: (bi, ti, 0, 0))],
        out_shape=[jax.ShapeDtypeStruct((b, FOX_HEADS, s, LANES), BF16),
                   jax.ShapeDtypeStruct((b, FOX_HEADS, s, LANES), BF16),
                   jax.ShapeDtypeStruct((b, nt, w, tile), BF16),
                   jax.ShapeDtypeStruct((b, s, w), F32),
                   jax.ShapeDtypeStruct((b, nt, 1, LANES), F32)],
        scratch_shapes=[pltpu.VMEM((1, LANES), F32)],
        compiler_params=_params("arbitrary", "arbitrary"),
        name="layer1_proj",
    )(x, mod, norm_g.reshape(1, d), w_pad, bf_pad, qg, kg, bd)


def _fox_kernel(base_ref, qa_ref, ka_ref, vt_ref, g_ref, o_ref, *, block, n_blocks):
    TB = block
    b = pl.program_id(0)
    pair = pl.program_id(1)
    i = pl.program_id(2)
    q_t = [qa_ref[0, e].astype(F32).T.astype(BF16) for e in range(2)]
    k_row = lax.broadcasted_iota(jnp.int32, (TB, TB), 0)
    q_col = lax.broadcasted_iota(jnp.int32, (TB, TB), 1)
    causal = k_row <= q_col

    def step(j, carry, masked):
        out = []
        for e in range(2):
            m_run, l_run, acc = carry[e]
            hd = 2 * pair + e
            k_blk = ka_ref[0, e, pl.ds(pl.multiple_of(j * TB, TB), TB), :]
            s_t = jnp.dot(k_blk, q_t[e], preferred_element_type=F32)
            if masked:
                s_t = jnp.where(causal, s_t, MASK_VALUE)
            delta = (base_ref[(b * n_blocks + i) * FOX_HEADS + hd]
                     - base_ref[(b * n_blocks + j) * FOX_HEADS + hd])
            m_new = jnp.maximum(m_run, jnp.max(s_t, axis=0, keepdims=True) + delta)
            p_t = jnp.exp(s_t - (m_new - delta))
            alpha = jnp.exp(m_run - m_new)
            l_new = alpha * l_run + jnp.sum(p_t, axis=0, keepdims=True)
            v_t = vt_ref[0, j, pl.ds(e * FOX_HEAD_DIM, FOX_HEAD_DIM), :]
            acc_new = alpha * acc + jnp.dot(v_t, p_t.astype(BF16), preferred_element_type=F32)
            out.append((m_new, l_new, acc_new))
        return tuple(out)

    init_head = (jnp.full((1, TB), MASK_VALUE, F32), jnp.zeros((1, TB), F32),
                 jnp.zeros((FOX_HEAD_DIM, TB), F32))
    carry = lax.fori_loop(0, i, lambda j, c: step(j, c, False), (init_head, init_head))
    carry = step(i, carry, True)
    o_t = jnp.concatenate([carry[e][2] / carry[e][1] for e in range(2)], axis=0)
    g = g_ref[0]
    o_ref[0] = (o_t.T * _silu(g)).astype(BF16)


def _fox_attention(base, qa, ka, vt, g):
    b, heads, s, _ = qa.shape
    blk = ATTN_BLOCK
    nb = s // blk
    w = g.shape[-1]
    return pl.pallas_call(
        functools.partial(_fox_kernel, block=blk, n_blocks=nb),
        grid=(b, heads // 2, nb),
        in_specs=[pl.BlockSpec(memory_space=pltpu.SMEM),
                  pl.BlockSpec((1, 2, blk, LANES), lambda bi, p, i: (bi, p, i, 0)),
                  pl.BlockSpec((1, 2, s, LANES), lambda bi, p, i: (bi, p, 0, 0)),
                  pl.BlockSpec((1, nb, LANES, blk), lambda bi, p, i: (bi, 0, p, 0)),
                  pl.BlockSpec((1, blk, LANES), lambda bi, p, i: (bi, i, p))],
        out_specs=pl.BlockSpec((1, blk, LANES), lambda bi, p, i: (bi, i, p)),
        out_shape=jax.ShapeDtypeStruct((b, s, w), BF16),
        compiler_params=_params("arbitrary", "arbitrary", "arbitrary"),
        name="fox_attention",
    )(base, qa, ka, vt, g)


def _out_proj_kernel(a_ref, x_ref, mod_ref, w_ref, o_ref):
    y = jnp.dot(a_ref[0], w_ref[...], preferred_element_type=F32)
    o_ref[0] = x_ref[0] + mod_ref[0, 2:3, :] * y


def _out_proj(a, x, mod, w_out):
    b, s, d = x.shape
    tile = 512
    k = a.shape[-1]
    return pl.pallas_call(
        _out_proj_kernel,
        grid=(b, s // tile),
        in_specs=[pl.BlockSpec((1, tile, k), lambda bi, ti: (bi, ti, 0)),
                  pl.BlockSpec((1, tile, d), lambda bi, ti: (bi, ti, 0)),
                  pl.BlockSpec((1, 3, d), lambda bi, ti: (bi, 0, 0)),
                  pl.BlockSpec((k, d), lambda bi, ti: (0, 0))],
        out_specs=pl.BlockSpec((1, tile, d), lambda bi, ti: (bi, ti, 0)),
        out_shape=jax.ShapeDtypeStruct((b, s, d), F32),
        compiler_params=_params("arbitrary", "arbitrary"),
        name="layer1_out_proj",
    )(a, x, mod, w_out.astype(BF16))


def _layer1(x, mod, norm_g, w_in, b_f, qnorm_g, knorm_g, w_out):
    qa, ka, vt, g, base = _layer1_proj(x, mod, norm_g, w_in, b_f, qnorm_g, knorm_g)
    base_flat = base[:, :, 0, :FOX_HEADS].reshape(-1)
    gated = _fox_attention(base_flat, qa, ka, vt, g)
    return _out_proj(gated, x, mod, w_out)


def kernel(x, c, norm_g, ada_w, ada_b, hgrn_lb, even_w_in, hgrn_onorm_g, pool_w, pool_scale,
           even_w_out, odd_w_in, fox_b_f, fox_qnorm_g, fox_knorm_g, odd_w_out):
    depth = norm_g.shape[0]
    mods = _adaln_mods(c, ada_w, ada_b)
    for l in range(depth):
        j = l // 2
        if l % 2 == 0:
            x = _layer0(x, mods[l], norm_g[l], even_w_in[j], hgrn_lb, hgrn_onorm_g[j],
                        pool_w[j], pool_scale[j], even_w_out[j], layer_slot=l)
        else:
            x = _layer1(x, mods[l], norm_g[l], odd_w_in[j], fox_b_f[j], fox_qnorm_g[j],
                        fox_knorm_g[j], odd_w_out[j])
    return x
```

```python
import functools

import jax
import jax.numpy as jnp
from jax import lax
from jax.experimental import pallas as pl
from jax.experimental.pallas import tpu as pltpu

F32 = jnp.float32
BF16 = jnp.bfloat16
EPS = 1e-6

HGRN_HEADS = 4
HGRN_KEY = 128
HGRN_VAL = 128
HGRN_KW = HGRN_HEADS * HGRN_KEY
HGRN_VW = HGRN_HEADS * HGRN_VAL
POOL_WINDOWS = (2, 4, 8, 16)
POOL_GROUP = 128
POOL_WIDTH = POOL_GROUP * len(POOL_WINDOWS)
POOL_HISTORY = 16
FOX_HEADS = 16
FOX_HEAD_DIM = 64
FOX_WIDTH = FOX_HEADS * FOX_HEAD_DIM

LANES = 128
SUBLANES = 8
VMEM_LIMIT_BYTES = 56 * 1024 * 1024

SEQ_TILE = 256
HGRN_CHUNK = 128
ATTN_BLOCK = SEQ_TILE
ATTN_Q_TILE = 2 * ATTN_BLOCK
LOG2E = 1.4426950408889634
MASK_VALUE = -1e30
AUG_LANE = FOX_HEAD_DIM


def _silu(x):
    return x * jax.nn.sigmoid(x)


def _params(*semantics):
    return pltpu.CompilerParams(dimension_semantics=semantics, vmem_limit_bytes=VMEM_LIMIT_BYTES)


def _mod_kernel(c_ref, w_ref, b_ref, o_ref):
    c = c_ref[...]
    cond = _silu(c)
    o_ref[0] = jnp.dot(cond, w_ref[0], preferred_element_type=F32,
                       precision=lax.Precision.HIGHEST) + b_ref[0]


def _adaln_mods(c, ada_w, ada_b):
    depth, d, n = ada_w.shape
    b = c.shape[0]
    rows = SUBLANES
    c_rows = jnp.zeros((rows, d), F32).at[:b].set(c)
    tn = 1024
    out = pl.pallas_call(
        _mod_kernel,
        grid=(depth, n // tn),
        in_specs=[pl.BlockSpec((rows, d), lambda l, j: (0, 0)),
                  pl.BlockSpec((1, d, tn), lambda l, j: (l, 0, j)),
                  pl.BlockSpec((1, 1, tn), lambda l, j: (l, 0, j))],
        out_specs=pl.BlockSpec((1, rows, tn), lambda l, j: (l, 0, j)),
        out_shape=jax.ShapeDtypeStruct((depth, rows, n), F32),
        compiler_params=_params("arbitrary", "arbitrary"),
        name="adaln_mods",
    )(c_rows, ada_w, ada_b.reshape(depth, 1, n))
    return out[:, :b].reshape(depth, b, 3, d)


def _modulated_norm(x, mod_ref, ng_ref):
    shift = mod_ref[0, 0:1, :]
    scale = mod_ref[0, 1:2, :]
    ms = jnp.mean(x * x, axis=-1, keepdims=True)
    return x * lax.rsqrt(ms + EPS) * ng_ref[...] * (1.0 + scale) + shift


def _layer0_kernel(x_ref, mod_ref, ng_ref, win_ref, lb_ref, og_ref, pw_ref, ps_ref, wout_ref,
                   o_ref, state_ref, carry_ref, *, tile, chunk, layer_slot):
    T, C, H = tile, chunk, HGRN_HEADS
    t_idx = pl.program_id(1)

    @pl.when(t_idx == 0)
    def _():
        state_ref[...] = jnp.zeros_like(state_ref)
        carry_ref[...] = jnp.zeros_like(carry_ref)

    x = x_ref[0]
    h = _modulated_norm(x, mod_ref, ng_ref)
    proj = jnp.dot(h.astype(BF16), win_ref[...], preferred_element_type=F32)
    o0 = 0
    q = proj[:, o0:o0 + HGRN_KW]; o0 += HGRN_KW
    f = proj[:, o0:o0 + HGRN_KW]; o0 += HGRN_KW
    val = proj[:, o0:o0 + HGRN_VW]; o0 += HGRN_VW
    g_a = proj[:, o0:o0 + HGRN_VW]; o0 += HGRN_VW
    u = proj[:, o0:o0 + POOL_WIDTH]; o0 += POOL_WIDTH
    g_b = proj[:, o0:o0 + POOL_WIDTH]

    lbv = lb_ref[...]
    e = jnp.exp(lbv - jnp.max(lbv, axis=0, keepdims=True))
    lower = (jnp.sum(e[0:layer_slot + 1], axis=0, keepdims=True)
             / jnp.sum(e, axis=0, keepdims=True))
    forget = lower + (1.0 - lower) * jax.nn.sigmoid(f)
    logf = jnp.log(forget)
    key = 1.0 - forget

    row = lax.broadcasted_iota(jnp.int32, (T, HGRN_KW), 0)
    ti = lax.broadcasted_iota(jnp.int32, (C, C), 0)
    si = lax.broadcasted_iota(jnp.int32, (C, C), 1)
    n_chunks = T // C
    nt_dims = (((1,), (1,)), ((), ()))

    def level_scores(qe, ke, mask, scores):
        qb = qe.astype(BF16)
        kb = ke.astype(BF16)
        out = []
        for c in range(n_chunks):
            for hh in range(H):
                rs = slice(c * C, (c + 1) * C)
                cs = slice(hh * HGRN_KEY, (hh + 1) * HGRN_KEY)
                d = lax.dot_general(qb[rs, cs], kb[rs, cs], nt_dims, preferred_element_type=F32)
                prev = scores[c * H + hh]
                out.append(jnp.where(mask, d, 0.0 if prev is None else prev))
        return out

    scores = level_scores(q, key, ti == si, [None] * (n_chunks * H))
    p_sum = logf
    q_sum = jnp.zeros_like(logf)
    total = logf
    m = 1
    while m < C:
        mask = ((ti ^ si) < 2 * m) & ((ti & m) != 0) & ((si & m) == 0)
        scores = level_scores(q * jnp.exp(p_sum), key * jnp.exp(q_sum), mask, scores)
        upper = (row & m) != 0
        t_dn = pltpu.roll(total, m, 0)
        t_up = pltpu.roll(total, T - m, 0)
        p_sum = p_sum + jnp.where(upper, t_dn, 0.0)
        q_sum = q_sum + jnp.where(upper, 0.0, t_up)
        total = total + jnp.where(upper, t_dn, t_up)
        m *= 2

    q_in = (q * jnp.exp(p_sum)).astype(BF16)
    k_out = (key * jnp.exp(q_sum)).astype(BF16)
    decay = jnp.exp(total)
    val_b = val.astype(BF16)
    tn_dims = (((0,), (0,)), ((), ()))
    oa_rows = []
    for c in range(n_chunks):
        rs = slice(c * C, (c + 1) * C)
        heads_out = []
        for hh in range(H):
            cs = slice(hh * HGRN_KEY, (hh + 1) * HGRN_KEY)
            st = state_ref[hh]
            vb = val_b[rs, cs]
            o = jnp.dot(scores[c * H + hh].astype(BF16), vb, preferred_element_type=F32)
            o = o + lax.dot_general(q_in[rs, cs], st.astype(BF16), nt_dims,
                                    preferred_element_type=F32)
            state_ref[hh] = (st * decay[c * C:c * C + 1, cs]
                             + lax.dot_general(vb, k_out[rs, cs], tn_dims,
                                               preferred_element_type=F32))
            ms_o = jnp.mean(o * o, axis=-1, keepdims=True)
            heads_out.append(o * lax.rsqrt(ms_o + EPS))
        oa_rows.append(jnp.concatenate(heads_out, axis=1))
    o_a = jnp.concatenate(oa_rows, axis=0) * og_ref[...] * _silu(g_a)

    ub = jnp.concatenate([carry_ref[...], u], axis=0)
    carry_ref[...] = u[T - POOL_HISTORY:T, :]
    wins = []
    acc = ub
    sh = 1
    while sh < max(POOL_WINDOWS):
        acc = acc + pltpu.roll(acc, sh, 0)
        sh *= 2
        wins.append(acc)
    pos = lax.broadcasted_iota(jnp.int32, (T, POOL_GROUP), 0) + t_idx * T + 1
    ob = []
    for gi, w in enumerate(POOL_WINDOWS):
        cs = slice(gi * POOL_GROUP, (gi + 1) * POOL_GROUP)
        win = wins[w.bit_length() - 2][POOL_HISTORY:, cs]
        cnt = jnp.minimum(pos, w).astype(F32)
        pooled = win / cnt - u[:, cs]
        ob.append(jnp.dot(pooled.astype(BF16), pw_ref[gi], preferred_element_type=F32))
    o_b = jnp.concatenate(ob, axis=1) * ps_ref[...] * _silu(g_b)

    mixed = jnp.concatenate([o_a, o_b], axis=1).astype(BF16)
    y = jnp.dot(mixed, wout_ref[...], preferred_element_type=F32)
    o_ref[0] = x + mod_ref[0, 2:3, :] * y


def _layer0(x, mod, norm_g, w_in, hgrn_lb, onorm_g, pool_w, pool_scale, w_out, layer_slot):
    b, s, d = x.shape
    tile = SEQ_TILE
    n_in = w_in.shape[1]
    mix = w_out.shape[0]
    const2 = lambda bi, ti: (0, 0)
    return pl.pallas_call(
        functools.partial(_layer0_kernel, tile=tile, chunk=HGRN_CHUNK, layer_slot=layer_slot),
        grid=(b, s // tile),
        in_specs=[pl.BlockSpec((1, tile, d), lambda bi, ti: (bi, ti, 0)),
                  pl.BlockSpec((1, 3, d), lambda bi, ti: (bi, 0, 0)),
                  pl.BlockSpec((1, d), const2),
                  pl.BlockSpec((d, n_in), const2),
                  pl.BlockSpec(hgrn_lb.shape, const2),
                  pl.BlockSpec((1, HGRN_VW), const2),
                  pl.BlockSpec(pool_w.shape, lambda bi, ti: (0, 0, 0)),
                  pl.BlockSpec((1, POOL_WIDTH), const2),
                  pl.BlockSpec((mix, d), const2)],
        out_specs=pl.BlockSpec((1, tile, d), lambda bi, ti: (bi, ti, 0)),
        out_shape=jax.ShapeDtypeStruct((b, s, d), F32),
        scratch_shapes=[pltpu.VMEM((HGRN_HEADS, HGRN_VAL, HGRN_KEY), F32),
                        pltpu.VMEM((POOL_HISTORY, POOL_WIDTH), F32)],
        compiler_params=_params("arbitrary", "arbitrary"),
        name="layer0_hgrn_pool",
    )(x, mod, norm_g.reshape(1, d), w_in.astype(BF16), hgrn_lb, onorm_g.reshape(1, HGRN_VW),
      pool_w.astype(BF16), pool_scale.reshape(1, POOL_WIDTH), w_out.astype(BF16))


def _split3(c):
    hi = c.astype(BF16).astype(F32)
    r = c - hi
    mid = r.astype(BF16).astype(F32)
    lo = (r - mid).astype(BF16).astype(F32)
    return hi, mid, lo


def _layer1_proj_kernel(x_ref, mod_ref, ng_ref, win_ref, bf_ref, qg_ref, kg_ref, bd_ref,
                        qa_ref, ka_ref, vt_ref, g_ref, base_ref, carry_ref, *, tile):
    T, W = tile, FOX_WIDTH
    t_idx = pl.program_id(1)

    @pl.when(t_idx == 0)
    def _():
        carry_ref[...] = jnp.zeros_like(carry_ref)

    x = x_ref[0]
    h = _modulated_norm(x, mod_ref, ng_ref)
    proj = jnp.dot(h.astype(BF16), win_ref[...], preferred_element_type=F32)
    q = proj[:, 0:W]
    k = proj[:, W:2 * W]
    v = proj[:, 2 * W:3 * W]
    g_ref[0] = proj[:, 3 * W:4 * W]
    fl = proj[:, 4 * W:4 * W + LANES]
    vt_ref[0, 0] = v.T.astype(BF16)

    z = fl + bf_ref[...]
    logf = jnp.minimum(z, 0.0) - jnp.log1p(jnp.exp(-jnp.abs(z)))
    row = lax.broadcasted_iota(jnp.int32, (T, LANES), 0)
    cum = logf
    sh = 1
    while sh < T:
        cum = cum + jnp.where(row >= sh, pltpu.roll(cum, sh, 0), 0.0)
        sh *= 2
    base = carry_ref[...]
    base_ref[0, 0] = base * LOG2E
    carry_ref[...] = base + cum[T - 1:T, :]
    c_hi, c_mid, c_lo = _split3(cum * LOG2E)

    lane = lax.broadcasted_iota(jnp.int32, (T, LANES), 1)
    ones_q = (lane >= AUG_LANE + 3) & (lane < AUG_LANE + 6)
    ones_k = (lane >= AUG_LANE) & (lane < AUG_LANE + 3)
    bd = bd_ref[...]
    qk_scale = FOX_HEAD_DIM ** -0.5 * LOG2E
    for pair in range(FOX_HEADS // 2):
        cs = slice(pair * LANES, (pair + 1) * LANES)
        qp = q[:, cs]
        kp = k[:, cs]
        q_ms = jnp.dot((qp * qp).astype(BF16), bd, preferred_element_type=F32)
        k_ms = jnp.dot((kp * kp).astype(BF16), bd, preferred_element_type=F32)
        qn = qp * lax.rsqrt(q_ms + EPS) * (qg_ref[...] * qk_scale)
        kn = kp * lax.rsqrt(k_ms + EPS) * kg_ref[...]
        for e in range(2):
            hd = 2 * pair + e
            hi = jnp.broadcast_to(c_hi[:, hd:hd + 1], (T, LANES))
            mid = jnp.broadcast_to(c_mid[:, hd:hd + 1], (T, LANES))
            lo = jnp.broadcast_to(c_lo[:, hd:hd + 1], (T, LANES))
            aug_q = jnp.where(lane == AUG_LANE, hi,
                              jnp.where(lane == AUG_LANE + 1, mid,
                                        jnp.where(lane == AUG_LANE + 2, lo,
                                                  jnp.where(ones_q, 1.0, 0.0))))
            aug_k = jnp.where(lane == AUG_LANE + 3, -hi,
                              jnp.where(lane == AUG_LANE + 4, -mid,
                                        jnp.where(lane == AUG_LANE + 5, -lo,
                                                  jnp.where(ones_k, 1.0, 0.0))))
            q_main = qn if e == 0 else pltpu.roll(qn, FOX_HEAD_DIM, 1)
            k_main = kn if e == 0 else pltpu.roll(kn, FOX_HEAD_DIM, 1)
            qa_ref[0, hd] = jnp.where(lane < FOX_HEAD_DIM, q_main, aug_q).astype(BF16)
            ka_ref[0, hd] = jnp.where(lane < FOX_HEAD_DIM, k_main, aug_k).astype(BF16)


def _layer1_proj(x, mod, norm_g, w_in, b_f, qnorm_g, knorm_g):
    b, s, d = x.shape
    tile = SEQ_TILE
    nt = s // tile
    w = FOX_WIDTH
    n_pad = 4 * w + LANES
    w_pad = jnp.zeros((d, n_pad), BF16).at[:, :w_in.shape[1]].set(w_in.astype(BF16))
    bf_pad = jnp.zeros((1, LANES), F32).at[0, :FOX_HEADS].set(b_f)
    blk = jnp.arange(LANES) // FOX_HEAD_DIM
    bd = jnp.where(blk[:, None] == blk[None, :], 1.0 / FOX_HEAD_DIM, 0.0).astype(BF16)
    qg = jnp.tile(qnorm_g, LANES // FOX_HEAD_DIM).reshape(1, LANES)
    kg = jnp.tile(knorm_g, LANES // FOX_HEAD_DIM).reshape(1, LANES)
    const2 = lambda bi, ti: (0, 0)
    return pl.pallas_call(
        functools.partial(_layer1_proj_kernel, tile=tile),
        grid=(b, nt),
        in_specs=[pl.BlockSpec((1, tile, d), lambda bi, ti: (bi, ti, 0)),
                  pl.BlockSpec((1, 3, d), lambda bi, ti: (bi, 0, 0)),
                  pl.BlockSpec((1, d), const2),
                  pl.BlockSpec((d, n_pad), const2),
                  pl.BlockSpec((1, LANES), const2),
                  pl.BlockSpec((1, LANES), const2),
                  pl.BlockSpec((1, LANES), const2),
                  pl.BlockSpec((LANES, LANES), const2)],
        out_specs=[pl.BlockSpec((1, FOX_HEADS, tile, LANES), lambda bi, ti: (bi, 0, ti, 0)),
                   pl.BlockSpec((1, FOX_HEADS, tile, LANES), lambda bi, ti: (bi, 0, ti, 0)),
                   pl.BlockSpec((1, 1, w, tile), lambda bi, ti: (bi, ti, 0, 0)),
                   pl.BlockSpec((1, tile, w), lambda bi, ti: (bi, ti, 0)),
                   pl.BlockSpec((1, 1, 1, LANES), lambda bi, ti: (bi, ti, 0, 0))],
        out_shape=[jax.ShapeDtypeStruct((b, FOX_HEADS, s, LANES), BF16),
                   jax.ShapeDtypeStruct((b, FOX_HEADS, s, LANES), BF16),
                   jax.ShapeDtypeStruct((b, nt, w, tile), BF16),
                   jax.ShapeDtypeStruct((b, s, w), F32),
                   jax.ShapeDtypeStruct((b, nt, 1, LANES), F32)],
        scratch_shapes=[pltpu.VMEM((1, LANES), F32)],
        compiler_params=_params("arbitrary", "arbitrary"),
        name="layer1_proj",
    )(x, mod, norm_g.reshape(1, d), w_pad, bf_pad, qg, kg, bd)


def _fox_kernel(base_ref, qa_ref, ka_ref, vt_ref, g_ref, o_ref, s_ref, acc_ref, *,
                q_tile, k_block, n_kblocks):
    TQ, TK = q_tile, k_block
    R = TQ // TK
    b = pl.program_id(0)
    pair = pl.program_id(1)
    i = pl.program_id(2)
    q_t = [qa_ref[0, e].astype(F32).T.astype(BF16) for e in range(2)]
    col = lax.broadcasted_iota(jnp.int32, (1, TQ), 1)
    k_row = lax.broadcasted_iota(jnp.int32, (TK, TQ), 0)
    q_col = lax.broadcasted_iota(jnp.int32, (TK, TQ), 1)

    def base_at(blk, hd):
        return base_ref[(b * n_kblocks + blk) * FOX_HEADS + hd]

    base_q = []
    for e in range(2):
        rowv = jnp.zeros((1, TQ), F32) + base_at(i * R, 2 * pair + e)
        for r in range(1, R):
            rowv = jnp.where(col >= r * TK, base_at(i * R + r, 2 * pair + e), rowv)
        base_q.append(rowv)

    def qk(e, n):
        k_blk = ka_ref[0, e, pl.ds(pl.multiple_of(n * TK, TK), TK), :]
        return jnp.dot(k_blk, q_t[e], preferred_element_type=F32)

    def softmax_pv(e, n, load_s, state):
        m_run, l_run = state
        delta = base_q[e] - base_at(n, 2 * pair + e)
        m_new = jnp.maximum(m_run, jnp.max(load_s(), axis=0, keepdims=True) + delta)
        p_t = jnp.exp2(load_s() - (m_new - delta))
        alpha = jnp.exp2(m_run - m_new)
        l_new = alpha * l_run + jnp.sum(p_t, axis=0, keepdims=True)
        v_t = vt_ref[0, n, pl.ds(e * FOX_HEAD_DIM, FOX_HEAD_DIM), :]
        acc_ref[e] = alpha * acc_ref[e] + jnp.dot(v_t, p_t.astype(BF16),
                                                  preferred_element_type=F32)
        return m_new, l_new

    acc_ref[...] = jnp.zeros_like(acc_ref)
    state = [(jnp.full((1, TQ), MASK_VALUE, F32), jnp.zeros((1, TQ), F32)) for _ in range(2)]

    for r in range(R):
        causal = (k_row + r * TK) <= q_col
        for e in range(2):
            s_diag = jnp.where(causal, qk(e, i * R + r), MASK_VALUE)
            state[e] = softmax_pv(e, i * R + r, lambda s=s_diag: s, state[e])
    for e in range(2):
        s_ref[e] = qk(e, 0)

    n_full = i * R

    def body(n, st):
        out = tuple(softmax_pv(e, n, lambda e=e: s_ref[e], st[e]) for e in range(2))
        for e in range(2):
            s_ref[e] = qk(e, n + 1)
        return out

    state = lax.fori_loop(0, n_full - 1, body, tuple(state))
    has_full = (jnp.zeros((TK, TQ), jnp.int32) + i) > 0
    n_last = jnp.maximum(n_full - 1, 0)
    final = [softmax_pv(e, n_last, lambda e=e: jnp.where(has_full, s_ref[e], MASK_VALUE), state[e])
             for e in range(2)]
    o_t = jnp.concatenate([acc_ref[e] * (1.0 / final[e][1]) for e in range(2)], axis=0)
    g = g_ref[0]
    o_ref[0] = (o_t.T * _silu(g)).astype(BF16)


def _fox_attention(base, qa, ka, vt, g):
    b, heads, s, _ = qa.shape
    tq, tk = ATTN_Q_TILE, ATTN_BLOCK
    nkb = s // tk
    w = g.shape[-1]
    return pl.pallas_call(
        functools.partial(_fox_kernel, q_tile=tq, k_block=tk, n_kblocks=nkb),
        grid=(b, heads // 2, s // tq),
        in_specs=[pl.BlockSpec(memory_space=pltpu.SMEM),
                  pl.BlockSpec((1, 2, tq, LANES), lambda bi, p, i: (bi, p, i, 0)),
                  pl.BlockSpec((1, 2, s, LANES), lambda bi, p, i: (bi, p, 0, 0)),
                  pl.BlockSpec((1, nkb, LANES, tk), lambda bi, p, i: (bi, 0, p, 0)),
                  pl.BlockSpec((1, tq, LANES), lambda bi, p, i: (bi, i, p))],
        out_specs=pl.BlockSpec((1, tq, LANES), lambda bi, p, i: (bi, i, p)),
        out_shape=jax.ShapeDtypeStruct((b, s, w), BF16),
        scratch_shapes=[pltpu.VMEM((2, tk, tq), F32),
                        pltpu.VMEM((2, FOX_HEAD_DIM, tq), F32)],
        compiler_params=_params("arbitrary", "arbitrary", "arbitrary"),
        name="fox_attention",
    )(base, qa, ka, vt, g)


def _out_proj_kernel(a_ref, x_ref, mod_ref, w_ref, o_ref):
    y = jnp.dot(a_ref[0], w_ref[...], preferred_element_type=F32)
    o_ref[0] = x_ref[0] + mod_ref[0, 2:3, :] * y


def _out_proj(a, x, mod, w_out):
    b, s, d = x.shape
    tile = 512
    k = a.shape[-1]
    return pl.pallas_call(
        _out_proj_kernel,
        grid=(b, s // tile),
        in_specs=[pl.BlockSpec((1, tile, k), lambda bi, ti: (bi, ti, 0)),
                  pl.BlockSpec((1, tile, d), lambda bi, ti: (bi, ti, 0)),
                  pl.BlockSpec((1, 3, d), lambda bi, ti: (bi, 0, 0)),
                  pl.BlockSpec((k, d), lambda bi, ti: (0, 0))],
        out_specs=pl.BlockSpec((1, tile, d), lambda bi, ti: (bi, ti, 0)),
        out_shape=jax.ShapeDtypeStruct((b, s, d), F32),
        compiler_params=_params("arbitrary", "arbitrary"),
        name="layer1_out_proj",
    )(a, x, mod, w_out.astype(BF16))


def _layer1(x, mod, norm_g, w_in, b_f, qnorm_g, knorm_g, w_out):
    qa, ka, vt, g, base = _layer1_proj(x, mod, norm_g, w_in, b_f, qnorm_g, knorm_g)
    base_flat = base[:, :, 0, :FOX_HEADS].reshape(-1)
    gated = _fox_attention(base_flat, qa, ka, vt, g)
    return _out_proj(gated, x, mod, w_out)


def kernel(x, c, norm_g, ada_w, ada_b, hgrn_lb, even_w_in, hgrn_onorm_g, pool_w, pool_scale,
           even_w_out, odd_w_in, fox_b_f, fox_qnorm_g, fox_knorm_g, odd_w_out):
    depth = norm_g.shape[0]
    mods = _adaln_mods(c, ada_w, ada_b)
    for l in range(depth):
        j = l // 2
        if l % 2 == 0:
            x = _layer0(x, mods[l], norm_g[l], even_w_in[j], hgrn_lb, hgrn_onorm_g[j],
                        pool_w[j], pool_scale[j], even_w_out[j], layer_slot=l)
        else:
            x = _layer1(x, mods[l], norm_g[l], odd_w_in[j], fox_b_f[j], fox_qnorm_g[j],
                        fox_knorm_g[j], odd_w_out[j])
    return x
```

```python
import functools

import jax
import jax.numpy as jnp
from jax import lax
from jax.experimental import pallas as pl
from jax.experimental.pallas import tpu as pltpu

F32 = jnp.float32
BF16 = jnp.bfloat16
EPS = 1e-6

HGRN_HEADS = 4
HGRN_KEY = 128
HGRN_VAL = 128
HGRN_KW = HGRN_HEADS * HGRN_KEY
HGRN_VW = HGRN_HEADS * HGRN_VAL
POOL_WINDOWS = (2, 4, 8, 16)
POOL_GROUP = 128
POOL_WIDTH = POOL_GROUP * len(POOL_WINDOWS)
POOL_HISTORY = 16
FOX_HEADS = 16
FOX_HEAD_DIM = 64
FOX_WIDTH = FOX_HEADS * FOX_HEAD_DIM

LANES = 128
SUBLANES = 8
VMEM_LIMIT_BYTES = 56 * 1024 * 1024

SEQ_TILE = 256
HGRN_CHUNK = 128
ATTN_Q_TILE = 512
ATTN_HEADS = 4
FOX_SAFE_LOGIT = 100.0
LOG2E = 1.4426950408889634
MASK_VALUE = -1e30
AUG_LANE = FOX_HEAD_DIM


def _silu(x):
    return x * jax.nn.sigmoid(x)


def _params(*semantics):
    return pltpu.CompilerParams(dimension_semantics=semantics, vmem_limit_bytes=VMEM_LIMIT_BYTES)


def _mod_kernel(c_ref, w_ref, b_ref, o_ref):
    c = c_ref[...]
    cond = _silu(c)
    o_ref[0] = jnp.dot(cond, w_ref[0], preferred_element_type=F32,
                       precision=lax.Precision.HIGHEST) + b_ref[0]


def _adaln_mods(c, ada_w, ada_b):
    depth, d, n = ada_w.shape
    b = c.shape[0]
    rows = SUBLANES
    c_rows = jnp.zeros((rows, d), F32).at[:b].set(c)
    tn = 1024
    out = pl.pallas_call(
        _mod_kernel,
        grid=(depth, n // tn),
        in_specs=[pl.BlockSpec((rows, d), lambda l, j: (0, 0)),
                  pl.BlockSpec((1, d, tn), lambda l, j: (l, 0, j)),
                  pl.BlockSpec((1, 1, tn), lambda l, j: (l, 0, j))],
        out_specs=pl.BlockSpec((1, rows, tn), lambda l, j: (l, 0, j)),
        out_shape=jax.ShapeDtypeStruct((depth, rows, n), F32),
        compiler_params=_params("arbitrary", "arbitrary"),
        name="adaln_mods",
    )(c_rows, ada_w, ada_b.reshape(depth, 1, n))
    return out[:, :b].reshape(depth, b, 3, d)


def _modulated_norm(x, mod_ref, ng_ref):
    shift = mod_ref[0, 0:1, :]
    scale = mod_ref[0, 1:2, :]
    ms = jnp.mean(x * x, axis=-1, keepdims=True)
    return x * lax.rsqrt(ms + EPS) * ng_ref[...] * (1.0 + scale) + shift


def _layer0_kernel(x_ref, mod_ref, ng_ref, win_ref, lb_ref, og_ref, pw_ref, ps_ref, wout_ref,
                   o_ref, state_ref, carry_ref, *, tile, chunk, layer_slot):
    T, C, H = tile, chunk, HGRN_HEADS
    t_idx = pl.program_id(1)

    @pl.when(t_idx == 0)
    def _():
        state_ref[...] = jnp.zeros_like(state_ref)
        carry_ref[...] = jnp.zeros_like(carry_ref)

    x = x_ref[0]
    h = _modulated_norm(x, mod_ref, ng_ref)
    proj = jnp.dot(h.astype(BF16), win_ref[...], preferred_element_type=F32)
    o0 = 0
    q = proj[:, o0:o0 + HGRN_KW]; o0 += HGRN_KW
    f = proj[:, o0:o0 + HGRN_KW]; o0 += HGRN_KW
    val = proj[:, o0:o0 + HGRN_VW]; o0 += HGRN_VW
    g_a = proj[:, o0:o0 + HGRN_VW]; o0 += HGRN_VW
    u = proj[:, o0:o0 + POOL_WIDTH]; o0 += POOL_WIDTH
    g_b = proj[:, o0:o0 + POOL_WIDTH]

    lbv = lb_ref[...]
    e = jnp.exp(lbv - jnp.max(lbv, axis=0, keepdims=True))
    lower = (jnp.sum(e[0:layer_slot + 1], axis=0, keepdims=True)
             / jnp.sum(e, axis=0, keepdims=True))
    forget = lower + (1.0 - lower) * jax.nn.sigmoid(f)
    logf = jnp.log(forget)
    key = 1.0 - forget

    row = lax.broadcasted_iota(jnp.int32, (T, HGRN_KW), 0)
    ti = lax.broadcasted_iota(jnp.int32, (C, C), 0)
    si = lax.broadcasted_iota(jnp.int32, (C, C), 1)
    n_chunks = T // C
    nt_dims = (((1,), (1,)), ((), ()))

    def level_scores(qe, ke, mask, scores):
        qb = qe.astype(BF16)
        kb = ke.astype(BF16)
        out = []
        for c in range(n_chunks):
            for hh in range(H):
                rs = slice(c * C, (c + 1) * C)
                cs = slice(hh * HGRN_KEY, (hh + 1) * HGRN_KEY)
                d = lax.dot_general(qb[rs, cs], kb[rs, cs], nt_dims, preferred_element_type=F32)
                prev = scores[c * H + hh]
                out.append(jnp.where(mask, d, 0.0 if prev is None else prev))
        return out

    scores = level_scores(q, key, ti == si, [None] * (n_chunks * H))
    p_sum = logf
    q_sum = jnp.zeros_like(logf)
    total = logf
    m = 1
    while m < C:
        mask = ((ti ^ si) < 2 * m) & ((ti & m) != 0) & ((si & m) == 0)
        scores = level_scores(q * jnp.exp(p_sum), key * jnp.exp(q_sum), mask, scores)
        upper = (row & m) != 0
        t_dn = pltpu.roll(total, m, 0)
        t_up = pltpu.roll(total, T - m, 0)
        p_sum = p_sum + jnp.where(upper, t_dn, 0.0)
        q_sum = q_sum + jnp.where(upper, 0.0, t_up)
        total = total + jnp.where(upper, t_dn, t_up)
        m *= 2

    q_in = (q * jnp.exp(p_sum)).astype(BF16)
    k_out = (key * jnp.exp(q_sum)).astype(BF16)
    decay = jnp.exp(total)
    val_b = val.astype(BF16)
    tn_dims = (((0,), (0,)), ((), ()))
    oa_rows = []
    for c in range(n_chunks):
        rs = slice(c * C, (c + 1) * C)
        heads_out = []
        for hh in range(H):
            cs = slice(hh * HGRN_KEY, (hh + 1) * HGRN_KEY)
            st = state_ref[hh]
            vb = val_b[rs, cs]
            o = jnp.dot(scores[c * H + hh].astype(BF16), vb, preferred_element_type=F32)
            o = o + lax.dot_general(q_in[rs, cs], st.astype(BF16), nt_dims,
                                    preferred_element_type=F32)
            state_ref[hh] = (st * decay[c * C:c * C + 1, cs]
                             + lax.dot_general(vb, k_out[rs, cs], tn_dims,
                                               preferred_element_type=F32))
            ms_o = jnp.mean(o * o, axis=-1, keepdims=True)
            heads_out.append(o * lax.rsqrt(ms_o + EPS))
        oa_rows.append(jnp.concatenate(heads_out, axis=1))
    o_a = jnp.concatenate(oa_rows, axis=0) * og_ref[...] * _silu(g_a)

    ub = jnp.concatenate([carry_ref[...], u], axis=0)
    carry_ref[...] = u[T - POOL_HISTORY:T, :]
    wins = []
    acc = ub
    sh = 1
    while sh < max(POOL_WINDOWS):
        acc = acc + pltpu.roll(acc, sh, 0)
        sh *= 2
        wins.append(acc)
    pos = lax.broadcasted_iota(jnp.int32, (T, POOL_GROUP), 0) + t_idx * T + 1
    ob = []
    for gi, w in enumerate(POOL_WINDOWS):
        cs = slice(gi * POOL_GROUP, (gi + 1) * POOL_GROUP)
        win = wins[w.bit_length() - 2][POOL_HISTORY:, cs]
        cnt = jnp.minimum(pos, w).astype(F32)
        pooled = win / cnt - u[:, cs]
        ob.append(jnp.dot(pooled.astype(BF16), pw_ref[gi], preferred_element_type=F32))
    o_b = jnp.concatenate(ob, axis=1) * ps_ref[...] * _silu(g_b)

    mixed = jnp.concatenate([o_a, o_b], axis=1).astype(BF16)
    y = jnp.dot(mixed, wout_ref[...], preferred_element_type=F32)
    o_ref[0] = x + mod_ref[0, 2:3, :] * y


def _layer0(x, mod, norm_g, w_in, hgrn_lb, onorm_g, pool_w, pool_scale, w_out, layer_slot):
    b, s, d = x.shape
    tile = SEQ_TILE
    n_in = w_in.shape[1]
    mix = w_out.shape[0]
    const2 = lambda bi, ti: (0, 0)
    return pl.pallas_call(
        functools.partial(_layer0_kernel, tile=tile, chunk=HGRN_CHUNK, layer_slot=layer_slot),
        grid=(b, s // tile),
        in_specs=[pl.BlockSpec((1, tile, d), lambda bi, ti: (bi, ti, 0)),
                  pl.BlockSpec((1, 3, d), lambda bi, ti: (bi, 0, 0)),
                  pl.BlockSpec((1, d), const2),
                  pl.BlockSpec((d, n_in), const2),
                  pl.BlockSpec(hgrn_lb.shape, const2),
                  pl.BlockSpec((1, HGRN_VW), const2),
                  pl.BlockSpec(pool_w.shape, lambda bi, ti: (0, 0, 0)),
                  pl.BlockSpec((1, POOL_WIDTH), const2),
                  pl.BlockSpec((mix, d), const2)],
        out_specs=pl.BlockSpec((1, tile, d), lambda bi, ti: (bi, ti, 0)),
        out_shape=jax.ShapeDtypeStruct((b, s, d), F32),
        scratch_shapes=[pltpu.VMEM((HGRN_HEADS, HGRN_VAL, HGRN_KEY), F32),
                        pltpu.VMEM((POOL_HISTORY, POOL_WIDTH), F32)],
        compiler_params=_params("arbitrary", "arbitrary"),
        name="layer0_hgrn_pool",
    )(x, mod, norm_g.reshape(1, d), w_in.astype(BF16), hgrn_lb, onorm_g.reshape(1, HGRN_VW),
      pool_w.astype(BF16), pool_scale.reshape(1, POOL_WIDTH), w_out.astype(BF16))


def _split3(c):
    hi = c.astype(BF16).astype(F32)
    r = c - hi
    mid = r.astype(BF16).astype(F32)
    lo = (r - mid).astype(BF16).astype(F32)
    return hi, mid, lo


def _layer1_proj_kernel(x_ref, mod_ref, ng_ref, win_ref, bf_ref, qg_ref, kg_ref, bd_ref,
                        qa_ref, ka_ref, vt_ref, g_ref, cq_ref, edge_ref, carry_ref, *, tile):
    T, W = tile, FOX_WIDTH
    t_idx = pl.program_id(1)

    @pl.when(t_idx == 0)
    def _():
        carry_ref[...] = jnp.zeros_like(carry_ref)

    x = x_ref[0]
    h = _modulated_norm(x, mod_ref, ng_ref)
    proj = jnp.dot(h.astype(BF16), win_ref[...], preferred_element_type=F32)
    q = proj[:, 0:W]
    k = proj[:, W:2 * W]
    v = proj[:, 2 * W:3 * W]
    g_ref[0] = proj[:, 3 * W:4 * W]
    fl = proj[:, 4 * W:4 * W + LANES]
    vt_ref[0, 0] = v.T.astype(BF16)

    z = fl + bf_ref[...]
    logf = jnp.minimum(z, 0.0) - jnp.log1p(jnp.exp(-jnp.abs(z)))
    row = lax.broadcasted_iota(jnp.int32, (T, LANES), 0)
    cum = logf
    sh = 1
    while sh < T:
        cum = cum + jnp.where(row >= sh, pltpu.roll(cum, sh, 0), 0.0)
        sh *= 2
    start = carry_ref[...]
    end = start + cum[T - 1:T, :]
    carry_ref[...] = end
    edge_ref[0, 0] = jnp.concatenate([start, end], axis=0) * LOG2E
    s_hi, s_mid, s_lo = _split3(cum * LOG2E)
    e_hi, e_mid, e_lo = _split3((cum - cum[T - 1:T, :]) * LOG2E)
    total_t = ((cum + start) * LOG2E).T
    for hd in range(FOX_HEADS):
        cq_ref[0, hd] = total_t[hd:hd + 1, :]

    lane = lax.broadcasted_iota(jnp.int32, (T, LANES), 1)
    ones_q = (lane >= AUG_LANE + 3) & (lane < AUG_LANE + 9)
    ones_k = (lane >= AUG_LANE) & (lane < AUG_LANE + 3)

    def spread(a, hd):
        return jnp.broadcast_to(a[:, hd:hd + 1], (T, LANES))
    bd = bd_ref[...]
    qk_scale = FOX_HEAD_DIM ** -0.5 * LOG2E
    for pair in range(FOX_HEADS // 2):
        cs = slice(pair * LANES, (pair + 1) * LANES)
        qp = q[:, cs]
        kp = k[:, cs]
        q_ms = jnp.dot((qp * qp).astype(BF16), bd, preferred_element_type=F32)
        k_ms = jnp.dot((kp * kp).astype(BF16), bd, preferred_element_type=F32)
        qn = qp * lax.rsqrt(q_ms + EPS) * (qg_ref[...] * qk_scale)
        kn = kp * lax.rsqrt(k_ms + EPS) * kg_ref[...]
        for e in range(LANES // FOX_HEAD_DIM):
            hd = (LANES // FOX_HEAD_DIM) * pair + e
            hi, mid, lo = spread(s_hi, hd), spread(s_mid, hd), spread(s_lo, hd)
            aug_q = jnp.where(lane == AUG_LANE, hi,
                              jnp.where(lane == AUG_LANE + 1, mid,
                                        jnp.where(lane == AUG_LANE + 2, lo,
                                                  jnp.where(ones_q, 1.0, 0.0))))
            aug_k = jnp.where(lane == AUG_LANE + 3, -hi,
                              jnp.where(lane == AUG_LANE + 4, -mid,
                                        jnp.where(lane == AUG_LANE + 5, -lo,
                                                  jnp.where(ones_k, 1.0, 0.0))))
            aug_k = jnp.where(lane == AUG_LANE + 6, -spread(e_hi, hd),
                              jnp.where(lane == AUG_LANE + 7, -spread(e_mid, hd),
                                        jnp.where(lane == AUG_LANE + 8, -spread(e_lo, hd), aug_k)))
            q_main = qn if e == 0 else pltpu.roll(qn, FOX_HEAD_DIM, 1)
            k_main = kn if e == 0 else pltpu.roll(kn, FOX_HEAD_DIM, 1)
            qa_ref[0, hd] = jnp.where(lane < FOX_HEAD_DIM, q_main, aug_q).astype(BF16)
            ka_ref[0, hd] = jnp.where(lane < FOX_HEAD_DIM, k_main, aug_k).astype(BF16)


def _layer1_proj(x, mod, norm_g, w_in, b_f, qnorm_g, knorm_g):
    b, s, d = x.shape
    tile = SEQ_TILE
    nt = s // tile
    w = FOX_WIDTH
    n_pad = 4 * w + LANES
    w_pad = jnp.zeros((d, n_pad), BF16).at[:, :w_in.shape[1]].set(w_in.astype(BF16))
    bf_pad = jnp.zeros((1, LANES), F32).at[0, :FOX_HEADS].set(b_f)
    blk = jnp.arange(LANES) // FOX_HEAD_DIM
    bd = jnp.where(blk[:, None] == blk[None, :], 1.0 / FOX_HEAD_DIM, 0.0).astype(BF16)
    qg = jnp.tile(qnorm_g, LANES // FOX_HEAD_DIM).reshape(1, LANES)
    kg = jnp.tile(knorm_g, LANES // FOX_HEAD_DIM).reshape(1, LANES)
    const2 = lambda bi, ti: (0, 0)
    return pl.pallas_call(
        functools.partial(_layer1_proj_kernel, tile=tile),
        grid=(b, nt),
        in_specs=[pl.BlockSpec((1, tile, d), lambda bi, ti: (bi, ti, 0)),
                  pl.BlockSpec((1, 3, d), lambda bi, ti: (bi, 0, 0)),
                  pl.BlockSpec((1, d), const2),
                  pl.BlockSpec((d, n_pad), const2),
                  pl.BlockSpec((1, LANES), const2),
                  pl.BlockSpec((1, LANES), const2),
                  pl.BlockSpec((1, LANES), const2),
                  pl.BlockSpec((LANES, LANES), const2)],
        out_specs=[pl.BlockSpec((1, FOX_HEADS, tile, LANES), lambda bi, ti: (bi, 0, ti, 0)),
                   pl.BlockSpec((1, FOX_HEADS, tile, LANES), lambda bi, ti: (bi, 0, ti, 0)),
                   pl.BlockSpec((1, 1, w, tile), lambda bi, ti: (bi, ti, 0, 0)),
                   pl.BlockSpec((1, tile, w), lambda bi, ti: (bi, ti, 0)),
                   pl.BlockSpec((1, FOX_HEADS, 1, tile), lambda bi, ti: (bi, 0, 0, ti)),
                   pl.BlockSpec((1, 1, 2, LANES), lambda bi, ti: (bi, ti, 0, 0))],
        out_shape=[jax.ShapeDtypeStruct((b, FOX_HEADS, s, LANES), BF16),
                   jax.ShapeDtypeStruct((b, FOX_HEADS, s, LANES), BF16),
                   jax.ShapeDtypeStruct((b, nt, w, tile), BF16),
                   jax.ShapeDtypeStruct((b, s, w), F32),
                   jax.ShapeDtypeStruct((b, FOX_HEADS, 1, s), F32),
                   jax.ShapeDtypeStruct((b, nt, 2, LANES), F32)],
        scratch_shapes=[pltpu.VMEM((1, LANES), F32)],
        compiler_params=_params("arbitrary", "arbitrary"),
        name="layer1_proj",
    )(x, mod, norm_g.reshape(1, d), w_pad, bf_pad, qg, kg, bd)


def _transposed_queries(qa_ref, heads, same_block):
    lane = lax.broadcasted_iota(jnp.int32, qa_ref.shape[2:], 1)
    if same_block:
        keep = lane < AUG_LANE + 6
    else:
        keep = (lane < AUG_LANE) | (lane >= AUG_LANE + 6)
    return [jnp.where(keep, qa_ref[0, e].astype(F32), 0.0).T.astype(BF16) for e in range(heads)]


def _gated_output(acc_t, g_ref, o_ref):
    o_ref[0] = (acc_t.T * _silu(g_ref[0])).astype(BF16)


def _fox_bounded_kernel(end_ref, qa_ref, ka_ref, vt_ref, cq_ref, g_ref, o_ref, p_ref, acc_ref, *,
                        tile, heads):
    TQ, HP = tile, heads
    SB = vt_ref.shape[-1]
    R = TQ // SB
    n_blocks = ka_ref.shape[2] // SB
    b = pl.program_id(0)
    group = pl.program_id(1)
    i = pl.program_id(2)
    q_far = _transposed_queries(qa_ref, HP, same_block=False)
    q_near = _transposed_queries(qa_ref, HP, same_block=True)
    causal = (lax.broadcasted_iota(jnp.int32, (SB, SB), 0)
              <= lax.broadcasted_iota(jnp.int32, (SB, SB), 1))

    def keys(e, blk, count):
        return ka_ref[0, e, pl.ds(pl.multiple_of(blk * SB, SB), count * SB), :]

    def values_t(e, blk):
        return vt_ref[0, blk, pl.ds(e * FOX_HEAD_DIM, FOX_HEAD_DIM), :]

    def rest(e, blk):
        end = end_ref[(b * n_blocks + blk) * FOX_HEADS + HP * group + e]
        return jnp.exp2(jnp.minimum(cq_ref[0, e] - end, 0.0))

    def col_partial(p):
        return jnp.sum(p.reshape(p.shape[0] // SUBLANES, SUBLANES, p.shape[1]), axis=0)

    def contract(e, blk0, p_bf16, lanes=slice(None)):
        for r in range(p_bf16.shape[0] // SB):
            pv = jnp.dot(values_t(e, blk0 + r), p_bf16[r * SB:(r + 1) * SB],
                         preferred_element_type=F32)
            acc_ref[e, :, lanes] += rest(e, blk0 + r)[:, lanes] * pv

    def weighted_partial(e, blk0, p):
        out = 0.0
        for r in range(p.shape[0] // SB):
            out = out + rest(e, blk0 + r) * col_partial(p[r * SB:(r + 1) * SB])
        return out

    acc_ref[...] = jnp.zeros_like(acc_ref)
    l_part = []
    for e in range(HP):
        l_cols = []
        for hq in range(R):
            cols = slice(hq * SB, (hq + 1) * SB)
            blk = i * R + hq
            s_t = jnp.dot(keys(e, blk, 1), q_near[e][:, cols], preferred_element_type=F32)
            p = jnp.where(causal, jnp.exp2(s_t), 0.0)
            l_col = col_partial(p)
            acc_ref[e, :, cols] += jnp.dot(values_t(e, blk), p.astype(BF16),
                                           preferred_element_type=F32)
            for hk in range(hq):
                blk_k = i * R + hk
                p = jnp.exp2(jnp.dot(keys(e, blk_k, 1), q_far[e][:, cols],
                                     preferred_element_type=F32))
                l_col = l_col + rest(e, blk_k)[:, cols] * col_partial(p)
                contract(e, blk_k, p.astype(BF16), cols)
            l_cols.append(l_col)
        l_part.append(jnp.concatenate(l_cols, axis=1))

    def scores(e, n):
        return jnp.dot(keys(e, n * R, R), q_far[e], preferred_element_type=F32)

    has_full = (jnp.zeros((TQ, TQ), jnp.int32) + i) > 0
    for e in range(HP):
        p = jnp.where(has_full, jnp.exp2(scores(e, 0)), 0.0)
        l_part[e] = l_part[e] + weighted_partial(e, 0, p)
        p_ref[e] = p.astype(BF16)

    def body(n, l_run):
        s_new = [scores(e, n) for e in range(HP)]
        for e in range(HP):
            contract(e, (n - 1) * R, p_ref[e])
        out = []
        for e in range(HP):
            p = jnp.exp2(s_new[e])
            out.append(l_run[e] + weighted_partial(e, n * R, p))
            p_ref[e] = p.astype(BF16)
        return tuple(out)

    l_part = lax.fori_loop(1, i, body, tuple(l_part))
    for e in range(HP):
        contract(e, jnp.maximum(i - 1, 0) * R, p_ref[e])
    out_t = [acc_ref[e] * (1.0 / jnp.sum(l_part[e], axis=0, keepdims=True)) for e in range(HP)]
    _gated_output(jnp.concatenate(out_t, axis=0), g_ref, o_ref)


def _fox_online_kernel(start_ref, qa_ref, ka_ref, vt_ref, g_ref, o_ref, *, tile, heads):
    TB, HP = tile, heads
    n_blocks = ka_ref.shape[2] // TB
    b = pl.program_id(0)
    group = pl.program_id(1)
    i = pl.program_id(2)
    q_t = _transposed_queries(qa_ref, HP, same_block=True)
    k_row = lax.broadcasted_iota(jnp.int32, (TB, TB), 0)
    q_col = lax.broadcasted_iota(jnp.int32, (TB, TB), 1)

    def step(j, carry, masked):
        out = []
        for e in range(HP):
            m_run, l_run, acc = carry[e]
            k_blk = ka_ref[0, e, pl.ds(pl.multiple_of(j * TB, TB), TB), :]
            s_t = jnp.dot(k_blk, q_t[e], preferred_element_type=F32)
            if masked:
                s_t = jnp.where(k_row <= q_col, s_t, MASK_VALUE)
            hd = HP * group + e
            delta = (start_ref[(b * n_blocks + i) * FOX_HEADS + hd]
                     - start_ref[(b * n_blocks + j) * FOX_HEADS + hd])
            m_new = jnp.maximum(m_run, jnp.max(s_t, axis=0, keepdims=True) + delta)
            p_t = jnp.exp2(s_t - (m_new - delta))
            alpha = jnp.exp2(m_run - m_new)
            l_new = alpha * l_run + jnp.sum(p_t, axis=0, keepdims=True)
            v_t = vt_ref[0, j, pl.ds(e * FOX_HEAD_DIM, FOX_HEAD_DIM), :]
            acc_new = alpha * acc + jnp.dot(v_t, p_t.astype(BF16), preferred_element_type=F32)
            out.append((m_new, l_new, acc_new))
        return tuple(out)

    init = (jnp.full((1, TB), MASK_VALUE, F32), jnp.zeros((1, TB), F32),
            jnp.zeros((FOX_HEAD_DIM, TB), F32))
    carry = lax.fori_loop(0, i, lambda j, c: step(j, c, False), (init,) * HP)
    carry = step(i, carry, True)
    _gated_output(jnp.concatenate([acc / l_run for _, l_run, acc in carry], axis=0), g_ref, o_ref)


def _fox_attention(qa, ka, vt, g, cq, edges, *, bounded):
    b, n_heads, s, _ = qa.shape
    _, nvb, w, vb = vt.shape
    if bounded:
        tile, hp = ATTN_Q_TILE, ATTN_HEADS
    else:
        tile, hp = vb, LANES // FOX_HEAD_DIM
    gw = hp * FOX_HEAD_DIM
    specs = dict(
        table=pl.BlockSpec(memory_space=pltpu.SMEM),
        q=pl.BlockSpec((1, hp, tile, LANES), lambda bi, p, i: (bi, p, i, 0)),
        k=pl.BlockSpec((1, hp, s, LANES), lambda bi, p, i: (bi, p, 0, 0)),
        v=pl.BlockSpec((1, nvb, gw, vb), lambda bi, p, i: (bi, 0, p, 0)),
        cq=pl.BlockSpec((1, hp, 1, tile), lambda bi, p, i: (bi, p, 0, i)),
        g=pl.BlockSpec((1, tile, gw), lambda bi, p, i: (bi, i, p)))
    if bounded:
        body = functools.partial(_fox_bounded_kernel, tile=tile, heads=hp)
        names = ("table", "q", "k", "v", "cq", "g")
        args = (edges[:, :, 1, :FOX_HEADS].reshape(-1), qa, ka, vt, cq, g)
        scratch = [pltpu.VMEM((hp, tile, tile), BF16),
                   pltpu.VMEM((hp, FOX_HEAD_DIM, tile), F32)]
    else:
        body = functools.partial(_fox_online_kernel, tile=tile, heads=hp)
        names = ("table", "q", "k", "v", "g")
        args = (edges[:, :, 0, :FOX_HEADS].reshape(-1), qa, ka, vt, g)
        scratch = []
    return pl.pallas_call(
        body,
        grid=(b, n_heads // hp, s // tile),
        in_specs=[specs[n] for n in names],
        out_specs=specs["g"],
        out_shape=jax.ShapeDtypeStruct((b, s, w), BF16),
        scratch_shapes=scratch,
        compiler_params=_params("arbitrary", "arbitrary", "arbitrary"),
        name="fox_attention_bounded" if bounded else "fox_attention_online",
    )(*args)


def _out_proj_kernel(a_ref, x_ref, mod_ref, w_ref, o_ref):
    y = jnp.dot(a_ref[0], w_ref[...], preferred_element_type=F32)
    o_ref[0] = x_ref[0] + mod_ref[0, 2:3, :] * y


def _out_proj(a, x, mod, w_out):
    b, s, d = x.shape
    tile = 512
    k = a.shape[-1]
    return pl.pallas_call(
        _out_proj_kernel,
        grid=(b, s // tile),
        in_specs=[pl.BlockSpec((1, tile, k), lambda bi, ti: (bi, ti, 0)),
                  pl.BlockSpec((1, tile, d), lambda bi, ti: (bi, ti, 0)),
                  pl.BlockSpec((1, 3, d), lambda bi, ti: (bi, 0, 0)),
                  pl.BlockSpec((k, d), lambda bi, ti: (0, 0))],
        out_specs=pl.BlockSpec((1, tile, d), lambda bi, ti: (bi, ti, 0)),
        out_shape=jax.ShapeDtypeStruct((b, s, d), F32),
        compiler_params=_params("arbitrary", "arbitrary"),
        name="layer1_out_proj",
    )(a, x, mod, w_out.astype(BF16))


def _layer1(x, mod, norm_g, w_in, b_f, qnorm_g, knorm_g, w_out):
    qa, ka, vt, g, cq, edges = _layer1_proj(x, mod, norm_g, w_in, b_f, qnorm_g, knorm_g)
    score_bound = (LOG2E * FOX_HEAD_DIM ** 0.5
                   * jnp.max(jnp.abs(qnorm_g)) * jnp.max(jnp.abs(knorm_g)))
    gated = lax.cond(score_bound <= FOX_SAFE_LOGIT,
                     functools.partial(_fox_attention, bounded=True),
                     functools.partial(_fox_attention, bounded=False),
                     qa, ka, vt, g, cq, edges)
    return _out_proj(gated, x, mod, w_out)


def kernel(x, c, norm_g, ada_w, ada_b, hgrn_lb, even_w_in, hgrn_onorm_g, pool_w, pool_scale,
           even_w_out, odd_w_in, fox_b_f, fox_qnorm_g, fox_knorm_g, odd_w_out):
    depth = norm_g.shape[0]
    mods = _adaln_mods(c, ada_w, ada_b)
    for l in range(depth):
        j = l // 2
        if l % 2 == 0:
            x = _layer0(x, mods[l], norm_g[l], even_w_in[j], hgrn_lb, hgrn_onorm_g[j],
                        pool_w[j], pool_scale[j], even_w_out[j], layer_slot=l)
        else:
            x = _layer1(x, mods[l], norm_g[l], odd_w_in[j], fox_b_f[j], fox_qnorm_g[j],
                        fox_knorm_g[j], odd_w_out[j])
    return x
```

```python
import functools

import jax
import jax.numpy as jnp
from jax import lax
from jax.experimental import pallas as pl
from jax.experimental.pallas import tpu as pltpu

F32 = jnp.float32
BF16 = jnp.bfloat16
EPS = 1e-6

HGRN_HEADS = 4
HGRN_KEY = 128
HGRN_VAL = 128
HGRN_KW = HGRN_HEADS * HGRN_KEY
HGRN_VW = HGRN_HEADS * HGRN_VAL
POOL_WINDOWS = (2, 4, 8, 16)
POOL_GROUP = 128
POOL_WIDTH = POOL_GROUP * len(POOL_WINDOWS)
POOL_HISTORY = 16
FOX_HEADS = 16
FOX_HEAD_DIM = 64
FOX_WIDTH = FOX_HEADS * FOX_HEAD_DIM

LANES = 128
SUBLANES = 8
VMEM_LIMIT_BYTES = 56 * 1024 * 1024

SEQ_TILE = 256
HGRN_CHUNK = 128
ATTN_Q_TILE = 512
ATTN_HEADS = 4
FOX_SAFE_LOGIT = 100.0
LOG2E = 1.4426950408889634
MASK_VALUE = -1e30
AUG_LANE = FOX_HEAD_DIM


def _silu(x):
    return x * jax.nn.sigmoid(x)


def _params(*semantics):
    return pltpu.CompilerParams(dimension_semantics=semantics, vmem_limit_bytes=VMEM_LIMIT_BYTES)


def _mod_kernel(c_ref, w_ref, b_ref, o_ref):
    c = c_ref[...]
    cond = _silu(c)
    o_ref[0] = jnp.dot(cond, w_ref[0], preferred_element_type=F32,
                       precision=lax.Precision.HIGHEST) + b_ref[0]


def _adaln_mods(c, ada_w, ada_b):
    depth, d, n = ada_w.shape
    b = c.shape[0]
    rows = SUBLANES
    c_rows = jnp.zeros((rows, d), F32).at[:b].set(c)
    tn = 1024
    out = pl.pallas_call(
        _mod_kernel,
        grid=(depth, n // tn),
        in_specs=[pl.BlockSpec((rows, d), lambda l, j: (0, 0)),
                  pl.BlockSpec((1, d, tn), lambda l, j: (l, 0, j)),
                  pl.BlockSpec((1, 1, tn), lambda l, j: (l, 0, j))],
        out_specs=pl.BlockSpec((1, rows, tn), lambda l, j: (l, 0, j)),
        out_shape=jax.ShapeDtypeStruct((depth, rows, n), F32),
        compiler_params=_params("arbitrary", "arbitrary"),
        name="adaln_mods",
    )(c_rows, ada_w, ada_b.reshape(depth, 1, n))
    return out[:, :b].reshape(depth, b, 3, d)


def _modulated_norm(x, mod_ref, ng_ref):
    shift = mod_ref[0, 0:1, :]
    scale = mod_ref[0, 1:2, :]
    ms = jnp.mean(x * x, axis=-1, keepdims=True)
    return x * lax.rsqrt(ms + EPS) * ng_ref[...] * (1.0 + scale) + shift


def _layer0_kernel(x_ref, mod_ref, ng_ref, win_ref, lb_ref, og_ref, pw_ref, ps_ref, wout_ref,
                   o_ref, state_ref, carry_ref, *, tile, chunk, layer_slot):
    T, C, H = tile, chunk, HGRN_HEADS
    t_idx = pl.program_id(1)

    @pl.when(t_idx == 0)
    def _():
        state_ref[...] = jnp.zeros_like(state_ref)
        carry_ref[...] = jnp.zeros_like(carry_ref)

    x = x_ref[0]
    h = _modulated_norm(x, mod_ref, ng_ref)
    proj = jnp.dot(h.astype(BF16), win_ref[...], preferred_element_type=F32)
    o0 = 0
    q = proj[:, o0:o0 + HGRN_KW]; o0 += HGRN_KW
    f = proj[:, o0:o0 + HGRN_KW]; o0 += HGRN_KW
    val = proj[:, o0:o0 + HGRN_VW]; o0 += HGRN_VW
    g_a = proj[:, o0:o0 + HGRN_VW]; o0 += HGRN_VW
    u = proj[:, o0:o0 + POOL_WIDTH]; o0 += POOL_WIDTH
    g_b = proj[:, o0:o0 + POOL_WIDTH]

    lbv = lb_ref[...]
    e = jnp.exp(lbv - jnp.max(lbv, axis=0, keepdims=True))
    lower = (jnp.sum(e[0:layer_slot + 1], axis=0, keepdims=True)
             / jnp.sum(e, axis=0, keepdims=True))
    forget = lower + (1.0 - lower) * jax.nn.sigmoid(f)
    logf = jnp.log(forget)
    key = 1.0 - forget

    row = lax.broadcasted_iota(jnp.int32, (T, HGRN_KW), 0)
    ti = lax.broadcasted_iota(jnp.int32, (C, C), 0)
    si = lax.broadcasted_iota(jnp.int32, (C, C), 1)
    n_chunks = T // C
    nt_dims = (((1,), (1,)), ((), ()))

    def level_scores(qe, ke, mask, scores):
        qb = qe.astype(BF16)
        kb = ke.astype(BF16)
        out = []
        for c in range(n_chunks):
            for hh in range(H):
                rs = slice(c * C, (c + 1) * C)
                cs = slice(hh * HGRN_KEY, (hh + 1) * HGRN_KEY)
                d = lax.dot_general(qb[rs, cs], kb[rs, cs], nt_dims, preferred_element_type=F32)
                prev = scores[c * H + hh]
                out.append(jnp.where(mask, d, 0.0 if prev is None else prev))
        return out

    scores = level_scores(q, key, ti == si, [None] * (n_chunks * H))
    p_sum = logf
    q_sum = jnp.zeros_like(logf)
    total = logf
    m = 1
    while m < C:
        mask = ((ti ^ si) < 2 * m) & ((ti & m) != 0) & ((si & m) == 0)
        scores = level_scores(q * jnp.exp(p_sum), key * jnp.exp(q_sum), mask, scores)
        upper = (row & m) != 0
        t_dn = pltpu.roll(total, m, 0)
        t_up = pltpu.roll(total, T - m, 0)
        p_sum = p_sum + jnp.where(upper, t_dn, 0.0)
        q_sum = q_sum + jnp.where(upper, 0.0, t_up)
        total = total + jnp.where(upper, t_dn, t_up)
        m *= 2

    q_in = (q * jnp.exp(p_sum)).astype(BF16)
    k_out = (key * jnp.exp(q_sum)).astype(BF16)
    decay = jnp.exp(total)
    val_b = val.astype(BF16)
    tn_dims = (((0,), (0,)), ((), ()))
    oa_rows = []
    for c in range(n_chunks):
        rs = slice(c * C, (c + 1) * C)
        heads_out = []
        for hh in range(H):
            cs = slice(hh * HGRN_KEY, (hh + 1) * HGRN_KEY)
            st = state_ref[hh]
            vb = val_b[rs, cs]
            o = jnp.dot(scores[c * H + hh].astype(BF16), vb, preferred_element_type=F32)
            o = o + lax.dot_general(q_in[rs, cs], st.astype(BF16), nt_dims,
                                    preferred_element_type=F32)
            state_ref[hh] = (st * decay[c * C:c * C + 1, cs]
                             + lax.dot_general(vb, k_out[rs, cs], tn_dims,
                                               preferred_element_type=F32))
            ms_o = jnp.mean(o * o, axis=-1, keepdims=True)
            heads_out.append(o * lax.rsqrt(ms_o + EPS))
        oa_rows.append(jnp.concatenate(heads_out, axis=1))
    o_a = jnp.concatenate(oa_rows, axis=0) * og_ref[...] * _silu(g_a)

    ub = jnp.concatenate([carry_ref[...], u], axis=0)
    carry_ref[...] = u[T - POOL_HISTORY:T, :]
    wins = []
    acc = ub
    sh = 1
    while sh < max(POOL_WINDOWS):
        acc = acc + pltpu.roll(acc, sh, 0)
        sh *= 2
        wins.append(acc)
    pos = lax.broadcasted_iota(jnp.int32, (T, POOL_GROUP), 0) + t_idx * T + 1
    ob = []
    for gi, w in enumerate(POOL_WINDOWS):
        cs = slice(gi * POOL_GROUP, (gi + 1) * POOL_GROUP)
        win = wins[w.bit_length() - 2][POOL_HISTORY:, cs]
        cnt = jnp.minimum(pos, w).astype(F32)
        pooled = win / cnt - u[:, cs]
        ob.append(jnp.dot(pooled.astype(BF16), pw_ref[gi], preferred_element_type=F32))
    o_b = jnp.concatenate(ob, axis=1) * ps_ref[...] * _silu(g_b)

    mixed = jnp.concatenate([o_a, o_b], axis=1).astype(BF16)
    y = jnp.dot(mixed, wout_ref[...], preferred_element_type=F32)
    o_ref[0] = x + mod_ref[0, 2:3, :] * y


def _layer0(x, mod, norm_g, w_in, hgrn_lb, onorm_g, pool_w, pool_scale, w_out, layer_slot):
    b, s, d = x.shape
    tile = SEQ_TILE
    n_in = w_in.shape[1]
    mix = w_out.shape[0]
    const2 = lambda bi, ti: (0, 0)
    return pl.pallas_call(
        functools.partial(_layer0_kernel, tile=tile, chunk=HGRN_CHUNK, layer_slot=layer_slot),
        grid=(b, s // tile),
        in_specs=[pl.BlockSpec((1, tile, d), lambda bi, ti: (bi, ti, 0)),
                  pl.BlockSpec((1, 3, d), lambda bi, ti: (bi, 0, 0)),
                  pl.BlockSpec((1, d), const2),
                  pl.BlockSpec((d, n_in), const2),
                  pl.BlockSpec(hgrn_lb.shape, const2),
                  pl.BlockSpec((1, HGRN_VW), const2),
                  pl.BlockSpec(pool_w.shape, lambda bi, ti: (0, 0, 0)),
                  pl.BlockSpec((1, POOL_WIDTH), const2),
                  pl.BlockSpec((mix, d), const2)],
        out_specs=pl.BlockSpec((1, tile, d), lambda bi, ti: (bi, ti, 0)),
        out_shape=jax.ShapeDtypeStruct((b, s, d), F32),
        scratch_shapes=[pltpu.VMEM((HGRN_HEADS, HGRN_VAL, HGRN_KEY), F32),
                        pltpu.VMEM((POOL_HISTORY, POOL_WIDTH), F32)],
        compiler_params=_params("arbitrary", "arbitrary"),
        name="layer0_hgrn_pool",
    )(x, mod, norm_g.reshape(1, d), w_in.astype(BF16), hgrn_lb, onorm_g.reshape(1, HGRN_VW),
      pool_w.astype(BF16), pool_scale.reshape(1, POOL_WIDTH), w_out.astype(BF16))


def _split3(c):
    hi = c.astype(BF16).astype(F32)
    r = c - hi
    mid = r.astype(BF16).astype(F32)
    lo = (r - mid).astype(BF16).astype(F32)
    return hi, mid, lo


def _layer1_proj_kernel(x_ref, mod_ref, ng_ref, win_ref, bf_ref, qg_ref, kg_ref, bd_ref,
                        qa_ref, ka_ref, vt_ref, g_ref, cq_ref, edge_ref, carry_ref, tile_start_ref,
                        *, tile):
    T, W = tile, FOX_WIDTH
    t_idx = pl.program_id(1)

    @pl.when(t_idx == 0)
    def _():
        carry_ref[...] = jnp.zeros_like(carry_ref)

    x = x_ref[0]
    h = _modulated_norm(x, mod_ref, ng_ref)
    proj = jnp.dot(h.astype(BF16), win_ref[...], preferred_element_type=F32)
    q = proj[:, 0:W]
    k = proj[:, W:2 * W]
    v = proj[:, 2 * W:3 * W]
    g_ref[0] = proj[:, 3 * W:4 * W]
    fl = proj[:, 4 * W:4 * W + LANES]
    vt_ref[0, 0] = v.T.astype(BF16)

    z = fl + bf_ref[...]
    logf = jnp.minimum(z, 0.0) - jnp.log1p(jnp.exp(-jnp.abs(z)))
    row = lax.broadcasted_iota(jnp.int32, (T, LANES), 0)
    cum = logf
    sh = 1
    while sh < T:
        cum = cum + jnp.where(row >= sh, pltpu.roll(cum, sh, 0), 0.0)
        sh *= 2
    start = carry_ref[...]
    end = start + cum[T - 1:T, :]
    carry_ref[...] = end

    @pl.when(t_idx % (ATTN_Q_TILE // T) == 0)
    def _():
        tile_start_ref[...] = start

    tile_start = tile_start_ref[...]
    edge_ref[0, 0] = jnp.concatenate([tile_start, end], axis=0) * LOG2E
    s_hi, s_mid, s_lo = _split3((cum + (start - tile_start)) * LOG2E)
    e_hi, e_mid, e_lo = _split3((cum - cum[T - 1:T, :]) * LOG2E)
    total_t = ((cum + start) * LOG2E).T
    for hd in range(FOX_HEADS):
        cq_ref[0, hd] = total_t[hd:hd + 1, :]

    lane = lax.broadcasted_iota(jnp.int32, (T, LANES), 1)
    ones_q = (lane >= AUG_LANE + 3) & (lane < AUG_LANE + 9)
    ones_k = (lane >= AUG_LANE) & (lane < AUG_LANE + 3)

    def spread(a, hd):
        return jnp.broadcast_to(a[:, hd:hd + 1], (T, LANES))
    bd = bd_ref[...]
    qk_scale = FOX_HEAD_DIM ** -0.5 * LOG2E
    for pair in range(FOX_HEADS // 2):
        cs = slice(pair * LANES, (pair + 1) * LANES)
        qp = q[:, cs]
        kp = k[:, cs]
        q_ms = jnp.dot((qp * qp).astype(BF16), bd, preferred_element_type=F32)
        k_ms = jnp.dot((kp * kp).astype(BF16), bd, preferred_element_type=F32)
        qn = qp * lax.rsqrt(q_ms + EPS) * (qg_ref[...] * qk_scale)
        kn = kp * lax.rsqrt(k_ms + EPS) * kg_ref[...]
        for e in range(LANES // FOX_HEAD_DIM):
            hd = (LANES // FOX_HEAD_DIM) * pair + e
            hi, mid, lo = spread(s_hi, hd), spread(s_mid, hd), spread(s_lo, hd)
            aug_q = jnp.where(lane == AUG_LANE, hi,
                              jnp.where(lane == AUG_LANE + 1, mid,
                                        jnp.where(lane == AUG_LANE + 2, lo,
                                                  jnp.where(ones_q, 1.0, 0.0))))
            aug_k = jnp.where(lane == AUG_LANE + 3, -hi,
                              jnp.where(lane == AUG_LANE + 4, -mid,
                                        jnp.where(lane == AUG_LANE + 5, -lo,
                                                  jnp.where(ones_k, 1.0, 0.0))))
            aug_k = jnp.where(lane == AUG_LANE + 6, -spread(e_hi, hd),
                              jnp.where(lane == AUG_LANE + 7, -spread(e_mid, hd),
                                        jnp.where(lane == AUG_LANE + 8, -spread(e_lo, hd), aug_k)))
            q_main = qn if e == 0 else pltpu.roll(qn, FOX_HEAD_DIM, 1)
            k_main = kn if e == 0 else pltpu.roll(kn, FOX_HEAD_DIM, 1)
            qa_ref[0, hd] = jnp.where(lane < FOX_HEAD_DIM, q_main, aug_q).astype(BF16)
            ka_ref[0, hd] = jnp.where(lane < FOX_HEAD_DIM, k_main, aug_k).astype(BF16)


def _layer1_proj(x, mod, norm_g, w_in, b_f, qnorm_g, knorm_g):
    b, s, d = x.shape
    tile = SEQ_TILE
    nt = s // tile
    w = FOX_WIDTH
    n_pad = 4 * w + LANES
    w_pad = jnp.zeros((d, n_pad), BF16).at[:, :w_in.shape[1]].set(w_in.astype(BF16))
    bf_pad = jnp.zeros((1, LANES), F32).at[0, :FOX_HEADS].set(b_f)
    blk = jnp.arange(LANES) // FOX_HEAD_DIM
    bd = jnp.where(blk[:, None] == blk[None, :], 1.0 / FOX_HEAD_DIM, 0.0).astype(BF16)
    qg = jnp.tile(qnorm_g, LANES // FOX_HEAD_DIM).reshape(1, LANES)
    kg = jnp.tile(knorm_g, LANES // FOX_HEAD_DIM).reshape(1, LANES)
    const2 = lambda bi, ti: (0, 0)
    return pl.pallas_call(
        functools.partial(_layer1_proj_kernel, tile=tile),
        grid=(b, nt),
        in_specs=[pl.BlockSpec((1, tile, d), lambda bi, ti: (bi, ti, 0)),
                  pl.BlockSpec((1, 3, d), lambda bi, ti: (bi, 0, 0)),
                  pl.BlockSpec((1, d), const2),
                  pl.BlockSpec((d, n_pad), const2),
                  pl.BlockSpec((1, LANES), const2),
                  pl.BlockSpec((1, LANES), const2),
                  pl.BlockSpec((1, LANES), const2),
                  pl.BlockSpec((LANES, LANES), const2)],
        out_specs=[pl.BlockSpec((1, FOX_HEADS, tile, LANES), lambda bi, ti: (bi, 0, ti, 0)),
                   pl.BlockSpec((1, FOX_HEADS, tile, LANES), lambda bi, ti: (bi, 0, ti, 0)),
                   pl.BlockSpec((1, 1, w, tile), lambda bi, ti: (bi, ti, 0, 0)),
                   pl.BlockSpec((1, tile, w), lambda bi, ti: (bi, ti, 0)),
                   pl.BlockSpec((1, FOX_HEADS, 1, tile), lambda bi, ti: (bi, 0, 0, ti)),
                   pl.BlockSpec((1, 1, 2, LANES), lambda bi, ti: (bi, ti, 0, 0))],
        out_shape=[jax.ShapeDtypeStruct((b, FOX_HEADS, s, LANES), BF16),
                   jax.ShapeDtypeStruct((b, FOX_HEADS, s, LANES), BF16),
                   jax.ShapeDtypeStruct((b, nt, w, tile), BF16),
                   jax.ShapeDtypeStruct((b, s, w), F32),
                   jax.ShapeDtypeStruct((b, FOX_HEADS, 1, s), F32),
                   jax.ShapeDtypeStruct((b, nt, 2, LANES), F32)],
        scratch_shapes=[pltpu.VMEM((1, LANES), F32), pltpu.VMEM((1, LANES), F32)],
        compiler_params=_params("arbitrary", "arbitrary"),
        name="layer1_proj",
    )(x, mod, norm_g.reshape(1, d), w_pad, bf_pad, qg, kg, bd)


def _transposed_queries(qa_ref, heads):
    slab = 2 * SUBLANES
    row = lax.broadcasted_iota(jnp.int32, (slab, qa_ref.shape[2]), 0)
    near, far = [], []
    for e in range(heads):
        q_t = qa_ref[0, e].astype(F32).T
        carriers = q_t[AUG_LANE:AUG_LANE + slab]
        for keep, out in ((row < 6, near), (row >= 6, far)):
            out.append(jnp.concatenate([q_t[:AUG_LANE], jnp.where(keep, carriers, 0.0),
                                        q_t[AUG_LANE + slab:]], axis=0).astype(BF16))
    return near, far


def _gated_output(acc_t, g_ref, o_ref):
    o_ref[0] = (acc_t.T * _silu(g_ref[0])).astype(BF16)


def _fox_bounded_kernel(end_ref, qa_ref, ka_ref, vt_ref, cq_ref, g_ref, o_ref, p_ref, acc_ref, *,
                        tile, heads):
    TQ, HP = tile, heads
    SB = vt_ref.shape[-1]
    R = TQ // SB
    n_blocks = ka_ref.shape[2] // SB
    b = pl.program_id(0)
    group = pl.program_id(1)
    i = pl.program_id(2)
    q_near, q_far = _transposed_queries(qa_ref, HP)
    causal = (lax.broadcasted_iota(jnp.int32, (TQ, TQ), 0)
              <= lax.broadcasted_iota(jnp.int32, (TQ, TQ), 1))

    def keys(e, n):
        return ka_ref[0, e, pl.ds(pl.multiple_of(n * TQ, TQ), TQ), :]

    def values_t(e, blk):
        return vt_ref[0, blk, pl.ds(e * FOX_HEAD_DIM, FOX_HEAD_DIM), :]

    def rest(e, blk, gate=1.0):
        end = end_ref[(b * n_blocks + blk) * FOX_HEADS + HP * group + e]
        return jnp.exp2(jnp.minimum(cq_ref[0, e] - end, 0.0)) * gate

    def col_partial(p):
        return jnp.sum(p.reshape(p.shape[0] // SUBLANES, SUBLANES, p.shape[1]), axis=0)

    def contract(e, n, p_bf16, gate=1.0):
        for r in range(R):
            pv = jnp.dot(values_t(e, n * R + r), p_bf16[r * SB:(r + 1) * SB],
                         preferred_element_type=F32)
            acc_ref[e] += rest(e, n * R + r, gate) * pv

    def weighted_partial(e, n, p, gate=1.0):
        return sum(rest(e, n * R + r, gate) * col_partial(p[r * SB:(r + 1) * SB])
                   for r in range(R))

    l_part = []
    for e in range(HP):
        s_t = jnp.dot(keys(e, i), q_near[e], preferred_element_type=F32)
        p = jnp.where(causal, jnp.exp2(s_t), 0.0)
        l_part.append(col_partial(p))
        v_t = jnp.concatenate([values_t(e, i * R + r) for r in range(R)], axis=1)
        acc_ref[e] = jnp.dot(v_t, p.astype(BF16), preferred_element_type=F32)

    def scores(e, n):
        return jnp.dot(keys(e, n), q_far[e], preferred_element_type=F32)

    gate0 = jnp.where(i > 0, 1.0, 0.0)
    for e in range(HP):
        p = jnp.exp2(scores(e, 0))
        l_part[e] = l_part[e] + weighted_partial(e, 0, p, gate0)
        p_ref[e] = p.astype(BF16)

    def body(n, l_run):
        s_new = [scores(e, n) for e in range(HP)]
        for e in range(HP):
            contract(e, n - 1, p_ref[e])
        out = []
        for e in range(HP):
            p = jnp.exp2(s_new[e])
            out.append(l_run[e] + weighted_partial(e, n, p))
            p_ref[e] = p.astype(BF16)
        return tuple(out)

    l_part = lax.fori_loop(1, i, body, tuple(l_part))
    for e in range(HP):
        contract(e, jnp.maximum(i - 1, 0), p_ref[e], gate0)
    out_t = [acc_ref[e] * (1.0 / jnp.sum(l_part[e], axis=0, keepdims=True)) for e in range(HP)]
    _gated_output(jnp.concatenate(out_t, axis=0), g_ref, o_ref)


def _fox_online_kernel(start_ref, qa_ref, ka_ref, vt_ref, g_ref, o_ref, *, tile, heads):
    TB, HP = tile, heads
    n_blocks = ka_ref.shape[2] // TB
    b = pl.program_id(0)
    group = pl.program_id(1)
    i = pl.program_id(2)
    q_t, _ = _transposed_queries(qa_ref, HP)
    k_row = lax.broadcasted_iota(jnp.int32, (TB, TB), 0)
    q_col = lax.broadcasted_iota(jnp.int32, (TB, TB), 1)

    def step(j, carry, masked):
        out = []
        for e in range(HP):
            m_run, l_run, acc = carry[e]
            k_blk = ka_ref[0, e, pl.ds(pl.multiple_of(j * TB, TB), TB), :]
            s_t = jnp.dot(k_blk, q_t[e], preferred_element_type=F32)
            if masked:
                s_t = jnp.where(k_row <= q_col, s_t, MASK_VALUE)
            hd = HP * group + e
            delta = (start_ref[(b * n_blocks + i) * FOX_HEADS + hd]
                     - start_ref[(b * n_blocks + j) * FOX_HEADS + hd])
            m_new = jnp.maximum(m_run, jnp.max(s_t, axis=0, keepdims=True) + delta)
            p_t = jnp.exp2(s_t - (m_new - delta))
            alpha = jnp.exp2(m_run - m_new)
            l_new = alpha * l_run + jnp.sum(p_t, axis=0, keepdims=True)
            v_t = vt_ref[0, j, pl.ds(e * FOX_HEAD_DIM, FOX_HEAD_DIM), :]
            acc_new = alpha * acc + jnp.dot(v_t, p_t.astype(BF16), preferred_element_type=F32)
            out.append((m_new, l_new, acc_new))
        return tuple(out)

    init = (jnp.full((1, TB), MASK_VALUE, F32), jnp.zeros((1, TB), F32),
            jnp.zeros((FOX_HEAD_DIM, TB), F32))
    carry = lax.fori_loop(0, i, lambda j, c: step(j, c, False), (init,) * HP)
    carry = step(i, carry, True)
    _gated_output(jnp.concatenate([acc / l_run for _, l_run, acc in carry], axis=0), g_ref, o_ref)


def _fox_attention(qa, ka, vt, g, cq, edges, *, bounded):
    b, n_heads, s, _ = qa.shape
    _, nvb, w, vb = vt.shape
    if bounded:
        tile, hp = ATTN_Q_TILE, ATTN_HEADS
    else:
        tile, hp = vb, LANES // FOX_HEAD_DIM
    gw = hp * FOX_HEAD_DIM
    specs = dict(
        table=pl.BlockSpec(memory_space=pltpu.SMEM),
        q=pl.BlockSpec((1, hp, tile, LANES), lambda bi, p, i: (bi, p, i, 0)),
        k=pl.BlockSpec((1, hp, s, LANES), lambda bi, p, i: (bi, p, 0, 0)),
        v=pl.BlockSpec((1, nvb, gw, vb), lambda bi, p, i: (bi, 0, p, 0)),
        cq=pl.BlockSpec((1, hp, 1, tile), lambda bi, p, i: (bi, p, 0, i)),
        g=pl.BlockSpec((1, tile, gw), lambda bi, p, i: (bi, i, p)))
    if bounded:
        body = functools.partial(_fox_bounded_kernel, tile=tile, heads=hp)
        names = ("table", "q", "k", "v", "cq", "g")
        args = (edges[:, :, 1, :FOX_HEADS].reshape(-1), qa, ka, vt, cq, g)
        scratch = [pltpu.VMEM((hp, tile, tile), BF16),
                   pltpu.VMEM((hp, FOX_HEAD_DIM, tile), F32)]
    else:
        body = functools.partial(_fox_online_kernel, tile=tile, heads=hp)
        names = ("table", "q", "k", "v", "g")
        args = (edges[:, :, 0, :FOX_HEADS].reshape(-1), qa, ka, vt, g)
        scratch = []
    return pl.pallas_call(
        body,
        grid=(b, n_heads // hp, s // tile),
        in_specs=[specs[n] for n in names],
        out_specs=specs["g"],
        out_shape=jax.ShapeDtypeStruct((b, s, w), BF16),
        scratch_shapes=scratch,
        compiler_params=_params("arbitrary", "arbitrary", "arbitrary"),
        name="fox_attention_bounded" if bounded else "fox_attention_online",
    )(*args)


def _out_proj_kernel(a_ref, x_ref, mod_ref, w_ref, o_ref):
    y = jnp.dot(a_ref[0], w_ref[...], preferred_element_type=F32)
    o_ref[0] = x_ref[0] + mod_ref[0, 2:3, :] * y


def _out_proj(a, x, mod, w_out):
    b, s, d = x.shape
    tile = 512
    k = a.shape[-1]
    return pl.pallas_call(
        _out_proj_kernel,
        grid=(b, s // tile),
        in_specs=[pl.BlockSpec((1, tile, k), lambda bi, ti: (bi, ti, 0)),
                  pl.BlockSpec((1, tile, d), lambda bi, ti: (bi, ti, 0)),
                  pl.BlockSpec((1, 3, d), lambda bi, ti: (bi, 0, 0)),
                  pl.BlockSpec((k, d), lambda bi, ti: (0, 0))],
        out_specs=pl.BlockSpec((1, tile, d), lambda bi, ti: (bi, ti, 0)),
        out_shape=jax.ShapeDtypeStruct((b, s, d), F32),
        compiler_params=_params("arbitrary", "arbitrary"),
        name="layer1_out_proj",
    )(a, x, mod, w_out.astype(BF16))


def _layer1(x, mod, norm_g, w_in, b_f, qnorm_g, knorm_g, w_out):
    qa, ka, vt, g, cq, edges = _layer1_proj(x, mod, norm_g, w_in, b_f, qnorm_g, knorm_g)
    score_bound = (LOG2E * FOX_HEAD_DIM ** 0.5
                   * jnp.max(jnp.abs(qnorm_g)) * jnp.max(jnp.abs(knorm_g)))
    gated = lax.cond(score_bound <= FOX_SAFE_LOGIT,
                     functools.partial(_fox_attention, bounded=True),
                     functools.partial(_fox_attention, bounded=False),
                     qa, ka, vt, g, cq, edges)
    return _out_proj(gated, x, mod, w_out)


def kernel(x, c, norm_g, ada_w, ada_b, hgrn_lb, even_w_in, hgrn_onorm_g, pool_w, pool_scale,
           even_w_out, odd_w_in, fox_b_f, fox_qnorm_g, fox_knorm_g, odd_w_out):
    depth = norm_g.shape[0]
    mods = _adaln_mods(c, ada_w, ada_b)
    for l in range(depth):
        j = l // 2
        if l % 2 == 0:
            x = _layer0(x, mods[l], norm_g[l], even_w_in[j], hgrn_lb, hgrn_onorm_g[j],
                        pool_w[j], pool_scale[j], even_w_out[j], layer_slot=l)
        else:
            x = _layer1(x, mods[l], norm_g[l], odd_w_in[j], fox_b_f[j], fox_qnorm_g[j],
                        fox_knorm_g[j], odd_w_out[j])
    return x
```

```python
import functools

import jax
import jax.numpy as jnp
from jax import lax
from jax.experimental import pallas as pl
from jax.experimental.pallas import tpu as pltpu

F32 = jnp.float32
BF16 = jnp.bfloat16
EPS = 1e-6

HGRN_HEADS = 4
HGRN_KEY = 128
HGRN_VAL = 128
HGRN_KW = HGRN_HEADS * HGRN_KEY
HGRN_VW = HGRN_HEADS * HGRN_VAL
POOL_WINDOWS = (2, 4, 8, 16)
POOL_GROUP = 128
POOL_WIDTH = POOL_GROUP * len(POOL_WINDOWS)
POOL_HISTORY = 16
FOX_HEADS = 16
FOX_HEAD_DIM = 64
FOX_WIDTH = FOX_HEADS * FOX_HEAD_DIM

LANES = 128
SUBLANES = 8
VMEM_LIMIT_BYTES = 56 * 1024 * 1024

SEQ_TILE = 256
HGRN_CHUNK = 128
ATTN_Q_TILE = 512
ATTN_HEADS = 4
ATTN_TILES_PER_STEP = 2
FOX_SAFE_LOGIT = 100.0
LOG2E = 1.4426950408889634
MASK_VALUE = -1e30
AUG_LANE = FOX_HEAD_DIM


def _silu(x):
    return x * jax.nn.sigmoid(x)


def _params(*semantics):
    return pltpu.CompilerParams(dimension_semantics=semantics, vmem_limit_bytes=VMEM_LIMIT_BYTES)


def _mod_kernel(c_ref, w_ref, b_ref, o_ref):
    c = c_ref[...]
    cond = _silu(c)
    o_ref[0] = jnp.dot(cond, w_ref[0], preferred_element_type=F32,
                       precision=lax.Precision.HIGHEST) + b_ref[0]


def _adaln_mods(c, ada_w, ada_b):
    depth, d, n = ada_w.shape
    b = c.shape[0]
    rows = SUBLANES
    c_rows = jnp.zeros((rows, d), F32).at[:b].set(c)
    tn = 1024
    out = pl.pallas_call(
        _mod_kernel,
        grid=(depth, n // tn),
        in_specs=[pl.BlockSpec((rows, d), lambda l, j: (0, 0)),
                  pl.BlockSpec((1, d, tn), lambda l, j: (l, 0, j)),
                  pl.BlockSpec((1, 1, tn), lambda l, j: (l, 0, j))],
        out_specs=pl.BlockSpec((1, rows, tn), lambda l, j: (l, 0, j)),
        out_shape=jax.ShapeDtypeStruct((depth, rows, n), F32),
        compiler_params=_params("arbitrary", "arbitrary"),
        name="adaln_mods",
    )(c_rows, ada_w, ada_b.reshape(depth, 1, n))
    return out[:, :b].reshape(depth, b, 3, d)


def _modulated_norm(x, mod_ref, ng_ref):
    shift = mod_ref[0, 0:1, :]
    scale = mod_ref[0, 1:2, :]
    ms = jnp.mean(x * x, axis=-1, keepdims=True)
    return x * lax.rsqrt(ms + EPS) * ng_ref[...] * (1.0 + scale) + shift


def _layer0_kernel(x_ref, mod_ref, ng_ref, win_ref, lb_ref, og_ref, pw_ref, ps_ref, wout_ref,
                   o_ref, state_ref, carry_ref, *, tile, chunk, layer_slot):
    T, C, H = tile, chunk, HGRN_HEADS
    t_idx = pl.program_id(1)

    @pl.when(t_idx == 0)
    def _():
        state_ref[...] = jnp.zeros_like(state_ref)
        carry_ref[...] = jnp.zeros_like(carry_ref)

    x = x_ref[0]
    h = _modulated_norm(x, mod_ref, ng_ref)
    proj = jnp.dot(h.astype(BF16), win_ref[...], preferred_element_type=F32)
    o0 = 0
    q = proj[:, o0:o0 + HGRN_KW]; o0 += HGRN_KW
    f = proj[:, o0:o0 + HGRN_KW]; o0 += HGRN_KW
    val = proj[:, o0:o0 + HGRN_VW]; o0 += HGRN_VW
    g_a = proj[:, o0:o0 + HGRN_VW]; o0 += HGRN_VW
    u = proj[:, o0:o0 + POOL_WIDTH]; o0 += POOL_WIDTH
    g_b = proj[:, o0:o0 + POOL_WIDTH]

    lbv = lb_ref[...]
    e = jnp.exp(lbv - jnp.max(lbv, axis=0, keepdims=True))
    lower = (jnp.sum(e[0:layer_slot + 1], axis=0, keepdims=True)
             / jnp.sum(e, axis=0, keepdims=True))
    forget = lower + (1.0 - lower) * jax.nn.sigmoid(f)
    logf = jnp.log(forget)
    key = 1.0 - forget

    row = lax.broadcasted_iota(jnp.int32, (T, HGRN_KW), 0)
    ti = lax.broadcasted_iota(jnp.int32, (C, C), 0)
    si = lax.broadcasted_iota(jnp.int32, (C, C), 1)
    n_chunks = T // C
    nt_dims = (((1,), (1,)), ((), ()))

    def level_scores(qe, ke, mask, scores):
        qb = qe.astype(BF16)
        kb = ke.astype(BF16)
        out = []
        for c in range(n_chunks):
            for hh in range(H):
                rs = slice(c * C, (c + 1) * C)
                cs = slice(hh * HGRN_KEY, (hh + 1) * HGRN_KEY)
                d = lax.dot_general(qb[rs, cs], kb[rs, cs], nt_dims, preferred_element_type=F32)
                prev = scores[c * H + hh]
                out.append(jnp.where(mask, d, 0.0 if prev is None else prev))
        return out

    scores = level_scores(q, key, ti == si, [None] * (n_chunks * H))
    p_sum = logf
    q_sum = jnp.zeros_like(logf)
    total = logf
    m = 1
    while m < C:
        mask = ((ti ^ si) < 2 * m) & ((ti & m) != 0) & ((si & m) == 0)
        scores = level_scores(q * jnp.exp(p_sum), key * jnp.exp(q_sum), mask, scores)
        upper = (row & m) != 0
        t_dn = pltpu.roll(total, m, 0)
        t_up = pltpu.roll(total, T - m, 0)
        p_sum = p_sum + jnp.where(upper, t_dn, 0.0)
        q_sum = q_sum + jnp.where(upper, 0.0, t_up)
        total = total + jnp.where(upper, t_dn, t_up)
        m *= 2

    q_in = (q * jnp.exp(p_sum)).astype(BF16)
    k_out = (key * jnp.exp(q_sum)).astype(BF16)
    decay = jnp.exp(total)
    val_b = val.astype(BF16)
    tn_dims = (((0,), (0,)), ((), ()))
    oa_rows = []
    for c in range(n_chunks):
        rs = slice(c * C, (c + 1) * C)
        heads_out = []
        for hh in range(H):
            cs = slice(hh * HGRN_KEY, (hh + 1) * HGRN_KEY)
            st = state_ref[hh]
            vb = val_b[rs, cs]
            o = jnp.dot(scores[c * H + hh].astype(BF16), vb, preferred_element_type=F32)
            o = o + lax.dot_general(q_in[rs, cs], st.astype(BF16), nt_dims,
                                    preferred_element_type=F32)
            state_ref[hh] = (st * decay[c * C:c * C + 1, cs]
                             + lax.dot_general(vb, k_out[rs, cs], tn_dims,
                                               preferred_element_type=F32))
            ms_o = jnp.mean(o * o, axis=-1, keepdims=True)
            heads_out.append(o * lax.rsqrt(ms_o + EPS))
        oa_rows.append(jnp.concatenate(heads_out, axis=1))
    o_a = jnp.concatenate(oa_rows, axis=0) * og_ref[...] * _silu(g_a)

    ub = jnp.concatenate([carry_ref[...], u], axis=0)
    carry_ref[...] = u[T - POOL_HISTORY:T, :]
    wins = []
    acc = ub
    sh = 1
    while sh < max(POOL_WINDOWS):
        acc = acc + pltpu.roll(acc, sh, 0)
        sh *= 2
        wins.append(acc)
    pos = lax.broadcasted_iota(jnp.int32, (T, POOL_GROUP), 0) + t_idx * T + 1
    ob = []
    for gi, w in enumerate(POOL_WINDOWS):
        cs = slice(gi * POOL_GROUP, (gi + 1) * POOL_GROUP)
        win = wins[w.bit_length() - 2][POOL_HISTORY:, cs]
        cnt = jnp.minimum(pos, w).astype(F32)
        pooled = win / cnt - u[:, cs]
        ob.append(jnp.dot(pooled.astype(BF16), pw_ref[gi], preferred_element_type=F32))
    o_b = jnp.concatenate(ob, axis=1) * ps_ref[...] * _silu(g_b)

    mixed = jnp.concatenate([o_a, o_b], axis=1).astype(BF16)
    y = jnp.dot(mixed, wout_ref[...], preferred_element_type=F32)
    o_ref[0] = x + mod_ref[0, 2:3, :] * y


def _layer0(x, mod, norm_g, w_in, hgrn_lb, onorm_g, pool_w, pool_scale, w_out, layer_slot):
    b, s, d = x.shape
    tile = SEQ_TILE
    n_in = w_in.shape[1]
    mix = w_out.shape[0]
    const2 = lambda bi, ti: (0, 0)
    return pl.pallas_call(
        functools.partial(_layer0_kernel, tile=tile, chunk=HGRN_CHUNK, layer_slot=layer_slot),
        grid=(b, s // tile),
        in_specs=[pl.BlockSpec((1, tile, d), lambda bi, ti: (bi, ti, 0)),
                  pl.BlockSpec((1, 3, d), lambda bi, ti: (bi, 0, 0)),
                  pl.BlockSpec((1, d), const2),
                  pl.BlockSpec((d, n_in), const2),
                  pl.BlockSpec(hgrn_lb.shape, const2),
                  pl.BlockSpec((1, HGRN_VW), const2),
                  pl.BlockSpec(pool_w.shape, lambda bi, ti: (0, 0, 0)),
                  pl.BlockSpec((1, POOL_WIDTH), const2),
                  pl.BlockSpec((mix, d), const2)],
        out_specs=pl.BlockSpec((1, tile, d), lambda bi, ti: (bi, ti, 0)),
        out_shape=jax.ShapeDtypeStruct((b, s, d), F32),
        scratch_shapes=[pltpu.VMEM((HGRN_HEADS, HGRN_VAL, HGRN_KEY), F32),
                        pltpu.VMEM((POOL_HISTORY, POOL_WIDTH), F32)],
        compiler_params=_params("arbitrary", "arbitrary"),
        name="layer0_hgrn_pool",
    )(x, mod, norm_g.reshape(1, d), w_in.astype(BF16), hgrn_lb, onorm_g.reshape(1, HGRN_VW),
      pool_w.astype(BF16), pool_scale.reshape(1, POOL_WIDTH), w_out.astype(BF16))


def _split3(c):
    hi = c.astype(BF16).astype(F32)
    r = c - hi
    mid = r.astype(BF16).astype(F32)
    lo = (r - mid).astype(BF16).astype(F32)
    return hi, mid, lo


def _layer1_proj_kernel(x_ref, mod_ref, ng_ref, win_ref, bf_ref, qg_ref, kg_ref, bd_ref,
                        qa_ref, ka_ref, vt_ref, g_ref, cq_ref, edge_ref, carry_ref, tile_start_ref,
                        *, tile):
    T, W = tile, FOX_WIDTH
    t_idx = pl.program_id(1)

    @pl.when(t_idx == 0)
    def _():
        carry_ref[...] = jnp.zeros_like(carry_ref)

    x = x_ref[0]
    h = _modulated_norm(x, mod_ref, ng_ref)
    proj = jnp.dot(h.astype(BF16), win_ref[...], preferred_element_type=F32)
    q = proj[:, 0:W]
    k = proj[:, W:2 * W]
    v = proj[:, 2 * W:3 * W]
    g_ref[0] = proj[:, 3 * W:4 * W]
    fl = proj[:, 4 * W:4 * W + LANES]
    vt_ref[0, 0] = v.T.astype(BF16)

    z = fl + bf_ref[...]
    logf = jnp.minimum(z, 0.0) - jnp.log1p(jnp.exp(-jnp.abs(z)))
    row = lax.broadcasted_iota(jnp.int32, (T, LANES), 0)
    cum = logf
    sh = 1
    while sh < T:
        cum = cum + jnp.where(row >= sh, pltpu.roll(cum, sh, 0), 0.0)
        sh *= 2
    start = carry_ref[...]
    end = start + cum[T - 1:T, :]
    carry_ref[...] = end

    @pl.when(t_idx % (ATTN_Q_TILE // T) == 0)
    def _():
        tile_start_ref[...] = start

    tile_start = tile_start_ref[...]
    edge_ref[0, 0] = jnp.concatenate([tile_start, end], axis=0) * LOG2E
    s_hi, s_mid, s_lo = _split3((cum + (start - tile_start)) * LOG2E)
    e_hi, e_mid, e_lo = _split3((cum - cum[T - 1:T, :]) * LOG2E)
    total_t = ((cum + start) * LOG2E).T
    for hd in range(FOX_HEADS):
        cq_ref[0, hd] = total_t[hd:hd + 1, :]

    lane = lax.broadcasted_iota(jnp.int32, (T, LANES), 1)
    ones_q = (lane >= AUG_LANE + 3) & (lane < AUG_LANE + 9)
    ones_k = (lane >= AUG_LANE) & (lane < AUG_LANE + 3)

    def spread(a, hd):
        return jnp.broadcast_to(a[:, hd:hd + 1], (T, LANES))
    bd = bd_ref[...]
    qk_scale = FOX_HEAD_DIM ** -0.5 * LOG2E
    for pair in range(FOX_HEADS // 2):
        cs = slice(pair * LANES, (pair + 1) * LANES)
        qp = q[:, cs]
        kp = k[:, cs]
        q_ms = jnp.dot((qp * qp).astype(BF16), bd, preferred_element_type=F32)
        k_ms = jnp.dot((kp * kp).astype(BF16), bd, preferred_element_type=F32)
        qn = qp * lax.rsqrt(q_ms + EPS) * (qg_ref[...] * qk_scale)
        kn = kp * lax.rsqrt(k_ms + EPS) * kg_ref[...]
        for e in range(LANES // FOX_HEAD_DIM):
            hd = (LANES // FOX_HEAD_DIM) * pair + e
            hi, mid, lo = spread(s_hi, hd), spread(s_mid, hd), spread(s_lo, hd)
            aug_q = jnp.where(lane == AUG_LANE, hi,
                              jnp.where(lane == AUG_LANE + 1, mid,
                                        jnp.where(lane == AUG_LANE + 2, lo,
                                                  jnp.where(ones_q, 1.0, 0.0))))
            aug_k = jnp.where(lane == AUG_LANE + 3, -hi,
                              jnp.where(lane == AUG_LANE + 4, -mid,
                                        jnp.where(lane == AUG_LANE + 5, -lo,
                                                  jnp.where(ones_k, 1.0, 0.0))))
            aug_k = jnp.where(lane == AUG_LANE + 6, -spread(e_hi, hd),
                              jnp.where(lane == AUG_LANE + 7, -spread(e_mid, hd),
                                        jnp.where(lane == AUG_LANE + 8, -spread(e_lo, hd), aug_k)))
            q_main = qn if e == 0 else pltpu.roll(qn, FOX_HEAD_DIM, 1)
            k_main = kn if e == 0 else pltpu.roll(kn, FOX_HEAD_DIM, 1)
            qa_ref[0, hd] = jnp.where(lane < FOX_HEAD_DIM, q_main, aug_q).astype(BF16)
            ka_ref[0, hd] = jnp.where(lane < FOX_HEAD_DIM, k_main, aug_k).astype(BF16)


def _layer1_proj(x, mod, norm_g, w_in, b_f, qnorm_g, knorm_g):
    b, s, d = x.shape
    tile = SEQ_TILE
    nt = s // tile
    w = FOX_WIDTH
    n_pad = 4 * w + LANES
    w_pad = jnp.zeros((d, n_pad), BF16).at[:, :w_in.shape[1]].set(w_in.astype(BF16))
    bf_pad = jnp.zeros((1, LANES), F32).at[0, :FOX_HEADS].set(b_f)
    blk = jnp.arange(LANES) // FOX_HEAD_DIM
    bd = jnp.where(blk[:, None] == blk[None, :], 1.0 / FOX_HEAD_DIM, 0.0).astype(BF16)
    qg = jnp.tile(qnorm_g, LANES // FOX_HEAD_DIM).reshape(1, LANES)
    kg = jnp.tile(knorm_g, LANES // FOX_HEAD_DIM).reshape(1, LANES)
    const2 = lambda bi, ti: (0, 0)
    return pl.pallas_call(
        functools.partial(_layer1_proj_kernel, tile=tile),
        grid=(b, nt),
        in_specs=[pl.BlockSpec((1, tile, d), lambda bi, ti: (bi, ti, 0)),
                  pl.BlockSpec((1, 3, d), lambda bi, ti: (bi, 0, 0)),
                  pl.BlockSpec((1, d), const2),
                  pl.BlockSpec((d, n_pad), const2),
                  pl.BlockSpec((1, LANES), const2),
                  pl.BlockSpec((1, LANES), const2),
                  pl.BlockSpec((1, LANES), const2),
                  pl.BlockSpec((LANES, LANES), const2)],
        out_specs=[pl.BlockSpec((1, FOX_HEADS, tile, LANES), lambda bi, ti: (bi, 0, ti, 0)),
                   pl.BlockSpec((1, FOX_HEADS, tile, LANES), lambda bi, ti: (bi, 0, ti, 0)),
                   pl.BlockSpec((1, 1, w, tile), lambda bi, ti: (bi, ti, 0, 0)),
                   pl.BlockSpec((1, tile, w), lambda bi, ti: (bi, ti, 0)),
                   pl.BlockSpec((1, FOX_HEADS, 1, tile), lambda bi, ti: (bi, 0, 0, ti)),
                   pl.BlockSpec((1, 1, 2, LANES), lambda bi, ti: (bi, ti, 0, 0))],
        out_shape=[jax.ShapeDtypeStruct((b, FOX_HEADS, s, LANES), BF16),
                   jax.ShapeDtypeStruct((b, FOX_HEADS, s, LANES), BF16),
                   jax.ShapeDtypeStruct((b, nt, w, tile), BF16),
                   jax.ShapeDtypeStruct((b, s, w), F32),
                   jax.ShapeDtypeStruct((b, FOX_HEADS, 1, s), F32),
                   jax.ShapeDtypeStruct((b, nt, 2, LANES), F32)],
        scratch_shapes=[pltpu.VMEM((1, LANES), F32), pltpu.VMEM((1, LANES), F32)],
        compiler_params=_params("arbitrary", "arbitrary"),
        name="layer1_proj",
    )(x, mod, norm_g.reshape(1, d), w_pad, bf_pad, qg, kg, bd)


def _query_operands(q):
    slab = 2 * SUBLANES
    row = lax.broadcasted_iota(jnp.int32, (slab, q.shape[0]), 0)
    q_t = q.astype(F32).T
    carriers = q_t[AUG_LANE:AUG_LANE + slab]
    return tuple(jnp.concatenate([q_t[:AUG_LANE], jnp.where(keep, carriers, 0.0),
                                  q_t[AUG_LANE + slab:]], axis=0).astype(BF16)
                 for keep in (row < 6, row >= 6))


def _gated_output(acc_t, g_ref, o_ref, rows):
    o_ref[0, rows] = (acc_t.T * _silu(g_ref[0, rows])).astype(BF16)


def _fox_bounded_kernel(end_ref, qa_ref, ka_ref, vt_ref, cq_ref, g_ref, o_ref, p_ref, acc_ref, *,
                        tile, heads):
    TQ, HP = tile, heads
    QT = qa_ref.shape[2] // TQ
    SB = vt_ref.shape[-1]
    R = TQ // SB
    n_blocks = ka_ref.shape[2] // SB
    b = pl.program_id(0)
    group = pl.program_id(1)
    i0 = pl.program_id(2) * QT
    chains = [(qs, e) for qs in range(QT) for e in range(HP)]
    cols = [slice(qs * TQ, (qs + 1) * TQ) for qs in range(QT)]
    operands = {(qs, e): _query_operands(qa_ref[0, e, cols[qs], :]) for qs, e in chains}
    causal = (lax.broadcasted_iota(jnp.int32, (TQ, TQ), 0)
              <= lax.broadcasted_iota(jnp.int32, (TQ, TQ), 1))

    def keys(e, n):
        return ka_ref[0, e, pl.ds(pl.multiple_of(n * TQ, TQ), TQ), :]

    def values_t(e, blk):
        return vt_ref[0, blk, pl.ds(e * FOX_HEAD_DIM, FOX_HEAD_DIM), :]

    def rest(c, blk, gate=1.0):
        qs, e = c
        end = end_ref[(b * n_blocks + blk) * FOX_HEADS + HP * group + e]
        return jnp.exp2(jnp.minimum(cq_ref[0, e][:, cols[qs]] - end, 0.0)) * gate

    def col_partial(p):
        return jnp.sum(p.reshape(p.shape[0] // SUBLANES, SUBLANES, p.shape[1]), axis=0)

    def far_scores(c, n):
        return jnp.dot(keys(c[1], n), operands[c][1], preferred_element_type=F32)

    def contract(c, n, p_bf16, gate=1.0):
        for r in range(R):
            pv = jnp.dot(values_t(c[1], n * R + r), p_bf16[r * SB:(r + 1) * SB],
                         preferred_element_type=F32)
            acc_ref[c] += rest(c, n * R + r, gate) * pv

    def weighted_partial(c, n, p, gate=1.0):
        return sum(rest(c, n * R + r, gate) * col_partial(p[r * SB:(r + 1) * SB])
                   for r in range(R))

    l_part = {}
    for c in chains:
        qs, e = c
        s_t = jnp.dot(keys(e, i0 + qs), operands[c][0], preferred_element_type=F32)
        p = jnp.where(causal, jnp.exp2(s_t), 0.0)
        l_part[c] = col_partial(p)
        v_t = jnp.concatenate([values_t(e, (i0 + qs) * R + r) for r in range(R)], axis=1)
        acc_ref[c] = jnp.dot(v_t, p.astype(BF16), preferred_element_type=F32)
    for c in chains:
        for t in range(c[0]):
            p = jnp.exp2(far_scores(c, i0 + t))
            l_part[c] = l_part[c] + weighted_partial(c, i0 + t, p)
            contract(c, i0 + t, p.astype(BF16))

    gate0 = jnp.where(i0 > 0, 1.0, 0.0)
    for c in chains:
        p = jnp.exp2(far_scores(c, 0))
        l_part[c] = l_part[c] + weighted_partial(c, 0, p, gate0)
        p_ref[c] = p.astype(BF16)

    def body(n, l_run):
        s_new = [far_scores(c, n) for c in chains]
        for c in chains:
            contract(c, n - 1, p_ref[c])
        out = []
        for c, s_t, l_c in zip(chains, s_new, l_run):
            p = jnp.exp2(s_t)
            out.append(l_c + weighted_partial(c, n, p))
            p_ref[c] = p.astype(BF16)
        return tuple(out)

    l_run = lax.fori_loop(1, i0, body, tuple(l_part[c] for c in chains))
    for c in chains:
        contract(c, jnp.maximum(i0 - 1, 0), p_ref[c], gate0)
    for qs in range(QT):
        out_t = [acc_ref[qs, e] * (1.0 / jnp.sum(l_run[qs * HP + e], axis=0, keepdims=True))
                 for e in range(HP)]
        _gated_output(jnp.concatenate(out_t, axis=0), g_ref, o_ref, cols[qs])


def _fox_online_kernel(start_ref, qa_ref, ka_ref, vt_ref, g_ref, o_ref, *, tile, heads):
    TB, HP = tile, heads
    n_blocks = ka_ref.shape[2] // TB
    b = pl.program_id(0)
    group = pl.program_id(1)
    i = pl.program_id(2)
    q_t = [_query_operands(qa_ref[0, e])[0] for e in range(HP)]
    k_row = lax.broadcasted_iota(jnp.int32, (TB, TB), 0)
    q_col = lax.broadcasted_iota(jnp.int32, (TB, TB), 1)

    def step(j, carry, masked):
        out = []
        for e in range(HP):
            m_run, l_run, acc = carry[e]
            k_blk = ka_ref[0, e, pl.ds(pl.multiple_of(j * TB, TB), TB), :]
            s_t = jnp.dot(k_blk, q_t[e], preferred_element_type=F32)
            if masked:
                s_t = jnp.where(k_row <= q_col, s_t, MASK_VALUE)
            hd = HP * group + e
            delta = (start_ref[(b * n_blocks + i) * FOX_HEADS + hd]
                     - start_ref[(b * n_blocks + j) * FOX_HEADS + hd])
            m_new = jnp.maximum(m_run, jnp.max(s_t, axis=0, keepdims=True) + delta)
            p_t = jnp.exp2(s_t - (m_new - delta))
            alpha = jnp.exp2(m_run - m_new)
            l_new = alpha * l_run + jnp.sum(p_t, axis=0, keepdims=True)
            v_t = vt_ref[0, j, pl.ds(e * FOX_HEAD_DIM, FOX_HEAD_DIM), :]
            acc_new = alpha * acc + jnp.dot(v_t, p_t.astype(BF16), preferred_element_type=F32)
            out.append((m_new, l_new, acc_new))
        return tuple(out)

    init = (jnp.full((1, TB), MASK_VALUE, F32), jnp.zeros((1, TB), F32),
            jnp.zeros((FOX_HEAD_DIM, TB), F32))
    carry = lax.fori_loop(0, i, lambda j, c: step(j, c, False), (init,) * HP)
    carry = step(i, carry, True)
    _gated_output(jnp.concatenate([acc / l_run for _, l_run, acc in carry], axis=0), g_ref, o_ref,
                  slice(None))


def _fox_attention(qa, ka, vt, g, cq, edges, *, bounded):
    b, n_heads, s, _ = qa.shape
    _, nvb, w, vb = vt.shape
    if bounded:
        tile, hp, qt = ATTN_Q_TILE, ATTN_HEADS, ATTN_TILES_PER_STEP
    else:
        tile, hp, qt = vb, LANES // FOX_HEAD_DIM, 1
    gw = hp * FOX_HEAD_DIM
    rows = qt * tile
    specs = dict(
        table=pl.BlockSpec(memory_space=pltpu.SMEM),
        q=pl.BlockSpec((1, hp, rows, LANES), lambda bi, p, i: (bi, p, i, 0)),
        k=pl.BlockSpec((1, hp, s, LANES), lambda bi, p, i: (bi, p, 0, 0)),
        v=pl.BlockSpec((1, nvb, gw, vb), lambda bi, p, i: (bi, 0, p, 0)),
        cq=pl.BlockSpec((1, hp, 1, rows), lambda bi, p, i: (bi, p, 0, i)),
        g=pl.BlockSpec((1, rows, gw), lambda bi, p, i: (bi, i, p)))
    if bounded:
        body = functools.partial(_fox_bounded_kernel, tile=tile, heads=hp)
        names = ("table", "q", "k", "v", "cq", "g")
        args = (edges[:, :, 1, :FOX_HEADS].reshape(-1), qa, ka, vt, cq, g)
        scratch = [pltpu.VMEM((qt, hp, tile, tile), BF16),
                   pltpu.VMEM((qt, hp, FOX_HEAD_DIM, tile), F32)]
    else:
        body = functools.partial(_fox_online_kernel, tile=tile, heads=hp)
        names = ("table", "q", "k", "v", "g")
        args = (edges[:, :, 0, :FOX_HEADS].reshape(-1), qa, ka, vt, g)
        scratch = []
    return pl.pallas_call(
        body,
        grid=(b, n_heads // hp, s // rows),
        in_specs=[specs[n] for n in names],
        out_specs=specs["g"],
        out_shape=jax.ShapeDtypeStruct((b, s, w), BF16),
        scratch_shapes=scratch,
        compiler_params=_params("arbitrary", "arbitrary", "arbitrary"),
        name="fox_attention_bounded" if bounded else "fox_attention_online",
    )(*args)


def _out_proj_kernel(a_ref, x_ref, mod_ref, w_ref, o_ref):
    y = jnp.dot(a_ref[0], w_ref[...], preferred_element_type=F32)
    o_ref[0] = x_ref[0] + mod_ref[0, 2:3, :] * y


def _out_proj(a, x, mod, w_out):
    b, s, d = x.shape
    tile = 512
    k = a.shape[-1]
    return pl.pallas_call(
        _out_proj_kernel,
        grid=(b, s // tile),
        in_specs=[pl.BlockSpec((1, tile, k), lambda bi, ti: (bi, ti, 0)),
                  pl.BlockSpec((1, tile, d), lambda bi, ti: (bi, ti, 0)),
                  pl.BlockSpec((1, 3, d), lambda bi, ti: (bi, 0, 0)),
                  pl.BlockSpec((k, d), lambda bi, ti: (0, 0))],
        out_specs=pl.BlockSpec((1, tile, d), lambda bi, ti: (bi, ti, 0)),
        out_shape=jax.ShapeDtypeStruct((b, s, d), F32),
        compiler_params=_params("arbitrary", "arbitrary"),
        name="layer1_out_proj",
    )(a, x, mod, w_out.astype(BF16))


def _layer1(x, mod, norm_g, w_in, b_f, qnorm_g, knorm_g, w_out):
    qa, ka, vt, g, cq, edges = _layer1_proj(x, mod, norm_g, w_in, b_f, qnorm_g, knorm_g)
    score_bound = (LOG2E * FOX_HEAD_DIM ** 0.5
                   * jnp.max(jnp.abs(qnorm_g)) * jnp.max(jnp.abs(knorm_g)))
    gated = lax.cond(score_bound <= FOX_SAFE_LOGIT,
                     functools.partial(_fox_attention, bounded=True),
                     functools.partial(_fox_attention, bounded=False),
                     qa, ka, vt, g, cq, edges)
    return _out_proj(gated, x, mod, w_out)


def kernel(x, c, norm_g, ada_w, ada_b, hgrn_lb, even_w_in, hgrn_onorm_g, pool_w, pool_scale,
           even_w_out, odd_w_in, fox_b_f, fox_qnorm_g, fox_knorm_g, odd_w_out):
    depth = norm_g.shape[0]
    mods = _adaln_mods(c, ada_w, ada_b)
    for l in range(depth):
        j = l // 2
        if l % 2 == 0:
            x = _layer0(x, mods[l], norm_g[l], even_w_in[j], hgrn_lb, hgrn_onorm_g[j],
                        pool_w[j], pool_scale[j], even_w_out[j], layer_slot=l)
        else:
            x = _layer1(x, mods[l], norm_g[l], odd_w_in[j], fox_b_f[j], fox_qnorm_g[j],
                        fox_knorm_g[j], odd_w_out[j])
    return x
```

```python
import functools

import jax
import jax.numpy as jnp
from jax import lax
from jax.experimental import pallas as pl
from jax.experimental.pallas import tpu as pltpu

F32 = jnp.float32
BF16 = jnp.bfloat16
EPS = 1e-6

HGRN_HEADS = 4
HGRN_KEY = 128
HGRN_VAL = 128
HGRN_KW = HGRN_HEADS * HGRN_KEY
HGRN_VW = HGRN_HEADS * HGRN_VAL
POOL_WINDOWS = (2, 4, 8, 16)
POOL_GROUP = 128
POOL_WIDTH = POOL_GROUP * len(POOL_WINDOWS)
POOL_HISTORY = 16
FOX_HEADS = 16
FOX_HEAD_DIM = 64
FOX_WIDTH = FOX_HEADS * FOX_HEAD_DIM

LANES = 128
SUBLANES = 8
VMEM_LIMIT_BYTES = 56 * 1024 * 1024

SEQ_TILE = 256
HGRN_CHUNK = 128
ATTN_Q_TILE = 512
ATTN_HEADS = 4
ATTN_TILES_PER_STEP = 2
FOX_SAFE_LOGIT = 100.0
FOX_DEAD_EXPONENT = 150.0
LOG2E = 1.4426950408889634
MASK_VALUE = -1e30
AUG_LANE = FOX_HEAD_DIM


def _silu(x):
    return x * jax.nn.sigmoid(x)


def _params(*semantics):
    return pltpu.CompilerParams(dimension_semantics=semantics, vmem_limit_bytes=VMEM_LIMIT_BYTES)


def _mod_kernel(c_ref, w_ref, b_ref, o_ref):
    c = c_ref[...]
    cond = _silu(c)
    o_ref[0] = jnp.dot(cond, w_ref[0], preferred_element_type=F32,
                       precision=lax.Precision.HIGHEST) + b_ref[0]


def _adaln_mods(c, ada_w, ada_b):
    depth, d, n = ada_w.shape
    b = c.shape[0]
    rows = SUBLANES
    c_rows = jnp.zeros((rows, d), F32).at[:b].set(c)
    tn = 1024
    out = pl.pallas_call(
        _mod_kernel,
        grid=(depth, n // tn),
        in_specs=[pl.BlockSpec((rows, d), lambda l, j: (0, 0)),
                  pl.BlockSpec((1, d, tn), lambda l, j: (l, 0, j)),
                  pl.BlockSpec((1, 1, tn), lambda l, j: (l, 0, j))],
        out_specs=pl.BlockSpec((1, rows, tn), lambda l, j: (l, 0, j)),
        out_shape=jax.ShapeDtypeStruct((depth, rows, n), F32),
        compiler_params=_params("arbitrary", "arbitrary"),
        name="adaln_mods",
    )(c_rows, ada_w, ada_b.reshape(depth, 1, n))
    return out[:, :b].reshape(depth, b, 3, d)


def _modulated_norm(x, mod_ref, ng_ref):
    shift = mod_ref[0, 0:1, :]
    scale = mod_ref[0, 1:2, :]
    ms = jnp.mean(x * x, axis=-1, keepdims=True)
    return x * lax.rsqrt(ms + EPS) * ng_ref[...] * (1.0 + scale) + shift


def _layer0_kernel(x_ref, mod_ref, ng_ref, win_ref, lb_ref, og_ref, pw_ref, ps_ref, wout_ref,
                   o_ref, state_ref, carry_ref, *, tile, chunk, layer_slot):
    T, C, H = tile, chunk, HGRN_HEADS
    t_idx = pl.program_id(1)

    @pl.when(t_idx == 0)
    def _():
        state_ref[...] = jnp.zeros_like(state_ref)
        carry_ref[...] = jnp.zeros_like(carry_ref)

    x = x_ref[0]
    h = _modulated_norm(x, mod_ref, ng_ref)
    proj = jnp.dot(h.astype(BF16), win_ref[...], preferred_element_type=F32)
    o0 = 0
    q = proj[:, o0:o0 + HGRN_KW]; o0 += HGRN_KW
    f = proj[:, o0:o0 + HGRN_KW]; o0 += HGRN_KW
    val = proj[:, o0:o0 + HGRN_VW]; o0 += HGRN_VW
    g_a = proj[:, o0:o0 + HGRN_VW]; o0 += HGRN_VW
    u = proj[:, o0:o0 + POOL_WIDTH]; o0 += POOL_WIDTH
    g_b = proj[:, o0:o0 + POOL_WIDTH]

    lbv = lb_ref[...]
    e = jnp.exp(lbv - jnp.max(lbv, axis=0, keepdims=True))
    lower = (jnp.sum(e[0:layer_slot + 1], axis=0, keepdims=True)
             / jnp.sum(e, axis=0, keepdims=True))
    forget = lower + (1.0 - lower) * jax.nn.sigmoid(f)
    logf = jnp.log(forget)
    key = 1.0 - forget

    row = lax.broadcasted_iota(jnp.int32, (T, HGRN_KW), 0)
    ti = lax.broadcasted_iota(jnp.int32, (C, C), 0)
    si = lax.broadcasted_iota(jnp.int32, (C, C), 1)
    n_chunks = T // C
    nt_dims = (((1,), (1,)), ((), ()))

    def level_scores(qe, ke, mask, scores):
        qb = qe.astype(BF16)
        kb = ke.astype(BF16)
        out = []
        for c in range(n_chunks):
            for hh in range(H):
                rs = slice(c * C, (c + 1) * C)
                cs = slice(hh * HGRN_KEY, (hh + 1) * HGRN_KEY)
                d = lax.dot_general(qb[rs, cs], kb[rs, cs], nt_dims, preferred_element_type=F32)
                prev = scores[c * H + hh]
                out.append(jnp.where(mask, d, 0.0 if prev is None else prev))
        return out

    scores = level_scores(q, key, ti == si, [None] * (n_chunks * H))
    p_sum = logf
    q_sum = jnp.zeros_like(logf)
    total = logf
    m = 1
    while m < C:
        mask = ((ti ^ si) < 2 * m) & ((ti & m) != 0) & ((si & m) == 0)
        scores = level_scores(q * jnp.exp(p_sum), key * jnp.exp(q_sum), mask, scores)
        upper = (row & m) != 0
        t_dn = pltpu.roll(total, m, 0)
        t_up = pltpu.roll(total, T - m, 0)
        p_sum = p_sum + jnp.where(upper, t_dn, 0.0)
        q_sum = q_sum + jnp.where(upper, 0.0, t_up)
        total = total + jnp.where(upper, t_dn, t_up)
        m *= 2

    q_in = (q * jnp.exp(p_sum)).astype(BF16)
    k_out = (key * jnp.exp(q_sum)).astype(BF16)
    decay = jnp.exp(total)
    val_b = val.astype(BF16)
    tn_dims = (((0,), (0,)), ((), ()))
    oa_rows = []
    for c in range(n_chunks):
        rs = slice(c * C, (c + 1) * C)
        heads_out = []
        for hh in range(H):
            cs = slice(hh * HGRN_KEY, (hh + 1) * HGRN_KEY)
            st = state_ref[hh]
            vb = val_b[rs, cs]
            o = jnp.dot(scores[c * H + hh].astype(BF16), vb, preferred_element_type=F32)
            o = o + lax.dot_general(q_in[rs, cs], st.astype(BF16), nt_dims,
                                    preferred_element_type=F32)
            state_ref[hh] = (st * decay[c * C:c * C + 1, cs]
                             + lax.dot_general(vb, k_out[rs, cs], tn_dims,
                                               preferred_element_type=F32))
            ms_o = jnp.mean(o * o, axis=-1, keepdims=True)
            heads_out.append(o * lax.rsqrt(ms_o + EPS))
        oa_rows.append(jnp.concatenate(heads_out, axis=1))
    o_a = jnp.concatenate(oa_rows, axis=0) * og_ref[...] * _silu(g_a)

    ub = jnp.concatenate([carry_ref[...], u], axis=0)
    carry_ref[...] = u[T - POOL_HISTORY:T, :]
    wins = []
    acc = ub
    sh = 1
    while sh < max(POOL_WINDOWS):
        acc = acc + pltpu.roll(acc, sh, 0)
        sh *= 2
        wins.append(acc)
    pos = lax.broadcasted_iota(jnp.int32, (T, POOL_GROUP), 0) + t_idx * T + 1
    ob = []
    for gi, w in enumerate(POOL_WINDOWS):
        cs = slice(gi * POOL_GROUP, (gi + 1) * POOL_GROUP)
        win = wins[w.bit_length() - 2][POOL_HISTORY:, cs]
        cnt = jnp.minimum(pos, w).astype(F32)
        pooled = win / cnt - u[:, cs]
        ob.append(jnp.dot(pooled.astype(BF16), pw_ref[gi], preferred_element_type=F32))
    o_b = jnp.concatenate(ob, axis=1) * ps_ref[...] * _silu(g_b)

    mixed = jnp.concatenate([o_a, o_b], axis=1).astype(BF16)
    y = jnp.dot(mixed, wout_ref[...], preferred_element_type=F32)
    o_ref[0] = x + mod_ref[0, 2:3, :] * y


def _layer0(x, mod, norm_g, w_in, hgrn_lb, onorm_g, pool_w, pool_scale, w_out, layer_slot):
    b, s, d = x.shape
    tile = SEQ_TILE
    n_in = w_in.shape[1]
    mix = w_out.shape[0]
    const2 = lambda bi, ti: (0, 0)
    return pl.pallas_call(
        functools.partial(_layer0_kernel, tile=tile, chunk=HGRN_CHUNK, layer_slot=layer_slot),
        grid=(b, s // tile),
        in_specs=[pl.BlockSpec((1, tile, d), lambda bi, ti: (bi, ti, 0)),
                  pl.BlockSpec((1, 3, d), lambda bi, ti: (bi, 0, 0)),
                  pl.BlockSpec((1, d), const2),
                  pl.BlockSpec((d, n_in), const2),
                  pl.BlockSpec(hgrn_lb.shape, const2),
                  pl.BlockSpec((1, HGRN_VW), const2),
                  pl.BlockSpec(pool_w.shape, lambda bi, ti: (0, 0, 0)),
                  pl.BlockSpec((1, POOL_WIDTH), const2),
                  pl.BlockSpec((mix, d), const2)],
        out_specs=pl.BlockSpec((1, tile, d), lambda bi, ti: (bi, ti, 0)),
        out_shape=jax.ShapeDtypeStruct((b, s, d), F32),
        scratch_shapes=[pltpu.VMEM((HGRN_HEADS, HGRN_VAL, HGRN_KEY), F32),
                        pltpu.VMEM((POOL_HISTORY, POOL_WIDTH), F32)],
        compiler_params=_params("arbitrary", "arbitrary"),
        name="layer0_hgrn_pool",
    )(x, mod, norm_g.reshape(1, d), w_in.astype(BF16), hgrn_lb, onorm_g.reshape(1, HGRN_VW),
      pool_w.astype(BF16), pool_scale.reshape(1, POOL_WIDTH), w_out.astype(BF16))


def _split3(c):
    hi = c.astype(BF16).astype(F32)
    r = c - hi
    mid = r.astype(BF16).astype(F32)
    lo = (r - mid).astype(BF16).astype(F32)
    return hi, mid, lo


def _layer1_proj_kernel(x_ref, mod_ref, ng_ref, win_ref, bf_ref, qg_ref, kg_ref, bd_ref,
                        qa_ref, ka_ref, vt_ref, g_ref, cq_ref, edge_ref, carry_ref, tile_start_ref,
                        *, tile):
    T, W = tile, FOX_WIDTH
    t_idx = pl.program_id(1)

    @pl.when(t_idx == 0)
    def _():
        carry_ref[...] = jnp.zeros_like(carry_ref)

    x = x_ref[0]
    h = _modulated_norm(x, mod_ref, ng_ref)
    proj = jnp.dot(h.astype(BF16), win_ref[...], preferred_element_type=F32)
    q = proj[:, 0:W]
    k = proj[:, W:2 * W]
    v = proj[:, 2 * W:3 * W]
    g_ref[0] = proj[:, 3 * W:4 * W]
    fl = proj[:, 4 * W:4 * W + LANES]
    vt_ref[0, 0] = v.T.astype(BF16)

    z = fl + bf_ref[...]
    logf = jnp.minimum(z, 0.0) - jnp.log1p(jnp.exp(-jnp.abs(z)))
    row = lax.broadcasted_iota(jnp.int32, (T, LANES), 0)
    cum = logf
    sh = 1
    while sh < T:
        cum = cum + jnp.where(row >= sh, pltpu.roll(cum, sh, 0), 0.0)
        sh *= 2
    start = carry_ref[...]
    end = start + cum[T - 1:T, :]
    carry_ref[...] = end

    @pl.when(t_idx % (ATTN_Q_TILE // T) == 0)
    def _():
        tile_start_ref[...] = start

    tile_start = tile_start_ref[...]
    edge_ref[0, 0] = jnp.concatenate([tile_start, end], axis=0) * LOG2E
    s_hi, s_mid, s_lo = _split3((cum + (start - tile_start)) * LOG2E)
    e_hi, e_mid, e_lo = _split3((cum - cum[T - 1:T, :]) * LOG2E)
    total_t = ((cum + start) * LOG2E).T
    for hd in range(FOX_HEADS):
        cq_ref[0, hd] = total_t[hd:hd + 1, :]

    lane = lax.broadcasted_iota(jnp.int32, (T, LANES), 1)
    ones_q = (lane >= AUG_LANE + 3) & (lane < AUG_LANE + 9)
    ones_k = (lane >= AUG_LANE) & (lane < AUG_LANE + 3)

    def spread(a, hd):
        return jnp.broadcast_to(a[:, hd:hd + 1], (T, LANES))
    bd = bd_ref[...]
    qk_scale = FOX_HEAD_DIM ** -0.5 * LOG2E
    for pair in range(FOX_HEADS // 2):
        cs = slice(pair * LANES, (pair + 1) * LANES)
        qp = q[:, cs]
        kp = k[:, cs]
        q_ms = jnp.dot((qp * qp).astype(BF16), bd, preferred_element_type=F32)
        k_ms = jnp.dot((kp * kp).astype(BF16), bd, preferred_element_type=F32)
        qn = qp * lax.rsqrt(q_ms + EPS) * (qg_ref[...] * qk_scale)
        kn = kp * lax.rsqrt(k_ms + EPS) * kg_ref[...]
        for e in range(LANES // FOX_HEAD_DIM):
            hd = (LANES // FOX_HEAD_DIM) * pair + e
            hi, mid, lo = spread(s_hi, hd), spread(s_mid, hd), spread(s_lo, hd)
            aug_q = jnp.where(lane == AUG_LANE, hi,
                              jnp.where(lane == AUG_LANE + 1, mid,
                                        jnp.where(lane == AUG_LANE + 2, lo,
                                                  jnp.where(ones_q, 1.0, 0.0))))
            aug_k = jnp.where(lane == AUG_LANE + 3, -hi,
                              jnp.where(lane == AUG_LANE + 4, -mid,
                                        jnp.where(lane == AUG_LANE + 5, -lo,
                                                  jnp.where(ones_k, 1.0, 0.0))))
            aug_k = jnp.where(lane == AUG_LANE + 6, -spread(e_hi, hd),
                              jnp.where(lane == AUG_LANE + 7, -spread(e_mid, hd),
                                        jnp.where(lane == AUG_LANE + 8, -spread(e_lo, hd), aug_k)))
            q_main = qn if e == 0 else pltpu.roll(qn, FOX_HEAD_DIM, 1)
            k_main = kn if e == 0 else pltpu.roll(kn, FOX_HEAD_DIM, 1)
            qa_ref[0, hd] = jnp.where(lane < FOX_HEAD_DIM, q_main, aug_q).astype(BF16)
            ka_ref[0, hd] = jnp.where(lane < FOX_HEAD_DIM, k_main, aug_k).astype(BF16)


def _layer1_proj(x, mod, norm_g, w_in, b_f, qnorm_g, knorm_g):
    b, s, d = x.shape
    tile = SEQ_TILE
    nt = s // tile
    w = FOX_WIDTH
    n_pad = 4 * w + LANES
    w_pad = jnp.zeros((d, n_pad), BF16).at[:, :w_in.shape[1]].set(w_in.astype(BF16))
    bf_pad = jnp.zeros((1, LANES), F32).at[0, :FOX_HEADS].set(b_f)
    blk = jnp.arange(LANES) // FOX_HEAD_DIM
    bd = jnp.where(blk[:, None] == blk[None, :], 1.0 / FOX_HEAD_DIM, 0.0).astype(BF16)
    qg = jnp.tile(qnorm_g, LANES // FOX_HEAD_DIM).reshape(1, LANES)
    kg = jnp.tile(knorm_g, LANES // FOX_HEAD_DIM).reshape(1, LANES)
    const2 = lambda bi, ti: (0, 0)
    return pl.pallas_call(
        functools.partial(_layer1_proj_kernel, tile=tile),
        grid=(b, nt),
        in_specs=[pl.BlockSpec((1, tile, d), lambda bi, ti: (bi, ti, 0)),
                  pl.BlockSpec((1, 3, d), lambda bi, ti: (bi, 0, 0)),
                  pl.BlockSpec((1, d), const2),
                  pl.BlockSpec((d, n_pad), const2),
                  pl.BlockSpec((1, LANES), const2),
                  pl.BlockSpec((1, LANES), const2),
                  pl.BlockSpec((1, LANES), const2),
                  pl.BlockSpec((LANES, LANES), const2)],
        out_specs=[pl.BlockSpec((1, FOX_HEADS, tile, LANES), lambda bi, ti: (bi, 0, ti, 0)),
                   pl.BlockSpec((1, FOX_HEADS, tile, LANES), lambda bi, ti: (bi, 0, ti, 0)),
                   pl.BlockSpec((1, 1, w, tile), lambda bi, ti: (bi, ti, 0, 0)),
                   pl.BlockSpec((1, tile, w), lambda bi, ti: (bi, ti, 0)),
                   pl.BlockSpec((1, FOX_HEADS, 1, tile), lambda bi, ti: (bi, 0, 0, ti)),
                   pl.BlockSpec((1, 1, 2, LANES), lambda bi, ti: (bi, ti, 0, 0))],
        out_shape=[jax.ShapeDtypeStruct((b, FOX_HEADS, s, LANES), BF16),
                   jax.ShapeDtypeStruct((b, FOX_HEADS, s, LANES), BF16),
                   jax.ShapeDtypeStruct((b, nt, w, tile), BF16),
                   jax.ShapeDtypeStruct((b, s, w), F32),
                   jax.ShapeDtypeStruct((b, FOX_HEADS, 1, s), F32),
                   jax.ShapeDtypeStruct((b, nt, 2, LANES), F32)],
        scratch_shapes=[pltpu.VMEM((1, LANES), F32), pltpu.VMEM((1, LANES), F32)],
        compiler_params=_params("arbitrary", "arbitrary"),
        name="layer1_proj",
    )(x, mod, norm_g.reshape(1, d), w_pad, bf_pad, qg, kg, bd)


def _query_operands(q):
    slab = 2 * SUBLANES
    row = lax.broadcasted_iota(jnp.int32, (slab, q.shape[0]), 0)
    q_t = q.astype(F32).T
    carriers = q_t[AUG_LANE:AUG_LANE + slab]
    return tuple(jnp.concatenate([q_t[:AUG_LANE], jnp.where(keep, carriers, 0.0),
                                  q_t[AUG_LANE + slab:]], axis=0).astype(BF16)
                 for keep in (row < 6, row >= 6))


def _gated_output(acc_t, g_ref, o_ref, rows):
    o_ref[0, rows] = (acc_t.T * _silu(g_ref[0, rows])).astype(BF16)


def _fox_bounded_kernel(first_ref, end_ref, qa_ref, ka_ref, vt_ref, cq_ref, g_ref, o_ref,
                        p_ref, acc_ref, *, tile, heads):
    TQ, HP = tile, heads
    QT = qa_ref.shape[2] // TQ
    SB = vt_ref.shape[-1]
    R = TQ // SB
    n_blocks = ka_ref.shape[2] // SB
    b = pl.program_id(0)
    group = pl.program_id(1)
    i0 = pl.program_id(2) * QT
    chains = [(qs, e) for qs in range(QT) for e in range(HP)]
    cols = [slice(qs * TQ, (qs + 1) * TQ) for qs in range(QT)]
    operands = {(qs, e): _query_operands(qa_ref[0, e, cols[qs], :]) for qs, e in chains}
    causal = (lax.broadcasted_iota(jnp.int32, (TQ, TQ), 0)
              <= lax.broadcasted_iota(jnp.int32, (TQ, TQ), 1))

    def keys(e, n):
        return ka_ref[0, e, pl.ds(pl.multiple_of(n * TQ, TQ), TQ), :]

    def values_t(e, blk):
        return vt_ref[0, blk, pl.ds(e * FOX_HEAD_DIM, FOX_HEAD_DIM), :]

    def rest(c, blk, gate=1.0):
        qs, e = c
        end = end_ref[(b * n_blocks + blk) * FOX_HEADS + HP * group + e]
        return jnp.exp2(jnp.minimum(cq_ref[0, e][:, cols[qs]] - end, 0.0)) * gate

    def col_partial(p):
        return jnp.sum(p.reshape(p.shape[0] // SUBLANES, SUBLANES, p.shape[1]), axis=0)

    def far_scores(c, n):
        return jnp.dot(keys(c[1], n), operands[c][1], preferred_element_type=F32)

    def contract(c, n, p_bf16, gate=1.0):
        for r in range(R):
            pv = jnp.dot(values_t(c[1], n * R + r), p_bf16[r * SB:(r + 1) * SB],
                         preferred_element_type=F32)
            acc_ref[c] += rest(c, n * R + r, gate) * pv

    def weighted_partial(c, n, p, gate=1.0):
        return sum(rest(c, n * R + r, gate) * col_partial(p[r * SB:(r + 1) * SB])
                   for r in range(R))

    l_part = {}
    for c in chains:
        qs, e = c
        s_t = jnp.dot(keys(e, i0 + qs), operands[c][0], preferred_element_type=F32)
        p = jnp.where(causal, jnp.exp2(s_t), 0.0)
        l_part[c] = col_partial(p)
        v_t = jnp.concatenate([values_t(e, (i0 + qs) * R + r) for r in range(R)], axis=1)
        acc_ref[c] = jnp.dot(v_t, p.astype(BF16), preferred_element_type=F32)
    for c in chains:
        for t in range(c[0]):
            p = jnp.exp2(far_scores(c, i0 + t))
            l_part[c] = l_part[c] + weighted_partial(c, i0 + t, p)
            contract(c, i0 + t, p.astype(BF16))

    first = first_ref[(b * pl.num_programs(1) + group) * pl.num_programs(2) + pl.program_id(2)]
    gate0 = jnp.where(i0 > 0, 1.0, 0.0)
    for c in chains:
        p = jnp.exp2(far_scores(c, first))
        l_part[c] = l_part[c] + weighted_partial(c, first, p, gate0)
        p_ref[c] = p.astype(BF16)

    def body(n, l_run):
        s_new = [far_scores(c, n) for c in chains]
        for c in chains:
            contract(c, n - 1, p_ref[c])
        out = []
        for c, s_t, l_c in zip(chains, s_new, l_run):
            p = jnp.exp2(s_t)
            out.append(l_c + weighted_partial(c, n, p))
            p_ref[c] = p.astype(BF16)
        return tuple(out)

    l_run = lax.fori_loop(first + 1, i0, body, tuple(l_part[c] for c in chains))
    for c in chains:
        contract(c, jnp.maximum(i0 - 1, 0), p_ref[c], gate0)
    for qs in range(QT):
        out_t = [acc_ref[qs, e] * (1.0 / jnp.sum(l_run[qs * HP + e], axis=0, keepdims=True))
                 for e in range(HP)]
        _gated_output(jnp.concatenate(out_t, axis=0), g_ref, o_ref, cols[qs])


def _fox_online_kernel(start_ref, qa_ref, ka_ref, vt_ref, g_ref, o_ref, *, tile, heads):
    TB, HP = tile, heads
    n_blocks = ka_ref.shape[2] // TB
    b = pl.program_id(0)
    group = pl.program_id(1)
    i = pl.program_id(2)
    q_t = [_query_operands(qa_ref[0, e])[0] for e in range(HP)]
    k_row = lax.broadcasted_iota(jnp.int32, (TB, TB), 0)
    q_col = lax.broadcasted_iota(jnp.int32, (TB, TB), 1)

    def step(j, carry, masked):
        out = []
        for e in range(HP):
            m_run, l_run, acc = carry[e]
            k_blk = ka_ref[0, e, pl.ds(pl.multiple_of(j * TB, TB), TB), :]
            s_t = jnp.dot(k_blk, q_t[e], preferred_element_type=F32)
            if masked:
                s_t = jnp.where(k_row <= q_col, s_t, MASK_VALUE)
            hd = HP * group + e
            delta = (start_ref[(b * n_blocks + i) * FOX_HEADS + hd]
                     - start_ref[(b * n_blocks + j) * FOX_HEADS + hd])
            m_new = jnp.maximum(m_run, jnp.max(s_t, axis=0, keepdims=True) + delta)
            p_t = jnp.exp2(s_t - (m_new - delta))
            alpha = jnp.exp2(m_run - m_new)
            l_new = alpha * l_run + jnp.sum(p_t, axis=0, keepdims=True)
            v_t = vt_ref[0, j, pl.ds(e * FOX_HEAD_DIM, FOX_HEAD_DIM), :]
            acc_new = alpha * acc + jnp.dot(v_t, p_t.astype(BF16), preferred_element_type=F32)
            out.append((m_new, l_new, acc_new))
        return tuple(out)

    init = (jnp.full((1, TB), MASK_VALUE, F32), jnp.zeros((1, TB), F32),
            jnp.zeros((FOX_HEAD_DIM, TB), F32))
    carry = lax.fori_loop(0, i, lambda j, c: step(j, c, False), (init,) * HP)
    carry = step(i, carry, True)
    _gated_output(jnp.concatenate([acc / l_run for _, l_run, acc in carry], axis=0), g_ref, o_ref,
                  slice(None))


def _fox_attention(qa, ka, vt, g, cq, edges, *, bounded):
    b, n_heads, s, _ = qa.shape
    _, nvb, w, vb = vt.shape
    if bounded:
        tile, hp, qt = ATTN_Q_TILE, ATTN_HEADS, ATTN_TILES_PER_STEP
    else:
        tile, hp, qt = vb, LANES // FOX_HEAD_DIM, 1
    gw = hp * FOX_HEAD_DIM
    rows = qt * tile
    specs = dict(
        table=pl.BlockSpec(memory_space=pltpu.SMEM),
        q=pl.BlockSpec((1, hp, rows, LANES), lambda bi, p, i: (bi, p, i, 0)),
        k=pl.BlockSpec((1, hp, s, LANES), lambda bi, p, i: (bi, p, 0, 0)),
        v=pl.BlockSpec((1, nvb, gw, vb), lambda bi, p, i: (bi, 0, p, 0)),
        cq=pl.BlockSpec((1, hp, 1, rows), lambda bi, p, i: (bi, p, 0, i)),
        g=pl.BlockSpec((1, rows, gw), lambda bi, p, i: (bi, i, p)))
    if bounded:
        body = functools.partial(_fox_bounded_kernel, tile=tile, heads=hp)
        names = ("table", "table", "q", "k", "v", "cq", "g")
        step_start = edges[:, ::rows // vb, 0, :FOX_HEADS]
        tile_end = edges[:, tile // vb - 1::tile // vb, 1, :FOX_HEADS]
        dead = (step_start[:, :, None] - tile_end[:, None]) < -FOX_DEAD_EXPONENT
        before = jnp.arange(s // rows)[:, None] * qt > jnp.arange(s // tile)[None, :]
        first_live = jnp.sum(dead & before[None, :, :, None], axis=2)
        last_before = jnp.maximum(jnp.arange(s // rows) * qt - 1, 0)
        first_live = jnp.minimum(first_live, last_before[None, :, None])
        first_live = jnp.min(first_live.reshape(b, s // rows, n_heads // hp, hp), axis=-1)
        args = (first_live.transpose(0, 2, 1).reshape(-1).astype(jnp.int32),
                edges[:, :, 1, :FOX_HEADS].reshape(-1), qa, ka, vt, cq, g)
        scratch = [pltpu.VMEM((qt, hp, tile, tile), BF16),
                   pltpu.VMEM((qt, hp, FOX_HEAD_DIM, tile), F32)]
    else:
        body = functools.partial(_fox_online_kernel, tile=tile, heads=hp)
        names = ("table", "q", "k", "v", "g")
        args = (edges[:, :, 0, :FOX_HEADS].reshape(-1), qa, ka, vt, g)
        scratch = []
    return pl.pallas_call(
        body,
        grid=(b, n_heads // hp, s // rows),
        in_specs=[specs[n] for n in names],
        out_specs=specs["g"],
        out_shape=jax.ShapeDtypeStruct((b, s, w), BF16),
        scratch_shapes=scratch,
        compiler_params=_params("arbitrary", "arbitrary", "arbitrary"),
        name="fox_attention_bounded" if bounded else "fox_attention_online",
    )(*args)


def _out_proj_kernel(a_ref, x_ref, mod_ref, w_ref, o_ref):
    y = jnp.dot(a_ref[0], w_ref[...], preferred_element_type=F32)
    o_ref[0] = x_ref[0] + mod_ref[0, 2:3, :] * y


def _out_proj(a, x, mod, w_out):
    b, s, d = x.shape
    tile = 512
    k = a.shape[-1]
    return pl.pallas_call(
        _out_proj_kernel,
        grid=(b, s // tile),
        in_specs=[pl.BlockSpec((1, tile, k), lambda bi, ti: (bi, ti, 0)),
                  pl.BlockSpec((1, tile, d), lambda bi, ti: (bi, ti, 0)),
                  pl.BlockSpec((1, 3, d), lambda bi, ti: (bi, 0, 0)),
                  pl.BlockSpec((k, d), lambda bi, ti: (0, 0))],
        out_specs=pl.BlockSpec((1, tile, d), lambda bi, ti: (bi, ti, 0)),
        out_shape=jax.ShapeDtypeStruct((b, s, d), F32),
        compiler_params=_params("arbitrary", "arbitrary"),
        name="layer1_out_proj",
    )(a, x, mod, w_out.astype(BF16))


def _layer1(x, mod, norm_g, w_in, b_f, qnorm_g, knorm_g, w_out):
    qa, ka, vt, g, cq, edges = _layer1_proj(x, mod, norm_g, w_in, b_f, qnorm_g, knorm_g)
    score_bound = (LOG2E * FOX_HEAD_DIM ** 0.5
                   * jnp.max(jnp.abs(qnorm_g)) * jnp.max(jnp.abs(knorm_g)))
    gated = lax.cond(score_bound <= FOX_SAFE_LOGIT,
                     functools.partial(_fox_attention, bounded=True),
                     functools.partial(_fox_attention, bounded=False),
                     qa, ka, vt, g, cq, edges)
    return _out_proj(gated, x, mod, w_out)


def kernel(x, c, norm_g, ada_w, ada_b, hgrn_lb, even_w_in, hgrn_onorm_g, pool_w, pool_scale,
           even_w_out, odd_w_in, fox_b_f, fox_qnorm_g, fox_knorm_g, odd_w_out):
    depth = norm_g.shape[0]
    mods = _adaln_mods(c, ada_w, ada_b)
    for l in range(depth):
        j = l // 2
        if l % 2 == 0:
            x = _layer0(x, mods[l], norm_g[l], even_w_in[j], hgrn_lb, hgrn_onorm_g[j],
                        pool_w[j], pool_scale[j], even_w_out[j], layer_slot=l)
        else:
            x = _layer1(x, mods[l], norm_g[l], odd_w_in[j], fox_b_f[j], fox_qnorm_g[j],
                        fox_knorm_g[j], odd_w_out[j])
    return x
```

```python
import functools

import jax
import jax.numpy as jnp
from jax import lax
from jax.experimental import pallas as pl
from jax.experimental.pallas import tpu as pltpu

F32 = jnp.float32
BF16 = jnp.bfloat16
EPS = 1e-6

HGRN_HEADS = 4
HGRN_KEY = 128
HGRN_VAL = 128
HGRN_KW = HGRN_HEADS * HGRN_KEY
HGRN_VW = HGRN_HEADS * HGRN_VAL
POOL_WINDOWS = (2, 4, 8, 16)
POOL_GROUP = 128
POOL_WIDTH = POOL_GROUP * len(POOL_WINDOWS)
POOL_HISTORY = 16
FOX_HEADS = 16
FOX_HEAD_DIM = 64
FOX_WIDTH = FOX_HEADS * FOX_HEAD_DIM

LANES = 128
SUBLANES = 8
VMEM_LIMIT_BYTES = 56 * 1024 * 1024

SEQ_TILE = 256
HGRN_CHUNK = 128
ATTN_Q_TILE = 512
ATTN_HEADS = 4
ATTN_TILES_PER_STEP = 2
FOX_SAFE_LOGIT = 100.0
FOX_DEAD_EXPONENT = 150.0
LOG2E = 1.4426950408889634
MASK_VALUE = -1e30
AUG_LANE = FOX_HEAD_DIM


def _silu(x):
    return x * jax.nn.sigmoid(x)


def _params(*semantics):
    return pltpu.CompilerParams(dimension_semantics=semantics, vmem_limit_bytes=VMEM_LIMIT_BYTES)


def _mod_kernel(c_ref, w_ref, b_ref, o_ref):
    c = c_ref[...]
    cond = _silu(c)
    o_ref[0] = jnp.dot(cond, w_ref[0], preferred_element_type=F32,
                       precision=lax.Precision.HIGHEST) + b_ref[0]


def _adaln_mods(c, ada_w, ada_b):
    depth, d, n = ada_w.shape
    b = c.shape[0]
    rows = SUBLANES
    c_rows = jnp.zeros((rows, d), F32).at[:b].set(c)
    tn = 1024
    out = pl.pallas_call(
        _mod_kernel,
        grid=(depth, n // tn),
        in_specs=[pl.BlockSpec((rows, d), lambda l, j: (0, 0)),
                  pl.BlockSpec((1, d, tn), lambda l, j: (l, 0, j)),
                  pl.BlockSpec((1, 1, tn), lambda l, j: (l, 0, j))],
        out_specs=pl.BlockSpec((1, rows, tn), lambda l, j: (l, 0, j)),
        out_shape=jax.ShapeDtypeStruct((depth, rows, n), F32),
        compiler_params=_params("arbitrary", "arbitrary"),
        name="adaln_mods",
    )(c_rows, ada_w, ada_b.reshape(depth, 1, n))
    return out[:, :b].reshape(depth, b, 3, d)


def _modulated_norm(x, mod_ref, ng_ref):
    shift = mod_ref[0, 0:1, :]
    scale = mod_ref[0, 1:2, :]
    ms = jnp.mean(x * x, axis=-1, keepdims=True)
    return x * lax.rsqrt(ms + EPS) * ng_ref[...] * (1.0 + scale) + shift


def _layer0_kernel(x_ref, mod_ref, ng_ref, win_ref, lb_ref, og_ref, pw_ref, ps_ref, wout_ref,
                   o_ref, state_ref, carry_ref, *, tile, chunk, layer_slot):
    T, C, H = tile, chunk, HGRN_HEADS
    t_idx = pl.program_id(1)

    @pl.when(t_idx == 0)
    def _():
        state_ref[...] = jnp.zeros_like(state_ref)
        carry_ref[...] = jnp.zeros_like(carry_ref)

    x = x_ref[0]
    h = _modulated_norm(x, mod_ref, ng_ref)
    proj = jnp.dot(h.astype(BF16), win_ref[...], preferred_element_type=F32)
    o0 = 0
    q = proj[:, o0:o0 + HGRN_KW]; o0 += HGRN_KW
    f = proj[:, o0:o0 + HGRN_KW]; o0 += HGRN_KW
    val = proj[:, o0:o0 + HGRN_VW]; o0 += HGRN_VW
    g_a = proj[:, o0:o0 + HGRN_VW]; o0 += HGRN_VW
    u = proj[:, o0:o0 + POOL_WIDTH]; o0 += POOL_WIDTH
    g_b = proj[:, o0:o0 + POOL_WIDTH]

    lbv = lb_ref[...]
    e = jnp.exp(lbv - jnp.max(lbv, axis=0, keepdims=True))
    lower = (jnp.sum(e[0:layer_slot + 1], axis=0, keepdims=True)
             / jnp.sum(e, axis=0, keepdims=True))
    forget = lower + (1.0 - lower) * jax.nn.sigmoid(f)
    logf = jnp.log(forget)
    key = 1.0 - forget

    row = lax.broadcasted_iota(jnp.int32, (T, HGRN_KW), 0)
    ti = lax.broadcasted_iota(jnp.int32, (C, C), 0)
    si = lax.broadcasted_iota(jnp.int32, (C, C), 1)
    n_chunks = T // C
    nt_dims = (((1,), (1,)), ((), ()))

    def level_scores(qe, ke, mask, scores):
        qb = qe.astype(BF16)
        kb = ke.astype(BF16)
        out = []
        for c in range(n_chunks):
            for hh in range(H):
                rs = slice(c * C, (c + 1) * C)
                cs = slice(hh * HGRN_KEY, (hh + 1) * HGRN_KEY)
                d = lax.dot_general(qb[rs, cs], kb[rs, cs], nt_dims, preferred_element_type=F32)
                prev = scores[c * H + hh]
                out.append(jnp.where(mask, d, 0.0 if prev is None else prev))
        return out

    scores = level_scores(q, key, ti == si, [None] * (n_chunks * H))
    p_sum = logf
    q_sum = jnp.zeros_like(logf)
    total = logf
    m = 1
    while m < C:
        mask = ((ti ^ si) < 2 * m) & ((ti & m) != 0) & ((si & m) == 0)
        scores = level_scores(q * jnp.exp(p_sum), key * jnp.exp(q_sum), mask, scores)
        upper = (row & m) != 0
        t_dn = pltpu.roll(total, m, 0)
        t_up = pltpu.roll(total, T - m, 0)
        p_sum = p_sum + jnp.where(upper, t_dn, 0.0)
        q_sum = q_sum + jnp.where(upper, 0.0, t_up)
        total = total + jnp.where(upper, t_dn, t_up)
        m *= 2

    q_in = (q * jnp.exp(p_sum)).astype(BF16)
    k_out = (key * jnp.exp(q_sum)).astype(BF16)
    decay = jnp.exp(total)
    val_b = val.astype(BF16)
    tn_dims = (((0,), (0,)), ((), ()))
    oa_rows = []
    for c in range(n_chunks):
        rs = slice(c * C, (c + 1) * C)
        heads_out = []
        for hh in range(H):
            cs = slice(hh * HGRN_KEY, (hh + 1) * HGRN_KEY)
            st = state_ref[hh]
            vb = val_b[rs, cs]
            o = jnp.dot(scores[c * H + hh].astype(BF16), vb, preferred_element_type=F32)
            o = o + lax.dot_general(q_in[rs, cs], st.astype(BF16), nt_dims,
                                    preferred_element_type=F32)
            state_ref[hh] = (st * decay[c * C:c * C + 1, cs]
                             + lax.dot_general(vb, k_out[rs, cs], tn_dims,
                                               preferred_element_type=F32))
            ms_o = jnp.mean(o * o, axis=-1, keepdims=True)
            heads_out.append(o * lax.rsqrt(ms_o + EPS))
        oa_rows.append(jnp.concatenate(heads_out, axis=1))
    o_a = jnp.concatenate(oa_rows, axis=0) * og_ref[...] * _silu(g_a)

    ub = jnp.concatenate([carry_ref[...], u], axis=0)
    carry_ref[...] = u[T - POOL_HISTORY:T, :]
    wins = []
    acc = ub
    sh = 1
    while sh < max(POOL_WINDOWS):
        acc = acc + pltpu.roll(acc, sh, 0)
        sh *= 2
        wins.append(acc)
    pos = lax.broadcasted_iota(jnp.int32, (T, POOL_GROUP), 0) + t_idx * T + 1
    ob = []
    for gi, w in enumerate(POOL_WINDOWS):
        cs = slice(gi * POOL_GROUP, (gi + 1) * POOL_GROUP)
        win = wins[w.bit_length() - 2][POOL_HISTORY:, cs]
        cnt = jnp.minimum(pos, w).astype(F32)
        pooled = win / cnt - u[:, cs]
        ob.append(jnp.dot(pooled.astype(BF16), pw_ref[gi], preferred_element_type=F32))
    o_b = jnp.concatenate(ob, axis=1) * ps_ref[...] * _silu(g_b)

    mixed = jnp.concatenate([o_a, o_b], axis=1).astype(BF16)
    y = jnp.dot(mixed, wout_ref[...], preferred_element_type=F32)
    o_ref[0] = x + mod_ref[0, 2:3, :] * y


def _layer0(x, mod, norm_g, w_in, hgrn_lb, onorm_g, pool_w, pool_scale, w_out, layer_slot):
    b, s, d = x.shape
    tile = SEQ_TILE
    n_in = w_in.shape[1]
    mix = w_out.shape[0]
    const2 = lambda bi, ti: (0, 0)
    return pl.pallas_call(
        functools.partial(_layer0_kernel, tile=tile, chunk=HGRN_CHUNK, layer_slot=layer_slot),
        grid=(b, s // tile),
        in_specs=[pl.BlockSpec((1, tile, d), lambda bi, ti: (bi, ti, 0)),
                  pl.BlockSpec((1, 3, d), lambda bi, ti: (bi, 0, 0)),
                  pl.BlockSpec((1, d), const2),
                  pl.BlockSpec((d, n_in), const2),
                  pl.BlockSpec(hgrn_lb.shape, const2),
                  pl.BlockSpec((1, HGRN_VW), const2),
                  pl.BlockSpec(pool_w.shape, lambda bi, ti: (0, 0, 0)),
                  pl.BlockSpec((1, POOL_WIDTH), const2),
                  pl.BlockSpec((mix, d), const2)],
        out_specs=pl.BlockSpec((1, tile, d), lambda bi, ti: (bi, ti, 0)),
        out_shape=jax.ShapeDtypeStruct((b, s, d), F32),
        scratch_shapes=[pltpu.VMEM((HGRN_HEADS, HGRN_VAL, HGRN_KEY), F32),
                        pltpu.VMEM((POOL_HISTORY, POOL_WIDTH), F32)],
        compiler_params=_params("arbitrary", "arbitrary"),
        name="layer0_hgrn_pool",
    )(x, mod, norm_g.reshape(1, d), w_in.astype(BF16), hgrn_lb, onorm_g.reshape(1, HGRN_VW),
      pool_w.astype(BF16), pool_scale.reshape(1, POOL_WIDTH), w_out.astype(BF16))


def _split3(c):
    hi = c.astype(BF16).astype(F32)
    r = c - hi
    mid = r.astype(BF16).astype(F32)
    lo = (r - mid).astype(BF16).astype(F32)
    return hi, mid, lo


def _layer1_proj_kernel(x_ref, mod_ref, ng_ref, win_ref, bf_ref, qg_ref, kg_ref, bd_ref,
                        qa_ref, ka_ref, vt_ref, g_ref, cq_ref, edge_ref, carry_ref, tile_start_ref,
                        *, tile):
    T, W = tile, FOX_WIDTH
    t_idx = pl.program_id(1)

    @pl.when(t_idx == 0)
    def _():
        carry_ref[...] = jnp.zeros_like(carry_ref)

    x = x_ref[0]
    h = _modulated_norm(x, mod_ref, ng_ref)
    proj = jnp.dot(h.astype(BF16), win_ref[...], preferred_element_type=F32)
    q = proj[:, 0:W]
    k = proj[:, W:2 * W]
    v = proj[:, 2 * W:3 * W]
    g_ref[0] = proj[:, 3 * W:4 * W]
    fl = proj[:, 4 * W:4 * W + LANES]
    vt_ref[0, 0] = v.T.astype(BF16)

    z = fl + bf_ref[...]
    logf = jnp.minimum(z, 0.0) - jnp.log1p(jnp.exp(-jnp.abs(z)))
    row = lax.broadcasted_iota(jnp.int32, (T, LANES), 0)
    cum = logf
    sh = 1
    while sh < T:
        cum = cum + jnp.where(row >= sh, pltpu.roll(cum, sh, 0), 0.0)
        sh *= 2
    start = carry_ref[...]
    end = start + cum[T - 1:T, :]
    carry_ref[...] = end

    @pl.when(t_idx % (ATTN_Q_TILE // T) == 0)
    def _():
        tile_start_ref[...] = start

    tile_start = tile_start_ref[...]
    edge_ref[0, 0] = jnp.concatenate([tile_start, end], axis=0) * LOG2E
    s_hi, s_mid, s_lo = _split3((cum + (start - tile_start)) * LOG2E)
    e_hi, e_mid, e_lo = _split3((cum - cum[T - 1:T, :]) * LOG2E)
    total_t = ((cum + start) * LOG2E).T
    for hd in range(FOX_HEADS):
        cq_ref[0, hd] = total_t[hd:hd + 1, :]

    lane = lax.broadcasted_iota(jnp.int32, (T, LANES), 1)
    ones_q = (lane >= AUG_LANE + 3) & (lane < AUG_LANE + 9)
    ones_k = (lane >= AUG_LANE) & (lane < AUG_LANE + 3)

    def spread(a, hd):
        return jnp.broadcast_to(a[:, hd:hd + 1], (T, LANES))
    bd = bd_ref[...]
    qk_scale = FOX_HEAD_DIM ** -0.5 * LOG2E
    for pair in range(FOX_HEADS // 2):
        cs = slice(pair * LANES, (pair + 1) * LANES)
        qp = q[:, cs]
        kp = k[:, cs]
        q_ms = jnp.dot((qp * qp).astype(BF16), bd, preferred_element_type=F32)
        k_ms = jnp.dot((kp * kp).astype(BF16), bd, preferred_element_type=F32)
        qn = qp * lax.rsqrt(q_ms + EPS) * (qg_ref[...] * qk_scale)
        kn = kp * lax.rsqrt(k_ms + EPS) * kg_ref[...]
        for e in range(LANES // FOX_HEAD_DIM):
            hd = (LANES // FOX_HEAD_DIM) * pair + e
            hi, mid, lo = spread(s_hi, hd), spread(s_mid, hd), spread(s_lo, hd)
            aug_q = jnp.where(lane == AUG_LANE, hi,
                              jnp.where(lane == AUG_LANE + 1, mid,
                                        jnp.where(lane == AUG_LANE + 2, lo,
                                                  jnp.where(ones_q, 1.0, 0.0))))
            aug_k = jnp.where(lane == AUG_LANE + 3, -hi,
                              jnp.where(lane == AUG_LANE + 4, -mid,
                                        jnp.where(lane == AUG_LANE + 5, -lo,
                                                  jnp.where(ones_k, 1.0, 0.0))))
            aug_k = jnp.where(lane == AUG_LANE + 6, -spread(e_hi, hd),
                              jnp.where(lane == AUG_LANE + 7, -spread(e_mid, hd),
                                        jnp.where(lane == AUG_LANE + 8, -spread(e_lo, hd), aug_k)))
            q_main = qn if e == 0 else pltpu.roll(qn, FOX_HEAD_DIM, 1)
            k_main = kn if e == 0 else pltpu.roll(kn, FOX_HEAD_DIM, 1)
            qa_ref[0, hd] = jnp.where(lane < FOX_HEAD_DIM, q_main, aug_q).astype(BF16)
            ka_ref[0, hd] = jnp.where(lane < FOX_HEAD_DIM, k_main, aug_k).astype(BF16)


def _layer1_proj(x, mod, norm_g, w_in, b_f, qnorm_g, knorm_g):
    b, s, d = x.shape
    tile = SEQ_TILE
    nt = s // tile
    w = FOX_WIDTH
    n_pad = 4 * w + LANES
    w_pad = jnp.zeros((d, n_pad), BF16).at[:, :w_in.shape[1]].set(w_in.astype(BF16))
    bf_pad = jnp.zeros((1, LANES), F32).at[0, :FOX_HEADS].set(b_f)
    blk = jnp.arange(LANES) // FOX_HEAD_DIM
    bd = jnp.where(blk[:, None] == blk[None, :], 1.0 / FOX_HEAD_DIM, 0.0).astype(BF16)
    qg = jnp.tile(qnorm_g, LANES // FOX_HEAD_DIM).reshape(1, LANES)
    kg = jnp.tile(knorm_g, LANES // FOX_HEAD_DIM).reshape(1, LANES)
    const2 = lambda bi, ti: (0, 0)
    return pl.pallas_call(
        functools.partial(_layer1_proj_kernel, tile=tile),
        grid=(b, nt),
        in_specs=[pl.BlockSpec((1, tile, d), lambda bi, ti: (bi, ti, 0)),
                  pl.BlockSpec((1, 3, d), lambda bi, ti: (bi, 0, 0)),
                  pl.BlockSpec((1, d), const2),
                  pl.BlockSpec((d, n_pad), const2),
                  pl.BlockSpec((1, LANES), const2),
                  pl.BlockSpec((1, LANES), const2),
                  pl.BlockSpec((1, LANES), const2),
                  pl.BlockSpec((LANES, LANES), const2)],
        out_specs=[pl.BlockSpec((1, FOX_HEADS, tile, LANES), lambda bi, ti: (bi, 0, ti, 0)),
                   pl.BlockSpec((1, FOX_HEADS, tile, LANES), lambda bi, ti: (bi, 0, ti, 0)),
                   pl.BlockSpec((1, 1, w, tile), lambda bi, ti: (bi, ti, 0, 0)),
                   pl.BlockSpec((1, tile, w), lambda bi, ti: (bi, ti, 0)),
                   pl.BlockSpec((1, FOX_HEADS, 1, tile), lambda bi, ti: (bi, 0, 0, ti)),
                   pl.BlockSpec((1, 1, 2, LANES), lambda bi, ti: (bi, ti, 0, 0))],
        out_shape=[jax.ShapeDtypeStruct((b, FOX_HEADS, s, LANES), BF16),
                   jax.ShapeDtypeStruct((b, FOX_HEADS, s, LANES), BF16),
                   jax.ShapeDtypeStruct((b, nt, w, tile), BF16),
                   jax.ShapeDtypeStruct((b, s, w), F32),
                   jax.ShapeDtypeStruct((b, FOX_HEADS, 1, s), F32),
                   jax.ShapeDtypeStruct((b, nt, 2, LANES), F32)],
        scratch_shapes=[pltpu.VMEM((1, LANES), F32), pltpu.VMEM((1, LANES), F32)],
        compiler_params=_params("arbitrary", "arbitrary"),
        name="layer1_proj",
    )(x, mod, norm_g.reshape(1, d), w_pad, bf_pad, qg, kg, bd)


def _query_operands(q):
    slab = 2 * SUBLANES
    row = lax.broadcasted_iota(jnp.int32, (slab, q.shape[0]), 0)
    q_t = q.astype(F32).T
    carriers = q_t[AUG_LANE:AUG_LANE + slab]
    return tuple(jnp.concatenate([q_t[:AUG_LANE], jnp.where(keep, carriers, 0.0),
                                  q_t[AUG_LANE + slab:]], axis=0).astype(BF16)
                 for keep in (row < 6, row >= 6))


def _gated_output(acc_t, g_ref, o_ref, rows):
    o_ref[0, rows] = (acc_t.T * _silu(g_ref[0, rows])).astype(BF16)


def _fox_bounded_kernel(first_ref, end_ref, qa_ref, ka_ref, vt_ref, cq_ref, g_ref, o_ref,
                        p_ref, acc_ref, *, tile, heads):
    TQ, HP = tile, heads
    QT = qa_ref.shape[2] // TQ
    SB = vt_ref.shape[-1]
    R = TQ // SB
    n_blocks = ka_ref.shape[2] // SB
    b = pl.program_id(0)
    group = pl.program_id(1)
    i0 = pl.program_id(2) * QT
    chains = [(qs, e) for qs in range(QT) for e in range(HP)]
    cols = [slice(qs * TQ, (qs + 1) * TQ) for qs in range(QT)]
    operands = {(qs, e): _query_operands(qa_ref[0, e, cols[qs], :]) for qs, e in chains}
    causal = (lax.broadcasted_iota(jnp.int32, (TQ, TQ), 0)
              <= lax.broadcasted_iota(jnp.int32, (TQ, TQ), 1))

    def keys(e, n):
        return ka_ref[0, e, pl.ds(pl.multiple_of(n * TQ, TQ), TQ), :]

    def values_t(e, blk):
        return vt_ref[0, blk, pl.ds(e * FOX_HEAD_DIM, FOX_HEAD_DIM), :]

    def rest(c, blk, gate=1.0):
        qs, e = c
        end = end_ref[(b * n_blocks + blk) * FOX_HEADS + HP * group + e]
        return jnp.exp2(jnp.minimum(cq_ref[0, e][:, cols[qs]] - end, 0.0)) * gate

    def col_partial(p):
        return jnp.sum(p.reshape(p.shape[0] // SUBLANES, SUBLANES, p.shape[1]), axis=0)

    def far_scores(c, n):
        return jnp.dot(keys(c[1], n), operands[c][1], preferred_element_type=F32)

    def contract(c, n, p_bf16, gate=1.0):
        for r in range(R):
            pv = jnp.dot(values_t(c[1], n * R + r), p_bf16[r * SB:(r + 1) * SB],
                         preferred_element_type=F32)
            acc_ref[c] += rest(c, n * R + r, gate) * pv

    def weighted_partial(c, n, p, gate=1.0):
        return sum(rest(c, n * R + r, gate) * col_partial(p[r * SB:(r + 1) * SB])
                   for r in range(R))

    l_part = {}
    for c in chains:
        qs, e = c
        s_t = jnp.dot(keys(e, i0 + qs), operands[c][0], preferred_element_type=F32)
        p = jnp.where(causal, jnp.exp2(s_t), 0.0)
        l_part[c] = col_partial(p)
        v_t = jnp.concatenate([values_t(e, (i0 + qs) * R + r) for r in range(R)], axis=1)
        acc_ref[c] = jnp.dot(v_t, p.astype(BF16), preferred_element_type=F32)
    for c in chains:
        for t in range(c[0]):
            p = jnp.exp2(far_scores(c, i0 + t))
            l_part[c] = l_part[c] + weighted_partial(c, i0 + t, p)
            contract(c, i0 + t, p.astype(BF16))

    first = first_ref[(b * pl.num_programs(1) + group) * pl.num_programs(2) + pl.program_id(2)]
    gate0 = jnp.where(i0 > 0, 1.0, 0.0)
    for c in chains:
        p = jnp.exp2(far_scores(c, first))
        l_part[c] = l_part[c] + weighted_partial(c, first, p, gate0)
        p_ref[c] = p.astype(BF16)

    def body(n, l_run):
        s_new = [far_scores(c, n) for c in chains]
        for c in chains:
            contract(c, n - 1, p_ref[c])
        out = []
        for c, s_t, l_c in zip(chains, s_new, l_run):
            p = jnp.exp2(s_t)
            out.append(l_c + weighted_partial(c, n, p))
            p_ref[c] = p.astype(BF16)
        return tuple(out)

    l_run = lax.fori_loop(first + 1, i0, body, tuple(l_part[c] for c in chains))
    for c in chains:
        contract(c, jnp.maximum(i0 - 1, 0), p_ref[c], gate0)
    for qs in range(QT):
        out_t = [acc_ref[qs, e] * (1.0 / jnp.sum(l_run[qs * HP + e], axis=0, keepdims=True))
                 for e in range(HP)]
        _gated_output(jnp.concatenate(out_t, axis=0), g_ref, o_ref, cols[qs])


def _fox_online_kernel(start_ref, qa_ref, ka_ref, vt_ref, g_ref, o_ref, *, tile, heads):
    TB, HP = tile, heads
    n_blocks = ka_ref.shape[2] // TB
    b = pl.program_id(0)
    group = pl.program_id(1)
    i = pl.program_id(2)
    q_t = [_query_operands(qa_ref[0, e])[0] for e in range(HP)]
    k_row = lax.broadcasted_iota(jnp.int32, (TB, TB), 0)
    q_col = lax.broadcasted_iota(jnp.int32, (TB, TB), 1)

    def step(j, carry, masked):
        out = []
        for e in range(HP):
            m_run, l_run, acc = carry[e]
            k_blk = ka_ref[0, e, pl.ds(pl.multiple_of(j * TB, TB), TB), :]
            s_t = jnp.dot(k_blk, q_t[e], preferred_element_type=F32)
            if masked:
                s_t = jnp.where(k_row <= q_col, s_t, MASK_VALUE)
            hd = HP * group + e
            delta = (start_ref[(b * n_blocks + i) * FOX_HEADS + hd]
                     - start_ref[(b * n_blocks + j) * FOX_HEADS + hd])
            m_new = jnp.maximum(m_run, jnp.max(s_t, axis=0, keepdims=True) + delta)
            p_t = jnp.exp2(s_t - (m_new - delta))
            alpha = jnp.exp2(m_run - m_new)
            l_new = alpha * l_run + jnp.sum(p_t, axis=0, keepdims=True)
            v_t = vt_ref[0, j, pl.ds(e * FOX_HEAD_DIM, FOX_HEAD_DIM), :]
            acc_new = alpha * acc + jnp.dot(v_t, p_t.astype(BF16), preferred_element_type=F32)
            out.append((m_new, l_new, acc_new))
        return tuple(out)

    init = (jnp.full((1, TB), MASK_VALUE, F32), jnp.zeros((1, TB), F32),
            jnp.zeros((FOX_HEAD_DIM, TB), F32))
    carry = lax.fori_loop(0, i, lambda j, c: step(j, c, False), (init,) * HP)
    carry = step(i, carry, True)
    _gated_output(jnp.concatenate([acc / l_run for _, l_run, acc in carry], axis=0), g_ref, o_ref,
                  slice(None))


def _fox_attention(qa, ka, vt, g, cq, edges, *, bounded):
    b, n_heads, s, _ = qa.shape
    _, nvb, w, vb = vt.shape
    if bounded:
        tile, hp, qt = ATTN_Q_TILE, ATTN_HEADS, ATTN_TILES_PER_STEP
    else:
        tile, hp, qt = vb, LANES // FOX_HEAD_DIM, 1
    gw = hp * FOX_HEAD_DIM
    rows = qt * tile
    specs = dict(
        table=pl.BlockSpec(memory_space=pltpu.SMEM),
        q=pl.BlockSpec((1, hp, rows, LANES), lambda bi, p, i: (bi, p, i, 0)),
        k=pl.BlockSpec((1, hp, s, LANES), lambda bi, p, i: (bi, p, 0, 0)),
        v=pl.BlockSpec((1, nvb, gw, vb), lambda bi, p, i: (bi, 0, p, 0)),
        cq=pl.BlockSpec((1, hp, 1, rows), lambda bi, p, i: (bi, p, 0, i)),
        g=pl.BlockSpec((1, rows, gw), lambda bi, p, i: (bi, i, p)))
    if bounded:
        body = functools.partial(_fox_bounded_kernel, tile=tile, heads=hp)
        names = ("table", "table", "q", "k", "v", "cq", "g")
        step_start = edges[:, ::rows // vb, 0, :FOX_HEADS]
        tile_end = edges[:, tile // vb - 1::tile // vb, 1, :FOX_HEADS]
        dead = (step_start[:, :, None] - tile_end[:, None]) < -FOX_DEAD_EXPONENT
        before = jnp.arange(s // rows)[:, None] * qt > jnp.arange(s // tile)[None, :]
        first_live = jnp.sum(dead & before[None, :, :, None], axis=2)
        last_before = jnp.maximum(jnp.arange(s // rows) * qt - 1, 0)
        first_live = jnp.minimum(first_live, last_before[None, :, None])
        first_live = jnp.min(first_live.reshape(b, s // rows, n_heads // hp, hp), axis=-1)
        args = (first_live.transpose(0, 2, 1).reshape(-1).astype(jnp.int32),
                edges[:, :, 1, :FOX_HEADS].reshape(-1), qa, ka, vt, cq, g)
        scratch = [pltpu.VMEM((qt, hp, tile, tile), BF16),
                   pltpu.VMEM((qt, hp, FOX_HEAD_DIM, tile), F32)]
    else:
        body = functools.partial(_fox_online_kernel, tile=tile, heads=hp)
        names = ("table", "q", "k", "v", "g")
        args = (edges[:, :, 0, :FOX_HEADS].reshape(-1), qa, ka, vt, g)
        scratch = []
    return pl.pallas_call(
        body,
        grid=(b, n_heads // hp, s // rows),
        in_specs=[specs[n] for n in names],
        out_specs=specs["g"],
        out_shape=jax.ShapeDtypeStruct((b, s, w), BF16),
        scratch_shapes=scratch,
        compiler_params=_params("arbitrary", "arbitrary", "arbitrary"),
        name="fox_attention_bounded" if bounded else "fox_attention_online",
    )(*args)


def _out_proj_kernel(a_ref, x_ref, mod_ref, w_ref, o_ref):
    y = jnp.dot(a_ref[0], w_ref[...], preferred_element_type=F32)
    o_ref[0] = x_ref[0] + mod_ref[0, 2:3, :] * y


def _out_proj(a, x, mod, w_out):
    b, s, d = x.shape
    tile = 512
    k = a.shape[-1]
    return pl.pallas_call(
        _out_proj_kernel,
        grid=(b, s // tile),
        in_specs=[pl.BlockSpec((1, tile, k), lambda bi, ti: (bi, ti, 0)),
                  pl.BlockSpec((1, tile, d), lambda bi, ti: (bi, ti, 0)),
                  pl.BlockSpec((1, 3, d), lambda bi, ti: (bi, 0, 0)),
                  pl.BlockSpec((k, d), lambda bi, ti: (0, 0))],
        out_specs=pl.BlockSpec((1, tile, d), lambda bi, ti: (bi, ti, 0)),
        out_shape=jax.ShapeDtypeStruct((b, s, d), F32),
        compiler_params=_params("arbitrary", "arbitrary"),
        name="layer1_out_proj",
    )(a, x, mod, w_out.astype(BF16))


def _layer1(x, mod, norm_g, w_in, b_f, qnorm_g, knorm_g, w_out):
    d, w = x.shape[-1], FOX_WIDTH
    order = jnp.argsort(b_f)
    per_head = lambda cols: cols.reshape(d, FOX_HEADS, FOX_HEAD_DIM)[:, order].reshape(d, w)
    w_in = jnp.concatenate([per_head(w_in[:, j * w:(j + 1) * w]) for j in range(4)]
                           + [w_in[:, 4 * w:][:, order]], axis=1)
    b_f = b_f[order]
    w_out = w_out.reshape(FOX_HEADS, FOX_HEAD_DIM, d)[order].reshape(w, d)
    qa, ka, vt, g, cq, edges = _layer1_proj(x, mod, norm_g, w_in, b_f, qnorm_g, knorm_g)
    score_bound = (LOG2E * FOX_HEAD_DIM ** 0.5
                   * jnp.max(jnp.abs(qnorm_g)) * jnp.max(jnp.abs(knorm_g)))
    gated = lax.cond(score_bound <= FOX_SAFE_LOGIT,
                     functools.partial(_fox_attention, bounded=True),
                     functools.partial(_fox_attention, bounded=False),
                     qa, ka, vt, g, cq, edges)
    return _out_proj(gated, x, mod, w_out)


def kernel(x, c, norm_g, ada_w, ada_b, hgrn_lb, even_w_in, hgrn_onorm_g, pool_w, pool_scale,
           even_w_out, odd_w_in, fox_b_f, fox_qnorm_g, fox_knorm_g, odd_w_out):
    depth = norm_g.shape[0]
    mods = _adaln_mods(c, ada_w, ada_b)
    for l in range(depth):
        j = l // 2
        if l % 2 == 0:
            x = _layer0(x, mods[l], norm_g[l], even_w_in[j], hgrn_lb, hgrn_onorm_g[j],
                        pool_w[j], pool_scale[j], even_w_out[j], layer_slot=l)
        else:
            x = _layer1(x, mods[l], norm_g[l], odd_w_in[j], fox_b_f[j], fox_qnorm_g[j],
                        fox_knorm_g[j], odd_w_out[j])
    return x
```

```python
import functools

import jax
import jax.numpy as jnp
import numpy as np
from jax import lax
from jax.experimental import pallas as pl
from jax.experimental.pallas import tpu as pltpu

F32 = jnp.float32
BF16 = jnp.bfloat16
EPS = 1e-6

HGRN_HEADS = 4
HGRN_KEY = 128
HGRN_VAL = 128
HGRN_KW = HGRN_HEADS * HGRN_KEY
HGRN_VW = HGRN_HEADS * HGRN_VAL
POOL_WINDOWS = (2, 4, 8, 16)
POOL_GROUP = 128
POOL_WIDTH = POOL_GROUP * len(POOL_WINDOWS)
POOL_HISTORY = 16
FOX_HEADS = 16
FOX_HEAD_DIM = 64
FOX_WIDTH = FOX_HEADS * FOX_HEAD_DIM

LANES = 128
SUBLANES = 8
VMEM_LIMIT_BYTES = 56 * 1024 * 1024

SEQ_TILE = 256
HGRN_CHUNK = 128
ATTN_Q_TILE = 512
ATTN_HEADS = 4
ATTN_TILES_PER_STEP = 2
FOX_SAFE_LOGIT = 100.0
FOX_DEAD_EXPONENT = 150.0
LOG2E = 1.4426950408889634
MASK_VALUE = -1e30
AUG_LANE = FOX_HEAD_DIM


def _silu(x):
    return x * jax.nn.sigmoid(x)


def _params(*semantics):
    return pltpu.CompilerParams(dimension_semantics=semantics, vmem_limit_bytes=VMEM_LIMIT_BYTES)


def _mod_kernel(c_ref, w_ref, b_ref, o_ref):
    c = c_ref[...]
    cond = _silu(c)
    o_ref[0] = jnp.dot(cond, w_ref[0], preferred_element_type=F32,
                       precision=lax.Precision.HIGHEST) + b_ref[0]


def _adaln_mods(c, ada_w, ada_b):
    depth, d, n = ada_w.shape
    b = c.shape[0]
    rows = SUBLANES
    c_rows = jnp.zeros((rows, d), F32).at[:b].set(c)
    tn = 1024
    out = pl.pallas_call(
        _mod_kernel,
        grid=(depth, n // tn),
        in_specs=[pl.BlockSpec((rows, d), lambda l, j: (0, 0)),
                  pl.BlockSpec((1, d, tn), lambda l, j: (l, 0, j)),
                  pl.BlockSpec((1, 1, tn), lambda l, j: (l, 0, j))],
        out_specs=pl.BlockSpec((1, rows, tn), lambda l, j: (l, 0, j)),
        out_shape=jax.ShapeDtypeStruct((depth, rows, n), F32),
        compiler_params=_params("arbitrary", "arbitrary"),
        name="adaln_mods",
    )(c_rows, ada_w, ada_b.reshape(depth, 1, n))
    return out[:, :b].reshape(depth, b, 3, d)


def _modulated_norm(x, mod_ref, ng_ref):
    shift = mod_ref[0, 0:1, :]
    scale = mod_ref[0, 1:2, :]
    ms = jnp.mean(x * x, axis=-1, keepdims=True)
    return x * lax.rsqrt(ms + EPS) * ng_ref[...] * (1.0 + scale) + shift


def _layer0_kernel(x_ref, mod_ref, ng_ref, win_ref, lb_ref, og_ref, pw_ref, ps_ref, wout_ref,
                   o_ref, state_ref, carry_ref, *, tile, chunk, layer_slot):
    T, C, H = tile, chunk, HGRN_HEADS
    t_idx = pl.program_id(1)

    @pl.when(t_idx == 0)
    def _():
        state_ref[...] = jnp.zeros_like(state_ref)
        carry_ref[...] = jnp.zeros_like(carry_ref)

    x = x_ref[0]
    h = _modulated_norm(x, mod_ref, ng_ref)
    proj = jnp.dot(h.astype(BF16), win_ref[...], preferred_element_type=F32)
    o0 = 0
    q = proj[:, o0:o0 + HGRN_KW]; o0 += HGRN_KW
    f = proj[:, o0:o0 + HGRN_KW]; o0 += HGRN_KW
    val = proj[:, o0:o0 + HGRN_VW]; o0 += HGRN_VW
    g_a = proj[:, o0:o0 + HGRN_VW]; o0 += HGRN_VW
    u = proj[:, o0:o0 + POOL_WIDTH]; o0 += POOL_WIDTH
    g_b = proj[:, o0:o0 + POOL_WIDTH]

    lbv = lb_ref[...]
    e = jnp.exp(lbv - jnp.max(lbv, axis=0, keepdims=True))
    lower = (jnp.sum(e[0:layer_slot + 1], axis=0, keepdims=True)
             / jnp.sum(e, axis=0, keepdims=True))
    forget = lower + (1.0 - lower) * jax.nn.sigmoid(f)
    logf = jnp.log(forget)
    key = 1.0 - forget

    row = lax.broadcasted_iota(jnp.int32, (T, HGRN_KW), 0)
    ti = lax.broadcasted_iota(jnp.int32, (C, C), 0)
    si = lax.broadcasted_iota(jnp.int32, (C, C), 1)
    n_chunks = T // C
    nt_dims = (((1,), (1,)), ((), ()))

    def level_scores(qe, ke, mask, scores):
        qb = qe.astype(BF16)
        kb = ke.astype(BF16)
        out = []
        for c in range(n_chunks):
            for hh in range(H):
                rs = slice(c * C, (c + 1) * C)
                cs = slice(hh * HGRN_KEY, (hh + 1) * HGRN_KEY)
                d = lax.dot_general(qb[rs, cs], kb[rs, cs], nt_dims, preferred_element_type=F32)
                prev = scores[c * H + hh]
                out.append(jnp.where(mask, d, 0.0 if prev is None else prev))
        return out

    scores = level_scores(q, key, ti == si, [None] * (n_chunks * H))
    p_sum = logf
    q_sum = jnp.zeros_like(logf)
    total = logf
    m = 1
    while m < C:
        mask = ((ti ^ si) < 2 * m) & ((ti & m) != 0) & ((si & m) == 0)
        scores = level_scores(q * jnp.exp(p_sum), key * jnp.exp(q_sum), mask, scores)
        upper = (row & m) != 0
        t_dn = pltpu.roll(total, m, 0)
        t_up = pltpu.roll(total, T - m, 0)
        p_sum = p_sum + jnp.where(upper, t_dn, 0.0)
        q_sum = q_sum + jnp.where(upper, 0.0, t_up)
        total = total + jnp.where(upper, t_dn, t_up)
        m *= 2

    q_in = (q * jnp.exp(p_sum)).astype(BF16)
    k_out = (key * jnp.exp(q_sum)).astype(BF16)
    decay = jnp.exp(total)
    val_b = val.astype(BF16)
    tn_dims = (((0,), (0,)), ((), ()))
    oa_rows = []
    for c in range(n_chunks):
        rs = slice(c * C, (c + 1) * C)
        heads_out = []
        for hh in range(H):
            cs = slice(hh * HGRN_KEY, (hh + 1) * HGRN_KEY)
            st = state_ref[hh]
            vb = val_b[rs, cs]
            o = jnp.dot(scores[c * H + hh].astype(BF16), vb, preferred_element_type=F32)
            o = o + lax.dot_general(q_in[rs, cs], st.astype(BF16), nt_dims,
                                    preferred_element_type=F32)
            state_ref[hh] = (st * decay[c * C:c * C + 1, cs]
                             + lax.dot_general(vb, k_out[rs, cs], tn_dims,
                                               preferred_element_type=F32))
            ms_o = jnp.mean(o * o, axis=-1, keepdims=True)
            heads_out.append(o * lax.rsqrt(ms_o + EPS))
        oa_rows.append(jnp.concatenate(heads_out, axis=1))
    o_a = jnp.concatenate(oa_rows, axis=0) * og_ref[...] * _silu(g_a)

    ub = jnp.concatenate([carry_ref[...], u], axis=0)
    carry_ref[...] = u[T - POOL_HISTORY:T, :]
    wins = []
    acc = ub
    sh = 1
    while sh < max(POOL_WINDOWS):
        acc = acc + pltpu.roll(acc, sh, 0)
        sh *= 2
        wins.append(acc)
    pos = lax.broadcasted_iota(jnp.int32, (T, POOL_GROUP), 0) + t_idx * T + 1
    ob = []
    for gi, w in enumerate(POOL_WINDOWS):
        cs = slice(gi * POOL_GROUP, (gi + 1) * POOL_GROUP)
        win = wins[w.bit_length() - 2][POOL_HISTORY:, cs]
        cnt = jnp.minimum(pos, w).astype(F32)
        pooled = win / cnt - u[:, cs]
        ob.append(jnp.dot(pooled.astype(BF16), pw_ref[gi], preferred_element_type=F32))
    o_b = jnp.concatenate(ob, axis=1) * ps_ref[...] * _silu(g_b)

    mixed = jnp.concatenate([o_a, o_b], axis=1).astype(BF16)
    y = jnp.dot(mixed, wout_ref[...], preferred_element_type=F32)
    o_ref[0] = x + mod_ref[0, 2:3, :] * y


def _layer0(x, mod, norm_g, w_in, hgrn_lb, onorm_g, pool_w, pool_scale, w_out, layer_slot):
    b, s, d = x.shape
    tile = SEQ_TILE
    n_in = w_in.shape[1]
    mix = w_out.shape[0]
    const2 = lambda bi, ti: (0, 0)
    return pl.pallas_call(
        functools.partial(_layer0_kernel, tile=tile, chunk=HGRN_CHUNK, layer_slot=layer_slot),
        grid=(b, s // tile),
        in_specs=[pl.BlockSpec((1, tile, d), lambda bi, ti: (bi, ti, 0)),
                  pl.BlockSpec((1, 3, d), lambda bi, ti: (bi, 0, 0)),
                  pl.BlockSpec((1, d), const2),
                  pl.BlockSpec((d, n_in), const2),
                  pl.BlockSpec(hgrn_lb.shape, const2),
                  pl.BlockSpec((1, HGRN_VW), const2),
                  pl.BlockSpec(pool_w.shape, lambda bi, ti: (0, 0, 0)),
                  pl.BlockSpec((1, POOL_WIDTH), const2),
                  pl.BlockSpec((mix, d), const2)],
        out_specs=pl.BlockSpec((1, tile, d), lambda bi, ti: (bi, ti, 0)),
        out_shape=jax.ShapeDtypeStruct((b, s, d), F32),
        scratch_shapes=[pltpu.VMEM((HGRN_HEADS, HGRN_VAL, HGRN_KEY), F32),
                        pltpu.VMEM((POOL_HISTORY, POOL_WIDTH), F32)],
        compiler_params=_params("arbitrary", "arbitrary"),
        name="layer0_hgrn_pool",
    )(x, mod, norm_g.reshape(1, d), w_in.astype(BF16), hgrn_lb, onorm_g.reshape(1, HGRN_VW),
      pool_w.astype(BF16), pool_scale.reshape(1, POOL_WIDTH), w_out.astype(BF16))


def _split3(c):
    hi = c.astype(BF16).astype(F32)
    r = c - hi
    mid = r.astype(BF16).astype(F32)
    lo = (r - mid).astype(BF16).astype(F32)
    return hi, mid, lo


def _carrier_selectors():
    sel = np.zeros((FOX_HEADS, LANES, 2 * LANES), np.float32)
    one_lane = 6 * FOX_HEADS
    for hd in range(FOX_HEADS):
        for j in range(3):
            sel[hd, j * FOX_HEADS + hd, AUG_LANE + j] = 1.0
            sel[hd, j * FOX_HEADS + hd, LANES + AUG_LANE + 3 + j] = -1.0
            sel[hd, (3 + j) * FOX_HEADS + hd, LANES + AUG_LANE + 6 + j] = -1.0
        sel[hd, one_lane, AUG_LANE + 3:AUG_LANE + 9] = 1.0
        sel[hd, one_lane, LANES + AUG_LANE:LANES + AUG_LANE + 3] = 1.0
    return jnp.asarray(sel, BF16)


def _layer1_proj_kernel(x_ref, mod_ref, ng_ref, win_ref, bf_ref, qg_ref, kg_ref, bd_ref, sel_ref,
                        qa_ref, ka_ref, vt_ref, g_ref, cq_ref, edge_ref, carry_ref, tile_start_ref,
                        *, tile):
    T, W = tile, FOX_WIDTH
    t_idx = pl.program_id(1)

    @pl.when(t_idx == 0)
    def _():
        carry_ref[...] = jnp.zeros_like(carry_ref)

    x = x_ref[0]
    h = _modulated_norm(x, mod_ref, ng_ref)
    proj = jnp.dot(h.astype(BF16), win_ref[...], preferred_element_type=F32)
    q = proj[:, 0:W]
    k = proj[:, W:2 * W]
    v = proj[:, 2 * W:3 * W]
    g_ref[0] = proj[:, 3 * W:4 * W]
    fl = proj[:, 4 * W:4 * W + LANES]
    vt_ref[0, 0] = v.T.astype(BF16)

    z = fl + bf_ref[...]
    logf = jnp.minimum(z, 0.0) - jnp.log1p(jnp.exp(-jnp.abs(z)))
    row = lax.broadcasted_iota(jnp.int32, (T, LANES), 0)
    cum = logf
    sh = 1
    while sh < T:
        cum = cum + jnp.where(row >= sh, pltpu.roll(cum, sh, 0), 0.0)
        sh *= 2
    start = carry_ref[...]
    end = start + cum[T - 1:T, :]
    carry_ref[...] = end

    @pl.when(t_idx % (ATTN_Q_TILE // T) == 0)
    def _():
        tile_start_ref[...] = start

    tile_start = tile_start_ref[...]
    edge_ref[0, 0] = jnp.concatenate([tile_start, end], axis=0) * LOG2E
    s_hi, s_mid, s_lo = _split3((cum + (start - tile_start)) * LOG2E)
    e_hi, e_mid, e_lo = _split3((cum - cum[T - 1:T, :]) * LOG2E)
    total_t = ((cum + start) * LOG2E).T
    for hd in range(FOX_HEADS):
        cq_ref[0, hd] = total_t[hd:hd + 1, :]

    lane = lax.broadcasted_iota(jnp.int32, (T, LANES), 1)
    terms = (s_hi, s_mid, s_lo, e_hi, e_mid, e_lo)
    packed = jnp.where(lane == len(terms) * FOX_HEADS, 1.0, 0.0)
    for j, term in enumerate(terms):
        moved = term if j == 0 else pltpu.roll(term, j * FOX_HEADS, 1)
        packed = jnp.where((lane >= j * FOX_HEADS) & (lane < (j + 1) * FOX_HEADS), moved, packed)
    packed = packed.astype(BF16)
    bd = bd_ref[...]
    qk_scale = FOX_HEAD_DIM ** -0.5 * LOG2E
    for pair in range(FOX_HEADS // 2):
        cs = slice(pair * LANES, (pair + 1) * LANES)
        qp = q[:, cs]
        kp = k[:, cs]
        squares = jnp.concatenate([(qp * qp).astype(BF16), (kp * kp).astype(BF16)], axis=1)
        ms = jnp.dot(squares, bd, preferred_element_type=F32)
        qn = qp * lax.rsqrt(ms[:, :LANES] + EPS) * (qg_ref[...] * qk_scale)
        kn = kp * lax.rsqrt(ms[:, LANES:] + EPS) * kg_ref[...]
        for e in range(LANES // FOX_HEAD_DIM):
            hd = (LANES // FOX_HEAD_DIM) * pair + e
            aug = jnp.dot(packed, sel_ref[hd], preferred_element_type=F32)
            aug_q, aug_k = aug[:, :LANES], aug[:, LANES:]
            q_main = qn if e == 0 else pltpu.roll(qn, FOX_HEAD_DIM, 1)
            k_main = kn if e == 0 else pltpu.roll(kn, FOX_HEAD_DIM, 1)
            qa_ref[0, hd] = jnp.where(lane < FOX_HEAD_DIM, q_main, aug_q).astype(BF16)
            ka_ref[0, hd] = jnp.where(lane < FOX_HEAD_DIM, k_main, aug_k).astype(BF16)


def _layer1_proj(x, mod, norm_g, w_in, b_f, qnorm_g, knorm_g):
    b, s, d = x.shape
    tile = SEQ_TILE
    nt = s // tile
    w = FOX_WIDTH
    n_pad = 4 * w + LANES
    w_pad = jnp.zeros((d, n_pad), BF16).at[:, :w_in.shape[1]].set(w_in.astype(BF16))
    bf_pad = jnp.zeros((1, LANES), F32).at[0, :FOX_HEADS].set(b_f)
    blk = jnp.arange(2 * LANES) // FOX_HEAD_DIM
    bd = jnp.where(blk[:, None] == blk[None, :], 1.0 / FOX_HEAD_DIM, 0.0).astype(BF16)
    qg = jnp.tile(qnorm_g, LANES // FOX_HEAD_DIM).reshape(1, LANES)
    kg = jnp.tile(knorm_g, LANES // FOX_HEAD_DIM).reshape(1, LANES)
    const2 = lambda bi, ti: (0, 0)
    return pl.pallas_call(
        functools.partial(_layer1_proj_kernel, tile=tile),
        grid=(b, nt),
        in_specs=[pl.BlockSpec((1, tile, d), lambda bi, ti: (bi, ti, 0)),
                  pl.BlockSpec((1, 3, d), lambda bi, ti: (bi, 0, 0)),
                  pl.BlockSpec((1, d), const2),
                  pl.BlockSpec((d, n_pad), const2),
                  pl.BlockSpec((1, LANES), const2),
                  pl.BlockSpec((1, LANES), const2),
                  pl.BlockSpec((1, LANES), const2),
                  pl.BlockSpec((2 * LANES, 2 * LANES), const2),
                  pl.BlockSpec((FOX_HEADS, LANES, 2 * LANES), lambda bi, ti: (0, 0, 0))],
        out_specs=[pl.BlockSpec((1, FOX_HEADS, tile, LANES), lambda bi, ti: (bi, 0, ti, 0)),
                   pl.BlockSpec((1, FOX_HEADS, tile, LANES), lambda bi, ti: (bi, 0, ti, 0)),
                   pl.BlockSpec((1, 1, w, tile), lambda bi, ti: (bi, ti, 0, 0)),
                   pl.BlockSpec((1, tile, w), lambda bi, ti: (bi, ti, 0)),
                   pl.BlockSpec((1, FOX_HEADS, 1, tile), lambda bi, ti: (bi, 0, 0, ti)),
                   pl.BlockSpec((1, 1, 2, LANES), lambda bi, ti: (bi, ti, 0, 0))],
        out_shape=[jax.ShapeDtypeStruct((b, FOX_HEADS, s, LANES), BF16),
                   jax.ShapeDtypeStruct((b, FOX_HEADS, s, LANES), BF16),
                   jax.ShapeDtypeStruct((b, nt, w, tile), BF16),
                   jax.ShapeDtypeStruct((b, s, w), F32),
                   jax.ShapeDtypeStruct((b, FOX_HEADS, 1, s), F32),
                   jax.ShapeDtypeStruct((b, nt, 2, LANES), F32)],
        scratch_shapes=[pltpu.VMEM((1, LANES), F32), pltpu.VMEM((1, LANES), F32)],
        compiler_params=_params("arbitrary", "arbitrary"),
        name="layer1_proj",
    )(x, mod, norm_g.reshape(1, d), w_pad, bf_pad, qg, kg, bd, _carrier_selectors())


def _query_operands(q):
    slab = 2 * SUBLANES
    row = lax.broadcasted_iota(jnp.int32, (slab, q.shape[0]), 0)
    q_t = q.astype(F32).T
    carriers = q_t[AUG_LANE:AUG_LANE + slab]
    return tuple(jnp.concatenate([q_t[:AUG_LANE], jnp.where(keep, carriers, 0.0),
                                  q_t[AUG_LANE + slab:]], axis=0).astype(BF16)
                 for keep in (row < 6, row >= 6))


def _gated_output(acc_t, g_ref, o_ref, rows):
    o_ref[0, rows] = (acc_t.T * _silu(g_ref[0, rows])).astype(BF16)


def _fox_bounded_kernel(first_ref, end_ref, qa_ref, ka_ref, vt_ref, cq_ref, g_ref, o_ref,
                        p_ref, acc_ref, *, tile, heads):
    TQ, HP = tile, heads
    QT = qa_ref.shape[2] // TQ
    SB = vt_ref.shape[-1]
    R = TQ // SB
    n_blocks = ka_ref.shape[2] // SB
    b = pl.program_id(0)
    group = pl.program_id(1)
    i0 = pl.program_id(2) * QT
    chains = [(qs, e) for qs in range(QT) for e in range(HP)]
    cols = [slice(qs * TQ, (qs + 1) * TQ) for qs in range(QT)]
    operands = {(qs, e): _query_operands(qa_ref[0, e, cols[qs], :]) for qs, e in chains}
    causal = (lax.broadcasted_iota(jnp.int32, (TQ, TQ), 0)
              <= lax.broadcasted_iota(jnp.int32, (TQ, TQ), 1))

    def keys(e, n):
        return ka_ref[0, e, pl.ds(pl.multiple_of(n * TQ, TQ), TQ), :]

    def values_t(e, blk):
        return vt_ref[0, blk, pl.ds(e * FOX_HEAD_DIM, FOX_HEAD_DIM), :]

    def rest(c, blk, gate=1.0):
        qs, e = c
        end = end_ref[(b * n_blocks + blk) * FOX_HEADS + HP * group + e]
        return jnp.exp2(jnp.minimum(cq_ref[0, e][:, cols[qs]] - end, 0.0)) * gate

    def col_partial(p):
        return jnp.sum(p.reshape(p.shape[0] // SUBLANES, SUBLANES, p.shape[1]), axis=0)

    def far_scores(c, n):
        return jnp.dot(keys(c[1], n), operands[c][1], preferred_element_type=F32)

    def contract(c, n, p_bf16, gate=1.0):
        for r in range(R):
            pv = jnp.dot(values_t(c[1], n * R + r), p_bf16[r * SB:(r + 1) * SB],
                         preferred_element_type=F32)
            acc_ref[c] += rest(c, n * R + r, gate) * pv

    def weighted_partial(c, n, p, gate=1.0):
        return sum(rest(c, n * R + r, gate) * col_partial(p[r * SB:(r + 1) * SB])
                   for r in range(R))

    l_part = {}
    for c in chains:
        qs, e = c
        s_t = jnp.dot(keys(e, i0 + qs), operands[c][0], preferred_element_type=F32)
        p = jnp.where(causal, jnp.exp2(s_t), 0.0)
        l_part[c] = col_partial(p)
        v_t = jnp.concatenate([values_t(e, (i0 + qs) * R + r) for r in range(R)], axis=1)
        acc_ref[c] = jnp.dot(v_t, p.astype(BF16), preferred_element_type=F32)
    for c in chains:
        for t in range(c[0]):
            p = jnp.exp2(far_scores(c, i0 + t))
            l_part[c] = l_part[c] + weighted_partial(c, i0 + t, p)
            contract(c, i0 + t, p.astype(BF16))

    first = first_ref[(b * pl.num_programs(1) + group) * pl.num_programs(2) + pl.program_id(2)]
    gate0 = jnp.where(i0 > 0, 1.0, 0.0)
    for c in chains:
        p = jnp.exp2(far_scores(c, first))
        l_part[c] = l_part[c] + weighted_partial(c, first, p, gate0)
        p_ref[c] = p.astype(BF16)

    def body(n, l_run):
        s_new = [far_scores(c, n) for c in chains]
        for c in chains:
            contract(c, n - 1, p_ref[c])
        out = []
        for c, s_t, l_c in zip(chains, s_new, l_run):
            p = jnp.exp2(s_t)
            out.append(l_c + weighted_partial(c, n, p))
            p_ref[c] = p.astype(BF16)
        return tuple(out)

    l_run = lax.fori_loop(first + 1, i0, body, tuple(l_part[c] for c in chains))
    for c in chains:
        contract(c, jnp.maximum(i0 - 1, 0), p_ref[c], gate0)
    for qs in range(QT):
        out_t = [acc_ref[qs, e] * (1.0 / jnp.sum(l_run[qs * HP + e], axis=0, keepdims=True))
                 for e in range(HP)]
        _gated_output(jnp.concatenate(out_t, axis=0), g_ref, o_ref, cols[qs])


def _fox_online_kernel(start_ref, qa_ref, ka_ref, vt_ref, g_ref, o_ref, *, tile, heads):
    TB, HP = tile, heads
    n_blocks = ka_ref.shape[2] // TB
    b = pl.program_id(0)
    group = pl.program_id(1)
    i = pl.program_id(2)
    q_t = [_query_operands(qa_ref[0, e])[0] for e in range(HP)]
    k_row = lax.broadcasted_iota(jnp.int32, (TB, TB), 0)
    q_col = lax.broadcasted_iota(jnp.int32, (TB, TB), 1)

    def step(j, carry, masked):
        out = []
        for e in range(HP):
            m_run, l_run, acc = carry[e]
            k_blk = ka_ref[0, e, pl.ds(pl.multiple_of(j * TB, TB), TB), :]
            s_t = jnp.dot(k_blk, q_t[e], preferred_element_type=F32)
            if masked:
                s_t = jnp.where(k_row <= q_col, s_t, MASK_VALUE)
            hd = HP * group + e
            delta = (start_ref[(b * n_blocks + i) * FOX_HEADS + hd]
                     - start_ref[(b * n_blocks + j) * FOX_HEADS + hd])
            m_new = jnp.maximum(m_run, jnp.max(s_t, axis=0, keepdims=True) + delta)
            p_t = jnp.exp2(s_t - (m_new - delta))
            alpha = jnp.exp2(m_run - m_new)
            l_new = alpha * l_run + jnp.sum(p_t, axis=0, keepdims=True)
            v_t = vt_ref[0, j, pl.ds(e * FOX_HEAD_DIM, FOX_HEAD_DIM), :]
            acc_new = alpha * acc + jnp.dot(v_t, p_t.astype(BF16), preferred_element_type=F32)
            out.append((m_new, l_new, acc_new))
        return tuple(out)

    init = (jnp.full((1, TB), MASK_VALUE, F32), jnp.zeros((1, TB), F32),
            jnp.zeros((FOX_HEAD_DIM, TB), F32))
    carry = lax.fori_loop(0, i, lambda j, c: step(j, c, False), (init,) * HP)
    carry = step(i, carry, True)
    _gated_output(jnp.concatenate([acc / l_run for _, l_run, acc in carry], axis=0), g_ref, o_ref,
                  slice(None))


def _fox_attention(qa, ka, vt, g, cq, edges, *, bounded):
    b, n_heads, s, _ = qa.shape
    _, nvb, w, vb = vt.shape
    if bounded:
        tile, hp, qt = ATTN_Q_TILE, ATTN_HEADS, ATTN_TILES_PER_STEP
    else:
        tile, hp, qt = vb, LANES // FOX_HEAD_DIM, 1
    gw = hp * FOX_HEAD_DIM
    rows = qt * tile
    specs = dict(
        table=pl.BlockSpec(memory_space=pltpu.SMEM),
        q=pl.BlockSpec((1, hp, rows, LANES), lambda bi, p, i: (bi, p, i, 0)),
        k=pl.BlockSpec((1, hp, s, LANES), lambda bi, p, i: (bi, p, 0, 0)),
        v=pl.BlockSpec((1, nvb, gw, vb), lambda bi, p, i: (bi, 0, p, 0)),
        cq=pl.BlockSpec((1, hp, 1, rows), lambda bi, p, i: (bi, p, 0, i)),
        g=pl.BlockSpec((1, rows, gw), lambda bi, p, i: (bi, i, p)))
    if bounded:
        body = functools.partial(_fox_bounded_kernel, tile=tile, heads=hp)
        names = ("table", "table", "q", "k", "v", "cq", "g")
        step_start = edges[:, ::rows // vb, 0, :FOX_HEADS]
        tile_end = edges[:, tile // vb - 1::tile // vb, 1, :FOX_HEADS]
        dead = (step_start[:, :, None] - tile_end[:, None]) < -FOX_DEAD_EXPONENT
        before = jnp.arange(s // rows)[:, None] * qt > jnp.arange(s // tile)[None, :]
        first_live = jnp.sum(dead & before[None, :, :, None], axis=2)
        last_before = jnp.maximum(jnp.arange(s // rows) * qt - 1, 0)
        first_live = jnp.minimum(first_live, last_before[None, :, None])
        first_live = jnp.min(first_live.reshape(b, s // rows, n_heads // hp, hp), axis=-1)
        args = (first_live.transpose(0, 2, 1).reshape(-1).astype(jnp.int32),
                edges[:, :, 1, :FOX_HEADS].reshape(-1), qa, ka, vt, cq, g)
        scratch = [pltpu.VMEM((qt, hp, tile, tile), BF16),
                   pltpu.VMEM((qt, hp, FOX_HEAD_DIM, tile), F32)]
    else:
        body = functools.partial(_fox_online_kernel, tile=tile, heads=hp)
        names = ("table", "q", "k", "v", "g")
        args = (edges[:, :, 0, :FOX_HEADS].reshape(-1), qa, ka, vt, g)
        scratch = []
    return pl.pallas_call(
        body,
        grid=(b, n_heads // hp, s // rows),
        in_specs=[specs[n] for n in names],
        out_specs=specs["g"],
        out_shape=jax.ShapeDtypeStruct((b, s, w), BF16),
        scratch_shapes=scratch,
        compiler_params=_params("arbitrary", "arbitrary", "arbitrary"),
        name="fox_attention_bounded" if bounded else "fox_attention_online",
    )(*args)


def _out_proj_kernel(a_ref, x_ref, mod_ref, w_ref, o_ref):
    y = jnp.dot(a_ref[0], w_ref[...], preferred_element_type=F32)
    o_ref[0] = x_ref[0] + mod_ref[0, 2:3, :] * y


def _out_proj(a, x, mod, w_out):
    b, s, d = x.shape
    tile = 512
    k = a.shape[-1]
    return pl.pallas_call(
        _out_proj_kernel,
        grid=(b, s // tile),
        in_specs=[pl.BlockSpec((1, tile, k), lambda bi, ti: (bi, ti, 0)),
                  pl.BlockSpec((1, tile, d), lambda bi, ti: (bi, ti, 0)),
                  pl.BlockSpec((1, 3, d), lambda bi, ti: (bi, 0, 0)),
                  pl.BlockSpec((k, d), lambda bi, ti: (0, 0))],
        out_specs=pl.BlockSpec((1, tile, d), lambda bi, ti: (bi, ti, 0)),
        out_shape=jax.ShapeDtypeStruct((b, s, d), F32),
        compiler_params=_params("arbitrary", "arbitrary"),
        name="layer1_out_proj",
    )(a, x, mod, w_out.astype(BF16))


def _layer1(x, mod, norm_g, w_in, b_f, qnorm_g, knorm_g, w_out):
    d, w = x.shape[-1], FOX_WIDTH
    order = jnp.argsort(b_f)
    per_head = lambda cols: cols.reshape(d, FOX_HEADS, FOX_HEAD_DIM)[:, order].reshape(d, w)
    w_in = jnp.concatenate([per_head(w_in[:, j * w:(j + 1) * w]) for j in range(4)]
                           + [w_in[:, 4 * w:][:, order]], axis=1)
    b_f = b_f[order]
    w_out = w_out.reshape(FOX_HEADS, FOX_HEAD_DIM, d)[order].reshape(w, d)
    qa, ka, vt, g, cq, edges = _layer1_proj(x, mod, norm_g, w_in, b_f, qnorm_g, knorm_g)
    score_bound = (LOG2E * FOX_HEAD_DIM ** 0.5
                   * jnp.max(jnp.abs(qnorm_g)) * jnp.max(jnp.abs(knorm_g)))
    gated = lax.cond(score_bound <= FOX_SAFE_LOGIT,
                     functools.partial(_fox_attention, bounded=True),
                     functools.partial(_fox_attention, bounded=False),
                     qa, ka, vt, g, cq, edges)
    return _out_proj(gated, x, mod, w_out)


def kernel(x, c, norm_g, ada_w, ada_b, hgrn_lb, even_w_in, hgrn_onorm_g, pool_w, pool_scale,
           even_w_out, odd_w_in, fox_b_f, fox_qnorm_g, fox_knorm_g, odd_w_out):
    depth = norm_g.shape[0]
    mods = _adaln_mods(c, ada_w, ada_b)
    for l in range(depth):
        j = l // 2
        if l % 2 == 0:
            x = _layer0(x, mods[l], norm_g[l], even_w_in[j], hgrn_lb, hgrn_onorm_g[j],
                        pool_w[j], pool_scale[j], even_w_out[j], layer_slot=l)
        else:
            x = _layer1(x, mods[l], norm_g[l], odd_w_in[j], fox_b_f[j], fox_qnorm_g[j],
                        fox_knorm_g[j], odd_w_out[j])
    return x
```

```python
import functools

import jax
import jax.numpy as jnp
import numpy as np
from jax import lax
from jax.experimental import pallas as pl
from jax.experimental.pallas import tpu as pltpu

F32 = jnp.float32
BF16 = jnp.bfloat16
EPS = 1e-6

HGRN_HEADS = 4
HGRN_KEY = 128
HGRN_VAL = 128
HGRN_KW = HGRN_HEADS * HGRN_KEY
HGRN_VW = HGRN_HEADS * HGRN_VAL
POOL_WINDOWS = (2, 4, 8, 16)
POOL_GROUP = 128
POOL_WIDTH = POOL_GROUP * len(POOL_WINDOWS)
POOL_HISTORY = 16
FOX_HEADS = 16
FOX_HEAD_DIM = 64
FOX_WIDTH = FOX_HEADS * FOX_HEAD_DIM

LANES = 128
SUBLANES = 8
VMEM_LIMIT_BYTES = 56 * 1024 * 1024

SEQ_TILE = 256
HGRN_CHUNK = 128
ATTN_Q_TILE = 512
ATTN_HEADS = 4
ATTN_TILES_PER_STEP = 2
FOX_SAFE_LOGIT = 100.0
FOX_DEAD_EXPONENT = 150.0
LOG2E = 1.4426950408889634
MASK_VALUE = -1e30
AUG_LANE = FOX_HEAD_DIM


def _silu(x):
    return x * jax.nn.sigmoid(x)


def _params(*semantics):
    return pltpu.CompilerParams(dimension_semantics=semantics, vmem_limit_bytes=VMEM_LIMIT_BYTES)


def _mod_kernel(c_ref, w_ref, b_ref, o_ref):
    c = c_ref[...]
    cond = _silu(c)
    o_ref[0] = jnp.dot(cond, w_ref[0], preferred_element_type=F32,
                       precision=lax.Precision.HIGHEST) + b_ref[0]


def _adaln_mods(c, ada_w, ada_b):
    depth, d, n = ada_w.shape
    b = c.shape[0]
    rows = SUBLANES
    c_rows = jnp.zeros((rows, d), F32).at[:b].set(c)
    tn = 1024
    out = pl.pallas_call(
        _mod_kernel,
        grid=(depth, n // tn),
        in_specs=[pl.BlockSpec((rows, d), lambda l, j: (0, 0)),
                  pl.BlockSpec((1, d, tn), lambda l, j: (l, 0, j)),
                  pl.BlockSpec((1, 1, tn), lambda l, j: (l, 0, j))],
        out_specs=pl.BlockSpec((1, rows, tn), lambda l, j: (l, 0, j)),
        out_shape=jax.ShapeDtypeStruct((depth, rows, n), F32),
        compiler_params=_params("arbitrary", "arbitrary"),
        name="adaln_mods",
    )(c_rows, ada_w, ada_b.reshape(depth, 1, n))
    return out[:, :b].reshape(depth, b, 3, d)


def _modulated_norm(x, shift, scale, gain):
    ms = jnp.mean(x * x, axis=-1, keepdims=True)
    return x * lax.rsqrt(ms + EPS) * gain * (1.0 + scale) + shift


def _layer0_kernel(x_ref, mod_ref, ng_ref, win_ref, lb_ref, og_ref, pw_ref, ps_ref, wout_ref,
                   o_ref, state_ref, carry_ref, *, tile, chunk, layer_slot):
    T, C, H = tile, chunk, HGRN_HEADS
    batch = range(x_ref.shape[0])
    t_idx = pl.program_id(0)

    @pl.when(t_idx == 0)
    def _():
        state_ref[...] = jnp.zeros_like(state_ref)
        carry_ref[...] = jnp.zeros_like(carry_ref)

    lbv = lb_ref[...]
    e = jnp.exp(lbv - jnp.max(lbv, axis=0, keepdims=True))
    lower = (jnp.sum(e[0:layer_slot + 1], axis=0, keepdims=True)
             / jnp.sum(e, axis=0, keepdims=True))

    xs, parts = [], []
    for bi in batch:
        x = x_ref[bi]
        h = _modulated_norm(x, mod_ref[bi, 0:1, :], mod_ref[bi, 1:2, :], ng_ref[...])
        proj = jnp.dot(h.astype(BF16), win_ref[...], preferred_element_type=F32)
        o0, widths, piece = 0, (HGRN_KW, HGRN_KW, HGRN_VW, HGRN_VW, POOL_WIDTH, POOL_WIDTH), []
        for wd in widths:
            piece.append(proj[:, o0:o0 + wd])
            o0 += wd
        xs.append(x)
        parts.append(piece)

    row = lax.broadcasted_iota(jnp.int32, (T, HGRN_KW), 0)
    ti = lax.broadcasted_iota(jnp.int32, (C, C), 0)
    si = lax.broadcasted_iota(jnp.int32, (C, C), 1)
    n_chunks = T // C
    nt_dims = (((1,), (1,)), ((), ()))
    tn_dims = (((0,), (0,)), ((), ()))

    def level_scores(qe, ke, mask, scores):
        qb = qe.astype(BF16)
        kb = ke.astype(BF16)
        out = []
        for c in range(n_chunks):
            for hh in range(H):
                rs = slice(c * C, (c + 1) * C)
                cs = slice(hh * HGRN_KEY, (hh + 1) * HGRN_KEY)
                d = lax.dot_general(qb[rs, cs], kb[rs, cs], nt_dims, preferred_element_type=F32)
                prev = scores[c * H + hh]
                out.append(jnp.where(mask, d, 0.0 if prev is None else prev))
        return out

    keys, scores, p_sum, q_sum, total = [], [], [], [], []
    for bi in batch:
        q, f = parts[bi][0], parts[bi][1]
        forget = lower + (1.0 - lower) * jax.nn.sigmoid(f)
        logf = jnp.log(forget) * LOG2E
        keys.append(1.0 - forget)
        scores.append(level_scores(q, keys[bi], ti == si, [None] * (n_chunks * H)))
        p_sum.append(logf)
        q_sum.append(jnp.zeros_like(logf))
        total.append(logf)
    m = 1
    while m < C:
        mask = ((ti ^ si) < 2 * m) & ((ti & m) != 0) & ((si & m) == 0)
        upper = (row & m) != 0
        for bi in batch:
            scores[bi] = level_scores(parts[bi][0] * jnp.exp2(p_sum[bi]),
                                      keys[bi] * jnp.exp2(q_sum[bi]), mask, scores[bi])
            t_dn = pltpu.roll(total[bi], m, 0)
            t_up = pltpu.roll(total[bi], T - m, 0)
            p_sum[bi] = p_sum[bi] + jnp.where(upper, t_dn, 0.0)
            q_sum[bi] = q_sum[bi] + jnp.where(upper, 0.0, t_up)
            total[bi] = total[bi] + jnp.where(upper, t_dn, t_up)
        m *= 2

    o_a = []
    for bi in batch:
        q, _, val, g_a = parts[bi][:4]
        q_in = (q * jnp.exp2(p_sum[bi])).astype(BF16)
        k_out = (keys[bi] * jnp.exp2(q_sum[bi])).astype(BF16)
        decay = jnp.exp2(total[bi])
        val_b = val.astype(BF16)
        oa_rows = []
        for c in range(n_chunks):
            rs = slice(c * C, (c + 1) * C)
            heads_out = []
            for hh in range(H):
                cs = slice(hh * HGRN_KEY, (hh + 1) * HGRN_KEY)
                st = state_ref[bi, hh]
                vb = val_b[rs, cs]
                o = jnp.dot(scores[bi][c * H + hh].astype(BF16), vb, preferred_element_type=F32)
                o = o + lax.dot_general(q_in[rs, cs], st.astype(BF16), nt_dims,
                                        preferred_element_type=F32)
                state_ref[bi, hh] = (st * decay[c * C:c * C + 1, cs]
                                     + lax.dot_general(vb, k_out[rs, cs], tn_dims,
                                                       preferred_element_type=F32))
                ms_o = jnp.mean(o * o, axis=-1, keepdims=True)
                heads_out.append(o * lax.rsqrt(ms_o + EPS))
            oa_rows.append(jnp.concatenate(heads_out, axis=1))
        o_a.append(jnp.concatenate(oa_rows, axis=0) * og_ref[...] * _silu(g_a))

    pos = lax.broadcasted_iota(jnp.int32, (T, POOL_GROUP), 0) + t_idx * T + 1
    for bi in batch:
        u, g_b = parts[bi][4], parts[bi][5]
        ub = jnp.concatenate([carry_ref[bi], u], axis=0)
        carry_ref[bi] = u[T - POOL_HISTORY:T, :]
        wins = []
        acc = ub
        sh = 1
        while sh < max(POOL_WINDOWS):
            acc = acc + pltpu.roll(acc, sh, 0)
            sh *= 2
            wins.append(acc)
        ob = []
        for gi, w in enumerate(POOL_WINDOWS):
            cs = slice(gi * POOL_GROUP, (gi + 1) * POOL_GROUP)
            win = wins[w.bit_length() - 2][POOL_HISTORY:, cs]
            cnt = jnp.minimum(pos, w).astype(F32)
            pooled = win / cnt - u[:, cs]
            ob.append(jnp.dot(pooled.astype(BF16), pw_ref[gi], preferred_element_type=F32))
        o_b = jnp.concatenate(ob, axis=1) * ps_ref[...] * _silu(g_b)
        mixed = jnp.concatenate([o_a[bi], o_b], axis=1).astype(BF16)
        y = jnp.dot(mixed, wout_ref[...], preferred_element_type=F32)
        o_ref[bi] = xs[bi] + mod_ref[bi, 2:3, :] * y


def _layer0(x, mod, norm_g, w_in, hgrn_lb, onorm_g, pool_w, pool_scale, w_out, layer_slot):
    b, s, d = x.shape
    tile = SEQ_TILE
    n_in = w_in.shape[1]
    mix = w_out.shape[0]
    const2 = lambda ti: (0, 0)
    const3 = lambda ti: (0, 0, 0)
    return pl.pallas_call(
        functools.partial(_layer0_kernel, tile=tile, chunk=HGRN_CHUNK, layer_slot=layer_slot),
        grid=(s // tile,),
        in_specs=[pl.BlockSpec((b, tile, d), lambda ti: (0, ti, 0)),
                  pl.BlockSpec((b, 3, d), const3),
                  pl.BlockSpec((1, d), const2),
                  pl.BlockSpec((d, n_in), const2, pipeline_mode=pl.Buffered(1)),
                  pl.BlockSpec(hgrn_lb.shape, const2),
                  pl.BlockSpec((1, HGRN_VW), const2),
                  pl.BlockSpec(pool_w.shape, const3),
                  pl.BlockSpec((1, POOL_WIDTH), const2),
                  pl.BlockSpec((mix, d), const2, pipeline_mode=pl.Buffered(1))],
        out_specs=pl.BlockSpec((b, tile, d), lambda ti: (0, ti, 0)),
        out_shape=jax.ShapeDtypeStruct((b, s, d), F32),
        scratch_shapes=[pltpu.VMEM((b, HGRN_HEADS, HGRN_VAL, HGRN_KEY), F32),
                        pltpu.VMEM((b, POOL_HISTORY, POOL_WIDTH), F32)],
        compiler_params=_params("arbitrary"),
        name="layer0_hgrn_pool",
    )(x, mod, norm_g.reshape(1, d), w_in.astype(BF16), hgrn_lb, onorm_g.reshape(1, HGRN_VW),
      pool_w.astype(BF16), pool_scale.reshape(1, POOL_WIDTH), w_out.astype(BF16))


def _split3(c):
    hi = c.astype(BF16).astype(F32)
    r = c - hi
    mid = r.astype(BF16).astype(F32)
    lo = (r - mid).astype(BF16).astype(F32)
    return hi, mid, lo


def _carrier_selectors():
    sel = np.zeros((FOX_HEADS, LANES, 2 * LANES), np.float32)
    one_lane = 6 * FOX_HEADS
    for hd in range(FOX_HEADS):
        for j in range(3):
            sel[hd, j * FOX_HEADS + hd, AUG_LANE + j] = 1.0
            sel[hd, j * FOX_HEADS + hd, LANES + AUG_LANE + 3 + j] = -1.0
            sel[hd, (3 + j) * FOX_HEADS + hd, LANES + AUG_LANE + 6 + j] = -1.0
        sel[hd, one_lane, AUG_LANE + 3:AUG_LANE + 9] = 1.0
        sel[hd, one_lane, LANES + AUG_LANE:LANES + AUG_LANE + 3] = 1.0
    return jnp.asarray(sel, BF16)


def _layer1_proj_kernel(x_ref, mod_ref, ng_ref, win_ref, bf_ref, qg_ref, kg_ref, bd_ref, sel_ref,
                        qa_ref, ka_ref, vt_ref, g_ref, cq_ref, edge_ref, carry_ref, tile_start_ref,
                        *, tile):
    T, W = tile, FOX_WIDTH
    t_idx = pl.program_id(1)

    @pl.when(t_idx == 0)
    def _():
        carry_ref[...] = jnp.zeros_like(carry_ref)

    x = x_ref[0]
    h = _modulated_norm(x, mod_ref[0, 0:1, :], mod_ref[0, 1:2, :], ng_ref[...])
    proj = jnp.dot(h.astype(BF16), win_ref[...], preferred_element_type=F32)
    q = proj[:, 0:W]
    k = proj[:, W:2 * W]
    v = proj[:, 2 * W:3 * W]
    g_ref[0] = proj[:, 3 * W:4 * W]
    fl = proj[:, 4 * W:4 * W + LANES]
    vt_ref[0, 0] = v.T.astype(BF16)

    z = fl + bf_ref[...]
    logf = jnp.minimum(z, 0.0) - jnp.log1p(jnp.exp(-jnp.abs(z)))
    row = lax.broadcasted_iota(jnp.int32, (T, LANES), 0)
    cum = logf
    sh = 1
    while sh < T:
        cum = cum + jnp.where(row >= sh, pltpu.roll(cum, sh, 0), 0.0)
        sh *= 2
    start = carry_ref[...]
    end = start + cum[T - 1:T, :]
    carry_ref[...] = end

    @pl.when(t_idx % (ATTN_Q_TILE // T) == 0)
    def _():
        tile_start_ref[...] = start

    tile_start = tile_start_ref[...]
    edge_ref[0, 0] = jnp.concatenate([tile_start, end], axis=0) * LOG2E
    s_hi, s_mid, s_lo = _split3((cum + (start - tile_start)) * LOG2E)
    e_hi, e_mid, e_lo = _split3((cum - cum[T - 1:T, :]) * LOG2E)
    total_t = ((cum + start) * LOG2E).T
    for hd in range(FOX_HEADS):
        cq_ref[0, hd] = total_t[hd:hd + 1, :]

    lane = lax.broadcasted_iota(jnp.int32, (T, LANES), 1)
    terms = (s_hi, s_mid, s_lo, e_hi, e_mid, e_lo)
    packed = jnp.where(lane == len(terms) * FOX_HEADS, 1.0, 0.0)
    for j, term in enumerate(terms):
        moved = term if j == 0 else pltpu.roll(term, j * FOX_HEADS, 1)
        packed = jnp.where((lane >= j * FOX_HEADS) & (lane < (j + 1) * FOX_HEADS), moved, packed)
    packed = packed.astype(BF16)
    bd = bd_ref[...]
    qk_scale = FOX_HEAD_DIM ** -0.5 * LOG2E
    for pair in range(FOX_HEADS // 2):
        cs = slice(pair * LANES, (pair + 1) * LANES)
        qp = q[:, cs]
        kp = k[:, cs]
        squares = jnp.concatenate([(qp * qp).astype(BF16), (kp * kp).astype(BF16)], axis=1)
        ms = jnp.dot(squares, bd, preferred_element_type=F32)
        qn = qp * lax.rsqrt(ms[:, :LANES] + EPS) * (qg_ref[...] * qk_scale)
        kn = kp * lax.rsqrt(ms[:, LANES:] + EPS) * kg_ref[...]
        for e in range(LANES // FOX_HEAD_DIM):
            hd = (LANES // FOX_HEAD_DIM) * pair + e
            aug = jnp.dot(packed, sel_ref[hd], preferred_element_type=F32)
            aug_q, aug_k = aug[:, :LANES], aug[:, LANES:]
            q_main = qn if e == 0 else pltpu.roll(qn, FOX_HEAD_DIM, 1)
            k_main = kn if e == 0 else pltpu.roll(kn, FOX_HEAD_DIM, 1)
            qa_ref[0, hd] = jnp.where(lane < FOX_HEAD_DIM, q_main, aug_q).astype(BF16)
            ka_ref[0, hd] = jnp.where(lane < FOX_HEAD_DIM, k_main, aug_k).astype(BF16)


def _layer1_proj(x, mod, norm_g, w_in, b_f, qnorm_g, knorm_g):
    b, s, d = x.shape
    tile = SEQ_TILE
    nt = s // tile
    w = FOX_WIDTH
    n_pad = w_in.shape[1]
    bf_pad = jnp.zeros((1, LANES), F32).at[0, :FOX_HEADS].set(b_f)
    blk = jnp.arange(2 * LANES) // FOX_HEAD_DIM
    bd = jnp.where(blk[:, None] == blk[None, :], 1.0 / FOX_HEAD_DIM, 0.0).astype(BF16)
    qg = jnp.tile(qnorm_g, LANES // FOX_HEAD_DIM).reshape(1, LANES)
    kg = jnp.tile(knorm_g, LANES // FOX_HEAD_DIM).reshape(1, LANES)
    const2 = lambda bi, ti: (0, 0)
    return pl.pallas_call(
        functools.partial(_layer1_proj_kernel, tile=tile),
        grid=(b, nt),
        in_specs=[pl.BlockSpec((1, tile, d), lambda bi, ti: (bi, ti, 0)),
                  pl.BlockSpec((1, 3, d), lambda bi, ti: (bi, 0, 0)),
                  pl.BlockSpec((1, d), const2),
                  pl.BlockSpec((d, n_pad), const2),
                  pl.BlockSpec((1, LANES), const2),
                  pl.BlockSpec((1, LANES), const2),
                  pl.BlockSpec((1, LANES), const2),
                  pl.BlockSpec((2 * LANES, 2 * LANES), const2),
                  pl.BlockSpec((FOX_HEADS, LANES, 2 * LANES), lambda bi, ti: (0, 0, 0))],
        out_specs=[pl.BlockSpec((1, FOX_HEADS, tile, LANES), lambda bi, ti: (bi, 0, ti, 0)),
                   pl.BlockSpec((1, FOX_HEADS, tile, LANES), lambda bi, ti: (bi, 0, ti, 0)),
                   pl.BlockSpec((1, 1, w, tile), lambda bi, ti: (bi, ti, 0, 0)),
                   pl.BlockSpec((1, tile, w), lambda bi, ti: (bi, ti, 0)),
                   pl.BlockSpec((1, FOX_HEADS, 1, tile), lambda bi, ti: (bi, 0, 0, ti)),
                   pl.BlockSpec((1, 1, 2, LANES), lambda bi, ti: (bi, ti, 0, 0))],
        out_shape=[jax.ShapeDtypeStruct((b, FOX_HEADS, s, LANES), BF16),
                   jax.ShapeDtypeStruct((b, FOX_HEADS, s, LANES), BF16),
                   jax.ShapeDtypeStruct((b, nt, w, tile), BF16),
                   jax.ShapeDtypeStruct((b, s, w), F32),
                   jax.ShapeDtypeStruct((b, FOX_HEADS, 1, s), F32),
                   jax.ShapeDtypeStruct((b, nt, 2, LANES), F32)],
        scratch_shapes=[pltpu.VMEM((1, LANES), F32), pltpu.VMEM((1, LANES), F32)],
        compiler_params=_params("arbitrary", "arbitrary"),
        name="layer1_proj",
    )(x, mod, norm_g.reshape(1, d), w_in, bf_pad, qg, kg, bd, _carrier_selectors())


def _query_operands(q):
    slab = 2 * SUBLANES
    row = lax.broadcasted_iota(jnp.int32, (slab, q.shape[0]), 0)
    q_t = q.astype(F32).T
    carriers = q_t[AUG_LANE:AUG_LANE + slab]
    return tuple(jnp.concatenate([q_t[:AUG_LANE], jnp.where(keep, carriers, 0.0),
                                  q_t[AUG_LANE + slab:]], axis=0).astype(BF16)
                 for keep in (row < 6, row >= 6))


def _gated_output(acc_t, g_ref, o_ref, rows):
    o_ref[0, rows] = (acc_t.T * _silu(g_ref[0, rows])).astype(BF16)


def _fox_bounded_kernel(first_ref, end_ref, qa_ref, ka_ref, vt_ref, cq_ref, g_ref, o_ref,
                        p_ref, acc_ref, *, tile, heads):
    TQ, HP = tile, heads
    QT = qa_ref.shape[2] // TQ
    SB = vt_ref.shape[-1]
    R = TQ // SB
    n_blocks = ka_ref.shape[2] // SB
    b = pl.program_id(0)
    group = pl.program_id(1)
    i0 = pl.program_id(2) * QT
    chains = [(qs, e) for qs in range(QT) for e in range(HP)]
    cols = [slice(qs * TQ, (qs + 1) * TQ) for qs in range(QT)]
    operands = {(qs, e): _query_operands(qa_ref[0, e, cols[qs], :]) for qs, e in chains}
    causal = (lax.broadcasted_iota(jnp.int32, (TQ, TQ), 0)
              <= lax.broadcasted_iota(jnp.int32, (TQ, TQ), 1))

    def keys(e, n):
        return ka_ref[0, e, pl.ds(pl.multiple_of(n * TQ, TQ), TQ), :]

    def values_t(e, blk):
        return vt_ref[0, blk, pl.ds(e * FOX_HEAD_DIM, FOX_HEAD_DIM), :]

    def rest(c, blk, gate=1.0):
        qs, e = c
        end = end_ref[(b * n_blocks + blk) * FOX_HEADS + HP * group + e]
        return jnp.exp2(jnp.minimum(cq_ref[0, e][:, cols[qs]] - end, 0.0)) * gate

    def col_partial(p):
        return jnp.sum(p.reshape(p.shape[0] // SUBLANES, SUBLANES, p.shape[1]), axis=0)

    def far_scores(c, n):
        return jnp.dot(keys(c[1], n), operands[c][1], preferred_element_type=F32)

    def contract(c, n, p_bf16, gate=1.0):
        for r in range(R):
            pv = jnp.dot(values_t(c[1], n * R + r), p_bf16[r * SB:(r + 1) * SB],
                         preferred_element_type=F32)
            acc_ref[c] += rest(c, n * R + r, gate) * pv

    def weighted_partial(c, n, p, gate=1.0):
        return sum(rest(c, n * R + r, gate) * col_partial(p[r * SB:(r + 1) * SB])
                   for r in range(R))

    l_part = {}
    for c in chains:
        qs, e = c
        s_t = jnp.dot(keys(e, i0 + qs), operands[c][0], preferred_element_type=F32)
        p = jnp.where(causal, jnp.exp2(s_t), 0.0)
        l_part[c] = col_partial(p)
        v_t = jnp.concatenate([values_t(e, (i0 + qs) * R + r) for r in range(R)], axis=1)
        acc_ref[c] = jnp.dot(v_t, p.astype(BF16), preferred_element_type=F32)
    for c in chains:
        for t in range(c[0]):
            p = jnp.exp2(far_scores(c, i0 + t))
            l_part[c] = l_part[c] + weighted_partial(c, i0 + t, p)
            contract(c, i0 + t, p.astype(BF16))

    first = first_ref[(b * pl.num_programs(1) + group) * pl.num_programs(2) + pl.program_id(2)]
    gate0 = jnp.where(i0 > 0, 1.0, 0.0)
    for c in chains:
        p = jnp.exp2(far_scores(c, first))
        l_part[c] = l_part[c] + weighted_partial(c, first, p, gate0)
        p_ref[c] = p.astype(BF16)

    def body(n, l_run):
        s_new = [far_scores(c, n) for c in chains]
        for c in chains:
            contract(c, n - 1, p_ref[c])
        out = []
        for c, s_t, l_c in zip(chains, s_new, l_run):
            p = jnp.exp2(s_t)
            out.append(l_c + weighted_partial(c, n, p))
            p_ref[c] = p.astype(BF16)
        return tuple(out)

    l_run = lax.fori_loop(first + 1, i0, body, tuple(l_part[c] for c in chains))
    for c in chains:
        contract(c, jnp.maximum(i0 - 1, 0), p_ref[c], gate0)
    for qs in range(QT):
        out_t = [acc_ref[qs, e] * (1.0 / jnp.sum(l_run[qs * HP + e], axis=0, keepdims=True))
                 for e in range(HP)]
        _gated_output(jnp.concatenate(out_t, axis=0), g_ref, o_ref, cols[qs])


def _fox_online_kernel(start_ref, qa_ref, ka_ref, vt_ref, g_ref, o_ref, *, tile, heads):
    TB, HP = tile, heads
    n_blocks = ka_ref.shape[2] // TB
    b = pl.program_id(0)
    group = pl.program_id(1)
    i = pl.program_id(2)
    q_t = [_query_operands(qa_ref[0, e])[0] for e in range(HP)]
    k_row = lax.broadcasted_iota(jnp.int32, (TB, TB), 0)
    q_col = lax.broadcasted_iota(jnp.int32, (TB, TB), 1)

    def step(j, carry, masked):
        out = []
        for e in range(HP):
            m_run, l_run, acc = carry[e]
            k_blk = ka_ref[0, e, pl.ds(pl.multiple_of(j * TB, TB), TB), :]
            s_t = jnp.dot(k_blk, q_t[e], preferred_element_type=F32)
            if masked:
                s_t = jnp.where(k_row <= q_col, s_t, MASK_VALUE)
            hd = HP * group + e
            delta = (start_ref[(b * n_blocks + i) * FOX_HEADS + hd]
                     - start_ref[(b * n_blocks + j) * FOX_HEADS + hd])
            m_new = jnp.maximum(m_run, jnp.max(s_t, axis=0, keepdims=True) + delta)
            p_t = jnp.exp2(s_t - (m_new - delta))
            alpha = jnp.exp2(m_run - m_new)
            l_new = alpha * l_run + jnp.sum(p_t, axis=0, keepdims=True)
            v_t = vt_ref[0, j, pl.ds(e * FOX_HEAD_DIM, FOX_HEAD_DIM), :]
            acc_new = alpha * acc + jnp.dot(v_t, p_t.astype(BF16), preferred_element_type=F32)
            out.append((m_new, l_new, acc_new))
        return tuple(out)

    init = (jnp.full((1, TB), MASK_VALUE, F32), jnp.zeros((1, TB), F32),
            jnp.zeros((FOX_HEAD_DIM, TB), F32))
    carry = lax.fori_loop(0, i, lambda j, c: step(j, c, False), (init,) * HP)
    carry = step(i, carry, True)
    _gated_output(jnp.concatenate([acc / l_run for _, l_run, acc in carry], axis=0), g_ref, o_ref,
                  slice(None))


def _fox_attention(qa, ka, vt, g, cq, edges, *, bounded):
    b, n_heads, s, _ = qa.shape
    _, nvb, w, vb = vt.shape
    if bounded:
        tile, hp, qt = ATTN_Q_TILE, ATTN_HEADS, ATTN_TILES_PER_STEP
    else:
        tile, hp, qt = vb, LANES // FOX_HEAD_DIM, 1
    gw = hp * FOX_HEAD_DIM
    rows = qt * tile
    specs = dict(
        table=pl.BlockSpec(memory_space=pltpu.SMEM),
        q=pl.BlockSpec((1, hp, rows, LANES), lambda bi, p, i: (bi, p, i, 0)),
        k=pl.BlockSpec((1, hp, s, LANES), lambda bi, p, i: (bi, p, 0, 0)),
        v=pl.BlockSpec((1, nvb, gw, vb), lambda bi, p, i: (bi, 0, p, 0)),
        cq=pl.BlockSpec((1, hp, 1, rows), lambda bi, p, i: (bi, p, 0, i)),
        g=pl.BlockSpec((1, rows, gw), lambda bi, p, i: (bi, i, p)))
    if bounded:
        body = functools.partial(_fox_bounded_kernel, tile=tile, heads=hp)
        names = ("table", "table", "q", "k", "v", "cq", "g")
        step_start = edges[:, ::rows // vb, 0, :FOX_HEADS]
        tile_end = edges[:, tile // vb - 1::tile // vb, 1, :FOX_HEADS]
        dead = (step_start[:, :, None] - tile_end[:, None]) < -FOX_DEAD_EXPONENT
        before = jnp.arange(s // rows)[:, None] * qt > jnp.arange(s // tile)[None, :]
        first_live = jnp.sum(dead & before[None, :, :, None], axis=2)
        last_before = jnp.maximum(jnp.arange(s // rows) * qt - 1, 0)
        first_live = jnp.minimum(first_live, last_before[None, :, None])
        first_live = jnp.min(first_live.reshape(b, s // rows, n_heads // hp, hp), axis=-1)
        args = (first_live.transpose(0, 2, 1).reshape(-1).astype(jnp.int32),
                edges[:, :, 1, :FOX_HEADS].reshape(-1), qa, ka, vt, cq, g)
        scratch = [pltpu.VMEM((qt, hp, tile, tile), BF16),
                   pltpu.VMEM((qt, hp, FOX_HEAD_DIM, tile), F32)]
    else:
        body = functools.partial(_fox_online_kernel, tile=tile, heads=hp)
        names = ("table", "q", "k", "v", "g")
        args = (edges[:, :, 0, :FOX_HEADS].reshape(-1), qa, ka, vt, g)
        scratch = []
    return pl.pallas_call(
        body,
        grid=(b, n_heads // hp, s // rows),
        in_specs=[specs[n] for n in names],
        out_specs=specs["g"],
        out_shape=jax.ShapeDtypeStruct((b, s, w), BF16),
        scratch_shapes=scratch,
        compiler_params=_params("arbitrary", "arbitrary", "arbitrary"),
        name="fox_attention_bounded" if bounded else "fox_attention_online",
    )(*args)


def _out_proj_kernel(a_ref, x_ref, mod_ref, w_ref, o_ref):
    y = jnp.dot(a_ref[0], w_ref[...], preferred_element_type=F32)
    o_ref[0] = x_ref[0] + mod_ref[0, 2:3, :] * y


def _out_proj(a, x, mod, w_out):
    b, s, d = x.shape
    tile = 512
    k = a.shape[-1]
    return pl.pallas_call(
        _out_proj_kernel,
        grid=(b, s // tile),
        in_specs=[pl.BlockSpec((1, tile, k), lambda bi, ti: (bi, ti, 0)),
                  pl.BlockSpec((1, tile, d), lambda bi, ti: (bi, ti, 0)),
                  pl.BlockSpec((1, 3, d), lambda bi, ti: (bi, 0, 0)),
                  pl.BlockSpec((k, d), lambda bi, ti: (0, 0))],
        out_specs=pl.BlockSpec((1, tile, d), lambda bi, ti: (bi, ti, 0)),
        out_shape=jax.ShapeDtypeStruct((b, s, d), F32),
        compiler_params=_params("arbitrary", "arbitrary"),
        name="layer1_out_proj",
    )(a, x, mod, w_out.astype(BF16))


def _layer1(x, mod, norm_g, w_in, b_f, qnorm_g, knorm_g, w_out):
    d, w = x.shape[-1], FOX_WIDTH
    order = jnp.argsort(b_f)
    w_heads = w_in[:, :4 * w].astype(BF16).reshape(d, 4, FOX_HEADS, FOX_HEAD_DIM)[:, :, order]
    w_forget = jnp.zeros((d, LANES), BF16).at[:, :FOX_HEADS].set(
        w_in[:, 4 * w:].astype(BF16)[:, order])
    w_in = jnp.concatenate([w_heads.reshape(d, 4 * w), w_forget], axis=1)
    w_out = w_out.astype(BF16).reshape(FOX_HEADS, FOX_HEAD_DIM, d)[order].reshape(w, d)
    qa, ka, vt, g, cq, edges = _layer1_proj(x, mod, norm_g, w_in, b_f[order], qnorm_g, knorm_g)
    score_bound = (LOG2E * FOX_HEAD_DIM ** 0.5
                   * jnp.max(jnp.abs(qnorm_g)) * jnp.max(jnp.abs(knorm_g)))
    gated = lax.cond(score_bound <= FOX_SAFE_LOGIT,
                     functools.partial(_fox_attention, bounded=True),
                     functools.partial(_fox_attention, bounded=False),
                     qa, ka, vt, g, cq, edges)
    return _out_proj(gated, x, mod, w_out)


def kernel(x, c, norm_g, ada_w, ada_b, hgrn_lb, even_w_in, hgrn_onorm_g, pool_w, pool_scale,
           even_w_out, odd_w_in, fox_b_f, fox_qnorm_g, fox_knorm_g, odd_w_out):
    depth = norm_g.shape[0]
    mods = _adaln_mods(c, ada_w, ada_b)
    for l in range(depth):
        j = l // 2
        if l % 2 == 0:
            x = _layer0(x, mods[l], norm_g[l], even_w_in[j], hgrn_lb, hgrn_onorm_g[j],
                        pool_w[j], pool_scale[j], even_w_out[j], layer_slot=l)
        else:
            x = _layer1(x, mods[l], norm_g[l], odd_w_in[j], fox_b_f[j], fox_qnorm_g[j],
                        fox_knorm_g[j], odd_w_out[j])
    return x
```

```python
import functools

import jax
import jax.numpy as jnp
import numpy as np
from jax import lax
from jax.experimental import pallas as pl
from jax.experimental.pallas import tpu as pltpu

F32 = jnp.float32
BF16 = jnp.bfloat16
EPS = 1e-6

HGRN_HEADS = 4
HGRN_KEY = 128
HGRN_VAL = 128
HGRN_KW = HGRN_HEADS * HGRN_KEY
HGRN_VW = HGRN_HEADS * HGRN_VAL
POOL_WINDOWS = (2, 4, 8, 16)
POOL_GROUP = 128
POOL_WIDTH = POOL_GROUP * len(POOL_WINDOWS)
POOL_HISTORY = 16
FOX_HEADS = 16
FOX_HEAD_DIM = 64
FOX_WIDTH = FOX_HEADS * FOX_HEAD_DIM

LANES = 128
SUBLANES = 8
VMEM_LIMIT_BYTES = 56 * 1024 * 1024

SEQ_TILE = 256
HGRN_CHUNK = 128
ATTN_Q_TILE = 512
ATTN_HEADS = 4
ATTN_TILES_PER_STEP = 2
FOX_SAFE_LOGIT = 100.0
FOX_DEAD_EXPONENT = 150.0
LOG2E = 1.4426950408889634
MASK_VALUE = -1e30
AUG_LANE = FOX_HEAD_DIM


def _silu(x):
    return x * jax.nn.sigmoid(x)


def _params(*semantics):
    return pltpu.CompilerParams(dimension_semantics=semantics, vmem_limit_bytes=VMEM_LIMIT_BYTES)


def _mod_kernel(c_ref, w_ref, b_ref, o_ref):
    c = c_ref[...]
    cond = _silu(c)
    o_ref[0] = jnp.dot(cond, w_ref[0], preferred_element_type=F32,
                       precision=lax.Precision.HIGHEST) + b_ref[0]


def _adaln_mods(c, ada_w, ada_b):
    depth, d, n = ada_w.shape
    b = c.shape[0]
    rows = SUBLANES
    c_rows = jnp.zeros((rows, d), F32).at[:b].set(c)
    tn = 1024
    out = pl.pallas_call(
        _mod_kernel,
        grid=(depth, n // tn),
        in_specs=[pl.BlockSpec((rows, d), lambda l, j: (0, 0)),
                  pl.BlockSpec((1, d, tn), lambda l, j: (l, 0, j)),
                  pl.BlockSpec((1, 1, tn), lambda l, j: (l, 0, j))],
        out_specs=pl.BlockSpec((1, rows, tn), lambda l, j: (l, 0, j)),
        out_shape=jax.ShapeDtypeStruct((depth, rows, n), F32),
        compiler_params=_params("arbitrary", "arbitrary"),
        name="adaln_mods",
    )(c_rows, ada_w, ada_b.reshape(depth, 1, n))
    return out[:, :b].reshape(depth, b, 3, d)


def _modulated_norm(x, shift, scale, gain):
    ms = jnp.mean(x * x, axis=-1, keepdims=True)
    return x * lax.rsqrt(ms + EPS) * gain * (1.0 + scale) + shift


def _layer0_kernel(x_ref, mod_ref, ng_ref, win_ref, lb_ref, og_ref, pw_ref, ps_ref, wout_ref,
                   o_ref, state_ref, carry_ref, *, tile, chunk, layer_slot):
    T, C, H = tile, chunk, HGRN_HEADS
    batch = range(x_ref.shape[0])
    t_idx = pl.program_id(0)

    @pl.when(t_idx == 0)
    def _():
        state_ref[...] = jnp.zeros_like(state_ref)
        carry_ref[...] = jnp.zeros_like(carry_ref)

    lbv = lb_ref[...]
    e = jnp.exp(lbv - jnp.max(lbv, axis=0, keepdims=True))
    lower = (jnp.sum(e[0:layer_slot + 1], axis=0, keepdims=True)
             / jnp.sum(e, axis=0, keepdims=True))

    xs, parts = [], []
    for bi in batch:
        x = x_ref[bi]
        h = _modulated_norm(x, mod_ref[bi, 0:1, :], mod_ref[bi, 1:2, :], ng_ref[...])
        proj = jnp.dot(h.astype(BF16), win_ref[...], preferred_element_type=F32)
        o0, widths, piece = 0, (HGRN_KW, HGRN_KW, HGRN_VW, HGRN_VW, POOL_WIDTH, POOL_WIDTH), []
        for wd in widths:
            piece.append(proj[:, o0:o0 + wd])
            o0 += wd
        xs.append(x)
        parts.append(piece)

    row = lax.broadcasted_iota(jnp.int32, (T, HGRN_KW), 0)
    ti = lax.broadcasted_iota(jnp.int32, (C, C), 0)
    si = lax.broadcasted_iota(jnp.int32, (C, C), 1)
    n_chunks = T // C
    nt_dims = (((1,), (1,)), ((), ()))
    tn_dims = (((0,), (0,)), ((), ()))

    def level_scores(qe, ke, mask, scores):
        qb = qe.astype(BF16)
        kb = ke.astype(BF16)
        out = []
        for c in range(n_chunks):
            for hh in range(H):
                rs = slice(c * C, (c + 1) * C)
                cs = slice(hh * HGRN_KEY, (hh + 1) * HGRN_KEY)
                d = lax.dot_general(qb[rs, cs], kb[rs, cs], nt_dims, preferred_element_type=F32)
                prev = scores[c * H + hh]
                out.append(jnp.where(mask, d, 0.0 if prev is None else prev))
        return out

    keys, scores, p_sum, q_sum, total = [], [], [], [], []
    for bi in batch:
        q, f = parts[bi][0], parts[bi][1]
        forget = lower + (1.0 - lower) * jax.nn.sigmoid(f)
        logf = jnp.log(forget) * LOG2E
        keys.append(1.0 - forget)
        scores.append(level_scores(q, keys[bi], ti == si, [None] * (n_chunks * H)))
        p_sum.append(logf)
        q_sum.append(jnp.zeros_like(logf))
        total.append(logf)
    m = 1
    while m < C:
        mask = ((ti ^ si) < 2 * m) & ((ti & m) != 0) & ((si & m) == 0)
        upper = (row & m) != 0
        for bi in batch:
            scores[bi] = level_scores(parts[bi][0] * jnp.exp2(p_sum[bi]),
                                      keys[bi] * jnp.exp2(q_sum[bi]), mask, scores[bi])
            t_dn = pltpu.roll(total[bi], m, 0)
            t_up = pltpu.roll(total[bi], T - m, 0)
            p_sum[bi] = p_sum[bi] + jnp.where(upper, t_dn, 0.0)
            q_sum[bi] = q_sum[bi] + jnp.where(upper, 0.0, t_up)
            total[bi] = total[bi] + jnp.where(upper, t_dn, t_up)
        m *= 2

    o_a = []
    for bi in batch:
        q, _, val, g_a = parts[bi][:4]
        q_in = (q * jnp.exp2(p_sum[bi])).astype(BF16)
        k_out = (keys[bi] * jnp.exp2(q_sum[bi])).astype(BF16)
        decay = jnp.exp2(total[bi])
        val_b = val.astype(BF16)
        oa_rows = []
        for c in range(n_chunks):
            rs = slice(c * C, (c + 1) * C)
            heads_out = []
            for hh in range(H):
                cs = slice(hh * HGRN_KEY, (hh + 1) * HGRN_KEY)
                st = state_ref[bi, hh]
                vb = val_b[rs, cs]
                o = jnp.dot(scores[bi][c * H + hh].astype(BF16), vb, preferred_element_type=F32)
                o = o + lax.dot_general(q_in[rs, cs], st.astype(BF16), nt_dims,
                                        preferred_element_type=F32)
                state_ref[bi, hh] = (st * decay[c * C:c * C + 1, cs]
                                     + lax.dot_general(vb, k_out[rs, cs], tn_dims,
                                                       preferred_element_type=F32))
                ms_o = jnp.mean(o * o, axis=-1, keepdims=True)
                heads_out.append(o * lax.rsqrt(ms_o + EPS))
            oa_rows.append(jnp.concatenate(heads_out, axis=1))
        o_a.append(jnp.concatenate(oa_rows, axis=0) * og_ref[...] * _silu(g_a))

    pos = lax.broadcasted_iota(jnp.int32, (T, POOL_GROUP), 0) + t_idx * T + 1
    for bi in batch:
        u, g_b = parts[bi][4], parts[bi][5]
        ub = jnp.concatenate([carry_ref[bi], u], axis=0)
        carry_ref[bi] = u[T - POOL_HISTORY:T, :]
        wins = []
        acc = ub
        sh = 1
        while sh < max(POOL_WINDOWS):
            acc = acc + pltpu.roll(acc, sh, 0)
            sh *= 2
            wins.append(acc)
        ob = []
        for gi, w in enumerate(POOL_WINDOWS):
            cs = slice(gi * POOL_GROUP, (gi + 1) * POOL_GROUP)
            win = wins[w.bit_length() - 2][POOL_HISTORY:, cs]
            cnt = jnp.minimum(pos, w).astype(F32)
            pooled = win / cnt - u[:, cs]
            ob.append(jnp.dot(pooled.astype(BF16), pw_ref[gi], preferred_element_type=F32))
        o_b = jnp.concatenate(ob, axis=1) * ps_ref[...] * _silu(g_b)
        mixed = jnp.concatenate([o_a[bi], o_b], axis=1).astype(BF16)
        y = jnp.dot(mixed, wout_ref[...], preferred_element_type=F32)
        o_ref[bi] = xs[bi] + mod_ref[bi, 2:3, :] * y


def _layer0(x, mod, norm_g, w_in, hgrn_lb, onorm_g, pool_w, pool_scale, w_out, layer_slot):
    b, s, d = x.shape
    tile = SEQ_TILE
    n_in = w_in.shape[1]
    mix = w_out.shape[0]
    const2 = lambda ti: (0, 0)
    const3 = lambda ti: (0, 0, 0)
    return pl.pallas_call(
        functools.partial(_layer0_kernel, tile=tile, chunk=HGRN_CHUNK, layer_slot=layer_slot),
        grid=(s // tile,),
        in_specs=[pl.BlockSpec((b, tile, d), lambda ti: (0, ti, 0)),
                  pl.BlockSpec((b, 3, d), const3),
                  pl.BlockSpec((1, d), const2),
                  pl.BlockSpec((d, n_in), const2, pipeline_mode=pl.Buffered(1)),
                  pl.BlockSpec(hgrn_lb.shape, const2),
                  pl.BlockSpec((1, HGRN_VW), const2),
                  pl.BlockSpec(pool_w.shape, const3),
                  pl.BlockSpec((1, POOL_WIDTH), const2),
                  pl.BlockSpec((mix, d), const2, pipeline_mode=pl.Buffered(1))],
        out_specs=pl.BlockSpec((b, tile, d), lambda ti: (0, ti, 0)),
        out_shape=jax.ShapeDtypeStruct((b, s, d), F32),
        scratch_shapes=[pltpu.VMEM((b, HGRN_HEADS, HGRN_VAL, HGRN_KEY), F32),
                        pltpu.VMEM((b, POOL_HISTORY, POOL_WIDTH), F32)],
        compiler_params=_params("arbitrary"),
        name="layer0_hgrn_pool",
    )(x, mod, norm_g.reshape(1, d), w_in.astype(BF16), hgrn_lb, onorm_g.reshape(1, HGRN_VW),
      pool_w.astype(BF16), pool_scale.reshape(1, POOL_WIDTH), w_out.astype(BF16))


def _split3(c):
    hi = c.astype(BF16).astype(F32)
    r = c - hi
    mid = r.astype(BF16).astype(F32)
    lo = (r - mid).astype(BF16).astype(F32)
    return hi, mid, lo


def _carrier_selectors():
    sel = np.zeros((FOX_HEADS, LANES, 2 * LANES), np.float32)
    one_lane = 6 * FOX_HEADS
    for hd in range(FOX_HEADS):
        for j in range(3):
            sel[hd, j * FOX_HEADS + hd, AUG_LANE + j] = 1.0
            sel[hd, j * FOX_HEADS + hd, LANES + AUG_LANE + 3 + j] = -1.0
            sel[hd, (3 + j) * FOX_HEADS + hd, LANES + AUG_LANE + 6 + j] = -1.0
        sel[hd, one_lane, AUG_LANE + 3:AUG_LANE + 9] = 1.0
        sel[hd, one_lane, LANES + AUG_LANE:LANES + AUG_LANE + 3] = 1.0
    return jnp.asarray(sel, BF16)


def _layer1_proj_kernel(x_ref, mod_ref, ng_ref, win_ref, bf_ref, qg_ref, kg_ref, bd_ref, sel_ref,
                        qa_ref, ka_ref, vt_ref, g_ref, cq_ref, edge_ref, carry_ref, tile_start_ref,
                        *, tile):
    T, W = tile, FOX_WIDTH
    t_idx = pl.program_id(1)

    @pl.when(t_idx == 0)
    def _():
        carry_ref[...] = jnp.zeros_like(carry_ref)

    x = x_ref[0]
    h = _modulated_norm(x, mod_ref[0, 0:1, :], mod_ref[0, 1:2, :], ng_ref[...])
    proj = jnp.dot(h.astype(BF16), win_ref[...], preferred_element_type=F32)
    q = proj[:, 0:W]
    k = proj[:, W:2 * W]
    v = proj[:, 2 * W:3 * W]
    g_ref[0] = proj[:, 3 * W:4 * W]
    fl = proj[:, 4 * W:4 * W + LANES]
    vt_ref[0, 0] = v.T.astype(BF16)

    z = fl + bf_ref[...]
    logf = jnp.minimum(z, 0.0) - jnp.log1p(jnp.exp(-jnp.abs(z)))
    row = lax.broadcasted_iota(jnp.int32, (T, LANES), 0)
    cum = logf
    sh = 1
    while sh < T:
        cum = cum + jnp.where(row >= sh, pltpu.roll(cum, sh, 0), 0.0)
        sh *= 2
    start = carry_ref[...]
    end = start + cum[T - 1:T, :]
    carry_ref[...] = end

    @pl.when(t_idx % (ATTN_Q_TILE // T) == 0)
    def _():
        tile_start_ref[...] = start

    tile_start = tile_start_ref[...]
    edge_ref[0, 0] = jnp.concatenate([tile_start, end], axis=0) * LOG2E
    s_hi, s_mid, s_lo = _split3((cum + (start - tile_start)) * LOG2E)
    e_hi, e_mid, e_lo = _split3((cum - cum[T - 1:T, :]) * LOG2E)
    total_t = ((cum + start) * LOG2E).T
    for hd in range(FOX_HEADS):
        cq_ref[0, hd] = total_t[hd:hd + 1, :]

    lane = lax.broadcasted_iota(jnp.int32, (T, LANES), 1)
    terms = (s_hi, s_mid, s_lo, e_hi, e_mid, e_lo)
    packed = jnp.where(lane == len(terms) * FOX_HEADS, 1.0, 0.0)
    for j, term in enumerate(terms):
        moved = term if j == 0 else pltpu.roll(term, j * FOX_HEADS, 1)
        packed = jnp.where((lane >= j * FOX_HEADS) & (lane < (j + 1) * FOX_HEADS), moved, packed)
    packed = packed.astype(BF16)
    bd = bd_ref[...]
    qk_scale = FOX_HEAD_DIM ** -0.5 * LOG2E
    for pair in range(FOX_HEADS // 2):
        cs = slice(pair * LANES, (pair + 1) * LANES)
        qp = q[:, cs]
        kp = k[:, cs]
        squares = jnp.concatenate([(qp * qp).astype(BF16), (kp * kp).astype(BF16)], axis=1)
        ms = jnp.dot(squares, bd, preferred_element_type=F32)
        qn = qp * lax.rsqrt(ms[:, :LANES] + EPS) * (qg_ref[...] * qk_scale)
        kn = kp * lax.rsqrt(ms[:, LANES:] + EPS) * kg_ref[...]
        for e in range(LANES // FOX_HEAD_DIM):
            hd = (LANES // FOX_HEAD_DIM) * pair + e
            aug = jnp.dot(packed, sel_ref[hd], preferred_element_type=F32)
            aug_q, aug_k = aug[:, :LANES], aug[:, LANES:]
            q_main = qn if e == 0 else pltpu.roll(qn, FOX_HEAD_DIM, 1)
            k_main = kn if e == 0 else pltpu.roll(kn, FOX_HEAD_DIM, 1)
            qa_ref[0, hd] = jnp.where(lane < FOX_HEAD_DIM, q_main, aug_q).astype(BF16)
            ka_ref[0, hd] = jnp.where(lane < FOX_HEAD_DIM, k_main, aug_k).astype(BF16)


def _layer1_proj(x, mod, norm_g, w_in, b_f, qnorm_g, knorm_g):
    b, s, d = x.shape
    tile = SEQ_TILE
    nt = s // tile
    w = FOX_WIDTH
    n_pad = w_in.shape[1]
    bf_pad = jnp.zeros((1, LANES), F32).at[0, :FOX_HEADS].set(b_f)
    blk = jnp.arange(2 * LANES) // FOX_HEAD_DIM
    bd = jnp.where(blk[:, None] == blk[None, :], 1.0 / FOX_HEAD_DIM, 0.0).astype(BF16)
    qg = jnp.tile(qnorm_g, LANES // FOX_HEAD_DIM).reshape(1, LANES)
    kg = jnp.tile(knorm_g, LANES // FOX_HEAD_DIM).reshape(1, LANES)
    const2 = lambda bi, ti: (0, 0)
    return pl.pallas_call(
        functools.partial(_layer1_proj_kernel, tile=tile),
        grid=(b, nt),
        in_specs=[pl.BlockSpec((1, tile, d), lambda bi, ti: (bi, ti, 0)),
                  pl.BlockSpec((1, 3, d), lambda bi, ti: (bi, 0, 0)),
                  pl.BlockSpec((1, d), const2),
                  pl.BlockSpec((d, n_pad), const2),
                  pl.BlockSpec((1, LANES), const2),
                  pl.BlockSpec((1, LANES), const2),
                  pl.BlockSpec((1, LANES), const2),
                  pl.BlockSpec((2 * LANES, 2 * LANES), const2),
                  pl.BlockSpec((FOX_HEADS, LANES, 2 * LANES), lambda bi, ti: (0, 0, 0))],
        out_specs=[pl.BlockSpec((1, FOX_HEADS, tile, LANES), lambda bi, ti: (bi, 0, ti, 0)),
                   pl.BlockSpec((1, FOX_HEADS, tile, LANES), lambda bi, ti: (bi, 0, ti, 0)),
                   pl.BlockSpec((1, 1, w, tile), lambda bi, ti: (bi, ti, 0, 0)),
                   pl.BlockSpec((1, tile, w), lambda bi, ti: (bi, ti, 0)),
                   pl.BlockSpec((1, FOX_HEADS, 1, tile), lambda bi, ti: (bi, 0, 0, ti)),
                   pl.BlockSpec((1, 1, 2, LANES), lambda bi, ti: (bi, ti, 0, 0))],
        out_shape=[jax.ShapeDtypeStruct((b, FOX_HEADS, s, LANES), BF16),
                   jax.ShapeDtypeStruct((b, FOX_HEADS, s, LANES), BF16),
                   jax.ShapeDtypeStruct((b, nt, w, tile), BF16),
                   jax.ShapeDtypeStruct((b, s, w), F32),
                   jax.ShapeDtypeStruct((b, FOX_HEADS, 1, s), F32),
                   jax.ShapeDtypeStruct((b, nt, 2, LANES), F32)],
        scratch_shapes=[pltpu.VMEM((1, LANES), F32), pltpu.VMEM((1, LANES), F32)],
        compiler_params=_params("arbitrary", "arbitrary"),
        name="layer1_proj",
    )(x, mod, norm_g.reshape(1, d), w_in, bf_pad, qg, kg, bd, _carrier_selectors())


def _query_operands(q):
    slab = 2 * SUBLANES
    row = lax.broadcasted_iota(jnp.int32, (slab, q.shape[0]), 0)
    q_t = q.astype(F32).T
    carriers = q_t[AUG_LANE:AUG_LANE + slab]
    return tuple(jnp.concatenate([q_t[:AUG_LANE], jnp.where(keep, carriers, 0.0),
                                  q_t[AUG_LANE + slab:]], axis=0).astype(BF16)
                 for keep in (row < 6, row >= 6))


def _gated_output(acc_t, g_ref, o_ref, rows):
    o_ref[0, rows] = (acc_t.T * _silu(g_ref[0, rows])).astype(BF16)


def _fox_bounded_kernel(first_ref, end_ref, qa_ref, ka_ref, vt_ref, cq_ref, g_ref, o_ref,
                        p_ref, acc_ref, *, tile, heads):
    TQ, HP = tile, heads
    QT = qa_ref.shape[2] // TQ
    SB = vt_ref.shape[-1]
    R = TQ // SB
    n_blocks = ka_ref.shape[2] // SB
    b = pl.program_id(0)
    group = pl.program_id(1)
    i0 = pl.program_id(2) * QT
    chains = [(qs, e) for qs in range(QT) for e in range(HP)]
    cols = [slice(qs * TQ, (qs + 1) * TQ) for qs in range(QT)]
    operands = {(qs, e): _query_operands(qa_ref[0, e, cols[qs], :]) for qs, e in chains}
    causal = (lax.broadcasted_iota(jnp.int32, (TQ, TQ), 0)
              <= lax.broadcasted_iota(jnp.int32, (TQ, TQ), 1))

    def keys(e, n):
        return ka_ref[0, e, pl.ds(pl.multiple_of(n * TQ, TQ), TQ), :]

    def values_t(e, blk):
        return vt_ref[0, blk, pl.ds(e * FOX_HEAD_DIM, FOX_HEAD_DIM), :]

    def rest(c, blk, gate=1.0):
        qs, e = c
        end = end_ref[(b * n_blocks + blk) * FOX_HEADS + HP * group + e]
        return jnp.exp2(jnp.minimum(cq_ref[0, e][:, cols[qs]] - end, 0.0)) * gate

    def col_partial(p):
        return jnp.sum(p.reshape(p.shape[0] // SUBLANES, SUBLANES, p.shape[1]), axis=0)

    def far_scores(c, n):
        return jnp.dot(keys(c[1], n), operands[c][1], preferred_element_type=F32)

    def contract(c, n, p_bf16, gate=1.0):
        for r in range(R):
            pv = jnp.dot(values_t(c[1], n * R + r), p_bf16[r * SB:(r + 1) * SB],
                         preferred_element_type=F32)
            acc_ref[c] += rest(c, n * R + r, gate) * pv

    def weighted_partial(c, n, p, gate=1.0):
        return sum(rest(c, n * R + r, gate) * col_partial(p[r * SB:(r + 1) * SB])
                   for r in range(R))

    first = first_ref[(b * pl.num_programs(1) + group) * pl.num_programs(2) + pl.program_id(2)]
    gate0 = jnp.where(i0 > 0, 1.0, 0.0)

    l_part = {}

    def own_tile(c):
        qs, e = c
        s_t = jnp.dot(keys(e, i0 + qs), operands[c][0], preferred_element_type=F32)
        p = jnp.where(causal, jnp.exp2(s_t), 0.0)
        l_part[c] = col_partial(p)
        p = p.astype(BF16)

        def later():
            v_t = jnp.concatenate([values_t(e, (i0 + qs) * R + r) for r in range(R)], axis=1)
            acc_ref[c] = jnp.dot(v_t, p, preferred_element_type=F32)
        return later

    def earlier_tile(c, n):
        p = jnp.exp2(far_scores(c, n))
        l_part[c] = l_part[c] + weighted_partial(c, n, p)
        p = p.astype(BF16)
        return lambda: contract(c, n, p)

    def fill(c):
        p = jnp.exp2(far_scores(c, first))
        l_part[c] = l_part[c] + weighted_partial(c, first, p, gate0)
        p_ref[c] = p.astype(BF16)

    jobs = ([functools.partial(own_tile, c) for c in chains]
            + [functools.partial(earlier_tile, c, i0 + t) for c in chains for t in range(c[0])]
            + [functools.partial(fill, c) for c in chains])
    pending = None
    for job in jobs:
        later = job()
        if pending is not None:
            pending()
        pending = later
    if pending is not None:
        pending()

    def body(n, l_run):
        s_new = [far_scores(c, n) for c in chains]
        for c in chains:
            contract(c, n - 1, p_ref[c])
        out = []
        for c, s_t, l_c in zip(chains, s_new, l_run):
            p = jnp.exp2(s_t)
            out.append(l_c + weighted_partial(c, n, p))
            p_ref[c] = p.astype(BF16)
        return tuple(out)

    l_run = lax.fori_loop(first + 1, i0, body, tuple(l_part[c] for c in chains))
    for c in chains:
        contract(c, jnp.maximum(i0 - 1, 0), p_ref[c], gate0)
    for qs in range(QT):
        out_t = [acc_ref[qs, e] * (1.0 / jnp.sum(l_run[qs * HP + e], axis=0, keepdims=True))
                 for e in range(HP)]
        _gated_output(jnp.concatenate(out_t, axis=0), g_ref, o_ref, cols[qs])


def _fox_online_kernel(start_ref, qa_ref, ka_ref, vt_ref, g_ref, o_ref, *, tile, heads):
    TB, HP = tile, heads
    n_blocks = ka_ref.shape[2] // TB
    b = pl.program_id(0)
    group = pl.program_id(1)
    i = pl.program_id(2)
    q_t = [_query_operands(qa_ref[0, e])[0] for e in range(HP)]
    k_row = lax.broadcasted_iota(jnp.int32, (TB, TB), 0)
    q_col = lax.broadcasted_iota(jnp.int32, (TB, TB), 1)

    def step(j, carry, masked):
        out = []
        for e in range(HP):
            m_run, l_run, acc = carry[e]
            k_blk = ka_ref[0, e, pl.ds(pl.multiple_of(j * TB, TB), TB), :]
            s_t = jnp.dot(k_blk, q_t[e], preferred_element_type=F32)
            if masked:
                s_t = jnp.where(k_row <= q_col, s_t, MASK_VALUE)
            hd = HP * group + e
            delta = (start_ref[(b * n_blocks + i) * FOX_HEADS + hd]
                     - start_ref[(b * n_blocks + j) * FOX_HEADS + hd])
            m_new = jnp.maximum(m_run, jnp.max(s_t, axis=0, keepdims=True) + delta)
            p_t = jnp.exp2(s_t - (m_new - delta))
            alpha = jnp.exp2(m_run - m_new)
            l_new = alpha * l_run + jnp.sum(p_t, axis=0, keepdims=True)
            v_t = vt_ref[0, j, pl.ds(e * FOX_HEAD_DIM, FOX_HEAD_DIM), :]
            acc_new = alpha * acc + jnp.dot(v_t, p_t.astype(BF16), preferred_element_type=F32)
            out.append((m_new, l_new, acc_new))
        return tuple(out)

    init = (jnp.full((1, TB), MASK_VALUE, F32), jnp.zeros((1, TB), F32),
            jnp.zeros((FOX_HEAD_DIM, TB), F32))
    carry = lax.fori_loop(0, i, lambda j, c: step(j, c, False), (init,) * HP)
    carry = step(i, carry, True)
    _gated_output(jnp.concatenate([acc / l_run for _, l_run, acc in carry], axis=0), g_ref, o_ref,
                  slice(None))


def _fox_attention(qa, ka, vt, g, cq, edges, *, bounded):
    b, n_heads, s, _ = qa.shape
    _, nvb, w, vb = vt.shape
    if bounded:
        tile, hp, qt = ATTN_Q_TILE, ATTN_HEADS, ATTN_TILES_PER_STEP
    else:
        tile, hp, qt = vb, LANES // FOX_HEAD_DIM, 1
    gw = hp * FOX_HEAD_DIM
    rows = qt * tile
    specs = dict(
        table=pl.BlockSpec(memory_space=pltpu.SMEM),
        q=pl.BlockSpec((1, hp, rows, LANES), lambda bi, p, i: (bi, p, i, 0)),
        k=pl.BlockSpec((1, hp, s, LANES), lambda bi, p, i: (bi, p, 0, 0)),
        v=pl.BlockSpec((1, nvb, gw, vb), lambda bi, p, i: (bi, 0, p, 0)),
        cq=pl.BlockSpec((1, hp, 1, rows), lambda bi, p, i: (bi, p, 0, i)),
        g=pl.BlockSpec((1, rows, gw), lambda bi, p, i: (bi, i, p)))
    if bounded:
        body = functools.partial(_fox_bounded_kernel, tile=tile, heads=hp)
        names = ("table", "table", "q", "k", "v", "cq", "g")
        step_start = edges[:, ::rows // vb, 0, :FOX_HEADS]
        tile_end = edges[:, tile // vb - 1::tile // vb, 1, :FOX_HEADS]
        dead = (step_start[:, :, None] - tile_end[:, None]) < -FOX_DEAD_EXPONENT
        before = jnp.arange(s // rows)[:, None] * qt > jnp.arange(s // tile)[None, :]
        first_live = jnp.sum(dead & before[None, :, :, None], axis=2)
        last_before = jnp.maximum(jnp.arange(s // rows) * qt - 1, 0)
        first_live = jnp.minimum(first_live, last_before[None, :, None])
        first_live = jnp.min(first_live.reshape(b, s // rows, n_heads // hp, hp), axis=-1)
        args = (first_live.transpose(0, 2, 1).reshape(-1).astype(jnp.int32),
                edges[:, :, 1, :FOX_HEADS].reshape(-1), qa, ka, vt, cq, g)
        scratch = [pltpu.VMEM((qt, hp, tile, tile), BF16),
                   pltpu.VMEM((qt, hp, FOX_HEAD_DIM, tile), F32)]
    else:
        body = functools.partial(_fox_online_kernel, tile=tile, heads=hp)
        names = ("table", "q", "k", "v", "g")
        args = (edges[:, :, 0, :FOX_HEADS].reshape(-1), qa, ka, vt, g)
        scratch = []
    return pl.pallas_call(
        body,
        grid=(b, n_heads // hp, s // rows),
        in_specs=[specs[n] for n in names],
        out_specs=specs["g"],
        out_shape=jax.ShapeDtypeStruct((b, s, w), BF16),
        scratch_shapes=scratch,
        compiler_params=_params("arbitrary", "arbitrary", "arbitrary"),
        name="fox_attention_bounded" if bounded else "fox_attention_online",
    )(*args)


def _out_proj_kernel(a_ref, x_ref, mod_ref, w_ref, o_ref):
    y = jnp.dot(a_ref[0], w_ref[...], preferred_element_type=F32)
    o_ref[0] = x_ref[0] + mod_ref[0, 2:3, :] * y


def _out_proj(a, x, mod, w_out):
    b, s, d = x.shape
    tile = 512
    k = a.shape[-1]
    return pl.pallas_call(
        _out_proj_kernel,
        grid=(b, s // tile),
        in_specs=[pl.BlockSpec((1, tile, k), lambda bi, ti: (bi, ti, 0)),
                  pl.BlockSpec((1, tile, d), lambda bi, ti: (bi, ti, 0)),
                  pl.BlockSpec((1, 3, d), lambda bi, ti: (bi, 0, 0)),
                  pl.BlockSpec((k, d), lambda bi, ti: (0, 0))],
        out_specs=pl.BlockSpec((1, tile, d), lambda bi, ti: (bi, ti, 0)),
        out_shape=jax.ShapeDtypeStruct((b, s, d), F32),
        compiler_params=_params("arbitrary", "arbitrary"),
        name="layer1_out_proj",
    )(a, x, mod, w_out.astype(BF16))


def _layer1(x, mod, norm_g, w_in, b_f, qnorm_g, knorm_g, w_out):
    d, w = x.shape[-1], FOX_WIDTH
    order = jnp.argsort(b_f)
    w_heads = w_in[:, :4 * w].astype(BF16).reshape(d, 4, FOX_HEADS, FOX_HEAD_DIM)[:, :, order]
    w_forget = jnp.zeros((d, LANES), BF16).at[:, :FOX_HEADS].set(
        w_in[:, 4 * w:].astype(BF16)[:, order])
    w_in = jnp.concatenate([w_heads.reshape(d, 4 * w), w_forget], axis=1)
    w_out = w_out.astype(BF16).reshape(FOX_HEADS, FOX_HEAD_DIM, d)[order].reshape(w, d)
    qa, ka, vt, g, cq, edges = _layer1_proj(x, mod, norm_g, w_in, b_f[order], qnorm_g, knorm_g)
    score_bound = (LOG2E * FOX_HEAD_DIM ** 0.5
                   * jnp.max(jnp.abs(qnorm_g)) * jnp.max(jnp.abs(knorm_g)))
    gated = lax.cond(score_bound <= FOX_SAFE_LOGIT,
                     functools.partial(_fox_attention, bounded=True),
                     functools.partial(_fox_attention, bounded=False),
                     qa, ka, vt, g, cq, edges)
    return _out_proj(gated, x, mod, w_out)


def kernel(x, c, norm_g, ada_w, ada_b, hgrn_lb, even_w_in, hgrn_onorm_g, pool_w, pool_scale,
           even_w_out, odd_w_in, fox_b_f, fox_qnorm_g, fox_knorm_g, odd_w_out):
    depth = norm_g.shape[0]
    mods = _adaln_mods(c, ada_w, ada_b)
    for l in range(depth):
        j = l // 2
        if l % 2 == 0:
            x = _layer0(x, mods[l], norm_g[l], even_w_in[j], hgrn_lb, hgrn_onorm_g[j],
                        pool_w[j], pool_scale[j], even_w_out[j], layer_slot=l)
        else:
            x = _layer1(x, mods[l], norm_g[l], odd_w_in[j], fox_b_f[j], fox_qnorm_g[j],
                        fox_knorm_g[j], odd_w_out[j])
    return x
```

```python
import functools

import jax
import jax.numpy as jnp
import numpy as np
from jax import lax
from jax.experimental import pallas as pl
from jax.experimental.pallas import tpu as pltpu

F32 = jnp.float32
BF16 = jnp.bfloat16
EPS = 1e-6

HGRN_HEADS = 4
HGRN_KEY = 128
HGRN_VAL = 128
HGRN_KW = HGRN_HEADS * HGRN_KEY
HGRN_VW = HGRN_HEADS * HGRN_VAL
POOL_WINDOWS = (2, 4, 8, 16)
POOL_GROUP = 128
POOL_WIDTH = POOL_GROUP * len(POOL_WINDOWS)
POOL_HISTORY = 16
FOX_HEADS = 16
FOX_HEAD_DIM = 64
FOX_WIDTH = FOX_HEADS * FOX_HEAD_DIM

LANES = 128
SUBLANES = 8
VMEM_LIMIT_BYTES = 56 * 1024 * 1024

SEQ_TILE = 256
HGRN_CHUNK = 128
ATTN_Q_TILE = 512
ATTN_HEADS = 4
ATTN_TILES_PER_STEP = 2
FOX_SAFE_LOGIT = 100.0
FOX_DEAD_EXPONENT = 150.0
LOG2E = 1.4426950408889634
MASK_VALUE = -1e30
AUG_LANE = FOX_HEAD_DIM


def _silu(x):
    return x * jax.nn.sigmoid(x)


def _params(*semantics):
    return pltpu.CompilerParams(dimension_semantics=semantics, vmem_limit_bytes=VMEM_LIMIT_BYTES)


def _mod_kernel(c_ref, w_ref, b_ref, o_ref):
    w = w_ref[0]
    for bi in range(c_ref.shape[0]):
        cond = _silu(c_ref[bi])
        cols = [jnp.sum(w[:, j:j + LANES] * cond, axis=0, keepdims=True)
                for j in range(0, w.shape[1], LANES)]
        o_ref[0, bi:bi + 1, :] = jnp.concatenate(cols, axis=1) + b_ref[0]


def _adaln_mods(c, ada_w, ada_b):
    depth, d, n = ada_w.shape
    b = c.shape[0]
    tn = 1024
    out = pl.pallas_call(
        _mod_kernel,
        grid=(depth, n // tn),
        in_specs=[pl.BlockSpec((b, d, LANES), lambda l, j: (0, 0, 0)),
                  pl.BlockSpec((1, d, tn), lambda l, j: (l, 0, j)),
                  pl.BlockSpec((1, 1, tn), lambda l, j: (l, 0, j))],
        out_specs=pl.BlockSpec((1, b, tn), lambda l, j: (l, 0, j)),
        out_shape=jax.ShapeDtypeStruct((depth, b, n), F32),
        compiler_params=_params("arbitrary", "arbitrary"),
        name="adaln_mods",
    )(jnp.broadcast_to(c[:, :, None], (b, d, LANES)), ada_w, ada_b.reshape(depth, 1, n))
    return out.reshape(depth, b, 3, d)


def _modulated_norm(x, shift, scale, gain):
    ms = jnp.mean(x * x, axis=-1, keepdims=True)
    return x * lax.rsqrt(ms + EPS) * gain * (1.0 + scale) + shift


def _layer0_kernel(x_ref, mod_ref, ng_ref, win_ref, lb_ref, og_ref, pw_ref, ps_ref, wout_ref,
                   o_ref, state_ref, carry_ref, *, tile, chunk, layer_slot):
    T, C, H = tile, chunk, HGRN_HEADS
    batch = range(x_ref.shape[0])
    t_idx = pl.program_id(0)

    @pl.when(t_idx == 0)
    def _():
        state_ref[...] = jnp.zeros_like(state_ref)
        carry_ref[...] = jnp.zeros_like(carry_ref)

    lbv = lb_ref[...]
    e = jnp.exp(lbv - jnp.max(lbv, axis=0, keepdims=True))
    lower = (jnp.sum(e[0:layer_slot + 1], axis=0, keepdims=True)
             / jnp.sum(e, axis=0, keepdims=True))

    xs, parts = [], []
    for bi in batch:
        x = x_ref[bi]
        h = _modulated_norm(x, mod_ref[bi, 0:1, :], mod_ref[bi, 1:2, :], ng_ref[...])
        proj = jnp.dot(h.astype(BF16), win_ref[...], preferred_element_type=F32)
        o0, widths, piece = 0, (HGRN_KW, HGRN_KW, HGRN_VW, HGRN_VW, POOL_WIDTH, POOL_WIDTH), []
        for wd in widths:
            piece.append(proj[:, o0:o0 + wd])
            o0 += wd
        xs.append(x)
        parts.append(piece)

    row = lax.broadcasted_iota(jnp.int32, (T, HGRN_KW), 0)
    ti = lax.broadcasted_iota(jnp.int32, (C, C), 0)
    si = lax.broadcasted_iota(jnp.int32, (C, C), 1)
    n_chunks = T // C
    nt_dims = (((1,), (1,)), ((), ()))
    tn_dims = (((0,), (0,)), ((), ()))

    def level_scores(qe, ke, mask, scores):
        qb = qe.astype(BF16)
        kb = ke.astype(BF16)
        out = []
        for c in range(n_chunks):
            for hh in range(H):
                rs = slice(c * C, (c + 1) * C)
                cs = slice(hh * HGRN_KEY, (hh + 1) * HGRN_KEY)
                d = lax.dot_general(qb[rs, cs], kb[rs, cs], nt_dims, preferred_element_type=F32)
                prev = scores[c * H + hh]
                out.append(jnp.where(mask, d, 0.0 if prev is None else prev))
        return out

    keys, scores, p_sum, q_sum, total = [], [], [], [], []
    for bi in batch:
        q, f = parts[bi][0], parts[bi][1]
        forget = lower + (1.0 - lower) * jax.nn.sigmoid(f)
        logf = jnp.log(forget) * LOG2E
        keys.append(1.0 - forget)
        scores.append(level_scores(q, keys[bi], ti == si, [None] * (n_chunks * H)))
        p_sum.append(logf)
        q_sum.append(jnp.zeros_like(logf))
        total.append(logf)
    m = 1
    while m < C:
        mask = ((ti ^ si) < 2 * m) & ((ti & m) != 0) & ((si & m) == 0)
        upper = (row & m) != 0
        for bi in batch:
            scores[bi] = level_scores(parts[bi][0] * jnp.exp2(p_sum[bi]),
                                      keys[bi] * jnp.exp2(q_sum[bi]), mask, scores[bi])
            t_dn = pltpu.roll(total[bi], m, 0)
            t_up = pltpu.roll(total[bi], T - m, 0)
            p_sum[bi] = p_sum[bi] + jnp.where(upper, t_dn, 0.0)
            q_sum[bi] = q_sum[bi] + jnp.where(upper, 0.0, t_up)
            total[bi] = total[bi] + jnp.where(upper, t_dn, t_up)
        m *= 2

    o_a = []
    for bi in batch:
        q, _, val, g_a = parts[bi][:4]
        q_in = (q * jnp.exp2(p_sum[bi])).astype(BF16)
        k_out = (keys[bi] * jnp.exp2(q_sum[bi])).astype(BF16)
        decay = jnp.exp2(total[bi])
        val_b = val.astype(BF16)
        oa_rows = []
        for c in range(n_chunks):
            rs = slice(c * C, (c + 1) * C)
            heads_out = []
            for hh in range(H):
                cs = slice(hh * HGRN_KEY, (hh + 1) * HGRN_KEY)
                st = state_ref[bi, hh]
                vb = val_b[rs, cs]
                o = jnp.dot(scores[bi][c * H + hh].astype(BF16), vb, preferred_element_type=F32)
                o = o + lax.dot_general(q_in[rs, cs], st.astype(BF16), nt_dims,
                                        preferred_element_type=F32)
                state_ref[bi, hh] = (st * decay[c * C:c * C + 1, cs]
                                     + lax.dot_general(vb, k_out[rs, cs], tn_dims,
                                                       preferred_element_type=F32))
                ms_o = jnp.mean(o * o, axis=-1, keepdims=True)
                heads_out.append(o * lax.rsqrt(ms_o + EPS))
            oa_rows.append(jnp.concatenate(heads_out, axis=1))
        o_a.append(jnp.concatenate(oa_rows, axis=0) * og_ref[...] * _silu(g_a))

    pos = lax.broadcasted_iota(jnp.int32, (T, POOL_GROUP), 0) + t_idx * T + 1
    for bi in batch:
        u, g_b = parts[bi][4], parts[bi][5]
        ub = jnp.concatenate([carry_ref[bi], u], axis=0)
        carry_ref[bi] = u[T - POOL_HISTORY:T, :]
        wins = []
        acc = ub
        sh = 1
        while sh < max(POOL_WINDOWS):
            acc = acc + pltpu.roll(acc, sh, 0)
            sh *= 2
            wins.append(acc)
        ob = []
        for gi, w in enumerate(POOL_WINDOWS):
            cs = slice(gi * POOL_GROUP, (gi + 1) * POOL_GROUP)
            win = wins[w.bit_length() - 2][POOL_HISTORY:, cs]
            cnt = jnp.minimum(pos, w).astype(F32)
            pooled = win / cnt - u[:, cs]
            ob.append(jnp.dot(pooled.astype(BF16), pw_ref[gi], preferred_element_type=F32))
        o_b = jnp.concatenate(ob, axis=1) * ps_ref[...] * _silu(g_b)
        mixed = jnp.concatenate([o_a[bi], o_b], axis=1).astype(BF16)
        y = jnp.dot(mixed, wout_ref[...], preferred_element_type=F32)
        o_ref[bi] = xs[bi] + mod_ref[bi, 2:3, :] * y


def _layer0(x, mod, norm_g, w_in, hgrn_lb, onorm_g, pool_w, pool_scale, w_out, layer_slot):
    b, s, d = x.shape
    tile = SEQ_TILE
    n_in = w_in.shape[1]
    mix = w_out.shape[0]
    const2 = lambda ti: (0, 0)
    const3 = lambda ti: (0, 0, 0)
    return pl.pallas_call(
        functools.partial(_layer0_kernel, tile=tile, chunk=HGRN_CHUNK, layer_slot=layer_slot),
        grid=(s // tile,),
        in_specs=[pl.BlockSpec((b, tile, d), lambda ti: (0, ti, 0)),
                  pl.BlockSpec((b, 3, d), const3),
                  pl.BlockSpec((1, d), const2),
                  pl.BlockSpec((d, n_in), const2, pipeline_mode=pl.Buffered(1)),
                  pl.BlockSpec(hgrn_lb.shape, const2),
                  pl.BlockSpec((1, HGRN_VW), const2),
                  pl.BlockSpec(pool_w.shape, const3),
                  pl.BlockSpec((1, POOL_WIDTH), const2),
                  pl.BlockSpec((mix, d), const2, pipeline_mode=pl.Buffered(1))],
        out_specs=pl.BlockSpec((b, tile, d), lambda ti: (0, ti, 0)),
        out_shape=jax.ShapeDtypeStruct((b, s, d), F32),
        scratch_shapes=[pltpu.VMEM((b, HGRN_HEADS, HGRN_VAL, HGRN_KEY), F32),
                        pltpu.VMEM((b, POOL_HISTORY, POOL_WIDTH), F32)],
        compiler_params=_params("arbitrary"),
        name="layer0_hgrn_pool",
    )(x, mod, norm_g.reshape(1, d), w_in.astype(BF16), hgrn_lb, onorm_g.reshape(1, HGRN_VW),
      pool_w.astype(BF16), pool_scale.reshape(1, POOL_WIDTH), w_out.astype(BF16))


def _split3(c):
    hi = c.astype(BF16).astype(F32)
    r = c - hi
    mid = r.astype(BF16).astype(F32)
    lo = (r - mid).astype(BF16).astype(F32)
    return hi, mid, lo


def _carrier_selectors():
    per_tile = LANES // FOX_HEAD_DIM
    sel = np.zeros((FOX_HEADS // per_tile, LANES, per_tile * LANES), np.float32)
    for hd in range(FOX_HEADS):
        base = (hd % per_tile) * LANES + AUG_LANE
        for j in range(3):
            sel[hd // per_tile, j * FOX_HEADS + hd, base + j] = 1.0
            sel[hd // per_tile, j * FOX_HEADS + hd, base + 3 + j] = -1.0
            sel[hd // per_tile, (3 + j) * FOX_HEADS + hd, base + 6 + j] = -1.0
    return jnp.asarray(sel, BF16)


def _layer1_proj_kernel(x_ref, mod_ref, ng_ref, win_ref, bf_ref, qg_ref, kg_ref, bd_ref, sel_ref,
                        qa_ref, ka_ref, vt_ref, g_ref, cq_ref, edge_ref, carry_ref, tile_start_ref,
                        *, tile):
    T, W = tile, FOX_WIDTH
    t_idx = pl.program_id(1)

    @pl.when(t_idx == 0)
    def _():
        carry_ref[...] = jnp.zeros_like(carry_ref)

    x = x_ref[0]
    h = _modulated_norm(x, mod_ref[0, 0:1, :], mod_ref[0, 1:2, :], ng_ref[...])
    proj = jnp.dot(h.astype(BF16), win_ref[...], preferred_element_type=F32)
    q = proj[:, 0:W]
    k = proj[:, W:2 * W]
    v = proj[:, 2 * W:3 * W]
    g_ref[0] = proj[:, 3 * W:4 * W]
    fl = proj[:, 4 * W:4 * W + LANES]
    vt_ref[0, 0] = v.T.astype(BF16)

    z = fl + bf_ref[...]
    logf = jnp.minimum(z, 0.0) - jnp.log1p(jnp.exp(-jnp.abs(z)))
    row = lax.broadcasted_iota(jnp.int32, (T, LANES), 0)
    cum = logf
    sh = 1
    while sh < T:
        cum = cum + jnp.where(row >= sh, pltpu.roll(cum, sh, 0), 0.0)
        sh *= 2
    start = carry_ref[...]
    end = start + cum[T - 1:T, :]
    carry_ref[...] = end

    @pl.when(t_idx % (ATTN_Q_TILE // T) == 0)
    def _():
        tile_start_ref[...] = start

    tile_start = tile_start_ref[...]
    edge_ref[0, 0] = jnp.concatenate([tile_start, end], axis=0) * LOG2E
    s_hi, s_mid, s_lo = _split3((cum + (start - tile_start)) * LOG2E)
    e_hi, e_mid, e_lo = _split3((cum - cum[T - 1:T, :]) * LOG2E)
    total_t = ((cum + start) * LOG2E).T
    for hd in range(FOX_HEADS):
        cq_ref[0, hd] = total_t[hd:hd + 1, :]

    lane = lax.broadcasted_iota(jnp.int32, (T, LANES), 1)
    ones_q = jnp.where((lane >= AUG_LANE + 3) & (lane < AUG_LANE + 9), 1.0, 0.0)
    ones_k = jnp.where((lane >= AUG_LANE) & (lane < AUG_LANE + 3), 1.0, 0.0)
    terms = (s_hi, s_mid, s_lo, e_hi, e_mid, e_lo)
    packed = jnp.zeros((T, LANES), F32)
    for j, term in enumerate(terms):
        moved = term if j == 0 else pltpu.roll(term, j * FOX_HEADS, 1)
        packed = jnp.where((lane >= j * FOX_HEADS) & (lane < (j + 1) * FOX_HEADS), moved, packed)
    packed = packed.astype(BF16)
    bd = bd_ref[...]
    qk_scale = FOX_HEAD_DIM ** -0.5 * LOG2E
    for pair in range(FOX_HEADS // 2):
        cs = slice(pair * LANES, (pair + 1) * LANES)
        qp = q[:, cs]
        kp = k[:, cs]
        squares = jnp.concatenate([(qp * qp).astype(BF16), (kp * kp).astype(BF16)], axis=1)
        ms = jnp.dot(squares, bd, preferred_element_type=F32)
        qn = qp * lax.rsqrt(ms[:, :LANES] + EPS) * (qg_ref[...] * qk_scale)
        kn = kp * lax.rsqrt(ms[:, LANES:] + EPS) * kg_ref[...]
        aug = jnp.dot(packed, sel_ref[pair], preferred_element_type=F32)
        for e in range(LANES // FOX_HEAD_DIM):
            hd = (LANES // FOX_HEAD_DIM) * pair + e
            carriers = aug[:, e * LANES:(e + 1) * LANES]
            q_main = qn if e == 0 else pltpu.roll(qn, FOX_HEAD_DIM, 1)
            k_main = kn if e == 0 else pltpu.roll(kn, FOX_HEAD_DIM, 1)
            q_aug = jnp.where(lane < AUG_LANE + 3, carriers, ones_q)
            k_aug = jnp.where(lane < AUG_LANE + 3, ones_k, carriers)
            qa_ref[0, hd] = jnp.where(lane < FOX_HEAD_DIM, q_main, q_aug).astype(BF16)
            ka_ref[0, hd] = jnp.where(lane < FOX_HEAD_DIM, k_main, k_aug).astype(BF16)


def _layer1_proj(x, mod, norm_g, w_in, b_f, qnorm_g, knorm_g):
    b, s, d = x.shape
    tile = SEQ_TILE
    nt = s // tile
    w = FOX_WIDTH
    n_pad = w_in.shape[1]
    bf_pad = jnp.zeros((1, LANES), F32).at[0, :FOX_HEADS].set(b_f)
    blk = jnp.arange(2 * LANES) // FOX_HEAD_DIM
    bd = jnp.where(blk[:, None] == blk[None, :], 1.0 / FOX_HEAD_DIM, 0.0).astype(BF16)
    qg = jnp.tile(qnorm_g, LANES // FOX_HEAD_DIM).reshape(1, LANES)
    kg = jnp.tile(knorm_g, LANES // FOX_HEAD_DIM).reshape(1, LANES)
    const2 = lambda bi, ti: (0, 0)
    return pl.pallas_call(
        functools.partial(_layer1_proj_kernel, tile=tile),
        grid=(b, nt),
        in_specs=[pl.BlockSpec((1, tile, d), lambda bi, ti: (bi, ti, 0)),
                  pl.BlockSpec((1, 3, d), lambda bi, ti: (bi, 0, 0)),
                  pl.BlockSpec((1, d), const2),
                  pl.BlockSpec((d, n_pad), const2),
                  pl.BlockSpec((1, LANES), const2),
                  pl.BlockSpec((1, LANES), const2),
                  pl.BlockSpec((1, LANES), const2),
                  pl.BlockSpec((2 * LANES, 2 * LANES), const2),
                  pl.BlockSpec((FOX_HEADS // 2, LANES, 2 * LANES), lambda bi, ti: (0, 0, 0))],
        out_specs=[pl.BlockSpec((1, FOX_HEADS, tile, LANES), lambda bi, ti: (bi, 0, ti, 0)),
                   pl.BlockSpec((1, FOX_HEADS, tile, LANES), lambda bi, ti: (bi, 0, ti, 0)),
                   pl.BlockSpec((1, 1, w, tile), lambda bi, ti: (bi, ti, 0, 0)),
                   pl.BlockSpec((1, tile, w), lambda bi, ti: (bi, ti, 0)),
                   pl.BlockSpec((1, FOX_HEADS, 1, tile), lambda bi, ti: (bi, 0, 0, ti)),
                   pl.BlockSpec((1, 1, 2, LANES), lambda bi, ti: (bi, ti, 0, 0))],
        out_shape=[jax.ShapeDtypeStruct((b, FOX_HEADS, s, LANES), BF16),
                   jax.ShapeDtypeStruct((b, FOX_HEADS, s, LANES), BF16),
                   jax.ShapeDtypeStruct((b, nt, w, tile), BF16),
                   jax.ShapeDtypeStruct((b, s, w), F32),
                   jax.ShapeDtypeStruct((b, FOX_HEADS, 1, s), F32),
                   jax.ShapeDtypeStruct((b, nt, 2, LANES), F32)],
        scratch_shapes=[pltpu.VMEM((1, LANES), F32), pltpu.VMEM((1, LANES), F32)],
        compiler_params=_params("arbitrary", "arbitrary"),
        name="layer1_proj",
    )(x, mod, norm_g.reshape(1, d), w_in, bf_pad, qg, kg, bd, _carrier_selectors())


def _query_operands(q):
    slab = 2 * SUBLANES
    row = lax.broadcasted_iota(jnp.int32, (slab, q.shape[0]), 0)
    q_t = q.astype(F32).T
    carriers = q_t[AUG_LANE:AUG_LANE + slab]
    return tuple(jnp.concatenate([q_t[:AUG_LANE], jnp.where(keep, carriers, 0.0),
                                  q_t[AUG_LANE + slab:]], axis=0).astype(BF16)
                 for keep in (row < 6, row >= 6))


def _gated_output(acc_t, g_ref, o_ref, rows):
    o_ref[0, rows] = (acc_t.T * _silu(g_ref[0, rows])).astype(BF16)


def _fox_bounded_kernel(first_ref, end_ref, qa_ref, ka_ref, vt_ref, cq_ref, g_ref, o_ref,
                        p_ref, acc_ref, *, tile, heads):
    TQ, HP = tile, heads
    QT = qa_ref.shape[2] // TQ
    SB = vt_ref.shape[-1]
    R = TQ // SB
    n_blocks = ka_ref.shape[2] // SB
    b = pl.program_id(0)
    group = pl.program_id(1)
    i0 = pl.program_id(2) * QT
    chains = [(qs, e) for qs in range(QT) for e in range(HP)]
    cols = [slice(qs * TQ, (qs + 1) * TQ) for qs in range(QT)]
    operands = {(qs, e): _query_operands(qa_ref[0, e, cols[qs], :]) for qs, e in chains}
    causal = (lax.broadcasted_iota(jnp.int32, (TQ, TQ), 0)
              <= lax.broadcasted_iota(jnp.int32, (TQ, TQ), 1))

    def keys(e, n):
        return ka_ref[0, e, pl.ds(pl.multiple_of(n * TQ, TQ), TQ), :]

    def values_t(e, blk):
        return vt_ref[0, blk, pl.ds(e * FOX_HEAD_DIM, FOX_HEAD_DIM), :]

    def rest(c, blk, gate=1.0):
        qs, e = c
        end = end_ref[(b * n_blocks + blk) * FOX_HEADS + HP * group + e]
        return jnp.exp2(jnp.minimum(cq_ref[0, e][:, cols[qs]] - end, 0.0)) * gate

    def col_partial(p):
        return jnp.sum(p.reshape(p.shape[0] // SUBLANES, SUBLANES, p.shape[1]), axis=0)

    def far_scores(c, n):
        return jnp.dot(keys(c[1], n), operands[c][1], preferred_element_type=F32)

    def contract(c, n, p_bf16, gate=1.0):
        for r in range(R):
            pv = jnp.dot(values_t(c[1], n * R + r), p_bf16[r * SB:(r + 1) * SB],
                         preferred_element_type=F32)
            acc_ref[c] += rest(c, n * R + r, gate) * pv

    def weighted_partial(c, n, p, gate=1.0):
        return sum(rest(c, n * R + r, gate) * col_partial(p[r * SB:(r + 1) * SB])
                   for r in range(R))

    first = first_ref[(b * pl.num_programs(1) + group) * pl.num_programs(2) + pl.program_id(2)]
    gate0 = jnp.where(i0 > 0, 1.0, 0.0)

    l_part = {}

    def own_tile(c):
        qs, e = c
        s_t = jnp.dot(keys(e, i0 + qs), operands[c][0], preferred_element_type=F32)
        p = jnp.where(causal, jnp.exp2(s_t), 0.0)
        l_part[c] = col_partial(p)
        p = p.astype(BF16)

        def later():
            v_t = jnp.concatenate([values_t(e, (i0 + qs) * R + r) for r in range(R)], axis=1)
            acc_ref[c] = jnp.dot(v_t, p, preferred_element_type=F32)
        return later

    def earlier_tile(c, n):
        p = jnp.exp2(far_scores(c, n))
        l_part[c] = l_part[c] + weighted_partial(c, n, p)
        p = p.astype(BF16)
        return lambda: contract(c, n, p)

    def fill(c):
        p = jnp.exp2(far_scores(c, first))
        l_part[c] = l_part[c] + weighted_partial(c, first, p, gate0)
        p_ref[c] = p.astype(BF16)

    jobs = ([functools.partial(own_tile, c) for c in chains]
            + [functools.partial(earlier_tile, c, i0 + t) for c in chains for t in range(c[0])]
            + [functools.partial(fill, c) for c in chains])
    pending = None
    for job in jobs:
        later = job()
        if pending is not None:
            pending()
        pending = later
    if pending is not None:
        pending()

    def body(n, l_run):
        s_new = []
        for c in chains:
            s_new.append(far_scores(c, n))
            contract(c, n - 1, p_ref[c])
        out = []
        for c, s_t, l_c in zip(chains, s_new, l_run):
            p = jnp.exp2(s_t)
            out.append(l_c + weighted_partial(c, n, p))
            p_ref[c] = p.astype(BF16)
        return tuple(out)

    l_run = lax.fori_loop(first + 1, i0, body, tuple(l_part[c] for c in chains))
    for c in chains:
        contract(c, jnp.maximum(i0 - 1, 0), p_ref[c], gate0)
    for qs in range(QT):
        out_t = [acc_ref[qs, e] * (1.0 / jnp.sum(l_run[qs * HP + e], axis=0, keepdims=True))
                 for e in range(HP)]
        _gated_output(jnp.concatenate(out_t, axis=0), g_ref, o_ref, cols[qs])


def _fox_online_kernel(start_ref, qa_ref, ka_ref, vt_ref, g_ref, o_ref, *, tile, heads):
    TB, HP = tile, heads
    n_blocks = ka_ref.shape[2] // TB
    b = pl.program_id(0)
    group = pl.program_id(1)
    i = pl.program_id(2)
    q_t = [_query_operands(qa_ref[0, e])[0] for e in range(HP)]
    k_row = lax.broadcasted_iota(jnp.int32, (TB, TB), 0)
    q_col = lax.broadcasted_iota(jnp.int32, (TB, TB), 1)

    def step(j, carry, masked):
        out = []
        for e in range(HP):
            m_run, l_run, acc = carry[e]
            k_blk = ka_ref[0, e, pl.ds(pl.multiple_of(j * TB, TB), TB), :]
            s_t = jnp.dot(k_blk, q_t[e], preferred_element_type=F32)
            if masked:
                s_t = jnp.where(k_row <= q_col, s_t, MASK_VALUE)
            hd = HP * group + e
            delta = (start_ref[(b * n_blocks + i) * FOX_HEADS + hd]
                     - start_ref[(b * n_blocks + j) * FOX_HEADS + hd])
            m_new = jnp.maximum(m_run, jnp.max(s_t, axis=0, keepdims=True) + delta)
            p_t = jnp.exp2(s_t - (m_new - delta))
            alpha = jnp.exp2(m_run - m_new)
            l_new = alpha * l_run + jnp.sum(p_t, axis=0, keepdims=True)
            v_t = vt_ref[0, j, pl.ds(e * FOX_HEAD_DIM, FOX_HEAD_DIM), :]
            acc_new = alpha * acc + jnp.dot(v_t, p_t.astype(BF16), preferred_element_type=F32)
            out.append((m_new, l_new, acc_new))
        return tuple(out)

    init = (jnp.full((1, TB), MASK_VALUE, F32), jnp.zeros((1, TB), F32),
            jnp.zeros((FOX_HEAD_DIM, TB), F32))
    carry = lax.fori_loop(0, i, lambda j, c: step(j, c, False), (init,) * HP)
    carry = step(i, carry, True)
    _gated_output(jnp.concatenate([acc / l_run for _, l_run, acc in carry], axis=0), g_ref, o_ref,
                  slice(None))


def _fox_attention(qa, ka, vt, g, cq, edges, *, bounded):
    b, n_heads, s, _ = qa.shape
    _, nvb, w, vb = vt.shape
    if bounded:
        tile, hp, qt = ATTN_Q_TILE, ATTN_HEADS, ATTN_TILES_PER_STEP
    else:
        tile, hp, qt = vb, LANES // FOX_HEAD_DIM, 1
    gw = hp * FOX_HEAD_DIM
    rows = qt * tile
    specs = dict(
        table=pl.BlockSpec(memory_space=pltpu.SMEM),
        q=pl.BlockSpec((1, hp, rows, LANES), lambda bi, p, i: (bi, p, i, 0)),
        k=pl.BlockSpec((1, hp, s, LANES), lambda bi, p, i: (bi, p, 0, 0)),
        v=pl.BlockSpec((1, nvb, gw, vb), lambda bi, p, i: (bi, 0, p, 0)),
        cq=pl.BlockSpec((1, hp, 1, rows), lambda bi, p, i: (bi, p, 0, i)),
        g=pl.BlockSpec((1, rows, gw), lambda bi, p, i: (bi, i, p)))
    if bounded:
        body = functools.partial(_fox_bounded_kernel, tile=tile, heads=hp)
        names = ("table", "table", "q", "k", "v", "cq", "g")
        step_start = edges[:, ::rows // vb, 0, :FOX_HEADS]
        tile_end = edges[:, tile // vb - 1::tile // vb, 1, :FOX_HEADS]
        dead = (step_start[:, :, None] - tile_end[:, None]) < -FOX_DEAD_EXPONENT
        before = jnp.arange(s // rows)[:, None] * qt > jnp.arange(s // tile)[None, :]
        first_live = jnp.sum(dead & before[None, :, :, None], axis=2)
        last_before = jnp.maximum(jnp.arange(s // rows) * qt - 1, 0)
        first_live = jnp.minimum(first_live, last_before[None, :, None])
        first_live = jnp.min(first_live.reshape(b, s // rows, n_heads // hp, hp), axis=-1)
        args = (first_live.transpose(0, 2, 1).reshape(-1).astype(jnp.int32),
                edges[:, :, 1, :FOX_HEADS].reshape(-1), qa, ka, vt, cq, g)
        scratch = [pltpu.VMEM((qt, hp, tile, tile), BF16),
                   pltpu.VMEM((qt, hp, FOX_HEAD_DIM, tile), F32)]
    else:
        body = functools.partial(_fox_online_kernel, tile=tile, heads=hp)
        names = ("table", "q", "k", "v", "g")
        args = (edges[:, :, 0, :FOX_HEADS].reshape(-1), qa, ka, vt, g)
        scratch = []
    return pl.pallas_call(
        body,
        grid=(b, n_heads // hp, s // rows),
        in_specs=[specs[n] for n in names],
        out_specs=specs["g"],
        out_shape=jax.ShapeDtypeStruct((b, s, w), BF16),
        scratch_shapes=scratch,
        compiler_params=_params("arbitrary", "arbitrary", "arbitrary"),
        name="fox_attention_bounded" if bounded else "fox_attention_online",
    )(*args)


def _out_proj_kernel(a_ref, x_ref, mod_ref, w_ref, o_ref):
    y = jnp.dot(a_ref[0], w_ref[...], preferred_element_type=F32)
    o_ref[0] = x_ref[0] + mod_ref[0, 2:3, :] * y


def _out_proj(a, x, mod, w_out):
    b, s, d = x.shape
    tile = 512
    k = a.shape[-1]
    return pl.pallas_call(
        _out_proj_kernel,
        grid=(b, s // tile),
        in_specs=[pl.BlockSpec((1, tile, k), lambda bi, ti: (bi, ti, 0)),
                  pl.BlockSpec((1, tile, d), lambda bi, ti: (bi, ti, 0)),
                  pl.BlockSpec((1, 3, d), lambda bi, ti: (bi, 0, 0)),
                  pl.BlockSpec((k, d), lambda bi, ti: (0, 0))],
        out_specs=pl.BlockSpec((1, tile, d), lambda bi, ti: (bi, ti, 0)),
        out_shape=jax.ShapeDtypeStruct((b, s, d), F32),
        compiler_params=_params("arbitrary", "arbitrary"),
        name="layer1_out_proj",
    )(a, x, mod, w_out.astype(BF16))


def _layer1(x, mod, norm_g, w_in, b_f, qnorm_g, knorm_g, w_out):
    d, w = x.shape[-1], FOX_WIDTH
    order = jnp.argsort(b_f)
    w_in = w_in.astype(BF16)
    w_heads = w_in[:, :4 * w].reshape(d, 4, FOX_HEADS, FOX_HEAD_DIM)[:, :, order]
    w_forget = jnp.zeros((d, LANES), BF16).at[:, :FOX_HEADS].set(
        w_in[:, 4 * w:][:, order])
    w_in = jnp.concatenate([w_heads.reshape(d, 4 * w), w_forget], axis=1)
    w_out = w_out.astype(BF16).reshape(FOX_HEADS, FOX_HEAD_DIM, d)[order].reshape(w, d)
    qa, ka, vt, g, cq, edges = _layer1_proj(x, mod, norm_g, w_in, b_f[order], qnorm_g, knorm_g)
    score_bound = (LOG2E * FOX_HEAD_DIM ** 0.5
                   * jnp.max(jnp.abs(qnorm_g)) * jnp.max(jnp.abs(knorm_g)))
    gated = lax.cond(score_bound <= FOX_SAFE_LOGIT,
                     functools.partial(_fox_attention, bounded=True),
                     functools.partial(_fox_attention, bounded=False),
                     qa, ka, vt, g, cq, edges)
    return _out_proj(gated, x, mod, w_out)


def kernel(x, c, norm_g, ada_w, ada_b, hgrn_lb, even_w_in, hgrn_onorm_g, pool_w, pool_scale,
           even_w_out, odd_w_in, fox_b_f, fox_qnorm_g, fox_knorm_g, odd_w_out):
    depth = norm_g.shape[0]
    mods = _adaln_mods(c, ada_w, ada_b)
    for l in range(depth):
        j = l // 2
        if l % 2 == 0:
            x = _layer0(x, mods[l], norm_g[l], even_w_in[j], hgrn_lb, hgrn_onorm_g[j],
                        pool_w[j], pool_scale[j], even_w_out[j], layer_slot=l)
        else:
            x = _layer1(x, mods[l], norm_g[l], odd_w_in[j], fox_b_f[j], fox_qnorm_g[j],
                        fox_knorm_g[j], odd_w_out[j])
    return x
```

```python
import functools

import jax
import jax.numpy as jnp
import numpy as np
from jax import lax
from jax.experimental import pallas as pl
from jax.experimental.pallas import tpu as pltpu

F32 = jnp.float32
BF16 = jnp.bfloat16
EPS = 1e-6

HGRN_HEADS = 4
HGRN_KEY = 128
HGRN_VAL = 128
HGRN_KW = HGRN_HEADS * HGRN_KEY
HGRN_VW = HGRN_HEADS * HGRN_VAL
POOL_WINDOWS = (2, 4, 8, 16)
POOL_GROUP = 128
POOL_WIDTH = POOL_GROUP * len(POOL_WINDOWS)
POOL_HISTORY = 16
FOX_HEADS = 16
FOX_HEAD_DIM = 64
FOX_WIDTH = FOX_HEADS * FOX_HEAD_DIM

LANES = 128
SUBLANES = 8
VMEM_LIMIT_BYTES = 56 * 1024 * 1024

SEQ_TILE = 256
OUT_PROJ_TILE = 1024
HGRN_CHUNK = 128
ATTN_Q_TILE = 512
ATTN_HEADS = 4
ATTN_TILES_PER_STEP = 2
FOX_SAFE_LOGIT = 100.0
FOX_DEAD_EXPONENT = 150.0
LOG2E = 1.4426950408889634
MASK_VALUE = -1e30
AUG_LANE = FOX_HEAD_DIM


def _silu(x):
    return x * jax.nn.sigmoid(x)


def _params(*semantics):
    return pltpu.CompilerParams(dimension_semantics=semantics, vmem_limit_bytes=VMEM_LIMIT_BYTES)


def _mod_kernel(c_ref, w_ref, b_ref, o_ref):
    w = w_ref[0]
    for bi in range(c_ref.shape[0]):
        cond = _silu(c_ref[bi])
        cols = [jnp.sum(w[:, j:j + LANES] * cond, axis=0, keepdims=True)
                for j in range(0, w.shape[1], LANES)]
        o_ref[0, bi:bi + 1, :] = jnp.concatenate(cols, axis=1) + b_ref[0]


def _adaln_mods(c, ada_w, ada_b):
    depth, d, n = ada_w.shape
    b = c.shape[0]
    tn = n
    out = pl.pallas_call(
        _mod_kernel,
        grid=(depth, n // tn),
        in_specs=[pl.BlockSpec((b, d, LANES), lambda l, j: (0, 0, 0)),
                  pl.BlockSpec((1, d, tn), lambda l, j: (l, 0, j)),
                  pl.BlockSpec((1, 1, tn), lambda l, j: (l, 0, j))],
        out_specs=pl.BlockSpec((1, b, tn), lambda l, j: (l, 0, j)),
        out_shape=jax.ShapeDtypeStruct((depth, b, n), F32),
        compiler_params=_params("arbitrary", "arbitrary"),
        name="adaln_mods",
    )(jnp.broadcast_to(c[:, :, None], (b, d, LANES)), ada_w, ada_b.reshape(depth, 1, n))
    return out.reshape(depth, b, 3, d)


def _modulated_norm(x, shift, scale, gain):
    ms = jnp.mean(x * x, axis=-1, keepdims=True)
    return x * lax.rsqrt(ms + EPS) * gain * (1.0 + scale) + shift


def _layer0_kernel(x_ref, mod_ref, ng_ref, win_ref, lb_ref, og_ref, pw_ref, ps_ref, wout_ref,
                   o_ref, state_ref, carry_ref, *, tile, chunk, layer_slot):
    T, C, H = tile, chunk, HGRN_HEADS
    batch = range(x_ref.shape[0])
    t_idx = pl.program_id(0)

    @pl.when(t_idx == 0)
    def _():
        state_ref[...] = jnp.zeros_like(state_ref)
        carry_ref[...] = jnp.zeros_like(carry_ref)

    lbv = lb_ref[...]
    e = jnp.exp(lbv - jnp.max(lbv, axis=0, keepdims=True))
    lower = (jnp.sum(e[0:layer_slot + 1], axis=0, keepdims=True)
             / jnp.sum(e, axis=0, keepdims=True))

    xs, parts = [], []
    for bi in batch:
        x = x_ref[bi]
        h = _modulated_norm(x, mod_ref[bi, 0:1, :], mod_ref[bi, 1:2, :], ng_ref[...])
        proj = jnp.dot(h.astype(BF16), win_ref[...], preferred_element_type=F32)
        o0, widths, piece = 0, (HGRN_KW, HGRN_KW, HGRN_VW, HGRN_VW, POOL_WIDTH, POOL_WIDTH), []
        for wd in widths:
            piece.append(proj[:, o0:o0 + wd])
            o0 += wd
        xs.append(x)
        parts.append(piece)

    row = lax.broadcasted_iota(jnp.int32, (T, HGRN_KW), 0)
    ti = lax.broadcasted_iota(jnp.int32, (C, C), 0)
    si = lax.broadcasted_iota(jnp.int32, (C, C), 1)
    n_chunks = T // C
    nt_dims = (((1,), (1,)), ((), ()))
    tn_dims = (((0,), (0,)), ((), ()))

    def level_scores(qe, ke, mask, scores):
        qb = qe.astype(BF16)
        kb = ke.astype(BF16)
        out = []
        for c in range(n_chunks):
            for hh in range(H):
                rs = slice(c * C, (c + 1) * C)
                cs = slice(hh * HGRN_KEY, (hh + 1) * HGRN_KEY)
                d = lax.dot_general(qb[rs, cs], kb[rs, cs], nt_dims, preferred_element_type=F32)
                prev = scores[c * H + hh]
                out.append(jnp.where(mask, d, 0.0 if prev is None else prev))
        return out

    keys, scores, p_sum, q_sum, total = [], [], [], [], []
    for bi in batch:
        q, f = parts[bi][0], parts[bi][1]
        forget = lower + (1.0 - lower) * jax.nn.sigmoid(f)
        logf = jnp.log(forget) * LOG2E
        keys.append(1.0 - forget)
        scores.append(level_scores(q, keys[bi], ti == si, [None] * (n_chunks * H)))
        p_sum.append(logf)
        q_sum.append(jnp.zeros_like(logf))
        total.append(logf)
    m = 1
    while m < C:
        mask = ((ti ^ si) < 2 * m) & ((ti & m) != 0) & ((si & m) == 0)
        upper = (row & m) != 0
        for bi in batch:
            scores[bi] = level_scores(parts[bi][0] * jnp.exp2(p_sum[bi]),
                                      keys[bi] * jnp.exp2(q_sum[bi]), mask, scores[bi])
            t_dn = pltpu.roll(total[bi], m, 0)
            t_up = pltpu.roll(total[bi], T - m, 0)
            p_sum[bi] = p_sum[bi] + jnp.where(upper, t_dn, 0.0)
            q_sum[bi] = q_sum[bi] + jnp.where(upper, 0.0, t_up)
            total[bi] = total[bi] + jnp.where(upper, t_dn, t_up)
        m *= 2

    o_a = []
    for bi in batch:
        q, _, val, g_a = parts[bi][:4]
        q_in = (q * jnp.exp2(p_sum[bi])).astype(BF16)
        k_out = (keys[bi] * jnp.exp2(q_sum[bi])).astype(BF16)
        decay = jnp.exp2(total[bi])
        val_b = val.astype(BF16)
        oa_rows = []
        for c in range(n_chunks):
            rs = slice(c * C, (c + 1) * C)
            heads_out = []
            for hh in range(H):
                cs = slice(hh * HGRN_KEY, (hh + 1) * HGRN_KEY)
                st = state_ref[bi, hh]
                vb = val_b[rs, cs]
                o = jnp.dot(scores[bi][c * H + hh].astype(BF16), vb, preferred_element_type=F32)
                o = o + lax.dot_general(q_in[rs, cs], st.astype(BF16), nt_dims,
                                        preferred_element_type=F32)
                state_ref[bi, hh] = (st * decay[c * C:c * C + 1, cs]
                                     + lax.dot_general(vb, k_out[rs, cs], tn_dims,
                                                       preferred_element_type=F32))
                ms_o = jnp.mean(o * o, axis=-1, keepdims=True)
                heads_out.append(o * lax.rsqrt(ms_o + EPS))
            oa_rows.append(jnp.concatenate(heads_out, axis=1))
        o_a.append(jnp.concatenate(oa_rows, axis=0) * og_ref[...] * _silu(g_a))

    pos = lax.broadcasted_iota(jnp.int32, (T, POOL_GROUP), 0) + t_idx * T + 1
    for bi in batch:
        u, g_b = parts[bi][4], parts[bi][5]
        ub = jnp.concatenate([carry_ref[bi], u], axis=0)
        carry_ref[bi] = u[T - POOL_HISTORY:T, :]
        wins = []
        acc = ub
        sh = 1
        while sh < max(POOL_WINDOWS):
            acc = acc + pltpu.roll(acc, sh, 0)
            sh *= 2
            wins.append(acc)
        ob = []
        for gi, w in enumerate(POOL_WINDOWS):
            cs = slice(gi * POOL_GROUP, (gi + 1) * POOL_GROUP)
            win = wins[w.bit_length() - 2][POOL_HISTORY:, cs]
            cnt = jnp.minimum(pos, w).astype(F32)
            pooled = win / cnt - u[:, cs]
            ob.append(jnp.dot(pooled.astype(BF16), pw_ref[gi], preferred_element_type=F32))
        o_b = jnp.concatenate(ob, axis=1) * ps_ref[...] * _silu(g_b)
        mixed = jnp.concatenate([o_a[bi], o_b], axis=1).astype(BF16)
        y = jnp.dot(mixed, wout_ref[...], preferred_element_type=F32)
        o_ref[bi] = xs[bi] + mod_ref[bi, 2:3, :] * y


def _layer0(x, mod, norm_g, w_in, hgrn_lb, onorm_g, pool_w, pool_scale, w_out, layer_slot):
    b, s, d = x.shape
    tile = SEQ_TILE
    n_in = w_in.shape[1]
    mix = w_out.shape[0]
    const2 = lambda ti: (0, 0)
    const3 = lambda ti: (0, 0, 0)
    return pl.pallas_call(
        functools.partial(_layer0_kernel, tile=tile, chunk=HGRN_CHUNK, layer_slot=layer_slot),
        grid=(s // tile,),
        in_specs=[pl.BlockSpec((b, tile, d), lambda ti: (0, ti, 0)),
                  pl.BlockSpec((b, 3, d), const3),
                  pl.BlockSpec((1, d), const2),
                  pl.BlockSpec((d, n_in), const2, pipeline_mode=pl.Buffered(1)),
                  pl.BlockSpec(hgrn_lb.shape, const2),
                  pl.BlockSpec((1, HGRN_VW), const2),
                  pl.BlockSpec(pool_w.shape, const3),
                  pl.BlockSpec((1, POOL_WIDTH), const2),
                  pl.BlockSpec((mix, d), const2, pipeline_mode=pl.Buffered(1))],
        out_specs=pl.BlockSpec((b, tile, d), lambda ti: (0, ti, 0)),
        out_shape=jax.ShapeDtypeStruct((b, s, d), F32),
        scratch_shapes=[pltpu.VMEM((b, HGRN_HEADS, HGRN_VAL, HGRN_KEY), F32),
                        pltpu.VMEM((b, POOL_HISTORY, POOL_WIDTH), F32)],
        compiler_params=_params("arbitrary"),
        name="layer0_hgrn_pool",
    )(x, mod, norm_g.reshape(1, d), w_in.astype(BF16), hgrn_lb, onorm_g.reshape(1, HGRN_VW),
      pool_w.astype(BF16), pool_scale.reshape(1, POOL_WIDTH), w_out.astype(BF16))


def _split3(c):
    hi = c.astype(BF16).astype(F32)
    r = c - hi
    mid = r.astype(BF16).astype(F32)
    lo = (r - mid).astype(BF16).astype(F32)
    return hi, mid, lo


def _carrier_selectors():
    per_tile = LANES // FOX_HEAD_DIM
    sel = np.zeros((FOX_HEADS // per_tile, LANES, per_tile * LANES), np.float32)
    for hd in range(FOX_HEADS):
        base = (hd % per_tile) * LANES + AUG_LANE
        for j in range(3):
            sel[hd // per_tile, j * FOX_HEADS + hd, base + j] = 1.0
            sel[hd // per_tile, j * FOX_HEADS + hd, base + 3 + j] = -1.0
            sel[hd // per_tile, (3 + j) * FOX_HEADS + hd, base + 6 + j] = -1.0
    return jnp.asarray(sel, BF16)


def _layer1_proj_kernel(x_ref, mod_ref, ng_ref, win_ref, bf_ref, qg_ref, kg_ref, bd_ref, sel_ref,
                        qa_ref, ka_ref, vt_ref, g_ref, cq_ref, edge_ref, carry_ref, tile_start_ref,
                        *, tile):
    T, W = tile, FOX_WIDTH
    t_idx = pl.program_id(1)

    @pl.when(t_idx == 0)
    def _():
        carry_ref[...] = jnp.zeros_like(carry_ref)

    x = x_ref[0]
    h = _modulated_norm(x, mod_ref[0, 0:1, :], mod_ref[0, 1:2, :], ng_ref[...])
    proj = jnp.dot(h.astype(BF16), win_ref[...], preferred_element_type=F32)
    q = proj[:, 0:W]
    k = proj[:, W:2 * W]
    v = proj[:, 2 * W:3 * W]
    g_ref[0] = proj[:, 3 * W:4 * W]
    fl = proj[:, 4 * W:4 * W + LANES]
    vt_ref[0, 0] = v.T.astype(BF16)

    z = fl + bf_ref[...]
    logf = jnp.minimum(z, 0.0) - jnp.log1p(jnp.exp(-jnp.abs(z)))
    row = lax.broadcasted_iota(jnp.int32, (T, LANES), 0)
    cum = logf
    sh = 1
    while sh < T:
        cum = cum + jnp.where(row >= sh, pltpu.roll(cum, sh, 0), 0.0)
        sh *= 2
    start = carry_ref[...]
    end = start + cum[T - 1:T, :]
    carry_ref[...] = end

    @pl.when(t_idx % (ATTN_Q_TILE // T) == 0)
    def _():
        tile_start_ref[...] = start

    tile_start = tile_start_ref[...]
    edge_ref[0, 0] = jnp.concatenate([tile_start, end], axis=0) * LOG2E
    s_hi, s_mid, s_lo = _split3((cum + (start - tile_start)) * LOG2E)
    e_hi, e_mid, e_lo = _split3((cum - cum[T - 1:T, :]) * LOG2E)
    total_t = ((cum + start) * LOG2E).T
    for hd in range(FOX_HEADS):
        cq_ref[0, hd] = total_t[hd:hd + 1, :]

    lane = lax.broadcasted_iota(jnp.int32, (T, LANES), 1)
    ones_q = jnp.where((lane >= AUG_LANE + 3) & (lane < AUG_LANE + 9), 1.0, 0.0)
    ones_k = jnp.where((lane >= AUG_LANE) & (lane < AUG_LANE + 3), 1.0, 0.0)
    terms = (s_hi, s_mid, s_lo, e_hi, e_mid, e_lo)
    packed = jnp.zeros((T, LANES), F32)
    for j, term in enumerate(terms):
        moved = term if j == 0 else pltpu.roll(term, j * FOX_HEADS, 1)
        packed = jnp.where((lane >= j * FOX_HEADS) & (lane < (j + 1) * FOX_HEADS), moved, packed)
    packed = packed.astype(BF16)
    bd = bd_ref[...]
    qk_scale = FOX_HEAD_DIM ** -0.5 * LOG2E
    for pair in range(FOX_HEADS // 2):
        cs = slice(pair * LANES, (pair + 1) * LANES)
        qp = q[:, cs]
        kp = k[:, cs]
        squares = jnp.concatenate([(qp * qp).astype(BF16), (kp * kp).astype(BF16)], axis=1)
        ms = jnp.dot(squares, bd, preferred_element_type=F32)
        qn = qp * lax.rsqrt(ms[:, :LANES] + EPS) * (qg_ref[...] * qk_scale)
        kn = kp * lax.rsqrt(ms[:, LANES:] + EPS) * kg_ref[...]
        aug = jnp.dot(packed, sel_ref[pair], preferred_element_type=F32)
        for e in range(LANES // FOX_HEAD_DIM):
            hd = (LANES // FOX_HEAD_DIM) * pair + e
            carriers = aug[:, e * LANES:(e + 1) * LANES]
            q_main = qn if e == 0 else pltpu.roll(qn, FOX_HEAD_DIM, 1)
            k_main = kn if e == 0 else pltpu.roll(kn, FOX_HEAD_DIM, 1)
            q_aug = jnp.where(lane < AUG_LANE + 3, carriers, ones_q)
            k_aug = jnp.where(lane < AUG_LANE + 3, ones_k, carriers)
            qa_ref[0, hd] = jnp.where(lane < FOX_HEAD_DIM, q_main, q_aug).astype(BF16)
            ka_ref[0, hd] = jnp.where(lane < FOX_HEAD_DIM, k_main, k_aug).astype(BF16)


def _layer1_proj(x, mod, norm_g, w_in, b_f, qnorm_g, knorm_g):
    b, s, d = x.shape
    tile = SEQ_TILE
    nt = s // tile
    w = FOX_WIDTH
    n_pad = w_in.shape[1]
    bf_pad = jnp.zeros((1, LANES), F32).at[0, :FOX_HEADS].set(b_f)
    blk = jnp.arange(2 * LANES) // FOX_HEAD_DIM
    bd = jnp.where(blk[:, None] == blk[None, :], 1.0 / FOX_HEAD_DIM, 0.0).astype(BF16)
    qg = jnp.tile(qnorm_g, LANES // FOX_HEAD_DIM).reshape(1, LANES)
    kg = jnp.tile(knorm_g, LANES // FOX_HEAD_DIM).reshape(1, LANES)
    const2 = lambda bi, ti: (0, 0)
    return pl.pallas_call(
        functools.partial(_layer1_proj_kernel, tile=tile),
        grid=(b, nt),
        in_specs=[pl.BlockSpec((1, tile, d), lambda bi, ti: (bi, ti, 0)),
                  pl.BlockSpec((1, 3, d), lambda bi, ti: (bi, 0, 0)),
                  pl.BlockSpec((1, d), const2),
                  pl.BlockSpec((d, n_pad), const2),
                  pl.BlockSpec((1, LANES), const2),
                  pl.BlockSpec((1, LANES), const2),
                  pl.BlockSpec((1, LANES), const2),
                  pl.BlockSpec((2 * LANES, 2 * LANES), const2),
                  pl.BlockSpec((FOX_HEADS // 2, LANES, 2 * LANES), lambda bi, ti: (0, 0, 0))],
        out_specs=[pl.BlockSpec((1, FOX_HEADS, tile, LANES), lambda bi, ti: (bi, 0, ti, 0)),
                   pl.BlockSpec((1, FOX_HEADS, tile, LANES), lambda bi, ti: (bi, 0, ti, 0)),
                   pl.BlockSpec((1, 1, w, tile), lambda bi, ti: (bi, ti, 0, 0)),
                   pl.BlockSpec((1, tile, w), lambda bi, ti: (bi, ti, 0)),
                   pl.BlockSpec((1, FOX_HEADS, 1, tile), lambda bi, ti: (bi, 0, 0, ti)),
                   pl.BlockSpec((1, 1, 2, LANES), lambda bi, ti: (bi, ti, 0, 0))],
        out_shape=[jax.ShapeDtypeStruct((b, FOX_HEADS, s, LANES), BF16),
                   jax.ShapeDtypeStruct((b, FOX_HEADS, s, LANES), BF16),
                   jax.ShapeDtypeStruct((b, nt, w, tile), BF16),
                   jax.ShapeDtypeStruct((b, s, w), F32),
                   jax.ShapeDtypeStruct((b, FOX_HEADS, 1, s), F32),
                   jax.ShapeDtypeStruct((b, nt, 2, LANES), F32)],
        scratch_shapes=[pltpu.VMEM((1, LANES), F32), pltpu.VMEM((1, LANES), F32)],
        compiler_params=_params("arbitrary", "arbitrary"),
        name="layer1_proj",
    )(x, mod, norm_g.reshape(1, d), w_in, bf_pad, qg, kg, bd, _carrier_selectors())


def _query_operands(q):
    slab = 2 * SUBLANES
    row = lax.broadcasted_iota(jnp.int32, (slab, q.shape[0]), 0)
    q_t = q.astype(F32).T
    carriers = q_t[AUG_LANE:AUG_LANE + slab]
    return tuple(jnp.concatenate([q_t[:AUG_LANE], jnp.where(keep, carriers, 0.0),
                                  q_t[AUG_LANE + slab:]], axis=0).astype(BF16)
                 for keep in (row < 6, row >= 6))


def _gated_output(acc_t, g_ref, o_ref, rows):
    o_ref[0, rows] = (acc_t.T * _silu(g_ref[0, rows])).astype(BF16)


def _fox_bounded_kernel(first_ref, end_ref, qa_ref, ka_ref, vt_ref, cq_ref, g_ref, o_ref,
                        p_ref, acc_ref, *, tile, heads):
    TQ, HP = tile, heads
    QT = qa_ref.shape[2] // TQ
    SB = vt_ref.shape[-1]
    R = TQ // SB
    n_blocks = ka_ref.shape[2] // SB
    b = pl.program_id(0)
    group = pl.program_id(1)
    i0 = pl.program_id(2) * QT
    chains = [(qs, e) for qs in range(QT) for e in range(HP)]
    cols = [slice(qs * TQ, (qs + 1) * TQ) for qs in range(QT)]
    operands = {(qs, e): _query_operands(qa_ref[0, e, cols[qs], :]) for qs, e in chains}
    causal = (lax.broadcasted_iota(jnp.int32, (TQ, TQ), 0)
              <= lax.broadcasted_iota(jnp.int32, (TQ, TQ), 1))

    def keys(e, n):
        return ka_ref[0, e, pl.ds(pl.multiple_of(n * TQ, TQ), TQ), :]

    def values_t(e, blk):
        return vt_ref[0, blk, pl.ds(e * FOX_HEAD_DIM, FOX_HEAD_DIM), :]

    def rest(c, blk, gate=1.0):
        qs, e = c
        end = end_ref[(b * n_blocks + blk) * FOX_HEADS + HP * group + e]
        return jnp.exp2(jnp.minimum(cq_ref[0, e][:, cols[qs]] - end, 0.0)) * gate

    def col_partial(p):
        return jnp.sum(p.reshape(p.shape[0] // SUBLANES, SUBLANES, p.shape[1]), axis=0)

    def far_scores(c, n):
        return jnp.dot(keys(c[1], n), operands[c][1], preferred_element_type=F32)

    def contract(c, n, p_bf16, gate=1.0):
        for r in range(R):
            pv = jnp.dot(values_t(c[1], n * R + r), p_bf16[r * SB:(r + 1) * SB],
                         preferred_element_type=F32)
            acc_ref[c] += rest(c, n * R + r, gate) * pv

    def weighted_partial(c, n, p, gate=1.0):
        return sum(rest(c, n * R + r, gate) * col_partial(p[r * SB:(r + 1) * SB])
                   for r in range(R))

    first = first_ref[(b * pl.num_programs(1) + group) * pl.num_programs(2) + pl.program_id(2)]
    gate0 = jnp.where(i0 > 0, 1.0, 0.0)

    l_part = {}

    def own_tile(c):
        qs, e = c
        s_t = jnp.dot(keys(e, i0 + qs), operands[c][0], preferred_element_type=F32)
        p = jnp.where(causal, jnp.exp2(s_t), 0.0)
        l_part[c] = col_partial(p)
        p = p.astype(BF16)

        def later():
            v_t = jnp.concatenate([values_t(e, (i0 + qs) * R + r) for r in range(R)], axis=1)
            acc_ref[c] = jnp.dot(v_t, p, preferred_element_type=F32)
        return later

    def earlier_tile(c, n):
        p = jnp.exp2(far_scores(c, n))
        l_part[c] = l_part[c] + weighted_partial(c, n, p)
        p = p.astype(BF16)
        return lambda: contract(c, n, p)

    def fill(c):
        p = jnp.exp2(far_scores(c, first))
        l_part[c] = l_part[c] + weighted_partial(c, first, p, gate0)
        p_ref[c] = p.astype(BF16)

    jobs = ([functools.partial(own_tile, c) for c in chains]
            + [functools.partial(earlier_tile, c, i0 + t) for c in chains for t in range(c[0])]
            + [functools.partial(fill, c) for c in chains])
    pending = None
    for job in jobs:
        later = job()
        if pending is not None:
            pending()
        pending = later
    if pending is not None:
        pending()

    def body(n, l_run):
        s_new = []
        for c in chains:
            s_new.append(far_scores(c, n))
            contract(c, n - 1, p_ref[c])
        out = []
        for c, s_t, l_c in zip(chains, s_new, l_run):
            p = jnp.exp2(s_t)
            out.append(l_c + weighted_partial(c, n, p))
            p_ref[c] = p.astype(BF16)
        return tuple(out)

    l_run = lax.fori_loop(first + 1, i0, body, tuple(l_part[c] for c in chains))
    for c in chains:
        contract(c, jnp.maximum(i0 - 1, 0), p_ref[c], gate0)
    for qs in range(QT):
        out_t = [acc_ref[qs, e] * (1.0 / jnp.sum(l_run[qs * HP + e], axis=0, keepdims=True))
                 for e in range(HP)]
        _gated_output(jnp.concatenate(out_t, axis=0), g_ref, o_ref, cols[qs])


def _fox_online_kernel(start_ref, qa_ref, ka_ref, vt_ref, g_ref, o_ref, *, tile, heads):
    TB, HP = tile, heads
    n_blocks = ka_ref.shape[2] // TB
    b = pl.program_id(0)
    group = pl.program_id(1)
    i = pl.program_id(2)
    q_t = [_query_operands(qa_ref[0, e])[0] for e in range(HP)]
    k_row = lax.broadcasted_iota(jnp.int32, (TB, TB), 0)
    q_col = lax.broadcasted_iota(jnp.int32, (TB, TB), 1)

    def step(j, carry, masked):
        out = []
        for e in range(HP):
            m_run, l_run, acc = carry[e]
            k_blk = ka_ref[0, e, pl.ds(pl.multiple_of(j * TB, TB), TB), :]
            s_t = jnp.dot(k_blk, q_t[e], preferred_element_type=F32)
            if masked:
                s_t = jnp.where(k_row <= q_col, s_t, MASK_VALUE)
            hd = HP * group + e
            delta = (start_ref[(b * n_blocks + i) * FOX_HEADS + hd]
                     - start_ref[(b * n_blocks + j) * FOX_HEADS + hd])
            m_new = jnp.maximum(m_run, jnp.max(s_t, axis=0, keepdims=True) + delta)
            p_t = jnp.exp2(s_t - (m_new - delta))
            alpha = jnp.exp2(m_run - m_new)
            l_new = alpha * l_run + jnp.sum(p_t, axis=0, keepdims=True)
            v_t = vt_ref[0, j, pl.ds(e * FOX_HEAD_DIM, FOX_HEAD_DIM), :]
            acc_new = alpha * acc + jnp.dot(v_t, p_t.astype(BF16), preferred_element_type=F32)
            out.append((m_new, l_new, acc_new))
        return tuple(out)

    init = (jnp.full((1, TB), MASK_VALUE, F32), jnp.zeros((1, TB), F32),
            jnp.zeros((FOX_HEAD_DIM, TB), F32))
    carry = lax.fori_loop(0, i, lambda j, c: step(j, c, False), (init,) * HP)
    carry = step(i, carry, True)
    _gated_output(jnp.concatenate([acc / l_run for _, l_run, acc in carry], axis=0), g_ref, o_ref,
                  slice(None))


def _fox_attention(qa, ka, vt, g, cq, edges, *, bounded):
    b, n_heads, s, _ = qa.shape
    _, nvb, w, vb = vt.shape
    if bounded:
        tile, hp, qt = ATTN_Q_TILE, ATTN_HEADS, ATTN_TILES_PER_STEP
    else:
        tile, hp, qt = vb, LANES // FOX_HEAD_DIM, 1
    gw = hp * FOX_HEAD_DIM
    rows = qt * tile
    specs = dict(
        table=pl.BlockSpec(memory_space=pltpu.SMEM),
        q=pl.BlockSpec((1, hp, rows, LANES), lambda bi, p, i: (bi, p, i, 0)),
        k=pl.BlockSpec((1, hp, s, LANES), lambda bi, p, i: (bi, p, 0, 0)),
        v=pl.BlockSpec((1, nvb, gw, vb), lambda bi, p, i: (bi, 0, p, 0)),
        cq=pl.BlockSpec((1, hp, 1, rows), lambda bi, p, i: (bi, p, 0, i)),
        g=pl.BlockSpec((1, rows, gw), lambda bi, p, i: (bi, i, p)))
    if bounded:
        body = functools.partial(_fox_bounded_kernel, tile=tile, heads=hp)
        names = ("table", "table", "q", "k", "v", "cq", "g")
        step_start = edges[:, ::rows // vb, 0, :FOX_HEADS]
        tile_end = edges[:, tile // vb - 1::tile // vb, 1, :FOX_HEADS]
        dead = (step_start[:, :, None] - tile_end[:, None]) < -FOX_DEAD_EXPONENT
        before = jnp.arange(s // rows)[:, None] * qt > jnp.arange(s // tile)[None, :]
        first_live = jnp.sum(dead & before[None, :, :, None], axis=2)
        last_before = jnp.maximum(jnp.arange(s // rows) * qt - 1, 0)
        first_live = jnp.minimum(first_live, last_before[None, :, None])
        first_live = jnp.min(first_live.reshape(b, s // rows, n_heads // hp, hp), axis=-1)
        args = (first_live.transpose(0, 2, 1).reshape(-1).astype(jnp.int32),
                edges[:, :, 1, :FOX_HEADS].reshape(-1), qa, ka, vt, cq, g)
        scratch = [pltpu.VMEM((qt, hp, tile, tile), BF16),
                   pltpu.VMEM((qt, hp, FOX_HEAD_DIM, tile), F32)]
    else:
        body = functools.partial(_fox_online_kernel, tile=tile, heads=hp)
        names = ("table", "q", "k", "v", "g")
        args = (edges[:, :, 0, :FOX_HEADS].reshape(-1), qa, ka, vt, g)
        scratch = []
    return pl.pallas_call(
        body,
        grid=(b, n_heads // hp, s // rows),
        in_specs=[specs[n] for n in names],
        out_specs=specs["g"],
        out_shape=jax.ShapeDtypeStruct((b, s, w), BF16),
        scratch_shapes=scratch,
        compiler_params=_params("arbitrary", "arbitrary", "arbitrary"),
        name="fox_attention_bounded" if bounded else "fox_attention_online",
    )(*args)


def _out_proj_kernel(a_ref, x_ref, mod_ref, w_ref, o_ref):
    y = jnp.dot(a_ref[0], w_ref[...], preferred_element_type=F32)
    o_ref[0] = x_ref[0] + mod_ref[0, 2:3, :] * y


def _out_proj(a, x, mod, w_out):
    b, s, d = x.shape
    tile = OUT_PROJ_TILE
    k = a.shape[-1]
    return pl.pallas_call(
        _out_proj_kernel,
        grid=(b, s // tile),
        in_specs=[pl.BlockSpec((1, tile, k), lambda bi, ti: (bi, ti, 0)),
                  pl.BlockSpec((1, tile, d), lambda bi, ti: (bi, ti, 0)),
                  pl.BlockSpec((1, 3, d), lambda bi, ti: (bi, 0, 0)),
                  pl.BlockSpec((k, d), lambda bi, ti: (0, 0))],
        out_specs=pl.BlockSpec((1, tile, d), lambda bi, ti: (bi, ti, 0)),
        out_shape=jax.ShapeDtypeStruct((b, s, d), F32),
        compiler_params=_params("arbitrary", "arbitrary"),
        name="layer1_out_proj",
    )(a, x, mod, w_out.astype(BF16))


def _layer1(x, mod, norm_g, w_in, b_f, qnorm_g, knorm_g, w_out):
    d, w = x.shape[-1], FOX_WIDTH
    order = jnp.argsort(b_f)
    w_in = w_in.astype(BF16)
    w_heads = w_in[:, :4 * w].reshape(d, 4, FOX_HEADS, FOX_HEAD_DIM)[:, :, order]
    w_forget = jnp.zeros((d, LANES), BF16).at[:, :FOX_HEADS].set(
        w_in[:, 4 * w:][:, order])
    w_in = jnp.concatenate([w_heads.reshape(d, 4 * w), w_forget], axis=1)
    w_out = w_out.astype(BF16).reshape(FOX_HEADS, FOX_HEAD_DIM, d)[order].reshape(w, d)
    qa, ka, vt, g, cq, edges = _layer1_proj(x, mod, norm_g, w_in, b_f[order], qnorm_g, knorm_g)
    score_bound = (LOG2E * FOX_HEAD_DIM ** 0.5
                   * jnp.max(jnp.abs(qnorm_g)) * jnp.max(jnp.abs(knorm_g)))
    gated = lax.cond(score_bound <= FOX_SAFE_LOGIT,
                     functools.partial(_fox_attention, bounded=True),
                     functools.partial(_fox_attention, bounded=False),
                     qa, ka, vt, g, cq, edges)
    return _out_proj(gated, x, mod, w_out)


def kernel(x, c, norm_g, ada_w, ada_b, hgrn_lb, even_w_in, hgrn_onorm_g, pool_w, pool_scale,
           even_w_out, odd_w_in, fox_b_f, fox_qnorm_g, fox_knorm_g, odd_w_out):
    depth = norm_g.shape[0]
    mods = _adaln_mods(c, ada_w, ada_b)
    for l in range(depth):
        j = l // 2
        if l % 2 == 0:
            x = _layer0(x, mods[l], norm_g[l], even_w_in[j], hgrn_lb, hgrn_onorm_g[j],
                        pool_w[j], pool_scale[j], even_w_out[j], layer_slot=l)
        else:
            x = _layer1(x, mods[l], norm_g[l], odd_w_in[j], fox_b_f[j], fox_qnorm_g[j],
                        fox_knorm_g[j], odd_w_out[j])
    return x
```

```python
import functools

import jax
import jax.numpy as jnp
import numpy as np
from jax import lax
from jax.experimental import pallas as pl
from jax.experimental.pallas import tpu as pltpu

F32 = jnp.float32
BF16 = jnp.bfloat16
EPS = 1e-6

HGRN_HEADS = 4
HGRN_KEY = 128
HGRN_VAL = 128
HGRN_KW = HGRN_HEADS * HGRN_KEY
HGRN_VW = HGRN_HEADS * HGRN_VAL
POOL_WINDOWS = (2, 4, 8, 16)
POOL_GROUP = 128
POOL_WIDTH = POOL_GROUP * len(POOL_WINDOWS)
POOL_HISTORY = 16
FOX_HEADS = 16
FOX_HEAD_DIM = 64
FOX_WIDTH = FOX_HEADS * FOX_HEAD_DIM

LANES = 128
SUBLANES = 8
VMEM_LIMIT_BYTES = 56 * 1024 * 1024

SEQ_TILE = 256
OUT_PROJ_TILE = 1024
HGRN_CHUNK = 128
ATTN_Q_TILE = 512
ATTN_HEADS = 4
ATTN_TILES_PER_STEP = 2
FOX_SAFE_LOGIT = 100.0
FOX_DEAD_EXPONENT = 150.0
LOG2E = 1.4426950408889634
MASK_VALUE = -1e30
AUG_LANE = FOX_HEAD_DIM


def _silu(x):
    return x * jax.nn.sigmoid(x)


def _params(*semantics):
    return pltpu.CompilerParams(dimension_semantics=semantics, vmem_limit_bytes=VMEM_LIMIT_BYTES)


def _mod_kernel(c_ref, w_ref, b_ref, o_ref):
    w = w_ref[0]
    for bi in range(c_ref.shape[0]):
        cond = _silu(c_ref[bi])
        cols = [jnp.sum(w[:, j:j + LANES] * cond, axis=0, keepdims=True)
                for j in range(0, w.shape[1], LANES)]
        o_ref[0, bi:bi + 1, :] = jnp.concatenate(cols, axis=1) + b_ref[0]


def _adaln_mods(c, ada_w, ada_b):
    depth, d, n = ada_w.shape
    b = c.shape[0]
    tn = 512
    out = pl.pallas_call(
        _mod_kernel,
        grid=(depth, n // tn),
        in_specs=[pl.BlockSpec((b, d, LANES), lambda l, j: (0, 0, 0)),
                  pl.BlockSpec((1, d, tn), lambda l, j: (l, 0, j)),
                  pl.BlockSpec((1, 1, tn), lambda l, j: (l, 0, j))],
        out_specs=pl.BlockSpec((1, b, tn), lambda l, j: (l, 0, j)),
        out_shape=jax.ShapeDtypeStruct((depth, b, n), F32),
        compiler_params=_params("arbitrary", "arbitrary"),
        name="adaln_mods",
    )(jnp.broadcast_to(c[:, :, None], (b, d, LANES)), ada_w, ada_b.reshape(depth, 1, n))
    return out.reshape(depth, b, 3, d)


def _modulated_norm(x, shift, scale, gain):
    ms = jnp.mean(x * x, axis=-1, keepdims=True)
    return x * lax.rsqrt(ms + EPS) * gain * (1.0 + scale) + shift


def _layer0_kernel(x_ref, mod_ref, ng_ref, win_ref, lb_ref, og_ref, pw_ref, ps_ref, wout_ref,
                   o_ref, state_ref, carry_ref, *, tile, chunk, layer_slot):
    T, C, H = tile, chunk, HGRN_HEADS
    batch = range(x_ref.shape[0])
    t_idx = pl.program_id(0)

    @pl.when(t_idx == 0)
    def _():
        state_ref[...] = jnp.zeros_like(state_ref)
        carry_ref[...] = jnp.zeros_like(carry_ref)

    lbv = lb_ref[...]
    e = jnp.exp(lbv - jnp.max(lbv, axis=0, keepdims=True))
    lower = (jnp.sum(e[0:layer_slot + 1], axis=0, keepdims=True)
             / jnp.sum(e, axis=0, keepdims=True))

    xs, parts = [], []
    for bi in batch:
        x = x_ref[bi]
        h = _modulated_norm(x, mod_ref[bi, 0:1, :], mod_ref[bi, 1:2, :], ng_ref[...])
        proj = jnp.dot(h.astype(BF16), win_ref[...], preferred_element_type=F32)
        o0, widths, piece = 0, (HGRN_KW, HGRN_KW, HGRN_VW, HGRN_VW, POOL_WIDTH, POOL_WIDTH), []
        for wd in widths:
            piece.append(proj[:, o0:o0 + wd])
            o0 += wd
        xs.append(x)
        parts.append(piece)

    row = lax.broadcasted_iota(jnp.int32, (T, HGRN_KW), 0)
    ti = lax.broadcasted_iota(jnp.int32, (C, C), 0)
    si = lax.broadcasted_iota(jnp.int32, (C, C), 1)
    n_chunks = T // C
    nt_dims = (((1,), (1,)), ((), ()))
    tn_dims = (((0,), (0,)), ((), ()))

    def level_scores(qe, ke, mask, scores):
        qb = qe.astype(BF16)
        kb = ke.astype(BF16)
        out = []
        for c in range(n_chunks):
            for hh in range(H):
                rs = slice(c * C, (c + 1) * C)
                cs = slice(hh * HGRN_KEY, (hh + 1) * HGRN_KEY)
                d = lax.dot_general(qb[rs, cs], kb[rs, cs], nt_dims, preferred_element_type=F32)
                prev = scores[c * H + hh]
                out.append(jnp.where(mask, d, 0.0 if prev is None else prev))
        return out

    scores, q_dec, k_dec, total = [], [], [], []
    for bi in batch:
        q, f = parts[bi][0], parts[bi][1]
        forget = lower + (1.0 - lower) * jax.nn.sigmoid(f)
        key = 1.0 - forget
        scores.append(level_scores(q, key, ti == si, [None] * (n_chunks * H)))
        q_dec.append(q * forget)
        k_dec.append(key)
        total.append(forget)
    m = 1
    while m < C:
        mask = ((ti ^ si) < 2 * m) & ((ti & m) != 0) & ((si & m) == 0)
        upper = (row & m) != 0
        for bi in batch:
            scores[bi] = level_scores(q_dec[bi], k_dec[bi], mask, scores[bi])
            t_dn = pltpu.roll(total[bi], m, 0)
            t_up = pltpu.roll(total[bi], T - m, 0)
            q_dec[bi] = q_dec[bi] * jnp.where(upper, t_dn, 1.0)
            k_dec[bi] = k_dec[bi] * jnp.where(upper, 1.0, t_up)
            total[bi] = total[bi] * jnp.where(upper, t_dn, t_up)
        m *= 2

    o_a = []
    for bi in batch:
        val, g_a = parts[bi][2], parts[bi][3]
        q_in = q_dec[bi].astype(BF16)
        k_out = k_dec[bi].astype(BF16)
        decay = total[bi]
        val_b = val.astype(BF16)
        oa_rows = []
        for c in range(n_chunks):
            rs = slice(c * C, (c + 1) * C)
            heads_out = []
            for hh in range(H):
                cs = slice(hh * HGRN_KEY, (hh + 1) * HGRN_KEY)
                st = state_ref[bi, hh]
                vb = val_b[rs, cs]
                o = jnp.dot(scores[bi][c * H + hh].astype(BF16), vb, preferred_element_type=F32)
                o = o + lax.dot_general(q_in[rs, cs], st.astype(BF16), nt_dims,
                                        preferred_element_type=F32)
                state_ref[bi, hh] = (st * decay[c * C:c * C + 1, cs]
                                     + lax.dot_general(vb, k_out[rs, cs], tn_dims,
                                                       preferred_element_type=F32))
                ms_o = jnp.mean(o * o, axis=-1, keepdims=True)
                heads_out.append(o * lax.rsqrt(ms_o + EPS))
            oa_rows.append(jnp.concatenate(heads_out, axis=1))
        o_a.append(jnp.concatenate(oa_rows, axis=0) * og_ref[...] * _silu(g_a))

    pos = lax.broadcasted_iota(jnp.int32, (T, POOL_GROUP), 0) + t_idx * T + 1
    for bi in batch:
        u, g_b = parts[bi][4], parts[bi][5]
        ub = jnp.concatenate([carry_ref[bi], u], axis=0)
        carry_ref[bi] = u[T - POOL_HISTORY:T, :]
        wins = []
        acc = ub
        sh = 1
        while sh < max(POOL_WINDOWS):
            acc = acc + pltpu.roll(acc, sh, 0)
            sh *= 2
            wins.append(acc)
        ob = []
        for gi, w in enumerate(POOL_WINDOWS):
            cs = slice(gi * POOL_GROUP, (gi + 1) * POOL_GROUP)
            win = wins[w.bit_length() - 2][POOL_HISTORY:, cs]
            cnt = jnp.minimum(pos, w).astype(F32)
            pooled = win / cnt - u[:, cs]
            ob.append(jnp.dot(pooled.astype(BF16), pw_ref[gi], preferred_element_type=F32))
        o_b = jnp.concatenate(ob, axis=1) * ps_ref[...] * _silu(g_b)
        mixed = jnp.concatenate([o_a[bi], o_b], axis=1).astype(BF16)
        y = jnp.dot(mixed, wout_ref[...], preferred_element_type=F32)
        o_ref[bi] = xs[bi] + mod_ref[bi, 2:3, :] * y


def _layer0(x, mod, norm_g, w_in, hgrn_lb, onorm_g, pool_w, pool_scale, w_out, layer_slot):
    b, s, d = x.shape
    tile = SEQ_TILE
    n_in = w_in.shape[1]
    mix = w_out.shape[0]
    const2 = lambda ti: (0, 0)
    const3 = lambda ti: (0, 0, 0)
    return pl.pallas_call(
        functools.partial(_layer0_kernel, tile=tile, chunk=HGRN_CHUNK, layer_slot=layer_slot),
        grid=(s // tile,),
        in_specs=[pl.BlockSpec((b, tile, d), lambda ti: (0, ti, 0)),
                  pl.BlockSpec((b, 3, d), const3),
                  pl.BlockSpec((1, d), const2),
                  pl.BlockSpec((d, n_in), const2, pipeline_mode=pl.Buffered(1)),
                  pl.BlockSpec(hgrn_lb.shape, const2),
                  pl.BlockSpec((1, HGRN_VW), const2),
                  pl.BlockSpec(pool_w.shape, const3),
                  pl.BlockSpec((1, POOL_WIDTH), const2),
                  pl.BlockSpec((mix, d), const2, pipeline_mode=pl.Buffered(1))],
        out_specs=pl.BlockSpec((b, tile, d), lambda ti: (0, ti, 0)),
        out_shape=jax.ShapeDtypeStruct((b, s, d), F32),
        scratch_shapes=[pltpu.VMEM((b, HGRN_HEADS, HGRN_VAL, HGRN_KEY), F32),
                        pltpu.VMEM((b, POOL_HISTORY, POOL_WIDTH), F32)],
        compiler_params=_params("arbitrary"),
        name="layer0_hgrn_pool",
    )(x, mod, norm_g.reshape(1, d), w_in.astype(BF16), hgrn_lb, onorm_g.reshape(1, HGRN_VW),
      pool_w.astype(BF16), pool_scale.reshape(1, POOL_WIDTH), w_out.astype(BF16))


def _split3(c):
    hi = c.astype(BF16).astype(F32)
    r = c - hi
    mid = r.astype(BF16).astype(F32)
    lo = (r - mid).astype(BF16).astype(F32)
    return hi, mid, lo


def _carrier_selectors():
    per_tile = LANES // FOX_HEAD_DIM
    sel = np.zeros((FOX_HEADS // per_tile, LANES, per_tile * LANES), np.float32)
    for hd in range(FOX_HEADS):
        base = (hd % per_tile) * LANES + AUG_LANE
        for j in range(3):
            sel[hd // per_tile, j * FOX_HEADS + hd, base + j] = 1.0
            sel[hd // per_tile, j * FOX_HEADS + hd, base + 3 + j] = -1.0
            sel[hd // per_tile, (3 + j) * FOX_HEADS + hd, base + 6 + j] = -1.0
    return jnp.asarray(sel, BF16)


def _layer1_proj_kernel(x_ref, mod_ref, ng_ref, win_ref, bf_ref, qg_ref, kg_ref, bd_ref, sel_ref,
                        qa_ref, ka_ref, vt_ref, g_ref, cq_ref, edge_ref, carry_ref, tile_start_ref,
                        *, tile):
    T, W = tile, FOX_WIDTH
    t_idx = pl.program_id(1)

    @pl.when(t_idx == 0)
    def _():
        carry_ref[...] = jnp.zeros_like(carry_ref)

    x = x_ref[0]
    h = _modulated_norm(x, mod_ref[0, 0:1, :], mod_ref[0, 1:2, :], ng_ref[...])
    proj = jnp.dot(h.astype(BF16), win_ref[...], preferred_element_type=F32)
    q = proj[:, 0:W]
    k = proj[:, W:2 * W]
    v = proj[:, 2 * W:3 * W]
    g_ref[0] = proj[:, 3 * W:4 * W]
    fl = proj[:, 4 * W:4 * W + LANES]
    vt_ref[0, 0] = v.T.astype(BF16)

    z = fl + bf_ref[...]
    logf = jnp.minimum(z, 0.0) - jnp.log1p(jnp.exp(-jnp.abs(z)))
    row = lax.broadcasted_iota(jnp.int32, (T, LANES), 0)
    cum = logf
    sh = 1
    while sh < T:
        cum = cum + jnp.where(row >= sh, pltpu.roll(cum, sh, 0), 0.0)
        sh *= 2
    start = carry_ref[...]
    end = start + cum[T - 1:T, :]
    carry_ref[...] = end

    @pl.when(t_idx % (ATTN_Q_TILE // T) == 0)
    def _():
        tile_start_ref[...] = start

    tile_start = tile_start_ref[...]
    edge_ref[0, 0] = jnp.concatenate([tile_start, end], axis=0) * LOG2E
    s_hi, s_mid, s_lo = _split3((cum + (start - tile_start)) * LOG2E)
    e_hi, e_mid, e_lo = _split3((cum - cum[T - 1:T, :]) * LOG2E)
    total_t = ((cum + start) * LOG2E).T
    for hd in range(FOX_HEADS):
        cq_ref[0, hd] = total_t[hd:hd + 1, :]

    lane = lax.broadcasted_iota(jnp.int32, (T, LANES), 1)
    ones_q = jnp.where((lane >= AUG_LANE + 3) & (lane < AUG_LANE + 9), 1.0, 0.0)
    ones_k = jnp.where((lane >= AUG_LANE) & (lane < AUG_LANE + 3), 1.0, 0.0)
    terms = (s_hi, s_mid, s_lo, e_hi, e_mid, e_lo)
    packed = jnp.zeros((T, LANES), F32)
    for j, term in enumerate(terms):
        moved = term if j == 0 else pltpu.roll(term, j * FOX_HEADS, 1)
        packed = jnp.where((lane >= j * FOX_HEADS) & (lane < (j + 1) * FOX_HEADS), moved, packed)
    packed = packed.astype(BF16)
    bd = bd_ref[...]
    qk_scale = FOX_HEAD_DIM ** -0.5 * LOG2E
    for pair in range(FOX_HEADS // 2):
        cs = slice(pair * LANES, (pair + 1) * LANES)
        qp = q[:, cs]
        kp = k[:, cs]
        squares = jnp.concatenate([(qp * qp).astype(BF16), (kp * kp).astype(BF16)], axis=1)
        ms = jnp.dot(squares, bd, preferred_element_type=F32)
        qn = qp * lax.rsqrt(ms[:, :LANES] + EPS) * (qg_ref[...] * qk_scale)
        kn = kp * lax.rsqrt(ms[:, LANES:] + EPS) * kg_ref[...]
        aug = jnp.dot(packed, sel_ref[pair], preferred_element_type=F32)
        for e in range(LANES // FOX_HEAD_DIM):
            hd = (LANES // FOX_HEAD_DIM) * pair + e
            carriers = aug[:, e * LANES:(e + 1) * LANES]
            q_main = qn if e == 0 else pltpu.roll(qn, FOX_HEAD_DIM, 1)
            k_main = kn if e == 0 else pltpu.roll(kn, FOX_HEAD_DIM, 1)
            q_aug = jnp.where(lane < AUG_LANE + 3, carriers, ones_q)
            k_aug = jnp.where(lane < AUG_LANE + 3, ones_k, carriers)
            qa_ref[0, hd] = jnp.where(lane < FOX_HEAD_DIM, q_main, q_aug).astype(BF16)
            ka_ref[0, hd] = jnp.where(lane < FOX_HEAD_DIM, k_main, k_aug).astype(BF16)


def _layer1_proj(x, mod, norm_g, w_in, b_f, qnorm_g, knorm_g):
    b, s, d = x.shape
    tile = SEQ_TILE
    nt = s // tile
    w = FOX_WIDTH
    n_pad = w_in.shape[1]
    bf_pad = jnp.zeros((1, LANES), F32).at[0, :FOX_HEADS].set(b_f)
    blk = jnp.arange(2 * LANES) // FOX_HEAD_DIM
    bd = jnp.where(blk[:, None] == blk[None, :], 1.0 / FOX_HEAD_DIM, 0.0).astype(BF16)
    qg = jnp.tile(qnorm_g, LANES // FOX_HEAD_DIM).reshape(1, LANES)
    kg = jnp.tile(knorm_g, LANES // FOX_HEAD_DIM).reshape(1, LANES)
    const2 = lambda bi, ti: (0, 0)
    return pl.pallas_call(
        functools.partial(_layer1_proj_kernel, tile=tile),
        grid=(b, nt),
        in_specs=[pl.BlockSpec((1, tile, d), lambda bi, ti: (bi, ti, 0)),
                  pl.BlockSpec((1, 3, d), lambda bi, ti: (bi, 0, 0)),
                  pl.BlockSpec((1, d), const2),
                  pl.BlockSpec((d, n_pad), const2),
                  pl.BlockSpec((1, LANES), const2),
                  pl.BlockSpec((1, LANES), const2),
                  pl.BlockSpec((1, LANES), const2),
                  pl.BlockSpec((2 * LANES, 2 * LANES), const2),
                  pl.BlockSpec((FOX_HEADS // 2, LANES, 2 * LANES), lambda bi, ti: (0, 0, 0))],
        out_specs=[pl.BlockSpec((1, FOX_HEADS, tile, LANES), lambda bi, ti: (bi, 0, ti, 0)),
                   pl.BlockSpec((1, FOX_HEADS, tile, LANES), lambda bi, ti: (bi, 0, ti, 0)),
                   pl.BlockSpec((1, 1, w, tile), lambda bi, ti: (bi, ti, 0, 0)),
                   pl.BlockSpec((1, tile, w), lambda bi, ti: (bi, ti, 0)),
                   pl.BlockSpec((1, FOX_HEADS, 1, tile), lambda bi, ti: (bi, 0, 0, ti)),
                   pl.BlockSpec((1, 1, 2, LANES), lambda bi, ti: (bi, ti, 0, 0))],
        out_shape=[jax.ShapeDtypeStruct((b, FOX_HEADS, s, LANES), BF16),
                   jax.ShapeDtypeStruct((b, FOX_HEADS, s, LANES), BF16),
                   jax.ShapeDtypeStruct((b, nt, w, tile), BF16),
                   jax.ShapeDtypeStruct((b, s, w), F32),
                   jax.ShapeDtypeStruct((b, FOX_HEADS, 1, s), F32),
                   jax.ShapeDtypeStruct((b, nt, 2, LANES), F32)],
        scratch_shapes=[pltpu.VMEM((1, LANES), F32), pltpu.VMEM((1, LANES), F32)],
        compiler_params=_params("arbitrary", "arbitrary"),
        name="layer1_proj",
    )(x, mod, norm_g.reshape(1, d), w_in, bf_pad, qg, kg, bd, _carrier_selectors())


def _query_operands(q):
    slab = 2 * SUBLANES
    row = lax.broadcasted_iota(jnp.int32, (slab, q.shape[0]), 0)
    q_t = q.astype(F32).T
    carriers = q_t[AUG_LANE:AUG_LANE + slab]
    return tuple(jnp.concatenate([q_t[:AUG_LANE], jnp.where(keep, carriers, 0.0),
                                  q_t[AUG_LANE + slab:]], axis=0).astype(BF16)
                 for keep in (row < 6, row >= 6))


def _gated_output(acc_t, g_ref, o_ref, rows):
    o_ref[0, rows] = (acc_t.T * _silu(g_ref[0, rows])).astype(BF16)


def _fox_bounded_kernel(first_ref, end_ref, qa_ref, ka_ref, vt_ref, cq_ref, g_ref, o_ref,
                        p_ref, acc_ref, *, tile, heads):
    TQ, HP = tile, heads
    QT = qa_ref.shape[2] // TQ
    SB = vt_ref.shape[-1]
    R = TQ // SB
    n_blocks = ka_ref.shape[2] // SB
    b = pl.program_id(0)
    group = pl.program_id(1)
    i0 = pl.program_id(2) * QT
    chains = [(qs, e) for qs in range(QT) for e in range(HP)]
    cols = [slice(qs * TQ, (qs + 1) * TQ) for qs in range(QT)]
    operands = {(qs, e): _query_operands(qa_ref[0, e, cols[qs], :]) for qs, e in chains}
    causal = (lax.broadcasted_iota(jnp.int32, (TQ, TQ), 0)
              <= lax.broadcasted_iota(jnp.int32, (TQ, TQ), 1))

    def keys(e, n):
        return ka_ref[0, e, pl.ds(pl.multiple_of(n * TQ, TQ), TQ), :]

    def values_t(e, blk):
        return vt_ref[0, blk, pl.ds(e * FOX_HEAD_DIM, FOX_HEAD_DIM), :]

    def rest(c, blk, gate=1.0):
        qs, e = c
        end = end_ref[(b * n_blocks + blk) * FOX_HEADS + HP * group + e]
        return jnp.exp2(jnp.minimum(cq_ref[0, e][:, cols[qs]] - end, 0.0)) * gate

    def col_partial(p):
        return jnp.sum(p.reshape(p.shape[0] // SUBLANES, SUBLANES, p.shape[1]), axis=0)

    def far_scores(c, n):
        return jnp.dot(keys(c[1], n), operands[c][1], preferred_element_type=F32)

    def contract(c, n, p_bf16, gate=1.0):
        for r in range(R):
            pv = jnp.dot(values_t(c[1], n * R + r), p_bf16[r * SB:(r + 1) * SB],
                         preferred_element_type=F32)
            acc_ref[c] += rest(c, n * R + r, gate) * pv

    def weighted_partial(c, n, p, gate=1.0):
        return sum(rest(c, n * R + r, gate) * col_partial(p[r * SB:(r + 1) * SB])
                   for r in range(R))

    first = first_ref[(b * pl.num_programs(1) + group) * pl.num_programs(2) + pl.program_id(2)]
    gate0 = jnp.where(i0 > 0, 1.0, 0.0)

    l_part = {}

    def own_tile(c):
        qs, e = c
        s_t = jnp.dot(keys(e, i0 + qs), operands[c][0], preferred_element_type=F32)
        p = jnp.where(causal, jnp.exp2(s_t), 0.0)
        l_part[c] = col_partial(p)
        p = p.astype(BF16)

        def later():
            v_t = jnp.concatenate([values_t(e, (i0 + qs) * R + r) for r in range(R)], axis=1)
            acc_ref[c] = jnp.dot(v_t, p, preferred_element_type=F32)
        return later

    def earlier_tile(c, n):
        p = jnp.exp2(far_scores(c, n))
        l_part[c] = l_part[c] + weighted_partial(c, n, p)
        p = p.astype(BF16)
        return lambda: contract(c, n, p)

    def fill(c):
        p = jnp.exp2(far_scores(c, first))
        l_part[c] = l_part[c] + weighted_partial(c, first, p, gate0)
        p_ref[c] = p.astype(BF16)

    jobs = ([functools.partial(own_tile, c) for c in chains]
            + [functools.partial(earlier_tile, c, i0 + t) for c in chains for t in range(c[0])]
            + [functools.partial(fill, c) for c in chains])
    pending = None
    for job in jobs:
        later = job()
        if pending is not None:
            pending()
        pending = later
    if pending is not None:
        pending()

    def body(n, l_run):
        s_new = []
        for c in chains:
            s_new.append(far_scores(c, n))
            contract(c, n - 1, p_ref[c])
        out = []
        for c, s_t, l_c in zip(chains, s_new, l_run):
            p = jnp.exp2(s_t)
            out.append(l_c + weighted_partial(c, n, p))
            p_ref[c] = p.astype(BF16)
        return tuple(out)

    l_run = lax.fori_loop(first + 1, i0, body, tuple(l_part[c] for c in chains))
    for c in chains:
        contract(c, jnp.maximum(i0 - 1, 0), p_ref[c], gate0)
    for qs in range(QT):
        out_t = [acc_ref[qs, e] * (1.0 / jnp.sum(l_run[qs * HP + e], axis=0, keepdims=True))
                 for e in range(HP)]
        _gated_output(jnp.concatenate(out_t, axis=0), g_ref, o_ref, cols[qs])


def _fox_online_kernel(start_ref, qa_ref, ka_ref, vt_ref, g_ref, o_ref, *, tile, heads):
    TB, HP = tile, heads
    n_blocks = ka_ref.shape[2] // TB
    b = pl.program_id(0)
    group = pl.program_id(1)
    i = pl.program_id(2)
    q_t = [_query_operands(qa_ref[0, e])[0] for e in range(HP)]
    k_row = lax.broadcasted_iota(jnp.int32, (TB, TB), 0)
    q_col = lax.broadcasted_iota(jnp.int32, (TB, TB), 1)

    def step(j, carry, masked):
        out = []
        for e in range(HP):
            m_run, l_run, acc = carry[e]
            k_blk = ka_ref[0, e, pl.ds(pl.multiple_of(j * TB, TB), TB), :]
            s_t = jnp.dot(k_blk, q_t[e], preferred_element_type=F32)
            if masked:
                s_t = jnp.where(k_row <= q_col, s_t, MASK_VALUE)
            hd = HP * group + e
            delta = (start_ref[(b * n_blocks + i) * FOX_HEADS + hd]
                     - start_ref[(b * n_blocks + j) * FOX_HEADS + hd])
            m_new = jnp.maximum(m_run, jnp.max(s_t, axis=0, keepdims=True) + delta)
            p_t = jnp.exp2(s_t - (m_new - delta))
            alpha = jnp.exp2(m_run - m_new)
            l_new = alpha * l_run + jnp.sum(p_t, axis=0, keepdims=True)
            v_t = vt_ref[0, j, pl.ds(e * FOX_HEAD_DIM, FOX_HEAD_DIM), :]
            acc_new = alpha * acc + jnp.dot(v_t, p_t.astype(BF16), preferred_element_type=F32)
            out.append((m_new, l_new, acc_new))
        return tuple(out)

    init = (jnp.full((1, TB), MASK_VALUE, F32), jnp.zeros((1, TB), F32),
            jnp.zeros((FOX_HEAD_DIM, TB), F32))
    carry = lax.fori_loop(0, i, lambda j, c: step(j, c, False), (init,) * HP)
    carry = step(i, carry, True)
    _gated_output(jnp.concatenate([acc / l_run for _, l_run, acc in carry], axis=0), g_ref, o_ref,
                  slice(None))


def _fox_attention(qa, ka, vt, g, cq, edges, *, bounded):
    b, n_heads, s, _ = qa.shape
    _, nvb, w, vb = vt.shape
    if bounded:
        tile, hp, qt = ATTN_Q_TILE, ATTN_HEADS, ATTN_TILES_PER_STEP
    else:
        tile, hp, qt = vb, LANES // FOX_HEAD_DIM, 1
    gw = hp * FOX_HEAD_DIM
    rows = qt * tile
    specs = dict(
        table=pl.BlockSpec(memory_space=pltpu.SMEM),
        q=pl.BlockSpec((1, hp, rows, LANES), lambda bi, p, i: (bi, p, i, 0)),
        k=pl.BlockSpec((1, hp, s, LANES), lambda bi, p, i: (bi, p, 0, 0)),
        v=pl.BlockSpec((1, nvb, gw, vb), lambda bi, p, i: (bi, 0, p, 0)),
        cq=pl.BlockSpec((1, hp, 1, rows), lambda bi, p, i: (bi, p, 0, i)),
        g=pl.BlockSpec((1, rows, gw), lambda bi, p, i: (bi, i, p)))
    if bounded:
        body = functools.partial(_fox_bounded_kernel, tile=tile, heads=hp)
        names = ("table", "table", "q", "k", "v", "cq", "g")
        step_start = edges[:, ::rows // vb, 0, :FOX_HEADS]
        tile_end = edges[:, tile // vb - 1::tile // vb, 1, :FOX_HEADS]
        dead = (step_start[:, :, None] - tile_end[:, None]) < -FOX_DEAD_EXPONENT
        before = jnp.arange(s // rows)[:, None] * qt > jnp.arange(s // tile)[None, :]
        first_live = jnp.sum(dead & before[None, :, :, None], axis=2)
        last_before = jnp.maximum(jnp.arange(s // rows) * qt - 1, 0)
        first_live = jnp.minimum(first_live, last_before[None, :, None])
        first_live = jnp.min(first_live.reshape(b, s // rows, n_heads // hp, hp), axis=-1)
        args = (first_live.transpose(0, 2, 1).reshape(-1).astype(jnp.int32),
                edges[:, :, 1, :FOX_HEADS].reshape(-1), qa, ka, vt, cq, g)
        scratch = [pltpu.VMEM((qt, hp, tile, tile), BF16),
                   pltpu.VMEM((qt, hp, FOX_HEAD_DIM, tile), F32)]
    else:
        body = functools.partial(_fox_online_kernel, tile=tile, heads=hp)
        names = ("table", "q", "k", "v", "g")
        args = (edges[:, :, 0, :FOX_HEADS].reshape(-1), qa, ka, vt, g)
        scratch = []
    return pl.pallas_call(
        body,
        grid=(b, n_heads // hp, s // rows),
        in_specs=[specs[n] for n in names],
        out_specs=specs["g"],
        out_shape=jax.ShapeDtypeStruct((b, s, w), BF16),
        scratch_shapes=scratch,
        compiler_params=_params("arbitrary", "arbitrary", "arbitrary"),
        name="fox_attention_bounded" if bounded else "fox_attention_online",
    )(*args)


def _out_proj_kernel(a_ref, x_ref, mod_ref, w_ref, o_ref):
    y = jnp.dot(a_ref[0], w_ref[...], preferred_element_type=F32)
    o_ref[0] = x_ref[0] + mod_ref[0, 2:3, :] * y


def _out_proj(a, x, mod, w_out):
    b, s, d = x.shape
    tile = OUT_PROJ_TILE
    k = a.shape[-1]
    return pl.pallas_call(
        _out_proj_kernel,
        grid=(b, s // tile),
        in_specs=[pl.BlockSpec((1, tile, k), lambda bi, ti: (bi, ti, 0)),
                  pl.BlockSpec((1, tile, d), lambda bi, ti: (bi, ti, 0)),
                  pl.BlockSpec((1, 3, d), lambda bi, ti: (bi, 0, 0)),
                  pl.BlockSpec((k, d), lambda bi, ti: (0, 0))],
        out_specs=pl.BlockSpec((1, tile, d), lambda bi, ti: (bi, ti, 0)),
        out_shape=jax.ShapeDtypeStruct((b, s, d), F32),
        compiler_params=_params("arbitrary", "arbitrary"),
        name="layer1_out_proj",
    )(a, x, mod, w_out.astype(BF16))


def _layer1(x, mod, norm_g, w_in, b_f, qnorm_g, knorm_g, w_out):
    d, w = x.shape[-1], FOX_WIDTH
    order = jnp.argsort(b_f)
    w_in = w_in.astype(BF16)
    w_heads = w_in[:, :4 * w].reshape(d, 4, FOX_HEADS, FOX_HEAD_DIM)[:, :, order]
    w_forget = jnp.zeros((d, LANES), BF16).at[:, :FOX_HEADS].set(
        w_in[:, 4 * w:][:, order])
    w_in = jnp.concatenate([w_heads.reshape(d, 4 * w), w_forget], axis=1)
    w_out = w_out.astype(BF16).reshape(FOX_HEADS, FOX_HEAD_DIM, d)[order].reshape(w, d)
    qa, ka, vt, g, cq, edges = _layer1_proj(x, mod, norm_g, w_in, b_f[order], qnorm_g, knorm_g)
    score_bound = (LOG2E * FOX_HEAD_DIM ** 0.5
                   * jnp.max(jnp.abs(qnorm_g)) * jnp.max(jnp.abs(knorm_g)))
    gated = lax.cond(score_bound <= FOX_SAFE_LOGIT,
                     functools.partial(_fox_attention, bounded=True),
                     functools.partial(_fox_attention, bounded=False),
                     qa, ka, vt, g, cq, edges)
    return _out_proj(gated, x, mod, w_out)


def kernel(x, c, norm_g, ada_w, ada_b, hgrn_lb, even_w_in, hgrn_onorm_g, pool_w, pool_scale,
           even_w_out, odd_w_in, fox_b_f, fox_qnorm_g, fox_knorm_g, odd_w_out):
    depth = norm_g.shape[0]
    mods = _adaln_mods(c, ada_w, ada_b)
    for l in range(depth):
        j = l // 2
        if l % 2 == 0:
            x = _layer0(x, mods[l], norm_g[l], even_w_in[j], hgrn_lb, hgrn_onorm_g[j],
                        pool_w[j], pool_scale[j], even_w_out[j], layer_slot=l)
        else:
            x = _layer1(x, mods[l], norm_g[l], odd_w_in[j], fox_b_f[j], fox_qnorm_g[j],
                        fox_knorm_g[j], odd_w_out[j])
    return x
```

```python
import functools

import jax
import jax.numpy as jnp
import numpy as np
from jax import lax
from jax.experimental import pallas as pl
from jax.experimental.pallas import tpu as pltpu

F32 = jnp.float32
BF16 = jnp.bfloat16
EPS = 1e-6

HGRN_HEADS = 4
HGRN_KEY = 128
HGRN_VAL = 128
HGRN_KW = HGRN_HEADS * HGRN_KEY
HGRN_VW = HGRN_HEADS * HGRN_VAL
POOL_WINDOWS = (2, 4, 8, 16)
POOL_GROUP = 128
POOL_WIDTH = POOL_GROUP * len(POOL_WINDOWS)
POOL_HISTORY = 16
FOX_HEADS = 16
FOX_HEAD_DIM = 64
FOX_WIDTH = FOX_HEADS * FOX_HEAD_DIM

LANES = 128
SUBLANES = 8
VMEM_LIMIT_BYTES = 56 * 1024 * 1024

SEQ_TILE = 256
OUT_PROJ_TILE = 1024
HGRN_CHUNK = 128
ATTN_Q_TILE = 512
ATTN_HEADS = 4
ATTN_TILES_PER_STEP = 2
FOX_SAFE_LOGIT = 100.0
FOX_DEAD_EXPONENT = 150.0
LOG2E = 1.4426950408889634
MASK_VALUE = -1e30
AUG_LANE = FOX_HEAD_DIM


def _silu(x):
    return x * jax.nn.sigmoid(x)


def _params(*semantics):
    return pltpu.CompilerParams(dimension_semantics=semantics, vmem_limit_bytes=VMEM_LIMIT_BYTES)


def _mod_kernel(c_ref, w_ref, b_ref, o_ref):
    w = w_ref[0]
    for bi in range(c_ref.shape[0]):
        cond = _silu(c_ref[bi])
        cols = [jnp.sum(w[:, j:j + LANES] * cond, axis=0, keepdims=True)
                for j in range(0, w.shape[1], LANES)]
        o_ref[0, bi:bi + 1, :] = jnp.concatenate(cols, axis=1) + b_ref[0]


def _adaln_mods(c, ada_w, ada_b):
    depth, d, n = ada_w.shape
    b = c.shape[0]
    tn = 512
    out = pl.pallas_call(
        _mod_kernel,
        grid=(depth, n // tn),
        in_specs=[pl.BlockSpec((b, d, LANES), lambda l, j: (0, 0, 0)),
                  pl.BlockSpec((1, d, tn), lambda l, j: (l, 0, j)),
                  pl.BlockSpec((1, 1, tn), lambda l, j: (l, 0, j))],
        out_specs=pl.BlockSpec((1, b, tn), lambda l, j: (l, 0, j)),
        out_shape=jax.ShapeDtypeStruct((depth, b, n), F32),
        compiler_params=_params("arbitrary", "arbitrary"),
        name="adaln_mods",
    )(jnp.broadcast_to(c[:, :, None], (b, d, LANES)), ada_w, ada_b.reshape(depth, 1, n))
    return out.reshape(depth, b, 3, d)


def _modulated_norm(x, shift, scale, gain):
    ms = jnp.mean(x * x, axis=-1, keepdims=True)
    return x * lax.rsqrt(ms + EPS) * gain * (1.0 + scale) + shift


def _layer0_kernel(x_ref, mod_ref, ng_ref, win_ref, lb_ref, og_ref, pw_ref, ps_ref, wout_ref,
                   o_ref, state_ref, carry_ref, *, tile, chunk, layer_slot):
    T, C, H = tile, chunk, HGRN_HEADS
    batch = range(x_ref.shape[0])
    t_idx = pl.program_id(0)

    @pl.when(t_idx == 0)
    def _():
        state_ref[...] = jnp.zeros_like(state_ref)
        carry_ref[...] = jnp.zeros_like(carry_ref)

    lbv = lb_ref[...]
    e = jnp.exp(lbv - jnp.max(lbv, axis=0, keepdims=True))
    lower = (jnp.sum(e[0:layer_slot + 1], axis=0, keepdims=True)
             / jnp.sum(e, axis=0, keepdims=True))

    xs, parts = [], []
    for bi in batch:
        x = x_ref[bi]
        h = _modulated_norm(x, mod_ref[bi, 0:1, :], mod_ref[bi, 1:2, :], ng_ref[...])
        proj = jnp.dot(h.astype(BF16), win_ref[...], preferred_element_type=F32)
        o0, widths, piece = 0, (HGRN_KW, HGRN_KW, HGRN_VW, HGRN_VW, POOL_WIDTH, POOL_WIDTH), []
        for wd in widths:
            piece.append(proj[:, o0:o0 + wd])
            o0 += wd
        xs.append(x)
        parts.append(piece)

    row = lax.broadcasted_iota(jnp.int32, (T, HGRN_KW), 0)
    ti = lax.broadcasted_iota(jnp.int32, (C, C), 0)
    si = lax.broadcasted_iota(jnp.int32, (C, C), 1)
    n_chunks = T // C
    nt_dims = (((1,), (1,)), ((), ()))
    tn_dims = (((0,), (0,)), ((), ()))

    def level_scores(qe, ke, mask, scores):
        qb = qe.astype(BF16)
        kb = ke.astype(BF16)
        out = []
        for c in range(n_chunks):
            for hh in range(H):
                rs = slice(c * C, (c + 1) * C)
                cs = slice(hh * HGRN_KEY, (hh + 1) * HGRN_KEY)
                d = lax.dot_general(qb[rs, cs], kb[rs, cs], nt_dims, preferred_element_type=F32)
                prev = scores[c * H + hh]
                out.append(jnp.where(mask, d, 0.0 if prev is None else prev))
        return out

    scores, q_dec, k_dec, total = [], [], [], []
    for bi in batch:
        q, f = parts[bi][0], parts[bi][1]
        forget = lower + (1.0 - lower) * jax.nn.sigmoid(f)
        key = 1.0 - forget
        scores.append(level_scores(q, key, ti == si, [None] * (n_chunks * H)))
        q_dec.append(q * forget)
        k_dec.append(key)
        total.append(forget)
    m = 1
    while m < C:
        mask = ((ti ^ si) < 2 * m) & ((ti & m) != 0) & ((si & m) == 0)
        upper = (row & m) != 0
        for bi in batch:
            scores[bi] = level_scores(q_dec[bi], k_dec[bi], mask, scores[bi])
            t_dn = pltpu.roll(total[bi], m, 0)
            t_up = pltpu.roll(total[bi], T - m, 0)
            q_dec[bi] = q_dec[bi] * jnp.where(upper, t_dn, 1.0)
            k_dec[bi] = k_dec[bi] * jnp.where(upper, 1.0, t_up)
            total[bi] = total[bi] * jnp.where(upper, t_dn, t_up)
        m *= 2

    o_a = []
    for bi in batch:
        val, g_a = parts[bi][2], parts[bi][3]
        q_in = q_dec[bi].astype(BF16)
        k_out = k_dec[bi].astype(BF16)
        decay = total[bi]
        val_b = val.astype(BF16)
        oa_rows = []
        for c in range(n_chunks):
            rs = slice(c * C, (c + 1) * C)
            heads_out = []
            for hh in range(H):
                cs = slice(hh * HGRN_KEY, (hh + 1) * HGRN_KEY)
                st = state_ref[bi, hh]
                vb = val_b[rs, cs]
                o = jnp.dot(scores[bi][c * H + hh].astype(BF16), vb, preferred_element_type=F32)
                o = o + lax.dot_general(q_in[rs, cs], st.astype(BF16), nt_dims,
                                        preferred_element_type=F32)
                state_ref[bi, hh] = (st * decay[c * C:c * C + 1, cs]
                                     + lax.dot_general(vb, k_out[rs, cs], tn_dims,
                                                       preferred_element_type=F32))
                ms_o = jnp.mean(o * o, axis=-1, keepdims=True)
                heads_out.append(o * lax.rsqrt(ms_o + EPS))
            oa_rows.append(jnp.concatenate(heads_out, axis=1))
        o_a.append(jnp.concatenate(oa_rows, axis=0) * og_ref[...] * _silu(g_a))

    pos = lax.broadcasted_iota(jnp.int32, (T, POOL_GROUP), 0) + t_idx * T + 1
    for bi in batch:
        u, g_b = parts[bi][4], parts[bi][5]
        ub = jnp.concatenate([carry_ref[bi], u], axis=0)
        carry_ref[bi] = u[T - POOL_HISTORY:T, :]
        wins = []
        acc = ub
        sh = 1
        while sh < max(POOL_WINDOWS):
            acc = acc + pltpu.roll(acc, sh, 0)
            sh *= 2
            wins.append(acc)
        ob = []
        for gi, w in enumerate(POOL_WINDOWS):
            cs = slice(gi * POOL_GROUP, (gi + 1) * POOL_GROUP)
            win = wins[w.bit_length() - 2][POOL_HISTORY:, cs]
            cnt = jnp.minimum(pos, w).astype(F32)
            pooled = win / cnt - u[:, cs]
            ob.append(jnp.dot(pooled.astype(BF16), pw_ref[gi], preferred_element_type=F32))
        o_b = jnp.concatenate(ob, axis=1) * ps_ref[...] * _silu(g_b)
        mixed = jnp.concatenate([o_a[bi], o_b], axis=1).astype(BF16)
        y = jnp.dot(mixed, wout_ref[...], preferred_element_type=F32)
        o_ref[bi] = xs[bi] + mod_ref[bi, 2:3, :] * y


def _layer0(x, mod, norm_g, w_in, hgrn_lb, onorm_g, pool_w, pool_scale, w_out, layer_slot):
    b, s, d = x.shape
    tile = SEQ_TILE
    n_in = w_in.shape[1]
    mix = w_out.shape[0]
    const2 = lambda ti: (0, 0)
    const3 = lambda ti: (0, 0, 0)
    return pl.pallas_call(
        functools.partial(_layer0_kernel, tile=tile, chunk=HGRN_CHUNK, layer_slot=layer_slot),
        grid=(s // tile,),
        in_specs=[pl.BlockSpec((b, tile, d), lambda ti: (0, ti, 0)),
                  pl.BlockSpec((b, 3, d), const3),
                  pl.BlockSpec((1, d), const2),
                  pl.BlockSpec((d, n_in), const2, pipeline_mode=pl.Buffered(1)),
                  pl.BlockSpec(hgrn_lb.shape, const2),
                  pl.BlockSpec((1, HGRN_VW), const2),
                  pl.BlockSpec(pool_w.shape, const3),
                  pl.BlockSpec((1, POOL_WIDTH), const2),
                  pl.BlockSpec((mix, d), const2, pipeline_mode=pl.Buffered(1))],
        out_specs=pl.BlockSpec((b, tile, d), lambda ti: (0, ti, 0)),
        out_shape=jax.ShapeDtypeStruct((b, s, d), F32),
        scratch_shapes=[pltpu.VMEM((b, HGRN_HEADS, HGRN_VAL, HGRN_KEY), F32),
                        pltpu.VMEM((b, POOL_HISTORY, POOL_WIDTH), F32)],
        compiler_params=_params("arbitrary"),
        name="layer0_hgrn_pool",
    )(x, mod, norm_g.reshape(1, d), w_in.astype(BF16), hgrn_lb, onorm_g.reshape(1, HGRN_VW),
      pool_w.astype(BF16), pool_scale.reshape(1, POOL_WIDTH), w_out.astype(BF16))


def _split3(c):
    hi = c.astype(BF16).astype(F32)
    r = c - hi
    mid = r.astype(BF16).astype(F32)
    lo = (r - mid).astype(BF16).astype(F32)
    return hi, mid, lo


def _carrier_selectors():
    per_tile = LANES // FOX_HEAD_DIM
    sel = np.zeros((FOX_HEADS // per_tile, LANES, per_tile * LANES), np.float32)
    for hd in range(FOX_HEADS):
        base = (hd % per_tile) * LANES + AUG_LANE
        for j in range(3):
            sel[hd // per_tile, j * FOX_HEADS + hd, base + j] = 1.0
            sel[hd // per_tile, j * FOX_HEADS + hd, base + 3 + j] = -1.0
            sel[hd // per_tile, (3 + j) * FOX_HEADS + hd, base + 6 + j] = -1.0
    return jnp.asarray(sel, BF16)


def _layer1_proj_kernel(x_ref, mod_ref, ng_ref, win_ref, bf_ref, qg_ref, kg_ref, bd_ref, sel_ref,
                        qa_ref, ka_ref, vt_ref, g_ref, cq_ref, edge_ref, carry_ref, tile_start_ref,
                        *, tile):
    T, W = tile, FOX_WIDTH
    batch = range(x_ref.shape[0])
    t_idx = pl.program_id(0)
    lane = lax.broadcasted_iota(jnp.int32, (T, LANES), 1)
    row = lax.broadcasted_iota(jnp.int32, (T, LANES), 0)
    bd = bd_ref[...]
    qk_scale = FOX_HEAD_DIM ** -0.5 * LOG2E

    hs = [_modulated_norm(x_ref[bi], mod_ref[bi, 0:1, :], mod_ref[bi, 1:2, :],
                          ng_ref[...]).astype(BF16) for bi in batch]
    projs, feats = [], []
    for bi in batch:
        projs.append(jnp.dot(hs[bi], win_ref[...], preferred_element_type=F32))
        q = projs[bi][:, 0:W]
        k = projs[bi][:, W:2 * W]
        vt_ref[bi, 0] = projs[bi][:, 2 * W:3 * W].T.astype(BF16)
        g_ref[bi] = projs[bi][:, 3 * W:4 * W]
        pairs = []
        for pair in range(FOX_HEADS // 2):
            cs = slice(pair * LANES, (pair + 1) * LANES)
            qp = q[:, cs]
            kp = k[:, cs]
            squares = jnp.concatenate([(qp * qp).astype(BF16), (kp * kp).astype(BF16)], axis=1)
            ms = jnp.dot(squares, bd, preferred_element_type=F32)
            pairs.append((qp * lax.rsqrt(ms[:, :LANES] + EPS) * (qg_ref[...] * qk_scale),
                          kp * lax.rsqrt(ms[:, LANES:] + EPS) * kg_ref[...]))
        feats.append(pairs)

    ones_q = jnp.where((lane >= AUG_LANE + 3) & (lane < AUG_LANE + 9), 1.0, 0.0)
    ones_k = jnp.where((lane >= AUG_LANE) & (lane < AUG_LANE + 3), 1.0, 0.0)
    for bi in batch:
        z = projs[bi][:, 4 * W:4 * W + LANES] + bf_ref[...]
        logf = jnp.minimum(z, 0.0) - jnp.log1p(jnp.exp(-jnp.abs(z)))
        cum = logf
        sh = 1
        while sh < T:
            cum = cum + jnp.where(row >= sh, pltpu.roll(cum, sh, 0), 0.0)
            sh *= 2
        start = jnp.where(t_idx == 0, 0.0, carry_ref[bi])
        end = start + cum[T - 1:T, :]
        carry_ref[bi] = end
        tile_start = jnp.where(t_idx % (ATTN_Q_TILE // T) == 0, start, tile_start_ref[bi])
        tile_start_ref[bi] = tile_start
        edge_ref[bi, 0] = jnp.concatenate([tile_start, end], axis=0) * LOG2E
        s_hi, s_mid, s_lo = _split3((cum + (start - tile_start)) * LOG2E)
        e_hi, e_mid, e_lo = _split3((cum - cum[T - 1:T, :]) * LOG2E)
        total_t = ((cum + start) * LOG2E).T
        for hd in range(FOX_HEADS):
            cq_ref[bi, hd] = total_t[hd:hd + 1, :]

        terms = (s_hi, s_mid, s_lo, e_hi, e_mid, e_lo)
        packed = jnp.zeros((T, LANES), F32)
        for j, term in enumerate(terms):
            moved = term if j == 0 else pltpu.roll(term, j * FOX_HEADS, 1)
            packed = jnp.where((lane >= j * FOX_HEADS) & (lane < (j + 1) * FOX_HEADS), moved,
                               packed)
        packed = packed.astype(BF16)
        for pair in range(FOX_HEADS // 2):
            qn, kn = feats[bi][pair]
            aug = jnp.dot(packed, sel_ref[pair], preferred_element_type=F32)
            for e in range(LANES // FOX_HEAD_DIM):
                hd = (LANES // FOX_HEAD_DIM) * pair + e
                carriers = aug[:, e * LANES:(e + 1) * LANES]
                q_main = qn if e == 0 else pltpu.roll(qn, FOX_HEAD_DIM, 1)
                k_main = kn if e == 0 else pltpu.roll(kn, FOX_HEAD_DIM, 1)
                q_aug = jnp.where(lane < AUG_LANE + 3, carriers, ones_q)
                k_aug = jnp.where(lane < AUG_LANE + 3, ones_k, carriers)
                qa_ref[bi, hd] = jnp.where(lane < FOX_HEAD_DIM, q_main, q_aug).astype(BF16)
                ka_ref[bi, hd] = jnp.where(lane < FOX_HEAD_DIM, k_main, k_aug).astype(BF16)


def _layer1_proj(x, mod, norm_g, w_in, b_f, qnorm_g, knorm_g):
    b, s, d = x.shape
    tile = SEQ_TILE
    nt = s // tile
    w = FOX_WIDTH
    n_pad = w_in.shape[1]
    bf_pad = jnp.zeros((1, LANES), F32).at[0, :FOX_HEADS].set(b_f)
    blk = jnp.arange(2 * LANES) // FOX_HEAD_DIM
    bd = jnp.where(blk[:, None] == blk[None, :], 1.0 / FOX_HEAD_DIM, 0.0).astype(BF16)
    qg = jnp.tile(qnorm_g, LANES // FOX_HEAD_DIM).reshape(1, LANES)
    kg = jnp.tile(knorm_g, LANES // FOX_HEAD_DIM).reshape(1, LANES)
    const2 = lambda ti: (0, 0)
    const3 = lambda ti: (0, 0, 0)
    return pl.pallas_call(
        functools.partial(_layer1_proj_kernel, tile=tile),
        grid=(nt,),
        in_specs=[pl.BlockSpec((b, tile, d), lambda ti: (0, ti, 0)),
                  pl.BlockSpec((b, 3, d), const3),
                  pl.BlockSpec((1, d), const2),
                  pl.BlockSpec((d, n_pad), const2, pipeline_mode=pl.Buffered(1)),
                  pl.BlockSpec((1, LANES), const2),
                  pl.BlockSpec((1, LANES), const2),
                  pl.BlockSpec((1, LANES), const2),
                  pl.BlockSpec((2 * LANES, 2 * LANES), const2),
                  pl.BlockSpec((FOX_HEADS // 2, LANES, 2 * LANES), const3)],
        out_specs=[pl.BlockSpec((b, FOX_HEADS, tile, LANES), lambda ti: (0, 0, ti, 0)),
                   pl.BlockSpec((b, FOX_HEADS, tile, LANES), lambda ti: (0, 0, ti, 0)),
                   pl.BlockSpec((b, 1, w, tile), lambda ti: (0, ti, 0, 0)),
                   pl.BlockSpec((b, tile, w), lambda ti: (0, ti, 0)),
                   pl.BlockSpec((b, FOX_HEADS, 1, tile), lambda ti: (0, 0, 0, ti)),
                   pl.BlockSpec((b, 1, 2, LANES), lambda ti: (0, ti, 0, 0))],
        out_shape=[jax.ShapeDtypeStruct((b, FOX_HEADS, s, LANES), BF16),
                   jax.ShapeDtypeStruct((b, FOX_HEADS, s, LANES), BF16),
                   jax.ShapeDtypeStruct((b, nt, w, tile), BF16),
                   jax.ShapeDtypeStruct((b, s, w), F32),
                   jax.ShapeDtypeStruct((b, FOX_HEADS, 1, s), F32),
                   jax.ShapeDtypeStruct((b, nt, 2, LANES), F32)],
        scratch_shapes=[pltpu.VMEM((b, 1, LANES), F32), pltpu.VMEM((b, 1, LANES), F32)],
        compiler_params=_params("arbitrary"),
        name="layer1_proj",
    )(x, mod, norm_g.reshape(1, d), w_in, bf_pad, qg, kg, bd, _carrier_selectors())


def _query_operands(q):
    slab = 2 * SUBLANES
    row = lax.broadcasted_iota(jnp.int32, (slab, q.shape[0]), 0)
    q_t = q.astype(F32).T
    carriers = q_t[AUG_LANE:AUG_LANE + slab]
    return tuple(jnp.concatenate([q_t[:AUG_LANE], jnp.where(keep, carriers, 0.0),
                                  q_t[AUG_LANE + slab:]], axis=0).astype(BF16)
                 for keep in (row < 6, row >= 6))


def _gated_output(acc_t, g_ref, o_ref, rows):
    o_ref[0, rows] = (acc_t.T * _silu(g_ref[0, rows])).astype(BF16)


def _fox_bounded_kernel(first_ref, end_ref, qa_ref, ka_ref, vt_ref, cq_ref, g_ref, o_ref,
                        p_ref, acc_ref, *, tile, heads):
    TQ, HP = tile, heads
    QT = qa_ref.shape[2] // TQ
    SB = vt_ref.shape[-1]
    R = TQ // SB
    n_blocks = ka_ref.shape[2] // SB
    b = pl.program_id(0)
    group = pl.program_id(1)
    i0 = pl.program_id(2) * QT
    chains = [(qs, e) for qs in range(QT) for e in range(HP)]
    cols = [slice(qs * TQ, (qs + 1) * TQ) for qs in range(QT)]
    operands = {(qs, e): _query_operands(qa_ref[0, e, cols[qs], :]) for qs, e in chains}
    causal = (lax.broadcasted_iota(jnp.int32, (TQ, TQ), 0)
              <= lax.broadcasted_iota(jnp.int32, (TQ, TQ), 1))

    def keys(e, n):
        return ka_ref[0, e, pl.ds(pl.multiple_of(n * TQ, TQ), TQ), :]

    def values_t(e, blk):
        return vt_ref[0, blk, pl.ds(e * FOX_HEAD_DIM, FOX_HEAD_DIM), :]

    def rest(c, blk, gate=1.0):
        qs, e = c
        end = end_ref[(b * n_blocks + blk) * FOX_HEADS + HP * group + e]
        return jnp.exp2(jnp.minimum(cq_ref[0, e][:, cols[qs]] - end, 0.0)) * gate

    def col_partial(p):
        return jnp.sum(p.reshape(p.shape[0] // SUBLANES, SUBLANES, p.shape[1]), axis=0)

    def far_scores(c, n):
        return jnp.dot(keys(c[1], n), operands[c][1], preferred_element_type=F32)

    def contract(c, n, p_bf16, gate=1.0):
        for r in range(R):
            pv = jnp.dot(values_t(c[1], n * R + r), p_bf16[r * SB:(r + 1) * SB],
                         preferred_element_type=F32)
            acc_ref[c] += rest(c, n * R + r, gate) * pv

    def weighted_partial(c, n, p, gate=1.0):
        return sum(rest(c, n * R + r, gate) * col_partial(p[r * SB:(r + 1) * SB])
                   for r in range(R))

    first = first_ref[(b * pl.num_programs(1) + group) * pl.num_programs(2) + pl.program_id(2)]
    gate0 = jnp.where(i0 > 0, 1.0, 0.0)

    l_part = {}

    def own_tile(c):
        qs, e = c
        s_t = jnp.dot(keys(e, i0 + qs), operands[c][0], preferred_element_type=F32)
        p = jnp.where(causal, jnp.exp2(s_t), 0.0)
        l_part[c] = col_partial(p)
        p = p.astype(BF16)

        def later():
            v_t = jnp.concatenate([values_t(e, (i0 + qs) * R + r) for r in range(R)], axis=1)
            acc_ref[c] = jnp.dot(v_t, p, preferred_element_type=F32)
        return later

    def earlier_tile(c, n):
        p = jnp.exp2(far_scores(c, n))
        l_part[c] = l_part[c] + weighted_partial(c, n, p)
        p = p.astype(BF16)
        return lambda: contract(c, n, p)

    def fill(c):
        p = jnp.exp2(far_scores(c, first))
        l_part[c] = l_part[c] + weighted_partial(c, first, p, gate0)
        p_ref[c] = p.astype(BF16)

    jobs = ([functools.partial(own_tile, c) for c in chains]
            + [functools.partial(earlier_tile, c, i0 + t) for c in chains for t in range(c[0])]
            + [functools.partial(fill, c) for c in chains])
    pending = None
    for job in jobs:
        later = job()
        if pending is not None:
            pending()
        pending = later
    if pending is not None:
        pending()

    def body(n, l_run):
        s_new = []
        for c in chains:
            s_new.append(far_scores(c, n))
            contract(c, n - 1, p_ref[c])
        out = []
        for c, s_t, l_c in zip(chains, s_new, l_run):
            p = jnp.exp2(s_t)
            out.append(l_c + weighted_partial(c, n, p))
            p_ref[c] = p.astype(BF16)
        return tuple(out)

    l_run = lax.fori_loop(first + 1, i0, body, tuple(l_part[c] for c in chains))
    for c in chains:
        contract(c, jnp.maximum(i0 - 1, 0), p_ref[c], gate0)
    for qs in range(QT):
        out_t = [acc_ref[qs, e] * (1.0 / jnp.sum(l_run[qs * HP + e], axis=0, keepdims=True))
                 for e in range(HP)]
        _gated_output(jnp.concatenate(out_t, axis=0), g_ref, o_ref, cols[qs])


def _fox_online_kernel(start_ref, qa_ref, ka_ref, vt_ref, g_ref, o_ref, *, tile, heads):
    TB, HP = tile, heads
    n_blocks = ka_ref.shape[2] // TB
    b = pl.program_id(0)
    group = pl.program_id(1)
    i = pl.program_id(2)
    q_t = [_query_operands(qa_ref[0, e])[0] for e in range(HP)]
    k_row = lax.broadcasted_iota(jnp.int32, (TB, TB), 0)
    q_col = lax.broadcasted_iota(jnp.int32, (TB, TB), 1)

    def step(j, carry, masked):
        out = []
        for e in range(HP):
            m_run, l_run, acc = carry[e]
            k_blk = ka_ref[0, e, pl.ds(pl.multiple_of(j * TB, TB), TB), :]
            s_t = jnp.dot(k_blk, q_t[e], preferred_element_type=F32)
            if masked:
                s_t = jnp.where(k_row <= q_col, s_t, MASK_VALUE)
            hd = HP * group + e
            delta = (start_ref[(b * n_blocks + i) * FOX_HEADS + hd]
                     - start_ref[(b * n_blocks + j) * FOX_HEADS + hd])
            m_new = jnp.maximum(m_run, jnp.max(s_t, axis=0, keepdims=True) + delta)
            p_t = jnp.exp2(s_t - (m_new - delta))
            alpha = jnp.exp2(m_run - m_new)
            l_new = alpha * l_run + jnp.sum(p_t, axis=0, keepdims=True)
            v_t = vt_ref[0, j, pl.ds(e * FOX_HEAD_DIM, FOX_HEAD_DIM), :]
            acc_new = alpha * acc + jnp.dot(v_t, p_t.astype(BF16), preferred_element_type=F32)
            out.append((m_new, l_new, acc_new))
        return tuple(out)

    init = (jnp.full((1, TB), MASK_VALUE, F32), jnp.zeros((1, TB), F32),
            jnp.zeros((FOX_HEAD_DIM, TB), F32))
    carry = lax.fori_loop(0, i, lambda j, c: step(j, c, False), (init,) * HP)
    carry = step(i, carry, True)
    _gated_output(jnp.concatenate([acc / l_run for _, l_run, acc in carry], axis=0), g_ref, o_ref,
                  slice(None))


def _fox_attention(qa, ka, vt, g, cq, edges, *, bounded):
    b, n_heads, s, _ = qa.shape
    _, nvb, w, vb = vt.shape
    if bounded:
        tile, hp, qt = ATTN_Q_TILE, ATTN_HEADS, ATTN_TILES_PER_STEP
    else:
        tile, hp, qt = vb, LANES // FOX_HEAD_DIM, 1
    gw = hp * FOX_HEAD_DIM
    rows = qt * tile
    specs = dict(
        table=pl.BlockSpec(memory_space=pltpu.SMEM),
        q=pl.BlockSpec((1, hp, rows, LANES), lambda bi, p, i: (bi, p, i, 0)),
        k=pl.BlockSpec((1, hp, s, LANES), lambda bi, p, i: (bi, p, 0, 0)),
        v=pl.BlockSpec((1, nvb, gw, vb), lambda bi, p, i: (bi, 0, p, 0)),
        cq=pl.BlockSpec((1, hp, 1, rows), lambda bi, p, i: (bi, p, 0, i)),
        g=pl.BlockSpec((1, rows, gw), lambda bi, p, i: (bi, i, p)))
    if bounded:
        body = functools.partial(_fox_bounded_kernel, tile=tile, heads=hp)
        names = ("table", "table", "q", "k", "v", "cq", "g")
        step_start = edges[:, ::rows // vb, 0, :FOX_HEADS]
        tile_end = edges[:, tile // vb - 1::tile // vb, 1, :FOX_HEADS]
        dead = (step_start[:, :, None] - tile_end[:, None]) < -FOX_DEAD_EXPONENT
        before = jnp.arange(s // rows)[:, None] * qt > jnp.arange(s // tile)[None, :]
        first_live = jnp.sum(dead & before[None, :, :, None], axis=2)
        last_before = jnp.maximum(jnp.arange(s // rows) * qt - 1, 0)
        first_live = jnp.minimum(first_live, last_before[None, :, None])
        first_live = jnp.min(first_live.reshape(b, s // rows, n_heads // hp, hp), axis=-1)
        args = (first_live.transpose(0, 2, 1).reshape(-1).astype(jnp.int32),
                edges[:, :, 1, :FOX_HEADS].reshape(-1), qa, ka, vt, cq, g)
        scratch = [pltpu.VMEM((qt, hp, tile, tile), BF16),
                   pltpu.VMEM((qt, hp, FOX_HEAD_DIM, tile), F32)]
    else:
        body = functools.partial(_fox_online_kernel, tile=tile, heads=hp)
        names = ("table", "q", "k", "v", "g")
        args = (edges[:, :, 0, :FOX_HEADS].reshape(-1), qa, ka, vt, g)
        scratch = []
    return pl.pallas_call(
        body,
        grid=(b, n_heads // hp, s // rows),
        in_specs=[specs[n] for n in names],
        out_specs=specs["g"],
        out_shape=jax.ShapeDtypeStruct((b, s, w), BF16),
        scratch_shapes=scratch,
        compiler_params=_params("arbitrary", "arbitrary", "arbitrary"),
        name="fox_attention_bounded" if bounded else "fox_attention_online",
    )(*args)


def _out_proj_kernel(a_ref, x_ref, mod_ref, w_ref, o_ref):
    y = jnp.dot(a_ref[0], w_ref[...], preferred_element_type=F32)
    o_ref[0] = x_ref[0] + mod_ref[0, 2:3, :] * y


def _out_proj(a, x, mod, w_out):
    b, s, d = x.shape
    tile = OUT_PROJ_TILE
    k = a.shape[-1]
    return pl.pallas_call(
        _out_proj_kernel,
        grid=(b, s // tile),
        in_specs=[pl.BlockSpec((1, tile, k), lambda bi, ti: (bi, ti, 0)),
                  pl.BlockSpec((1, tile, d), lambda bi, ti: (bi, ti, 0)),
                  pl.BlockSpec((1, 3, d), lambda bi, ti: (bi, 0, 0)),
                  pl.BlockSpec((k, d), lambda bi, ti: (0, 0))],
        out_specs=pl.BlockSpec((1, tile, d), lambda bi, ti: (bi, ti, 0)),
        out_shape=jax.ShapeDtypeStruct((b, s, d), F32),
        compiler_params=_params("arbitrary", "arbitrary"),
        name="layer1_out_proj",
    )(a, x, mod, w_out.astype(BF16))


def _layer1(x, mod, norm_g, w_in, b_f, qnorm_g, knorm_g, w_out):
    d, w = x.shape[-1], FOX_WIDTH
    order = jnp.argsort(b_f)
    w_in = w_in.astype(BF16)
    w_heads = w_in[:, :4 * w].reshape(d, 4, FOX_HEADS, FOX_HEAD_DIM)[:, :, order]
    w_forget = jnp.zeros((d, LANES), BF16).at[:, :FOX_HEADS].set(
        w_in[:, 4 * w:][:, order])
    w_in = jnp.concatenate([w_heads.reshape(d, 4 * w), w_forget], axis=1)
    w_out = w_out.astype(BF16).reshape(FOX_HEADS, FOX_HEAD_DIM, d)[order].reshape(w, d)
    qa, ka, vt, g, cq, edges = _layer1_proj(x, mod, norm_g, w_in, b_f[order], qnorm_g, knorm_g)
    score_bound = (LOG2E * FOX_HEAD_DIM ** 0.5
                   * jnp.max(jnp.abs(qnorm_g)) * jnp.max(jnp.abs(knorm_g)))
    gated = lax.cond(score_bound <= FOX_SAFE_LOGIT,
                     functools.partial(_fox_attention, bounded=True),
                     functools.partial(_fox_attention, bounded=False),
                     qa, ka, vt, g, cq, edges)
    return _out_proj(gated, x, mod, w_out)


def kernel(x, c, norm_g, ada_w, ada_b, hgrn_lb, even_w_in, hgrn_onorm_g, pool_w, pool_scale,
           even_w_out, odd_w_in, fox_b_f, fox_qnorm_g, fox_knorm_g, odd_w_out):
    depth = norm_g.shape[0]
    mods = _adaln_mods(c, ada_w, ada_b)
    for l in range(depth):
        j = l // 2
        if l % 2 == 0:
            x = _layer0(x, mods[l], norm_g[l], even_w_in[j], hgrn_lb, hgrn_onorm_g[j],
                        pool_w[j], pool_scale[j], even_w_out[j], layer_slot=l)
        else:
            x = _layer1(x, mods[l], norm_g[l], odd_w_in[j], fox_b_f[j], fox_qnorm_g[j],
                        fox_knorm_g[j], odd_w_out[j])
    return x
```

```python
import functools

import jax
import jax.numpy as jnp
import numpy as np
from jax import lax
from jax.experimental import pallas as pl
from jax.experimental.pallas import tpu as pltpu

F32 = jnp.float32
BF16 = jnp.bfloat16
EPS = 1e-6

HGRN_HEADS = 4
HGRN_KEY = 128
HGRN_VAL = 128
HGRN_KW = HGRN_HEADS * HGRN_KEY
HGRN_VW = HGRN_HEADS * HGRN_VAL
POOL_WINDOWS = (2, 4, 8, 16)
POOL_GROUP = 128
POOL_WIDTH = POOL_GROUP * len(POOL_WINDOWS)
POOL_HISTORY = 16
FOX_HEADS = 16
FOX_HEAD_DIM = 64
FOX_WIDTH = FOX_HEADS * FOX_HEAD_DIM

LANES = 128
SUBLANES = 8
VMEM_LIMIT_BYTES = 56 * 1024 * 1024

SEQ_TILE = 256
OUT_PROJ_TILE = 1024
HGRN_CHUNK = 128
ATTN_Q_TILE = 512
ATTN_HEADS = 4
ATTN_TILES_PER_STEP = 2
FOX_SAFE_LOGIT = 100.0
FOX_DEAD_EXPONENT = 150.0
LOG2E = 1.4426950408889634
MASK_VALUE = -1e30
AUG_LANE = FOX_HEAD_DIM


def _silu(x):
    return x * jax.nn.sigmoid(x)


def _params(*semantics):
    return pltpu.CompilerParams(dimension_semantics=semantics, vmem_limit_bytes=VMEM_LIMIT_BYTES)


def _mod_kernel(c_ref, w_ref, b_ref, o_ref, cond_ref):
    @pl.when((pl.program_id(0) == 0) & (pl.program_id(1) == 0))
    def _():
        cond_ref[...] = _silu(c_ref[...])

    w = w_ref[0]
    for bi in range(c_ref.shape[0]):
        cond = cond_ref[bi]
        cols = [jnp.sum(w[:, j:j + LANES] * cond, axis=0, keepdims=True)
                for j in range(0, w.shape[1], LANES)]
        o_ref[0, bi:bi + 1, :] = jnp.concatenate(cols, axis=1) + b_ref[0]


def _adaln_mods(c, ada_w, ada_b):
    depth, d, n = ada_w.shape
    b = c.shape[0]
    tn = 512
    out = pl.pallas_call(
        _mod_kernel,
        grid=(depth, n // tn),
        in_specs=[pl.BlockSpec((b, d, LANES), lambda l, j: (0, 0, 0)),
                  pl.BlockSpec((1, d, tn), lambda l, j: (l, 0, j)),
                  pl.BlockSpec((1, 1, tn), lambda l, j: (l, 0, j))],
        out_specs=pl.BlockSpec((1, b, tn), lambda l, j: (l, 0, j)),
        out_shape=jax.ShapeDtypeStruct((depth, b, n), F32),
        scratch_shapes=[pltpu.VMEM((b, d, LANES), F32)],
        compiler_params=_params("arbitrary", "arbitrary"),
        name="adaln_mods",
    )(jnp.broadcast_to(c[:, :, None], (b, d, LANES)), ada_w, ada_b.reshape(depth, 1, n))
    return out.reshape(depth, b, 3, d)


def _modulated_norm(x, shift, scale, gain):
    ms = jnp.mean(x * x, axis=-1, keepdims=True)
    return x * lax.rsqrt(ms + EPS) * gain * (1.0 + scale) + shift


def _layer0_kernel(x_ref, mod_ref, ng_ref, win32_ref, lb_ref, og_ref, pw_ref, ps_ref, wout32_ref,
                   o_ref, state_ref, carry_ref, win_ref, wout_ref, *, tile, chunk, layer_slot):
    T, C, H = tile, chunk, HGRN_HEADS
    batch = range(x_ref.shape[0])
    t_idx = pl.program_id(0)

    @pl.when(t_idx == 0)
    def _():
        state_ref[...] = jnp.zeros_like(state_ref)
        carry_ref[...] = jnp.zeros_like(carry_ref)
        win_ref[...] = win32_ref[...].astype(BF16)
        wout_ref[...] = wout32_ref[...].astype(BF16)

    lbv = lb_ref[...]
    e = jnp.exp(lbv - jnp.max(lbv, axis=0, keepdims=True))
    lower = (jnp.sum(e[0:layer_slot + 1], axis=0, keepdims=True)
             / jnp.sum(e, axis=0, keepdims=True))

    xs, parts = [], []
    for bi in batch:
        x = x_ref[bi]
        h = _modulated_norm(x, mod_ref[bi, 0:1, :], mod_ref[bi, 1:2, :], ng_ref[...])
        proj = jnp.dot(h.astype(BF16), win_ref[...], preferred_element_type=F32)
        o0, widths, piece = 0, (HGRN_KW, HGRN_KW, HGRN_VW, HGRN_VW, POOL_WIDTH, POOL_WIDTH), []
        for wd in widths:
            piece.append(proj[:, o0:o0 + wd])
            o0 += wd
        xs.append(x)
        parts.append(piece)

    row = lax.broadcasted_iota(jnp.int32, (T, HGRN_KW), 0)
    ti = lax.broadcasted_iota(jnp.int32, (C, C), 0)
    si = lax.broadcasted_iota(jnp.int32, (C, C), 1)
    n_chunks = T // C
    nt_dims = (((1,), (1,)), ((), ()))
    tn_dims = (((0,), (0,)), ((), ()))

    def level_scores(qe, ke, mask, scores):
        qb = qe.astype(BF16)
        kb = ke.astype(BF16)
        out = []
        for c in range(n_chunks):
            for hh in range(H):
                rs = slice(c * C, (c + 1) * C)
                cs = slice(hh * HGRN_KEY, (hh + 1) * HGRN_KEY)
                d = lax.dot_general(qb[rs, cs], kb[rs, cs], nt_dims, preferred_element_type=F32)
                prev = scores[c * H + hh]
                out.append(jnp.where(mask, d, 0.0 if prev is None else prev))
        return out

    scores, q_dec, k_dec, total = [], [], [], []
    for bi in batch:
        q, f = parts[bi][0], parts[bi][1]
        forget = lower + (1.0 - lower) * jax.nn.sigmoid(f)
        key = 1.0 - forget
        scores.append(level_scores(q, key, ti == si, [None] * (n_chunks * H)))
        q_dec.append(q * forget)
        k_dec.append(key)
        total.append(forget)
    m = 1
    while m < C:
        mask = ((ti ^ si) < 2 * m) & ((ti & m) != 0) & ((si & m) == 0)
        upper = (row & m) != 0
        for bi in batch:
            scores[bi] = level_scores(q_dec[bi], k_dec[bi], mask, scores[bi])
            t_dn = pltpu.roll(total[bi], m, 0)
            t_up = pltpu.roll(total[bi], T - m, 0)
            q_dec[bi] = q_dec[bi] * jnp.where(upper, t_dn, 1.0)
            k_dec[bi] = k_dec[bi] * jnp.where(upper, 1.0, t_up)
            total[bi] = total[bi] * jnp.where(upper, t_dn, t_up)
        m *= 2

    o_a = []
    for bi in batch:
        val, g_a = parts[bi][2], parts[bi][3]
        q_in = q_dec[bi].astype(BF16)
        k_out = k_dec[bi].astype(BF16)
        decay = total[bi]
        val_b = val.astype(BF16)
        oa_rows = []
        for c in range(n_chunks):
            rs = slice(c * C, (c + 1) * C)
            heads_out = []
            for hh in range(H):
                cs = slice(hh * HGRN_KEY, (hh + 1) * HGRN_KEY)
                st = state_ref[bi, hh]
                vb = val_b[rs, cs]
                o = jnp.dot(scores[bi][c * H + hh].astype(BF16), vb, preferred_element_type=F32)
                o = o + lax.dot_general(q_in[rs, cs], st.astype(BF16), nt_dims,
                                        preferred_element_type=F32)
                state_ref[bi, hh] = (st * decay[c * C:c * C + 1, cs]
                                     + lax.dot_general(vb, k_out[rs, cs], tn_dims,
                                                       preferred_element_type=F32))
                ms_o = jnp.mean(o * o, axis=-1, keepdims=True)
                heads_out.append(o * lax.rsqrt(ms_o + EPS))
            oa_rows.append(jnp.concatenate(heads_out, axis=1))
        o_a.append(jnp.concatenate(oa_rows, axis=0) * og_ref[...] * _silu(g_a))

    pos = lax.broadcasted_iota(jnp.int32, (T, POOL_GROUP), 0) + t_idx * T + 1
    for bi in batch:
        u, g_b = parts[bi][4], parts[bi][5]
        ub = jnp.concatenate([carry_ref[bi], u], axis=0)
        carry_ref[bi] = u[T - POOL_HISTORY:T, :]
        wins = []
        acc = ub
        sh = 1
        while sh < max(POOL_WINDOWS):
            acc = acc + pltpu.roll(acc, sh, 0)
            sh *= 2
            wins.append(acc)
        ob = []
        for gi, w in enumerate(POOL_WINDOWS):
            cs = slice(gi * POOL_GROUP, (gi + 1) * POOL_GROUP)
            win = wins[w.bit_length() - 2][POOL_HISTORY:, cs]
            cnt = jnp.minimum(pos, w).astype(F32)
            pooled = win / cnt - u[:, cs]
            ob.append(jnp.dot(pooled.astype(BF16), pw_ref[gi], preferred_element_type=F32))
        o_b = jnp.concatenate(ob, axis=1) * ps_ref[...] * _silu(g_b)
        mixed = jnp.concatenate([o_a[bi], o_b], axis=1).astype(BF16)
        y = jnp.dot(mixed, wout_ref[...], preferred_element_type=F32)
        o_ref[bi] = xs[bi] + mod_ref[bi, 2:3, :] * y


def _layer0(x, mod, norm_g, w_in, hgrn_lb, onorm_g, pool_w, pool_scale, w_out, layer_slot):
    b, s, d = x.shape
    tile = SEQ_TILE
    n_in = w_in.shape[1]
    mix = w_out.shape[0]
    const2 = lambda ti: (0, 0)
    const3 = lambda ti: (0, 0, 0)
    return pl.pallas_call(
        functools.partial(_layer0_kernel, tile=tile, chunk=HGRN_CHUNK, layer_slot=layer_slot),
        grid=(s // tile,),
        in_specs=[pl.BlockSpec((b, tile, d), lambda ti: (0, ti, 0)),
                  pl.BlockSpec((b, 3, d), const3),
                  pl.BlockSpec((1, d), const2),
                  pl.BlockSpec((d, n_in), const2, pipeline_mode=pl.Buffered(1)),
                  pl.BlockSpec(hgrn_lb.shape, const2),
                  pl.BlockSpec((1, HGRN_VW), const2),
                  pl.BlockSpec(pool_w.shape, const3),
                  pl.BlockSpec((1, POOL_WIDTH), const2),
                  pl.BlockSpec((mix, d), const2, pipeline_mode=pl.Buffered(1))],
        out_specs=pl.BlockSpec((b, tile, d), lambda ti: (0, ti, 0)),
        out_shape=jax.ShapeDtypeStruct((b, s, d), F32),
        scratch_shapes=[pltpu.VMEM((b, HGRN_HEADS, HGRN_VAL, HGRN_KEY), F32),
                        pltpu.VMEM((b, POOL_HISTORY, POOL_WIDTH), F32),
                        pltpu.VMEM((d, n_in), BF16),
                        pltpu.VMEM((mix, d), BF16)],
        compiler_params=_params("arbitrary"),
        name="layer0_hgrn_pool",
    )(x, mod, norm_g.reshape(1, d), w_in, hgrn_lb, onorm_g.reshape(1, HGRN_VW),
      pool_w.astype(BF16), pool_scale.reshape(1, POOL_WIDTH), w_out)


def _split3(c):
    hi = c.astype(BF16).astype(F32)
    r = c - hi
    mid = r.astype(BF16).astype(F32)
    lo = (r - mid).astype(BF16).astype(F32)
    return hi, mid, lo


def _carrier_selectors():
    per_tile = LANES // FOX_HEAD_DIM
    sel = np.zeros((FOX_HEADS // per_tile, LANES, per_tile * LANES), np.float32)
    for hd in range(FOX_HEADS):
        base = (hd % per_tile) * LANES + AUG_LANE
        for j in range(3):
            sel[hd // per_tile, j * FOX_HEADS + hd, base + j] = 1.0
            sel[hd // per_tile, j * FOX_HEADS + hd, base + 3 + j] = -1.0
            sel[hd // per_tile, (3 + j) * FOX_HEADS + hd, base + 6 + j] = -1.0
    return jnp.asarray(sel, BF16)


def _layer1_proj_kernel(x_ref, mod_ref, ng_ref, win_ref, bf_ref, qg_ref, kg_ref, bd_ref, sel_ref,
                        qa_ref, ka_ref, vt_ref, g_ref, cq_ref, edge_ref, carry_ref, tile_start_ref,
                        *, tile):
    T, W = tile, FOX_WIDTH
    batch = range(x_ref.shape[0])
    t_idx = pl.program_id(0)
    lane = lax.broadcasted_iota(jnp.int32, (T, LANES), 1)
    row = lax.broadcasted_iota(jnp.int32, (T, LANES), 0)
    bd = bd_ref[...]
    qk_scale = FOX_HEAD_DIM ** -0.5 * LOG2E

    hs = [_modulated_norm(x_ref[bi], mod_ref[bi, 0:1, :], mod_ref[bi, 1:2, :],
                          ng_ref[...]).astype(BF16) for bi in batch]
    projs, feats = [], []
    for bi in batch:
        projs.append(jnp.dot(hs[bi], win_ref[...], preferred_element_type=F32))
        q = projs[bi][:, 0:W]
        k = projs[bi][:, W:2 * W]
        vt_ref[bi, 0] = projs[bi][:, 2 * W:3 * W].T.astype(BF16)
        g_ref[bi] = projs[bi][:, 3 * W:4 * W]
        pairs = []
        for pair in range(FOX_HEADS // 2):
            cs = slice(pair * LANES, (pair + 1) * LANES)
            qp = q[:, cs]
            kp = k[:, cs]
            squares = jnp.concatenate([(qp * qp).astype(BF16), (kp * kp).astype(BF16)], axis=1)
            ms = jnp.dot(squares, bd, preferred_element_type=F32)
            pairs.append((qp * lax.rsqrt(ms[:, :LANES] + EPS) * (qg_ref[...] * qk_scale),
                          kp * lax.rsqrt(ms[:, LANES:] + EPS) * kg_ref[...]))
        feats.append(pairs)

    ones_q = jnp.where((lane >= AUG_LANE + 3) & (lane < AUG_LANE + 9), 1.0, 0.0)
    ones_k = jnp.where((lane >= AUG_LANE) & (lane < AUG_LANE + 3), 1.0, 0.0)
    for bi in batch:
        z = projs[bi][:, 4 * W:4 * W + LANES] + bf_ref[...]
        logf = jnp.minimum(z, 0.0) - jnp.log1p(jnp.exp(-jnp.abs(z)))
        cum = logf
        sh = 1
        while sh < T:
            cum = cum + jnp.where(row >= sh, pltpu.roll(cum, sh, 0), 0.0)
            sh *= 2
        start = jnp.where(t_idx == 0, 0.0, carry_ref[bi])
        end = start + cum[T - 1:T, :]
        carry_ref[bi] = end
        tile_start = jnp.where(t_idx % (ATTN_Q_TILE // T) == 0, start, tile_start_ref[bi])
        tile_start_ref[bi] = tile_start
        edge_ref[bi, 0] = jnp.concatenate([tile_start, end], axis=0) * LOG2E
        s_hi, s_mid, s_lo = _split3((cum + (start - tile_start)) * LOG2E)
        e_hi, e_mid, e_lo = _split3((cum - cum[T - 1:T, :]) * LOG2E)
        total_t = ((cum + start) * LOG2E).T
        for hd in range(FOX_HEADS):
            cq_ref[bi, hd] = total_t[hd:hd + 1, :]

        terms = (s_hi, s_mid, s_lo, e_hi, e_mid, e_lo)
        packed = jnp.zeros((T, LANES), F32)
        for j, term in enumerate(terms):
            moved = term if j == 0 else pltpu.roll(term, j * FOX_HEADS, 1)
            packed = jnp.where((lane >= j * FOX_HEADS) & (lane < (j + 1) * FOX_HEADS), moved,
                               packed)
        packed = packed.astype(BF16)
        for pair in range(FOX_HEADS // 2):
            qn, kn = feats[bi][pair]
            aug = jnp.dot(packed, sel_ref[pair], preferred_element_type=F32)
            for e in range(LANES // FOX_HEAD_DIM):
                hd = (LANES // FOX_HEAD_DIM) * pair + e
                carriers = aug[:, e * LANES:(e + 1) * LANES]
                q_main = qn if e == 0 else pltpu.roll(qn, FOX_HEAD_DIM, 1)
                k_main = kn if e == 0 else pltpu.roll(kn, FOX_HEAD_DIM, 1)
                q_aug = jnp.where(lane < AUG_LANE + 3, carriers, ones_q)
                k_aug = jnp.where(lane < AUG_LANE + 3, ones_k, carriers)
                qa_ref[bi, hd] = jnp.where(lane < FOX_HEAD_DIM, q_main, q_aug).astype(BF16)
                ka_ref[bi, hd] = jnp.where(lane < FOX_HEAD_DIM, k_main, k_aug).astype(BF16)


def _layer1_proj(x, mod, norm_g, w_in, b_f, qnorm_g, knorm_g):
    b, s, d = x.shape
    tile = SEQ_TILE
    nt = s // tile
    w = FOX_WIDTH
    n_pad = w_in.shape[1]
    bf_pad = jnp.zeros((1, LANES), F32).at[0, :FOX_HEADS].set(b_f)
    blk = jnp.arange(2 * LANES) // FOX_HEAD_DIM
    bd = jnp.where(blk[:, None] == blk[None, :], 1.0 / FOX_HEAD_DIM, 0.0).astype(BF16)
    qg = jnp.tile(qnorm_g, LANES // FOX_HEAD_DIM).reshape(1, LANES)
    kg = jnp.tile(knorm_g, LANES // FOX_HEAD_DIM).reshape(1, LANES)
    const2 = lambda ti: (0, 0)
    const3 = lambda ti: (0, 0, 0)
    return pl.pallas_call(
        functools.partial(_layer1_proj_kernel, tile=tile),
        grid=(nt,),
        in_specs=[pl.BlockSpec((b, tile, d), lambda ti: (0, ti, 0)),
                  pl.BlockSpec((b, 3, d), const3),
                  pl.BlockSpec((1, d), const2),
                  pl.BlockSpec((d, n_pad), const2, pipeline_mode=pl.Buffered(1)),
                  pl.BlockSpec((1, LANES), const2),
                  pl.BlockSpec((1, LANES), const2),
                  pl.BlockSpec((1, LANES), const2),
                  pl.BlockSpec((2 * LANES, 2 * LANES), const2),
                  pl.BlockSpec((FOX_HEADS // 2, LANES, 2 * LANES), const3)],
        out_specs=[pl.BlockSpec((b, FOX_HEADS, tile, LANES), lambda ti: (0, 0, ti, 0)),
                   pl.BlockSpec((b, FOX_HEADS, tile, LANES), lambda ti: (0, 0, ti, 0)),
                   pl.BlockSpec((b, 1, w, tile), lambda ti: (0, ti, 0, 0)),
                   pl.BlockSpec((b, tile, w), lambda ti: (0, ti, 0)),
                   pl.BlockSpec((b, FOX_HEADS, 1, tile), lambda ti: (0, 0, 0, ti)),
                   pl.BlockSpec((b, 1, 2, LANES), lambda ti: (0, ti, 0, 0))],
        out_shape=[jax.ShapeDtypeStruct((b, FOX_HEADS, s, LANES), BF16),
                   jax.ShapeDtypeStruct((b, FOX_HEADS, s, LANES), BF16),
                   jax.ShapeDtypeStruct((b, nt, w, tile), BF16),
                   jax.ShapeDtypeStruct((b, s, w), F32),
                   jax.ShapeDtypeStruct((b, FOX_HEADS, 1, s), F32),
                   jax.ShapeDtypeStruct((b, nt, 2, LANES), F32)],
        scratch_shapes=[pltpu.VMEM((b, 1, LANES), F32), pltpu.VMEM((b, 1, LANES), F32)],
        compiler_params=_params("arbitrary"),
        name="layer1_proj",
    )(x, mod, norm_g.reshape(1, d), w_in, bf_pad, qg, kg, bd, _carrier_selectors())


def _query_operands(q):
    slab = 2 * SUBLANES
    row = lax.broadcasted_iota(jnp.int32, (slab, q.shape[0]), 0)
    q_t = q.astype(F32).T
    carriers = q_t[AUG_LANE:AUG_LANE + slab]
    return tuple(jnp.concatenate([q_t[:AUG_LANE], jnp.where(keep, carriers, 0.0),
                                  q_t[AUG_LANE + slab:]], axis=0).astype(BF16)
                 for keep in (row < 6, row >= 6))


def _gated_output(acc_t, g_ref, o_ref, rows):
    o_ref[0, rows] = (acc_t.T * _silu(g_ref[0, rows])).astype(BF16)


def _fox_bounded_kernel(first_ref, end_ref, qa_ref, ka_ref, vt_ref, cq_ref, g_ref, o_ref,
                        p_ref, acc_ref, *, tile, heads):
    TQ, HP = tile, heads
    QT = qa_ref.shape[2] // TQ
    SB = vt_ref.shape[-1]
    R = TQ // SB
    n_blocks = ka_ref.shape[2] // SB
    b = pl.program_id(0)
    group = pl.program_id(1)
    i0 = pl.program_id(2) * QT
    chains = [(qs, e) for qs in range(QT) for e in range(HP)]
    cols = [slice(qs * TQ, (qs + 1) * TQ) for qs in range(QT)]
    operands = {(qs, e): _query_operands(qa_ref[0, e, cols[qs], :]) for qs, e in chains}
    causal = (lax.broadcasted_iota(jnp.int32, (TQ, TQ), 0)
              <= lax.broadcasted_iota(jnp.int32, (TQ, TQ), 1))

    def keys(e, n):
        return ka_ref[0, e, pl.ds(pl.multiple_of(n * TQ, TQ), TQ), :]

    def values_t(e, blk):
        return vt_ref[0, blk, pl.ds(e * FOX_HEAD_DIM, FOX_HEAD_DIM), :]

    def rest(c, blk, gate=1.0):
        qs, e = c
        end = end_ref[(b * n_blocks + blk) * FOX_HEADS + HP * group + e]
        return jnp.exp2(jnp.minimum(cq_ref[0, e][:, cols[qs]] - end, 0.0)) * gate

    def col_partial(p):
        return jnp.sum(p.reshape(p.shape[0] // SUBLANES, SUBLANES, p.shape[1]), axis=0)

    def far_scores(c, n):
        return jnp.dot(keys(c[1], n), operands[c][1], preferred_element_type=F32)

    def contract(c, n, p_bf16, gate=1.0):
        for r in range(R):
            pv = jnp.dot(values_t(c[1], n * R + r), p_bf16[r * SB:(r + 1) * SB],
                         preferred_element_type=F32)
            acc_ref[c] += rest(c, n * R + r, gate) * pv

    def weighted_partial(c, n, p, gate=1.0):
        return sum(rest(c, n * R + r, gate) * col_partial(p[r * SB:(r + 1) * SB])
                   for r in range(R))

    first = first_ref[(b * pl.num_programs(1) + group) * pl.num_programs(2) + pl.program_id(2)]
    gate0 = jnp.where(i0 > 0, 1.0, 0.0)

    l_part = {}

    def own_tile(c):
        qs, e = c
        s_t = jnp.dot(keys(e, i0 + qs), operands[c][0], preferred_element_type=F32)
        p = jnp.where(causal, jnp.exp2(s_t), 0.0)
        l_part[c] = col_partial(p)
        p = p.astype(BF16)

        def later():
            v_t = jnp.concatenate([values_t(e, (i0 + qs) * R + r) for r in range(R)], axis=1)
            acc_ref[c] = jnp.dot(v_t, p, preferred_element_type=F32)
        return later

    def earlier_tile(c, n):
        p = jnp.exp2(far_scores(c, n))
        l_part[c] = l_part[c] + weighted_partial(c, n, p)
        p = p.astype(BF16)
        return lambda: contract(c, n, p)

    def fill(c):
        p = jnp.exp2(far_scores(c, first))
        l_part[c] = l_part[c] + weighted_partial(c, first, p, gate0)
        p_ref[c] = p.astype(BF16)

    jobs = ([functools.partial(own_tile, c) for c in chains]
            + [functools.partial(earlier_tile, c, i0 + t) for c in chains for t in range(c[0])]
            + [functools.partial(fill, c) for c in chains])
    pending = None
    for job in jobs:
        later = job()
        if pending is not None:
            pending()
        pending = later
    if pending is not None:
        pending()

    def body(n, l_run):
        s_new = []
        for c in chains:
            s_new.append(far_scores(c, n))
            contract(c, n - 1, p_ref[c])
        out = []
        for c, s_t, l_c in zip(chains, s_new, l_run):
            p = jnp.exp2(s_t)
            out.append(l_c + weighted_partial(c, n, p))
            p_ref[c] = p.astype(BF16)
        return tuple(out)

    l_run = lax.fori_loop(first + 1, i0, body, tuple(l_part[c] for c in chains))
    for c in chains:
        contract(c, jnp.maximum(i0 - 1, 0), p_ref[c], gate0)
    for qs in range(QT):
        out_t = [acc_ref[qs, e] * (1.0 / jnp.sum(l_run[qs * HP + e], axis=0, keepdims=True))
                 for e in range(HP)]
        _gated_output(jnp.concatenate(out_t, axis=0), g_ref, o_ref, cols[qs])


def _fox_online_kernel(start_ref, qa_ref, ka_ref, vt_ref, g_ref, o_ref, *, tile, heads):
    TB, HP = tile, heads
    n_blocks = ka_ref.shape[2] // TB
    b = pl.program_id(0)
    group = pl.program_id(1)
    i = pl.program_id(2)
    q_t = [_query_operands(qa_ref[0, e])[0] for e in range(HP)]
    k_row = lax.broadcasted_iota(jnp.int32, (TB, TB), 0)
    q_col = lax.broadcasted_iota(jnp.int32, (TB, TB), 1)

    def step(j, carry, masked):
        out = []
        for e in range(HP):
            m_run, l_run, acc = carry[e]
            k_blk = ka_ref[0, e, pl.ds(pl.multiple_of(j * TB, TB), TB), :]
            s_t = jnp.dot(k_blk, q_t[e], preferred_element_type=F32)
            if masked:
                s_t = jnp.where(k_row <= q_col, s_t, MASK_VALUE)
            hd = HP * group + e
            delta = (start_ref[(b * n_blocks + i) * FOX_HEADS + hd]
                     - start_ref[(b * n_blocks + j) * FOX_HEADS + hd])
            m_new = jnp.maximum(m_run, jnp.max(s_t, axis=0, keepdims=True) + delta)
            p_t = jnp.exp2(s_t - (m_new - delta))
            alpha = jnp.exp2(m_run - m_new)
            l_new = alpha * l_run + jnp.sum(p_t, axis=0, keepdims=True)
            v_t = vt_ref[0, j, pl.ds(e * FOX_HEAD_DIM, FOX_HEAD_DIM), :]
            acc_new = alpha * acc + jnp.dot(v_t, p_t.astype(BF16), preferred_element_type=F32)
            out.append((m_new, l_new, acc_new))
        return tuple(out)

    init = (jnp.full((1, TB), MASK_VALUE, F32), jnp.zeros((1, TB), F32),
            jnp.zeros((FOX_HEAD_DIM, TB), F32))
    carry = lax.fori_loop(0, i, lambda j, c: step(j, c, False), (init,) * HP)
    carry = step(i, carry, True)
    _gated_output(jnp.concatenate([acc / l_run for _, l_run, acc in carry], axis=0), g_ref, o_ref,
                  slice(None))


def _fox_attention(qa, ka, vt, g, cq, edges, *, bounded):
    b, n_heads, s, _ = qa.shape
    _, nvb, w, vb = vt.shape
    if bounded:
        tile, hp, qt = ATTN_Q_TILE, ATTN_HEADS, ATTN_TILES_PER_STEP
    else:
        tile, hp, qt = vb, LANES // FOX_HEAD_DIM, 1
    gw = hp * FOX_HEAD_DIM
    rows = qt * tile
    specs = dict(
        table=pl.BlockSpec(memory_space=pltpu.SMEM),
        q=pl.BlockSpec((1, hp, rows, LANES), lambda bi, p, i: (bi, p, i, 0)),
        k=pl.BlockSpec((1, hp, s, LANES), lambda bi, p, i: (bi, p, 0, 0)),
        v=pl.BlockSpec((1, nvb, gw, vb), lambda bi, p, i: (bi, 0, p, 0)),
        cq=pl.BlockSpec((1, hp, 1, rows), lambda bi, p, i: (bi, p, 0, i)),
        g=pl.BlockSpec((1, rows, gw), lambda bi, p, i: (bi, i, p)))
    if bounded:
        body = functools.partial(_fox_bounded_kernel, tile=tile, heads=hp)
        names = ("table", "table", "q", "k", "v", "cq", "g")
        step_start = edges[:, ::rows // vb, 0, :FOX_HEADS]
        tile_end = edges[:, tile // vb - 1::tile // vb, 1, :FOX_HEADS]
        dead = (step_start[:, :, None] - tile_end[:, None]) < -FOX_DEAD_EXPONENT
        before = jnp.arange(s // rows)[:, None] * qt > jnp.arange(s // tile)[None, :]
        first_live = jnp.sum(dead & before[None, :, :, None], axis=2)
        last_before = jnp.maximum(jnp.arange(s // rows) * qt - 1, 0)
        first_live = jnp.minimum(first_live, last_before[None, :, None])
        first_live = jnp.min(first_live.reshape(b, s // rows, n_heads // hp, hp), axis=-1)
        args = (first_live.transpose(0, 2, 1).reshape(-1).astype(jnp.int32),
                edges[:, :, 1, :FOX_HEADS].reshape(-1), qa, ka, vt, cq, g)
        scratch = [pltpu.VMEM((qt, hp, tile, tile), BF16),
                   pltpu.VMEM((qt, hp, FOX_HEAD_DIM, tile), F32)]
    else:
        body = functools.partial(_fox_online_kernel, tile=tile, heads=hp)
        names = ("table", "q", "k", "v", "g")
        args = (edges[:, :, 0, :FOX_HEADS].reshape(-1), qa, ka, vt, g)
        scratch = []
    return pl.pallas_call(
        body,
        grid=(b, n_heads // hp, s // rows),
        in_specs=[specs[n] for n in names],
        out_specs=specs["g"],
        out_shape=jax.ShapeDtypeStruct((b, s, w), BF16),
        scratch_shapes=scratch,
        compiler_params=_params("arbitrary", "arbitrary", "arbitrary"),
        name="fox_attention_bounded" if bounded else "fox_attention_online",
    )(*args)


def _out_proj_kernel(a_ref, x_ref, mod_ref, w_ref, o_ref):
    y = jnp.dot(a_ref[0], w_ref[...], preferred_element_type=F32)
    o_ref[0] = x_ref[0] + mod_ref[0, 2:3, :] * y


def _out_proj(a, x, mod, w_out):
    b, s, d = x.shape
    tile = OUT_PROJ_TILE
    k = a.shape[-1]
    return pl.pallas_call(
        _out_proj_kernel,
        grid=(b, s // tile),
        in_specs=[pl.BlockSpec((1, tile, k), lambda bi, ti: (bi, ti, 0)),
                  pl.BlockSpec((1, tile, d), lambda bi, ti: (bi, ti, 0)),
                  pl.BlockSpec((1, 3, d), lambda bi, ti: (bi, 0, 0)),
                  pl.BlockSpec((k, d), lambda bi, ti: (0, 0))],
        out_specs=pl.BlockSpec((1, tile, d), lambda bi, ti: (bi, ti, 0)),
        out_shape=jax.ShapeDtypeStruct((b, s, d), F32),
        compiler_params=_params("arbitrary", "arbitrary"),
        name="layer1_out_proj",
    )(a, x, mod, w_out.astype(BF16))


def _layer1(x, mod, norm_g, w_in, b_f, qnorm_g, knorm_g, w_out):
    d, w = x.shape[-1], FOX_WIDTH
    order = jnp.argsort(b_f)
    w_in = w_in.astype(BF16)
    w_heads = w_in[:, :4 * w].reshape(d, 4, FOX_HEADS, FOX_HEAD_DIM)[:, :, order]
    w_forget = jnp.zeros((d, LANES), BF16).at[:, :FOX_HEADS].set(
        w_in[:, 4 * w:][:, order])
    w_in = jnp.concatenate([w_heads.reshape(d, 4 * w), w_forget], axis=1)
    w_out = w_out.astype(BF16).reshape(FOX_HEADS, FOX_HEAD_DIM, d)[order].reshape(w, d)
    qa, ka, vt, g, cq, edges = _layer1_proj(x, mod, norm_g, w_in, b_f[order], qnorm_g, knorm_g)
    score_bound = (LOG2E * FOX_HEAD_DIM ** 0.5
                   * jnp.max(jnp.abs(qnorm_g)) * jnp.max(jnp.abs(knorm_g)))
    gated = lax.cond(score_bound <= FOX_SAFE_LOGIT,
                     functools.partial(_fox_attention, bounded=True),
                     functools.partial(_fox_attention, bounded=False),
                     qa, ka, vt, g, cq, edges)
    return _out_proj(gated, x, mod, w_out)


def kernel(x, c, norm_g, ada_w, ada_b, hgrn_lb, even_w_in, hgrn_onorm_g, pool_w, pool_scale,
           even_w_out, odd_w_in, fox_b_f, fox_qnorm_g, fox_knorm_g, odd_w_out):
    depth = norm_g.shape[0]
    mods = _adaln_mods(c, ada_w, ada_b)
    for l in range(depth):
        j = l // 2
        if l % 2 == 0:
            x = _layer0(x, mods[l], norm_g[l], even_w_in[j], hgrn_lb, hgrn_onorm_g[j],
                        pool_w[j], pool_scale[j], even_w_out[j], layer_slot=l)
        else:
            x = _layer1(x, mods[l], norm_g[l], odd_w_in[j], fox_b_f[j], fox_qnorm_g[j],
                        fox_knorm_g[j], odd_w_out[j])
    return x
```

```python
import functools

import jax
import jax.numpy as jnp
import numpy as np
from jax import lax
from jax.experimental import pallas as pl
from jax.experimental.pallas import tpu as pltpu

F32 = jnp.float32
BF16 = jnp.bfloat16
EPS = 1e-6

HGRN_HEADS = 4
HGRN_KEY = 128
HGRN_VAL = 128
HGRN_KW = HGRN_HEADS * HGRN_KEY
HGRN_VW = HGRN_HEADS * HGRN_VAL
POOL_WINDOWS = (2, 4, 8, 16)
POOL_GROUP = 128
POOL_WIDTH = POOL_GROUP * len(POOL_WINDOWS)
POOL_HISTORY = 16
FOX_HEADS = 16
FOX_HEAD_DIM = 64
FOX_WIDTH = FOX_HEADS * FOX_HEAD_DIM

LANES = 128
SUBLANES = 8
VMEM_LIMIT_BYTES = 56 * 1024 * 1024

SEQ_TILE = 256
OUT_PROJ_TILE = 1024
MOD_READ_STREAMS = 4
HGRN_CHUNK = 128
ATTN_Q_TILE = 512
ATTN_HEADS = 4
ATTN_TILES_PER_STEP = 2
FOX_SAFE_LOGIT = 100.0
FOX_DEAD_EXPONENT = 150.0
LOG2E = 1.4426950408889634
MASK_VALUE = -1e30
AUG_LANE = FOX_HEAD_DIM


def _silu(x):
    return x * jax.nn.sigmoid(x)


def _params(*semantics):
    return pltpu.CompilerParams(dimension_semantics=semantics, vmem_limit_bytes=VMEM_LIMIT_BYTES)


def _mod_kernel(c_ref, *refs):
    *w_refs, b_ref, o_ref, cond_ref = refs

    @pl.when((pl.program_id(0) == 0) & (pl.program_id(1) == 0))
    def _():
        cond_ref[...] = _silu(c_ref[...])

    band = w_refs[0].shape[1]
    for bi in range(c_ref.shape[0]):
        acc = b_ref[0]
        for r, w_ref in enumerate(w_refs):
            w = w_ref[0]
            cond = cond_ref[bi, r * band:(r + 1) * band]
            acc = acc + jnp.concatenate(
                [jnp.sum(w[:, j:j + LANES] * cond, axis=0, keepdims=True)
                 for j in range(0, w.shape[1], LANES)], axis=1)
        o_ref[0, bi:bi + 1, :] = acc


def _adaln_mods(c, ada_w, ada_b):
    depth, d, n = ada_w.shape
    b = c.shape[0]
    tn, bands = 1024, MOD_READ_STREAMS
    band_spec = lambda r: pl.BlockSpec((1, d // bands, tn), lambda l, j: (l, r, j))
    out = pl.pallas_call(
        _mod_kernel,
        grid=(depth, n // tn),
        in_specs=[pl.BlockSpec((b, d, LANES), lambda l, j: (0, 0, 0))]
                 + [band_spec(r) for r in range(bands)]
                 + [pl.BlockSpec((1, 1, tn), lambda l, j: (l, 0, j))],
        out_specs=pl.BlockSpec((1, b, tn), lambda l, j: (l, 0, j)),
        out_shape=jax.ShapeDtypeStruct((depth, b, n), F32),
        scratch_shapes=[pltpu.VMEM((b, d, LANES), F32)],
        compiler_params=_params("arbitrary", "arbitrary"),
        name="adaln_mods",
    )(jnp.broadcast_to(c[:, :, None], (b, d, LANES)), *([ada_w] * bands),
      ada_b.reshape(depth, 1, n))
    return out.reshape(depth, b, 3, d)


def _modulated_norm(x, shift, scale, gain):
    ms = jnp.mean(x * x, axis=-1, keepdims=True)
    return x * lax.rsqrt(ms + EPS) * gain * (1.0 + scale) + shift


def _layer0_kernel(x_ref, mod_ref, ng_ref, win32_ref, lb_ref, og_ref, pw_ref, ps_ref, wout32_ref,
                   o_ref, state_ref, carry_ref, win_ref, wout_ref, *, tile, chunk, layer_slot):
    T, C, H = tile, chunk, HGRN_HEADS
    batch = range(x_ref.shape[0])
    t_idx = pl.program_id(0)

    @pl.when(t_idx == 0)
    def _():
        state_ref[...] = jnp.zeros_like(state_ref)
        carry_ref[...] = jnp.zeros_like(carry_ref)
        win_ref[...] = win32_ref[...].astype(BF16)
        wout_ref[...] = wout32_ref[...].astype(BF16)

    lbv = lb_ref[...]
    e = jnp.exp(lbv - jnp.max(lbv, axis=0, keepdims=True))
    lower = (jnp.sum(e[0:layer_slot + 1], axis=0, keepdims=True)
             / jnp.sum(e, axis=0, keepdims=True))

    xs, parts = [], []
    for bi in batch:
        x = x_ref[bi]
        h = _modulated_norm(x, mod_ref[bi, 0:1, :], mod_ref[bi, 1:2, :], ng_ref[...])
        proj = jnp.dot(h.astype(BF16), win_ref[...], preferred_element_type=F32)
        o0, widths, piece = 0, (HGRN_KW, HGRN_KW, HGRN_VW, HGRN_VW, POOL_WIDTH, POOL_WIDTH), []
        for wd in widths:
            piece.append(proj[:, o0:o0 + wd])
            o0 += wd
        xs.append(x)
        parts.append(piece)

    row = lax.broadcasted_iota(jnp.int32, (T, HGRN_KW), 0)
    ti = lax.broadcasted_iota(jnp.int32, (C, C), 0)
    si = lax.broadcasted_iota(jnp.int32, (C, C), 1)
    n_chunks = T // C
    nt_dims = (((1,), (1,)), ((), ()))
    tn_dims = (((0,), (0,)), ((), ()))

    def level_scores(qe, ke, mask, scores):
        qb = qe.astype(BF16)
        kb = ke.astype(BF16)
        out = []
        for c in range(n_chunks):
            for hh in range(H):
                rs = slice(c * C, (c + 1) * C)
                cs = slice(hh * HGRN_KEY, (hh + 1) * HGRN_KEY)
                d = lax.dot_general(qb[rs, cs], kb[rs, cs], nt_dims, preferred_element_type=F32)
                prev = scores[c * H + hh]
                out.append(jnp.where(mask, d, 0.0 if prev is None else prev))
        return out

    scores, q_dec, k_dec, total = [], [], [], []
    for bi in batch:
        q, f = parts[bi][0], parts[bi][1]
        forget = lower + (1.0 - lower) * jax.nn.sigmoid(f)
        key = 1.0 - forget
        scores.append(level_scores(q, key, ti == si, [None] * (n_chunks * H)))
        q_dec.append(q * forget)
        k_dec.append(key)
        total.append(forget)
    m = 1
    while m < C:
        mask = ((ti ^ si) < 2 * m) & ((ti & m) != 0) & ((si & m) == 0)
        upper = (row & m) != 0
        for bi in batch:
            scores[bi] = level_scores(q_dec[bi], k_dec[bi], mask, scores[bi])
            t_dn = pltpu.roll(total[bi], m, 0)
            t_up = pltpu.roll(total[bi], T - m, 0)
            q_dec[bi] = q_dec[bi] * jnp.where(upper, t_dn, 1.0)
            k_dec[bi] = k_dec[bi] * jnp.where(upper, 1.0, t_up)
            total[bi] = total[bi] * jnp.where(upper, t_dn, t_up)
        m *= 2

    o_a = []
    for bi in batch:
        val, g_a = parts[bi][2], parts[bi][3]
        q_in = q_dec[bi].astype(BF16)
        k_out = k_dec[bi].astype(BF16)
        decay = total[bi]
        val_b = val.astype(BF16)
        oa_rows = []
        for c in range(n_chunks):
            rs = slice(c * C, (c + 1) * C)
            heads_out = []
            for hh in range(H):
                cs = slice(hh * HGRN_KEY, (hh + 1) * HGRN_KEY)
                st = state_ref[bi, hh]
                vb = val_b[rs, cs]
                o = jnp.dot(scores[bi][c * H + hh].astype(BF16), vb, preferred_element_type=F32)
                o = o + lax.dot_general(q_in[rs, cs], st.astype(BF16), nt_dims,
                                        preferred_element_type=F32)
                state_ref[bi, hh] = (st * decay[c * C:c * C + 1, cs]
                                     + lax.dot_general(vb, k_out[rs, cs], tn_dims,
                                                       preferred_element_type=F32))
                ms_o = jnp.mean(o * o, axis=-1, keepdims=True)
                heads_out.append(o * lax.rsqrt(ms_o + EPS))
            oa_rows.append(jnp.concatenate(heads_out, axis=1))
        o_a.append(jnp.concatenate(oa_rows, axis=0) * og_ref[...] * _silu(g_a))

    pos = lax.broadcasted_iota(jnp.int32, (T, POOL_GROUP), 0) + t_idx * T + 1
    for bi in batch:
        u, g_b = parts[bi][4], parts[bi][5]
        ub = jnp.concatenate([carry_ref[bi], u], axis=0)
        carry_ref[bi] = u[T - POOL_HISTORY:T, :]
        wins = []
        acc = ub
        sh = 1
        while sh < max(POOL_WINDOWS):
            acc = acc + pltpu.roll(acc, sh, 0)
            sh *= 2
            wins.append(acc)
        ob = []
        for gi, w in enumerate(POOL_WINDOWS):
            cs = slice(gi * POOL_GROUP, (gi + 1) * POOL_GROUP)
            win = wins[w.bit_length() - 2][POOL_HISTORY:, cs]
            cnt = jnp.minimum(pos, w).astype(F32)
            pooled = win / cnt - u[:, cs]
            ob.append(jnp.dot(pooled.astype(BF16), pw_ref[gi], preferred_element_type=F32))
        o_b = jnp.concatenate(ob, axis=1) * ps_ref[...] * _silu(g_b)
        mixed = jnp.concatenate([o_a[bi], o_b], axis=1).astype(BF16)
        y = jnp.dot(mixed, wout_ref[...], preferred_element_type=F32)
        o_ref[bi] = xs[bi] + mod_ref[bi, 2:3, :] * y


def _layer0(x, mod, norm_g, w_in, hgrn_lb, onorm_g, pool_w, pool_scale, w_out, layer_slot):
    b, s, d = x.shape
    tile = SEQ_TILE
    n_in = w_in.shape[1]
    mix = w_out.shape[0]
    const2 = lambda ti: (0, 0)
    const3 = lambda ti: (0, 0, 0)
    return pl.pallas_call(
        functools.partial(_layer0_kernel, tile=tile, chunk=HGRN_CHUNK, layer_slot=layer_slot),
        grid=(s // tile,),
        in_specs=[pl.BlockSpec((b, tile, d), lambda ti: (0, ti, 0)),
                  pl.BlockSpec((b, 3, d), const3),
                  pl.BlockSpec((1, d), const2),
                  pl.BlockSpec((d, n_in), const2, pipeline_mode=pl.Buffered(1)),
                  pl.BlockSpec(hgrn_lb.shape, const2),
                  pl.BlockSpec((1, HGRN_VW), const2),
                  pl.BlockSpec(pool_w.shape, const3),
                  pl.BlockSpec((1, POOL_WIDTH), const2),
                  pl.BlockSpec((mix, d), const2, pipeline_mode=pl.Buffered(1))],
        out_specs=pl.BlockSpec((b, tile, d), lambda ti: (0, ti, 0)),
        out_shape=jax.ShapeDtypeStruct((b, s, d), F32),
        scratch_shapes=[pltpu.VMEM((b, HGRN_HEADS, HGRN_VAL, HGRN_KEY), F32),
                        pltpu.VMEM((b, POOL_HISTORY, POOL_WIDTH), F32),
                        pltpu.VMEM((d, n_in), BF16),
                        pltpu.VMEM((mix, d), BF16)],
        compiler_params=_params("arbitrary"),
        name="layer0_hgrn_pool",
    )(x, mod, norm_g.reshape(1, d), w_in, hgrn_lb, onorm_g.reshape(1, HGRN_VW),
      pool_w.astype(BF16), pool_scale.reshape(1, POOL_WIDTH), w_out)


def _split3(c):
    hi = c.astype(BF16).astype(F32)
    r = c - hi
    mid = r.astype(BF16).astype(F32)
    lo = (r - mid).astype(BF16).astype(F32)
    return hi, mid, lo


def _carrier_selectors():
    per_tile = LANES // FOX_HEAD_DIM
    sel = np.zeros((FOX_HEADS // per_tile, LANES, per_tile * LANES), np.float32)
    for hd in range(FOX_HEADS):
        base = (hd % per_tile) * LANES + AUG_LANE
        for j in range(3):
            sel[hd // per_tile, j * FOX_HEADS + hd, base + j] = 1.0
            sel[hd // per_tile, j * FOX_HEADS + hd, base + 3 + j] = -1.0
            sel[hd // per_tile, (3 + j) * FOX_HEADS + hd, base + 6 + j] = -1.0
    return jnp.asarray(sel, BF16)


def _layer1_proj_kernel(x_ref, mod_ref, ng_ref, win_ref, bf_ref, qg_ref, kg_ref, bd_ref, sel_ref,
                        qa_ref, ka_ref, vt_ref, g_ref, cq_ref, edge_ref, carry_ref, tile_start_ref,
                        *, tile):
    T, W = tile, FOX_WIDTH
    batch = range(x_ref.shape[0])
    t_idx = pl.program_id(0)
    lane = lax.broadcasted_iota(jnp.int32, (T, LANES), 1)
    row = lax.broadcasted_iota(jnp.int32, (T, LANES), 0)
    bd = bd_ref[...]
    qk_scale = FOX_HEAD_DIM ** -0.5 * LOG2E

    hs = [_modulated_norm(x_ref[bi], mod_ref[bi, 0:1, :], mod_ref[bi, 1:2, :],
                          ng_ref[...]).astype(BF16) for bi in batch]
    projs, feats = [], []
    for bi in batch:
        projs.append(jnp.dot(hs[bi], win_ref[...], preferred_element_type=F32))
        q = projs[bi][:, 0:W]
        k = projs[bi][:, W:2 * W]
        vt_ref[bi, 0] = projs[bi][:, 2 * W:3 * W].T.astype(BF16)
        g_ref[bi] = projs[bi][:, 3 * W:4 * W]
        pairs = []
        for pair in range(FOX_HEADS // 2):
            cs = slice(pair * LANES, (pair + 1) * LANES)
            qp = q[:, cs]
            kp = k[:, cs]
            squares = jnp.concatenate([(qp * qp).astype(BF16), (kp * kp).astype(BF16)], axis=1)
            ms = jnp.dot(squares, bd, preferred_element_type=F32)
            pairs.append((qp * lax.rsqrt(ms[:, :LANES] + EPS) * (qg_ref[...] * qk_scale),
                          kp * lax.rsqrt(ms[:, LANES:] + EPS) * kg_ref[...]))
        feats.append(pairs)

    ones_q = jnp.where((lane >= AUG_LANE + 3) & (lane < AUG_LANE + 9), 1.0, 0.0)
    ones_k = jnp.where((lane >= AUG_LANE) & (lane < AUG_LANE + 3), 1.0, 0.0)
    for bi in batch:
        z = projs[bi][:, 4 * W:4 * W + LANES] + bf_ref[...]
        logf = jnp.minimum(z, 0.0) - jnp.log1p(jnp.exp(-jnp.abs(z)))
        cum = logf
        sh = 1
        while sh < T:
            cum = cum + jnp.where(row >= sh, pltpu.roll(cum, sh, 0), 0.0)
            sh *= 2
        start = jnp.where(t_idx == 0, 0.0, carry_ref[bi])
        end = start + cum[T - 1:T, :]
        carry_ref[bi] = end
        tile_start = jnp.where(t_idx % (ATTN_Q_TILE // T) == 0, start, tile_start_ref[bi])
        tile_start_ref[bi] = tile_start
        edge_ref[bi, 0] = jnp.concatenate([tile_start, end], axis=0) * LOG2E
        s_hi, s_mid, s_lo = _split3((cum + (start - tile_start)) * LOG2E)
        e_hi, e_mid, e_lo = _split3((cum - cum[T - 1:T, :]) * LOG2E)
        total_t = ((cum + start) * LOG2E).T
        for hd in range(FOX_HEADS):
            cq_ref[bi, hd] = total_t[hd:hd + 1, :]

        terms = (s_hi, s_mid, s_lo, e_hi, e_mid, e_lo)
        packed = jnp.zeros((T, LANES), F32)
        for j, term in enumerate(terms):
            moved = term if j == 0 else pltpu.roll(term, j * FOX_HEADS, 1)
            packed = jnp.where((lane >= j * FOX_HEADS) & (lane < (j + 1) * FOX_HEADS), moved,
                               packed)
        packed = packed.astype(BF16)
        for pair in range(FOX_HEADS // 2):
            qn, kn = feats[bi][pair]
            aug = jnp.dot(packed, sel_ref[pair], preferred_element_type=F32)
            for e in range(LANES // FOX_HEAD_DIM):
                hd = (LANES // FOX_HEAD_DIM) * pair + e
                carriers = aug[:, e * LANES:(e + 1) * LANES]
                q_main = qn if e == 0 else pltpu.roll(qn, FOX_HEAD_DIM, 1)
                k_main = kn if e == 0 else pltpu.roll(kn, FOX_HEAD_DIM, 1)
                q_aug = jnp.where(lane < AUG_LANE + 3, carriers, ones_q)
                k_aug = jnp.where(lane < AUG_LANE + 3, ones_k, carriers)
                qa_ref[bi, hd] = jnp.where(lane < FOX_HEAD_DIM, q_main, q_aug).astype(BF16)
                ka_ref[bi, hd] = jnp.where(lane < FOX_HEAD_DIM, k_main, k_aug).astype(BF16)


def _layer1_proj(x, mod, norm_g, w_in, b_f, qnorm_g, knorm_g):
    b, s, d = x.shape
    tile = SEQ_TILE
    nt = s // tile
    w = FOX_WIDTH
    n_pad = w_in.shape[1]
    bf_pad = jnp.zeros((1, LANES), F32).at[0, :FOX_HEADS].set(b_f)
    blk = jnp.arange(2 * LANES) // FOX_HEAD_DIM
    bd = jnp.where(blk[:, None] == blk[None, :], 1.0 / FOX_HEAD_DIM, 0.0).astype(BF16)
    qg = jnp.tile(qnorm_g, LANES // FOX_HEAD_DIM).reshape(1, LANES)
    kg = jnp.tile(knorm_g, LANES // FOX_HEAD_DIM).reshape(1, LANES)
    const2 = lambda ti: (0, 0)
    const3 = lambda ti: (0, 0, 0)
    return pl.pallas_call(
        functools.partial(_layer1_proj_kernel, tile=tile),
        grid=(nt,),
        in_specs=[pl.BlockSpec((b, tile, d), lambda ti: (0, ti, 0)),
                  pl.BlockSpec((b, 3, d), const3),
                  pl.BlockSpec((1, d), const2),
                  pl.BlockSpec((d, n_pad), const2, pipeline_mode=pl.Buffered(1)),
                  pl.BlockSpec((1, LANES), const2),
                  pl.BlockSpec((1, LANES), const2),
                  pl.BlockSpec((1, LANES), const2),
                  pl.BlockSpec((2 * LANES, 2 * LANES), const2),
                  pl.BlockSpec((FOX_HEADS // 2, LANES, 2 * LANES), const3)],
        out_specs=[pl.BlockSpec((b, FOX_HEADS, tile, LANES), lambda ti: (0, 0, ti, 0)),
                   pl.BlockSpec((b, FOX_HEADS, tile, LANES), lambda ti: (0, 0, ti, 0)),
                   pl.BlockSpec((b, 1, w, tile), lambda ti: (0, ti, 0, 0)),
                   pl.BlockSpec((b, tile, w), lambda ti: (0, ti, 0)),
                   pl.BlockSpec((b, FOX_HEADS, 1, tile), lambda ti: (0, 0, 0, ti)),
                   pl.BlockSpec((b, 1, 2, LANES), lambda ti: (0, ti, 0, 0))],
        out_shape=[jax.ShapeDtypeStruct((b, FOX_HEADS, s, LANES), BF16),
                   jax.ShapeDtypeStruct((b, FOX_HEADS, s, LANES), BF16),
                   jax.ShapeDtypeStruct((b, nt, w, tile), BF16),
                   jax.ShapeDtypeStruct((b, s, w), F32),
                   jax.ShapeDtypeStruct((b, FOX_HEADS, 1, s), F32),
                   jax.ShapeDtypeStruct((b, nt, 2, LANES), F32)],
        scratch_shapes=[pltpu.VMEM((b, 1, LANES), F32), pltpu.VMEM((b, 1, LANES), F32)],
        compiler_params=_params("arbitrary"),
        name="layer1_proj",
    )(x, mod, norm_g.reshape(1, d), w_in, bf_pad, qg, kg, bd, _carrier_selectors())


def _query_operands(q):
    slab = 2 * SUBLANES
    row = lax.broadcasted_iota(jnp.int32, (slab, q.shape[0]), 0)
    q_t = q.astype(F32).T
    carriers = q_t[AUG_LANE:AUG_LANE + slab]
    return tuple(jnp.concatenate([q_t[:AUG_LANE], jnp.where(keep, carriers, 0.0),
                                  q_t[AUG_LANE + slab:]], axis=0).astype(BF16)
                 for keep in (row < 6, row >= 6))


def _gated_output(acc_t, g_ref, o_ref, rows):
    o_ref[0, rows] = (acc_t.T * _silu(g_ref[0, rows])).astype(BF16)


def _fox_bounded_kernel(first_ref, end_ref, qa_ref, ka_ref, vt_ref, cq_ref, g_ref, o_ref,
                        p_ref, acc_ref, *, tile, heads):
    TQ, HP = tile, heads
    QT = qa_ref.shape[2] // TQ
    SB = vt_ref.shape[-1]
    R = TQ // SB
    n_blocks = ka_ref.shape[2] // SB
    b = pl.program_id(0)
    group = pl.program_id(1)
    i0 = pl.program_id(2) * QT
    chains = [(qs, e) for qs in range(QT) for e in range(HP)]
    cols = [slice(qs * TQ, (qs + 1) * TQ) for qs in range(QT)]
    operands = {(qs, e): _query_operands(qa_ref[0, e, cols[qs], :]) for qs, e in chains}
    causal = (lax.broadcasted_iota(jnp.int32, (TQ, TQ), 0)
              <= lax.broadcasted_iota(jnp.int32, (TQ, TQ), 1))

    def keys(e, n):
        return ka_ref[0, e, pl.ds(pl.multiple_of(n * TQ, TQ), TQ), :]

    def values_t(e, blk):
        return vt_ref[0, blk, pl.ds(e * FOX_HEAD_DIM, FOX_HEAD_DIM), :]

    def rest(c, blk, gate=1.0):
        qs, e = c
        end = end_ref[(b * n_blocks + blk) * FOX_HEADS + HP * group + e]
        return jnp.exp2(jnp.minimum(cq_ref[0, e][:, cols[qs]] - end, 0.0)) * gate

    def col_partial(p):
        return jnp.sum(p.reshape(p.shape[0] // SUBLANES, SUBLANES, p.shape[1]), axis=0)

    def far_scores(c, n):
        return jnp.dot(keys(c[1], n), operands[c][1], preferred_element_type=F32)

    def contract(c, n, p_bf16, gate=1.0):
        for r in range(R):
            pv = jnp.dot(values_t(c[1], n * R + r), p_bf16[r * SB:(r + 1) * SB],
                         preferred_element_type=F32)
            acc_ref[c] += rest(c, n * R + r, gate) * pv

    def weighted_partial(c, n, p, gate=1.0):
        return sum(rest(c, n * R + r, gate) * col_partial(p[r * SB:(r + 1) * SB])
                   for r in range(R))

    first = first_ref[(b * pl.num_programs(1) + group) * pl.num_programs(2) + pl.program_id(2)]
    gate0 = jnp.where(i0 > 0, 1.0, 0.0)

    l_part = {}

    def own_tile(c):
        qs, e = c
        s_t = jnp.dot(keys(e, i0 + qs), operands[c][0], preferred_element_type=F32)
        p = jnp.where(causal, jnp.exp2(s_t), 0.0)
        l_part[c] = col_partial(p)
        p = p.astype(BF16)

        def later():
            v_t = jnp.concatenate([values_t(e, (i0 + qs) * R + r) for r in range(R)], axis=1)
            acc_ref[c] = jnp.dot(v_t, p, preferred_element_type=F32)
        return later

    def earlier_tile(c, n):
        p = jnp.exp2(far_scores(c, n))
        l_part[c] = l_part[c] + weighted_partial(c, n, p)
        p = p.astype(BF16)
        return lambda: contract(c, n, p)

    def fill(c):
        p = jnp.exp2(far_scores(c, first))
        l_part[c] = l_part[c] + weighted_partial(c, first, p, gate0)
        p_ref[c] = p.astype(BF16)

    jobs = ([functools.partial(own_tile, c) for c in chains]
            + [functools.partial(earlier_tile, c, i0 + t) for c in chains for t in range(c[0])]
            + [functools.partial(fill, c) for c in chains])
    pending = None
    for job in jobs:
        later = job()
        if pending is not None:
            pending()
        pending = later
    if pending is not None:
        pending()

    def body(n, l_run):
        s_new = []
        for c in chains:
            s_new.append(far_scores(c, n))
            contract(c, n - 1, p_ref[c])
        out = []
        for c, s_t, l_c in zip(chains, s_new, l_run):
            p = jnp.exp2(s_t)
            out.append(l_c + weighted_partial(c, n, p))
            p_ref[c] = p.astype(BF16)
        return tuple(out)

    l_run = lax.fori_loop(first + 1, i0, body, tuple(l_part[c] for c in chains))
    for c in chains:
        contract(c, jnp.maximum(i0 - 1, 0), p_ref[c], gate0)
    for qs in range(QT):
        out_t = [acc_ref[qs, e] * (1.0 / jnp.sum(l_run[qs * HP + e], axis=0, keepdims=True))
                 for e in range(HP)]
        _gated_output(jnp.concatenate(out_t, axis=0), g_ref, o_ref, cols[qs])


def _fox_online_kernel(start_ref, qa_ref, ka_ref, vt_ref, g_ref, o_ref, *, tile, heads):
    TB, HP = tile, heads
    n_blocks = ka_ref.shape[2] // TB
    b = pl.program_id(0)
    group = pl.program_id(1)
    i = pl.program_id(2)
    q_t = [_query_operands(qa_ref[0, e])[0] for e in range(HP)]
    k_row = lax.broadcasted_iota(jnp.int32, (TB, TB), 0)
    q_col = lax.broadcasted_iota(jnp.int32, (TB, TB), 1)

    def step(j, carry, masked):
        out = []
        for e in range(HP):
            m_run, l_run, acc = carry[e]
            k_blk = ka_ref[0, e, pl.ds(pl.multiple_of(j * TB, TB), TB), :]
            s_t = jnp.dot(k_blk, q_t[e], preferred_element_type=F32)
            if masked:
                s_t = jnp.where(k_row <= q_col, s_t, MASK_VALUE)
            hd = HP * group + e
            delta = (start_ref[(b * n_blocks + i) * FOX_HEADS + hd]
                     - start_ref[(b * n_blocks + j) * FOX_HEADS + hd])
            m_new = jnp.maximum(m_run, jnp.max(s_t, axis=0, keepdims=True) + delta)
            p_t = jnp.exp2(s_t - (m_new - delta))
            alpha = jnp.exp2(m_run - m_new)
            l_new = alpha * l_run + jnp.sum(p_t, axis=0, keepdims=True)
            v_t = vt_ref[0, j, pl.ds(e * FOX_HEAD_DIM, FOX_HEAD_DIM), :]
            acc_new = alpha * acc + jnp.dot(v_t, p_t.astype(BF16), preferred_element_type=F32)
            out.append((m_new, l_new, acc_new))
        return tuple(out)

    init = (jnp.full((1, TB), MASK_VALUE, F32), jnp.zeros((1, TB), F32),
            jnp.zeros((FOX_HEAD_DIM, TB), F32))
    carry = lax.fori_loop(0, i, lambda j, c: step(j, c, False), (init,) * HP)
    carry = step(i, carry, True)
    _gated_output(jnp.concatenate([acc / l_run for _, l_run, acc in carry], axis=0), g_ref, o_ref,
                  slice(None))


def _fox_attention(qa, ka, vt, g, cq, edges, *, bounded):
    b, n_heads, s, _ = qa.shape
    _, nvb, w, vb = vt.shape
    if bounded:
        tile, hp, qt = ATTN_Q_TILE, ATTN_HEADS, ATTN_TILES_PER_STEP
    else:
        tile, hp, qt = vb, LANES // FOX_HEAD_DIM, 1
    gw = hp * FOX_HEAD_DIM
    rows = qt * tile
    specs = dict(
        table=pl.BlockSpec(memory_space=pltpu.SMEM),
        q=pl.BlockSpec((1, hp, rows, LANES), lambda bi, p, i: (bi, p, i, 0)),
        k=pl.BlockSpec((1, hp, s, LANES), lambda bi, p, i: (bi, p, 0, 0)),
        v=pl.BlockSpec((1, nvb, gw, vb), lambda bi, p, i: (bi, 0, p, 0)),
        cq=pl.BlockSpec((1, hp, 1, rows), lambda bi, p, i: (bi, p, 0, i)),
        g=pl.BlockSpec((1, rows, gw), lambda bi, p, i: (bi, i, p)))
    if bounded:
        body = functools.partial(_fox_bounded_kernel, tile=tile, heads=hp)
        names = ("table", "table", "q", "k", "v", "cq", "g")
        step_start = edges[:, ::rows // vb, 0, :FOX_HEADS]
        tile_end = edges[:, tile // vb - 1::tile // vb, 1, :FOX_HEADS]
        dead = (step_start[:, :, None] - tile_end[:, None]) < -FOX_DEAD_EXPONENT
        before = jnp.arange(s // rows)[:, None] * qt > jnp.arange(s // tile)[None, :]
        first_live = jnp.sum(dead & before[None, :, :, None], axis=2)
        last_before = jnp.maximum(jnp.arange(s // rows) * qt - 1, 0)
        first_live = jnp.minimum(first_live, last_before[None, :, None])
        first_live = jnp.min(first_live.reshape(b, s // rows, n_heads // hp, hp), axis=-1)
        args = (first_live.transpose(0, 2, 1).reshape(-1).astype(jnp.int32),
                edges[:, :, 1, :FOX_HEADS].reshape(-1), qa, ka, vt, cq, g)
        scratch = [pltpu.VMEM((qt, hp, tile, tile), BF16),
                   pltpu.VMEM((qt, hp, FOX_HEAD_DIM, tile), F32)]
    else:
        body = functools.partial(_fox_online_kernel, tile=tile, heads=hp)
        names = ("table", "q", "k", "v", "g")
        args = (edges[:, :, 0, :FOX_HEADS].reshape(-1), qa, ka, vt, g)
        scratch = []
    return pl.pallas_call(
        body,
        grid=(b, n_heads // hp, s // rows),
        in_specs=[specs[n] for n in names],
        out_specs=specs["g"],
        out_shape=jax.ShapeDtypeStruct((b, s, w), BF16),
        scratch_shapes=scratch,
        compiler_params=_params("arbitrary", "arbitrary", "arbitrary"),
        name="fox_attention_bounded" if bounded else "fox_attention_online",
    )(*args)


def _out_proj_kernel(a_ref, x_ref, mod_ref, w_ref, o_ref):
    y = jnp.dot(a_ref[0], w_ref[...], preferred_element_type=F32)
    o_ref[0] = x_ref[0] + mod_ref[0, 2:3, :] * y


def _out_proj(a, x, mod, w_out):
    b, s, d = x.shape
    tile = OUT_PROJ_TILE
    k = a.shape[-1]
    return pl.pallas_call(
        _out_proj_kernel,
        grid=(b, s // tile),
        in_specs=[pl.BlockSpec((1, tile, k), lambda bi, ti: (bi, ti, 0)),
                  pl.BlockSpec((1, tile, d), lambda bi, ti: (bi, ti, 0)),
                  pl.BlockSpec((1, 3, d), lambda bi, ti: (bi, 0, 0)),
                  pl.BlockSpec((k, d), lambda bi, ti: (0, 0))],
        out_specs=pl.BlockSpec((1, tile, d), lambda bi, ti: (bi, ti, 0)),
        out_shape=jax.ShapeDtypeStruct((b, s, d), F32),
        compiler_params=_params("arbitrary", "arbitrary"),
        name="layer1_out_proj",
    )(a, x, mod, w_out.astype(BF16))


def _layer1(x, mod, norm_g, w_in, b_f, qnorm_g, knorm_g, w_out):
    d, w = x.shape[-1], FOX_WIDTH
    order = jnp.argsort(b_f)
    w_in = w_in.astype(BF16)
    w_heads = w_in[:, :4 * w].reshape(d, 4, FOX_HEADS, FOX_HEAD_DIM)[:, :, order]
    w_forget = jnp.zeros((d, LANES), BF16).at[:, :FOX_HEADS].set(
        w_in[:, 4 * w:][:, order])
    w_in = jnp.concatenate([w_heads.reshape(d, 4 * w), w_forget], axis=1)
    w_out = w_out.astype(BF16).reshape(FOX_HEADS, FOX_HEAD_DIM, d)[order].reshape(w, d)
    qa, ka, vt, g, cq, edges = _layer1_proj(x, mod, norm_g, w_in, b_f[order], qnorm_g, knorm_g)
    score_bound = (LOG2E * FOX_HEAD_DIM ** 0.5
                   * jnp.max(jnp.abs(qnorm_g)) * jnp.max(jnp.abs(knorm_g)))
    gated = lax.cond(score_bound <= FOX_SAFE_LOGIT,
                     functools.partial(_fox_attention, bounded=True),
                     functools.partial(_fox_attention, bounded=False),
                     qa, ka, vt, g, cq, edges)
    return _out_proj(gated, x, mod, w_out)


def kernel(x, c, norm_g, ada_w, ada_b, hgrn_lb, even_w_in, hgrn_onorm_g, pool_w, pool_scale,
           even_w_out, odd_w_in, fox_b_f, fox_qnorm_g, fox_knorm_g, odd_w_out):
    depth = norm_g.shape[0]
    mods = _adaln_mods(c, ada_w, ada_b)
    for l in range(depth):
        j = l // 2
        if l % 2 == 0:
            x = _layer0(x, mods[l], norm_g[l], even_w_in[j], hgrn_lb, hgrn_onorm_g[j],
                        pool_w[j], pool_scale[j], even_w_out[j], layer_slot=l)
        else:
            x = _layer1(x, mods[l], norm_g[l], odd_w_in[j], fox_b_f[j], fox_qnorm_g[j],
                        fox_knorm_g[j], odd_w_out[j])
    return x
```

```python
import functools

import jax
import jax.numpy as jnp
import numpy as np
from jax import lax
from jax.experimental import pallas as pl
from jax.experimental.pallas import tpu as pltpu

F32 = jnp.float32
BF16 = jnp.bfloat16
EPS = 1e-6

HGRN_HEADS = 4
HGRN_KEY = 128
HGRN_VAL = 128
HGRN_KW = HGRN_HEADS * HGRN_KEY
HGRN_VW = HGRN_HEADS * HGRN_VAL
POOL_WINDOWS = (2, 4, 8, 16)
POOL_GROUP = 128
POOL_WIDTH = POOL_GROUP * len(POOL_WINDOWS)
POOL_HISTORY = 16
FOX_HEADS = 16
FOX_HEAD_DIM = 64
FOX_WIDTH = FOX_HEADS * FOX_HEAD_DIM

LANES = 128
SUBLANES = 8
VMEM_LIMIT_BYTES = 56 * 1024 * 1024

SEQ_TILE = 256
OUT_PROJ_TILE = 2048
MOD_READ_STREAMS = 4
HGRN_CHUNK = 128
ATTN_Q_TILE = 512
ATTN_HEADS = 4
ATTN_TILES_PER_STEP = 2
FOX_SAFE_LOGIT = 100.0
FOX_DEAD_EXPONENT = 150.0
LOG2E = 1.4426950408889634
MASK_VALUE = -1e30
AUG_LANE = FOX_HEAD_DIM


def _sigmoid(x):
    return 0.5 * jnp.tanh(0.5 * x) + 0.5


def _silu(x):
    return x * _sigmoid(x)


def _params(*semantics):
    return pltpu.CompilerParams(dimension_semantics=semantics, vmem_limit_bytes=VMEM_LIMIT_BYTES)


def _mod_kernel(c_ref, *refs):
    *w_refs, b_ref, o_ref, cond_ref = refs

    @pl.when((pl.program_id(0) == 0) & (pl.program_id(1) == 0))
    def _():
        cond_ref[...] = _silu(c_ref[...])

    band = w_refs[0].shape[1]
    for bi in range(c_ref.shape[0]):
        acc = b_ref[0]
        for r, w_ref in enumerate(w_refs):
            w = w_ref[0]
            cond = cond_ref[bi, r * band:(r + 1) * band]
            acc = acc + jnp.concatenate(
                [jnp.sum(w[:, j:j + LANES] * cond, axis=0, keepdims=True)
                 for j in range(0, w.shape[1], LANES)], axis=1)
        o_ref[0, bi:bi + 1, :] = acc


def _adaln_mods(c, ada_w, ada_b):
    depth, d, n = ada_w.shape
    b = c.shape[0]
    tn, bands = 1024, MOD_READ_STREAMS
    band_spec = lambda r: pl.BlockSpec((1, d // bands, tn), lambda l, j: (l, r, j))
    out = pl.pallas_call(
        _mod_kernel,
        grid=(depth, n // tn),
        in_specs=[pl.BlockSpec((b, d, LANES), lambda l, j: (0, 0, 0))]
                 + [band_spec(r) for r in range(bands)]
                 + [pl.BlockSpec((1, 1, tn), lambda l, j: (l, 0, j))],
        out_specs=pl.BlockSpec((1, b, tn), lambda l, j: (l, 0, j)),
        out_shape=jax.ShapeDtypeStruct((depth, b, n), F32),
        scratch_shapes=[pltpu.VMEM((b, d, LANES), F32)],
        compiler_params=_params("arbitrary", "arbitrary"),
        name="adaln_mods",
    )(jnp.broadcast_to(c[:, :, None], (b, d, LANES)), *([ada_w] * bands),
      ada_b.reshape(depth, 1, n))
    return out.reshape(depth, b, 3, d)


def _modulated_norm(x, shift, scale, gain):
    ms = jnp.mean(x * x, axis=-1, keepdims=True)
    return x * lax.rsqrt(ms + EPS) * gain * (1.0 + scale) + shift


def _layer0_kernel(x_ref, mod_ref, ng_ref, win32_ref, lb_ref, og_ref, pw_ref, ps_ref, wout32_ref,
                   o_ref, state_ref, carry_ref, win_ref, wout_ref, *, tile, chunk, layer_slot):
    T, C, H = tile, chunk, HGRN_HEADS
    batch = range(x_ref.shape[0])
    t_idx = pl.program_id(0)

    @pl.when(t_idx == 0)
    def _():
        state_ref[...] = jnp.zeros_like(state_ref)
        carry_ref[...] = jnp.zeros_like(carry_ref)
        win_ref[...] = win32_ref[...].astype(BF16)
        wout_ref[...] = wout32_ref[...].astype(BF16)

    lbv = lb_ref[...]
    e = jnp.exp(lbv - jnp.max(lbv, axis=0, keepdims=True))
    lower = (jnp.sum(e[0:layer_slot + 1], axis=0, keepdims=True)
             / jnp.sum(e, axis=0, keepdims=True))

    xs, parts = [], []
    for bi in batch:
        x = x_ref[bi]
        h = _modulated_norm(x, mod_ref[bi, 0:1, :], mod_ref[bi, 1:2, :], ng_ref[...])
        proj = jnp.dot(h.astype(BF16), win_ref[...], preferred_element_type=F32)
        o0, widths, piece = 0, (HGRN_KW, HGRN_KW, HGRN_VW, HGRN_VW, POOL_WIDTH, POOL_WIDTH), []
        for wd in widths:
            piece.append(proj[:, o0:o0 + wd])
            o0 += wd
        xs.append(x)
        parts.append(piece)

    row = lax.broadcasted_iota(jnp.int32, (T, HGRN_KW), 0)
    ti = lax.broadcasted_iota(jnp.int32, (C, C), 0)
    si = lax.broadcasted_iota(jnp.int32, (C, C), 1)
    n_chunks = T // C
    nt_dims = (((1,), (1,)), ((), ()))
    tn_dims = (((0,), (0,)), ((), ()))

    def level_scores(qe, ke, mask, scores):
        qb = qe.astype(BF16)
        kb = ke.astype(BF16)
        out = []
        for c in range(n_chunks):
            for hh in range(H):
                rs = slice(c * C, (c + 1) * C)
                cs = slice(hh * HGRN_KEY, (hh + 1) * HGRN_KEY)
                d = lax.dot_general(qb[rs, cs], kb[rs, cs], nt_dims, preferred_element_type=F32)
                prev = scores[c * H + hh]
                out.append(jnp.where(mask, d, 0.0 if prev is None else prev))
        return out

    scores, q_dec, k_dec, total = [], [], [], []
    for bi in batch:
        q, f = parts[bi][0], parts[bi][1]
        forget = lower + (1.0 - lower) * _sigmoid(f)
        key = 1.0 - forget
        scores.append(level_scores(q, key, ti == si, [None] * (n_chunks * H)))
        q_dec.append(q * forget)
        k_dec.append(key)
        total.append(forget)
    m = 1
    while m < C:
        mask = ((ti ^ si) < 2 * m) & ((ti & m) != 0) & ((si & m) == 0)
        upper = (row & m) != 0
        for bi in batch:
            scores[bi] = level_scores(q_dec[bi], k_dec[bi], mask, scores[bi])
            t_dn = pltpu.roll(total[bi], m, 0)
            t_up = pltpu.roll(total[bi], T - m, 0)
            q_dec[bi] = q_dec[bi] * jnp.where(upper, t_dn, 1.0)
            k_dec[bi] = k_dec[bi] * jnp.where(upper, 1.0, t_up)
            total[bi] = total[bi] * jnp.where(upper, t_dn, t_up)
        m *= 2

    o_a = []
    for bi in batch:
        val, g_a = parts[bi][2], parts[bi][3]
        q_in = q_dec[bi].astype(BF16)
        k_out = k_dec[bi].astype(BF16)
        decay = total[bi]
        val_b = val.astype(BF16)
        oa_rows = []
        for c in range(n_chunks):
            rs = slice(c * C, (c + 1) * C)
            heads_out = []
            for hh in range(H):
                cs = slice(hh * HGRN_KEY, (hh + 1) * HGRN_KEY)
                st = state_ref[bi, hh]
                vb = val_b[rs, cs]
                o = jnp.dot(scores[bi][c * H + hh].astype(BF16), vb, preferred_element_type=F32)
                o = o + lax.dot_general(q_in[rs, cs], st.astype(BF16), nt_dims,
                                        preferred_element_type=F32)
                state_ref[bi, hh] = (st * decay[c * C:c * C + 1, cs]
                                     + lax.dot_general(vb, k_out[rs, cs], tn_dims,
                                                       preferred_element_type=F32))
                ms_o = jnp.mean(o * o, axis=-1, keepdims=True)
                heads_out.append(o * lax.rsqrt(ms_o + EPS))
            oa_rows.append(jnp.concatenate(heads_out, axis=1))
        o_a.append(jnp.concatenate(oa_rows, axis=0) * og_ref[...] * _silu(g_a))

    pos = lax.broadcasted_iota(jnp.int32, (T, POOL_GROUP), 0) + t_idx * T + 1
    for bi in batch:
        u, g_b = parts[bi][4], parts[bi][5]
        ub = jnp.concatenate([carry_ref[bi], u], axis=0)
        carry_ref[bi] = u[T - POOL_HISTORY:T, :]
        wins = []
        acc = ub
        sh = 1
        while sh < max(POOL_WINDOWS):
            acc = acc + pltpu.roll(acc, sh, 0)
            sh *= 2
            wins.append(acc)
        ob = []
        for gi, w in enumerate(POOL_WINDOWS):
            cs = slice(gi * POOL_GROUP, (gi + 1) * POOL_GROUP)
            win = wins[w.bit_length() - 2][POOL_HISTORY:, cs]
            cnt = jnp.minimum(pos, w).astype(F32)
            pooled = win / cnt - u[:, cs]
            ob.append(jnp.dot(pooled.astype(BF16), pw_ref[gi], preferred_element_type=F32))
        o_b = jnp.concatenate(ob, axis=1) * ps_ref[...] * _silu(g_b)
        mixed = jnp.concatenate([o_a[bi], o_b], axis=1).astype(BF16)
        y = jnp.dot(mixed, wout_ref[...], preferred_element_type=F32)
        o_ref[bi] = xs[bi] + mod_ref[bi, 2:3, :] * y


def _layer0(x, mod, norm_g, w_in, hgrn_lb, onorm_g, pool_w, pool_scale, w_out, layer_slot):
    b, s, d = x.shape
    tile = SEQ_TILE
    n_in = w_in.shape[1]
    mix = w_out.shape[0]
    const2 = lambda ti: (0, 0)
    const3 = lambda ti: (0, 0, 0)
    return pl.pallas_call(
        functools.partial(_layer0_kernel, tile=tile, chunk=HGRN_CHUNK, layer_slot=layer_slot),
        grid=(s // tile,),
        in_specs=[pl.BlockSpec((b, tile, d), lambda ti: (0, ti, 0)),
                  pl.BlockSpec((b, 3, d), const3),
                  pl.BlockSpec((1, d), const2),
                  pl.BlockSpec((d, n_in), const2, pipeline_mode=pl.Buffered(1)),
                  pl.BlockSpec(hgrn_lb.shape, const2),
                  pl.BlockSpec((1, HGRN_VW), const2),
                  pl.BlockSpec(pool_w.shape, const3),
                  pl.BlockSpec((1, POOL_WIDTH), const2),
                  pl.BlockSpec((mix, d), const2, pipeline_mode=pl.Buffered(1))],
        out_specs=pl.BlockSpec((b, tile, d), lambda ti: (0, ti, 0)),
        out_shape=jax.ShapeDtypeStruct((b, s, d), F32),
        scratch_shapes=[pltpu.VMEM((b, HGRN_HEADS, HGRN_VAL, HGRN_KEY), F32),
                        pltpu.VMEM((b, POOL_HISTORY, POOL_WIDTH), F32),
                        pltpu.VMEM((d, n_in), BF16),
                        pltpu.VMEM((mix, d), BF16)],
        compiler_params=_params("arbitrary"),
        name="layer0_hgrn_pool",
    )(x, mod, norm_g.reshape(1, d), w_in, hgrn_lb, onorm_g.reshape(1, HGRN_VW),
      pool_w.astype(BF16), pool_scale.reshape(1, POOL_WIDTH), w_out)


def _split3(c):
    hi = c.astype(BF16).astype(F32)
    r = c - hi
    mid = r.astype(BF16).astype(F32)
    lo = (r - mid).astype(BF16).astype(F32)
    return hi, mid, lo


def _carrier_selectors():
    per_tile = LANES // FOX_HEAD_DIM
    sel = np.zeros((FOX_HEADS // per_tile, LANES, per_tile * LANES), np.float32)
    for hd in range(FOX_HEADS):
        base = (hd % per_tile) * LANES + AUG_LANE
        for j in range(3):
            sel[hd // per_tile, j * FOX_HEADS + hd, base + j] = 1.0
            sel[hd // per_tile, j * FOX_HEADS + hd, base + 3 + j] = -1.0
            sel[hd // per_tile, (3 + j) * FOX_HEADS + hd, base + 6 + j] = -1.0
    return jnp.asarray(sel, BF16)


def _layer1_proj_kernel(x_ref, mod_ref, ng_ref, win_ref, bf_ref, qg_ref, kg_ref, bd_ref, sel_ref,
                        qa_ref, ka_ref, vt_ref, g_ref, cq_ref, edge_ref, carry_ref, tile_start_ref,
                        *, tile):
    T, W = tile, FOX_WIDTH
    batch = range(x_ref.shape[0])
    t_idx = pl.program_id(0)
    lane = lax.broadcasted_iota(jnp.int32, (T, LANES), 1)
    row = lax.broadcasted_iota(jnp.int32, (T, LANES), 0)
    bd = bd_ref[...]
    qk_scale = FOX_HEAD_DIM ** -0.5 * LOG2E

    hs = [_modulated_norm(x_ref[bi], mod_ref[bi, 0:1, :], mod_ref[bi, 1:2, :],
                          ng_ref[...]).astype(BF16) for bi in batch]
    projs, feats = [], []
    for bi in batch:
        projs.append(jnp.dot(hs[bi], win_ref[...], preferred_element_type=F32))
        q = projs[bi][:, 0:W]
        k = projs[bi][:, W:2 * W]
        vt_ref[bi, 0] = projs[bi][:, 2 * W:3 * W].T.astype(BF16)
        g_ref[bi] = projs[bi][:, 3 * W:4 * W]
        pairs = []
        for pair in range(FOX_HEADS // 2):
            cs = slice(pair * LANES, (pair + 1) * LANES)
            qp = q[:, cs]
            kp = k[:, cs]
            squares = jnp.concatenate([(qp * qp).astype(BF16), (kp * kp).astype(BF16)], axis=1)
            ms = jnp.dot(squares, bd, preferred_element_type=F32)
            pairs.append((qp * lax.rsqrt(ms[:, :LANES] + EPS) * (qg_ref[...] * qk_scale),
                          kp * lax.rsqrt(ms[:, LANES:] + EPS) * kg_ref[...]))
        feats.append(pairs)

    ones_q = jnp.where((lane >= AUG_LANE + 3) & (lane < AUG_LANE + 9), 1.0, 0.0)
    ones_k = jnp.where((lane >= AUG_LANE) & (lane < AUG_LANE + 3), 1.0, 0.0)
    for bi in batch:
        z = projs[bi][:, 4 * W:4 * W + LANES] + bf_ref[...]
        logf = jnp.minimum(z, 0.0) - jnp.log1p(jnp.exp(-jnp.abs(z)))
        cum = logf
        sh = 1
        while sh < T:
            cum = cum + jnp.where(row >= sh, pltpu.roll(cum, sh, 0), 0.0)
            sh *= 2
        start = jnp.where(t_idx == 0, 0.0, carry_ref[bi])
        end = start + cum[T - 1:T, :]
        carry_ref[bi] = end
        tile_start = jnp.where(t_idx % (ATTN_Q_TILE // T) == 0, start, tile_start_ref[bi])
        tile_start_ref[bi] = tile_start
        edge_ref[bi, 0] = jnp.concatenate([tile_start, end], axis=0) * LOG2E
        s_hi, s_mid, s_lo = _split3((cum + (start - tile_start)) * LOG2E)
        e_hi, e_mid, e_lo = _split3((cum - cum[T - 1:T, :]) * LOG2E)
        total_t = ((cum + start) * LOG2E).T
        for hd in range(FOX_HEADS):
            cq_ref[bi, hd] = total_t[hd:hd + 1, :]

        terms = (s_hi, s_mid, s_lo, e_hi, e_mid, e_lo)
        packed = jnp.zeros((T, LANES), F32)
        for j, term in enumerate(terms):
            moved = term if j == 0 else pltpu.roll(term, j * FOX_HEADS, 1)
            packed = jnp.where((lane >= j * FOX_HEADS) & (lane < (j + 1) * FOX_HEADS), moved,
                               packed)
        packed = packed.astype(BF16)
        for pair in range(FOX_HEADS // 2):
            qn, kn = feats[bi][pair]
            aug = jnp.dot(packed, sel_ref[pair], preferred_element_type=F32)
            for e in range(LANES // FOX_HEAD_DIM):
                hd = (LANES // FOX_HEAD_DIM) * pair + e
                carriers = aug[:, e * LANES:(e + 1) * LANES]
                q_main = qn if e == 0 else pltpu.roll(qn, FOX_HEAD_DIM, 1)
                k_main = kn if e == 0 else pltpu.roll(kn, FOX_HEAD_DIM, 1)
                q_aug = jnp.where(lane < AUG_LANE + 3, carriers, ones_q)
                k_aug = jnp.where(lane < AUG_LANE + 3, ones_k, carriers)
                qa_ref[bi, hd] = jnp.where(lane < FOX_HEAD_DIM, q_main, q_aug).astype(BF16)
                ka_ref[bi, hd] = jnp.where(lane < FOX_HEAD_DIM, k_main, k_aug).astype(BF16)


def _layer1_proj(x, mod, norm_g, w_in, b_f, qnorm_g, knorm_g):
    b, s, d = x.shape
    tile = SEQ_TILE
    nt = s // tile
    w = FOX_WIDTH
    n_pad = w_in.shape[1]
    bf_pad = jnp.zeros((1, LANES), F32).at[0, :FOX_HEADS].set(b_f)
    blk = jnp.arange(2 * LANES) // FOX_HEAD_DIM
    bd = jnp.where(blk[:, None] == blk[None, :], 1.0 / FOX_HEAD_DIM, 0.0).astype(BF16)
    qg = jnp.tile(qnorm_g, LANES // FOX_HEAD_DIM).reshape(1, LANES)
    kg = jnp.tile(knorm_g, LANES // FOX_HEAD_DIM).reshape(1, LANES)
    const2 = lambda ti: (0, 0)
    const3 = lambda ti: (0, 0, 0)
    return pl.pallas_call(
        functools.partial(_layer1_proj_kernel, tile=tile),
        grid=(nt,),
        in_specs=[pl.BlockSpec((b, tile, d), lambda ti: (0, ti, 0)),
                  pl.BlockSpec((b, 3, d), const3),
                  pl.BlockSpec((1, d), const2),
                  pl.BlockSpec((d, n_pad), const2, pipeline_mode=pl.Buffered(1)),
                  pl.BlockSpec((1, LANES), const2),
                  pl.BlockSpec((1, LANES), const2),
                  pl.BlockSpec((1, LANES), const2),
                  pl.BlockSpec((2 * LANES, 2 * LANES), const2),
                  pl.BlockSpec((FOX_HEADS // 2, LANES, 2 * LANES), const3)],
        out_specs=[pl.BlockSpec((b, FOX_HEADS, tile, LANES), lambda ti: (0, 0, ti, 0)),
                   pl.BlockSpec((b, FOX_HEADS, tile, LANES), lambda ti: (0, 0, ti, 0)),
                   pl.BlockSpec((b, 1, w, tile), lambda ti: (0, ti, 0, 0)),
                   pl.BlockSpec((b, tile, w), lambda ti: (0, ti, 0)),
                   pl.BlockSpec((b, FOX_HEADS, 1, tile), lambda ti: (0, 0, 0, ti)),
                   pl.BlockSpec((b, 1, 2, LANES), lambda ti: (0, ti, 0, 0))],
        out_shape=[jax.ShapeDtypeStruct((b, FOX_HEADS, s, LANES), BF16),
                   jax.ShapeDtypeStruct((b, FOX_HEADS, s, LANES), BF16),
                   jax.ShapeDtypeStruct((b, nt, w, tile), BF16),
                   jax.ShapeDtypeStruct((b, s, w), F32),
                   jax.ShapeDtypeStruct((b, FOX_HEADS, 1, s), F32),
                   jax.ShapeDtypeStruct((b, nt, 2, LANES), F32)],
        scratch_shapes=[pltpu.VMEM((b, 1, LANES), F32), pltpu.VMEM((b, 1, LANES), F32)],
        compiler_params=_params("arbitrary"),
        name="layer1_proj",
    )(x, mod, norm_g.reshape(1, d), w_in, bf_pad, qg, kg, bd, _carrier_selectors())


def _query_operands(q):
    slab = 2 * SUBLANES
    row = lax.broadcasted_iota(jnp.int32, (slab, q.shape[0]), 0)
    q_t = q.astype(F32).T
    carriers = q_t[AUG_LANE:AUG_LANE + slab]
    return tuple(jnp.concatenate([q_t[:AUG_LANE], jnp.where(keep, carriers, 0.0),
                                  q_t[AUG_LANE + slab:]], axis=0).astype(BF16)
                 for keep in (row < 6, row >= 6))


def _gated_output(acc_t, g_ref, o_ref, rows):
    o_ref[0, rows] = (acc_t.T * _silu(g_ref[0, rows])).astype(BF16)


def _fox_bounded_kernel(first_ref, end_ref, qa_ref, ka_ref, vt_ref, cq_ref, g_ref, o_ref,
                        p_ref, acc_ref, *, tile, heads):
    TQ, HP = tile, heads
    QT = qa_ref.shape[2] // TQ
    SB = vt_ref.shape[-1]
    R = TQ // SB
    n_blocks = ka_ref.shape[2] // SB
    b = pl.program_id(0)
    group = pl.program_id(1)
    i0 = pl.program_id(2) * QT
    chains = [(qs, e) for qs in range(QT) for e in range(HP)]
    cols = [slice(qs * TQ, (qs + 1) * TQ) for qs in range(QT)]
    operands = {(qs, e): _query_operands(qa_ref[0, e, cols[qs], :]) for qs, e in chains}
    causal = (lax.broadcasted_iota(jnp.int32, (TQ, TQ), 0)
              <= lax.broadcasted_iota(jnp.int32, (TQ, TQ), 1))

    def keys(e, n):
        return ka_ref[0, e, pl.ds(pl.multiple_of(n * TQ, TQ), TQ), :]

    def values_t(e, blk):
        return vt_ref[0, blk, pl.ds(e * FOX_HEAD_DIM, FOX_HEAD_DIM), :]

    def rest(c, blk, gate=1.0):
        qs, e = c
        end = end_ref[(b * n_blocks + blk) * FOX_HEADS + HP * group + e]
        return jnp.exp2(jnp.minimum(cq_ref[0, e][:, cols[qs]] - end, 0.0)) * gate

    def col_partial(p):
        return jnp.sum(p.reshape(p.shape[0] // SUBLANES, SUBLANES, p.shape[1]), axis=0)

    def far_scores(c, n):
        return jnp.dot(keys(c[1], n), operands[c][1], preferred_element_type=F32)

    def contract(c, n, p_bf16, gate=1.0):
        for r in range(R):
            pv = jnp.dot(values_t(c[1], n * R + r), p_bf16[r * SB:(r + 1) * SB],
                         preferred_element_type=F32)
            acc_ref[c] += rest(c, n * R + r, gate) * pv

    def weighted_partial(c, n, p, gate=1.0):
        return sum(rest(c, n * R + r, gate) * col_partial(p[r * SB:(r + 1) * SB])
                   for r in range(R))

    first = first_ref[(b * pl.num_programs(1) + group) * pl.num_programs(2) + pl.program_id(2)]
    gate0 = jnp.where(i0 > 0, 1.0, 0.0)

    l_part = {}

    def own_tile(c):
        qs, e = c
        s_t = jnp.dot(keys(e, i0 + qs), operands[c][0], preferred_element_type=F32)
        p = jnp.where(causal, jnp.exp2(s_t), 0.0)
        l_part[c] = col_partial(p)
        p = p.astype(BF16)

        def later():
            v_t = jnp.concatenate([values_t(e, (i0 + qs) * R + r) for r in range(R)], axis=1)
            acc_ref[c] = jnp.dot(v_t, p, preferred_element_type=F32)
        return later

    def earlier_tile(c, n):
        p = jnp.exp2(far_scores(c, n))
        l_part[c] = l_part[c] + weighted_partial(c, n, p)
        p = p.astype(BF16)
        return lambda: contract(c, n, p)

    def fill(c):
        p = jnp.exp2(far_scores(c, first))
        l_part[c] = l_part[c] + weighted_partial(c, first, p, gate0)
        p_ref[c] = p.astype(BF16)

    jobs = ([functools.partial(own_tile, c) for c in chains]
            + [functools.partial(earlier_tile, c, i0 + t) for c in chains for t in range(c[0])]
            + [functools.partial(fill, c) for c in chains])
    pending = None
    for job in jobs:
        later = job()
        if pending is not None:
            pending()
        pending = later
    if pending is not None:
        pending()

    def body(n, l_run):
        s_new = []
        for c in chains:
            s_new.append(far_scores(c, n))
            contract(c, n - 1, p_ref[c])
        out = []
        for c, s_t, l_c in zip(chains, s_new, l_run):
            p = jnp.exp2(s_t)
            out.append(l_c + weighted_partial(c, n, p))
            p_ref[c] = p.astype(BF16)
        return tuple(out)

    l_run = lax.fori_loop(first + 1, i0, body, tuple(l_part[c] for c in chains))
    for c in chains:
        contract(c, jnp.maximum(i0 - 1, 0), p_ref[c], gate0)
    for qs in range(QT):
        out_t = [acc_ref[qs, e] * (1.0 / jnp.sum(l_run[qs * HP + e], axis=0, keepdims=True))
                 for e in range(HP)]
        _gated_output(jnp.concatenate(out_t, axis=0), g_ref, o_ref, cols[qs])


def _fox_online_kernel(start_ref, qa_ref, ka_ref, vt_ref, g_ref, o_ref, *, tile, heads):
    TB, HP = tile, heads
    n_blocks = ka_ref.shape[2] // TB
    b = pl.program_id(0)
    group = pl.program_id(1)
    i = pl.program_id(2)
    q_t = [_query_operands(qa_ref[0, e])[0] for e in range(HP)]
    k_row = lax.broadcasted_iota(jnp.int32, (TB, TB), 0)
    q_col = lax.broadcasted_iota(jnp.int32, (TB, TB), 1)

    def step(j, carry, masked):
        out = []
        for e in range(HP):
            m_run, l_run, acc = carry[e]
            k_blk = ka_ref[0, e, pl.ds(pl.multiple_of(j * TB, TB), TB), :]
            s_t = jnp.dot(k_blk, q_t[e], preferred_element_type=F32)
            if masked:
                s_t = jnp.where(k_row <= q_col, s_t, MASK_VALUE)
            hd = HP * group + e
            delta = (start_ref[(b * n_blocks + i) * FOX_HEADS + hd]
                     - start_ref[(b * n_blocks + j) * FOX_HEADS + hd])
            m_new = jnp.maximum(m_run, jnp.max(s_t, axis=0, keepdims=True) + delta)
            p_t = jnp.exp2(s_t - (m_new - delta))
            alpha = jnp.exp2(m_run - m_new)
            l_new = alpha * l_run + jnp.sum(p_t, axis=0, keepdims=True)
            v_t = vt_ref[0, j, pl.ds(e * FOX_HEAD_DIM, FOX_HEAD_DIM), :]
            acc_new = alpha * acc + jnp.dot(v_t, p_t.astype(BF16), preferred_element_type=F32)
            out.append((m_new, l_new, acc_new))
        return tuple(out)

    init = (jnp.full((1, TB), MASK_VALUE, F32), jnp.zeros((1, TB), F32),
            jnp.zeros((FOX_HEAD_DIM, TB), F32))
    carry = lax.fori_loop(0, i, lambda j, c: step(j, c, False), (init,) * HP)
    carry = step(i, carry, True)
    _gated_output(jnp.concatenate([acc / l_run for _, l_run, acc in carry], axis=0), g_ref, o_ref,
                  slice(None))


def _fox_attention(qa, ka, vt, g, cq, edges, *, bounded):
    b, n_heads, s, _ = qa.shape
    _, nvb, w, vb = vt.shape
    if bounded:
        tile, hp, qt = ATTN_Q_TILE, ATTN_HEADS, ATTN_TILES_PER_STEP
    else:
        tile, hp, qt = vb, LANES // FOX_HEAD_DIM, 1
    gw = hp * FOX_HEAD_DIM
    rows = qt * tile
    specs = dict(
        table=pl.BlockSpec(memory_space=pltpu.SMEM),
        q=pl.BlockSpec((1, hp, rows, LANES), lambda bi, p, i: (bi, p, i, 0)),
        k=pl.BlockSpec((1, hp, s, LANES), lambda bi, p, i: (bi, p, 0, 0)),
        v=pl.BlockSpec((1, nvb, gw, vb), lambda bi, p, i: (bi, 0, p, 0)),
        cq=pl.BlockSpec((1, hp, 1, rows), lambda bi, p, i: (bi, p, 0, i)),
        g=pl.BlockSpec((1, rows, gw), lambda bi, p, i: (bi, i, p)))
    if bounded:
        body = functools.partial(_fox_bounded_kernel, tile=tile, heads=hp)
        names = ("table", "table", "q", "k", "v", "cq", "g")
        step_start = edges[:, ::rows // vb, 0, :FOX_HEADS]
        tile_end = edges[:, tile // vb - 1::tile // vb, 1, :FOX_HEADS]
        dead = (step_start[:, :, None] - tile_end[:, None]) < -FOX_DEAD_EXPONENT
        before = jnp.arange(s // rows)[:, None] * qt > jnp.arange(s // tile)[None, :]
        first_live = jnp.sum(dead & before[None, :, :, None], axis=2)
        last_before = jnp.maximum(jnp.arange(s // rows) * qt - 1, 0)
        first_live = jnp.minimum(first_live, last_before[None, :, None])
        first_live = jnp.min(first_live.reshape(b, s // rows, n_heads // hp, hp), axis=-1)
        args = (first_live.transpose(0, 2, 1).reshape(-1).astype(jnp.int32),
                edges[:, :, 1, :FOX_HEADS].reshape(-1), qa, ka, vt, cq, g)
        scratch = [pltpu.VMEM((qt, hp, tile, tile), BF16),
                   pltpu.VMEM((qt, hp, FOX_HEAD_DIM, tile), F32)]
    else:
        body = functools.partial(_fox_online_kernel, tile=tile, heads=hp)
        names = ("table", "q", "k", "v", "g")
        args = (edges[:, :, 0, :FOX_HEADS].reshape(-1), qa, ka, vt, g)
        scratch = []
    return pl.pallas_call(
        body,
        grid=(b, n_heads // hp, s // rows),
        in_specs=[specs[n] for n in names],
        out_specs=specs["g"],
        out_shape=jax.ShapeDtypeStruct((b, s, w), BF16),
        scratch_shapes=scratch,
        compiler_params=_params("arbitrary", "arbitrary", "arbitrary"),
        name="fox_attention_bounded" if bounded else "fox_attention_online",
    )(*args)


def _out_proj_kernel(a_ref, x_ref, mod_ref, w_ref, o_ref):
    y = jnp.dot(a_ref[0], w_ref[...], preferred_element_type=F32)
    o_ref[0] = x_ref[0] + mod_ref[0, 2:3, :] * y


def _out_proj(a, x, mod, w_out):
    b, s, d = x.shape
    tile = OUT_PROJ_TILE
    k = a.shape[-1]
    return pl.pallas_call(
        _out_proj_kernel,
        grid=(b, s // tile),
        in_specs=[pl.BlockSpec((1, tile, k), lambda bi, ti: (bi, ti, 0)),
                  pl.BlockSpec((1, tile, d), lambda bi, ti: (bi, ti, 0)),
                  pl.BlockSpec((1, 3, d), lambda bi, ti: (bi, 0, 0)),
                  pl.BlockSpec((k, d), lambda bi, ti: (0, 0))],
        out_specs=pl.BlockSpec((1, tile, d), lambda bi, ti: (bi, ti, 0)),
        out_shape=jax.ShapeDtypeStruct((b, s, d), F32),
        compiler_params=_params("arbitrary", "arbitrary"),
        name="layer1_out_proj",
    )(a, x, mod, w_out.astype(BF16))


def _layer1(x, mod, norm_g, w_in, b_f, qnorm_g, knorm_g, w_out):
    d, w = x.shape[-1], FOX_WIDTH
    order = jnp.argsort(b_f)
    w_in = w_in.astype(BF16)
    w_heads = w_in[:, :4 * w].reshape(d, 4, FOX_HEADS, FOX_HEAD_DIM)[:, :, order]
    w_forget = jnp.zeros((d, LANES), BF16).at[:, :FOX_HEADS].set(
        w_in[:, 4 * w:][:, order])
    w_in = jnp.concatenate([w_heads.reshape(d, 4 * w), w_forget], axis=1)
    w_out = w_out.astype(BF16).reshape(FOX_HEADS, FOX_HEAD_DIM, d)[order].reshape(w, d)
    qa, ka, vt, g, cq, edges = _layer1_proj(x, mod, norm_g, w_in, b_f[order], qnorm_g, knorm_g)
    score_bound = (LOG2E * FOX_HEAD_DIM ** 0.5
                   * jnp.max(jnp.abs(qnorm_g)) * jnp.max(jnp.abs(knorm_g)))
    gated = lax.cond(score_bound <= FOX_SAFE_LOGIT,
                     functools.partial(_fox_attention, bounded=True),
                     functools.partial(_fox_attention, bounded=False),
                     qa, ka, vt, g, cq, edges)
    return _out_proj(gated, x, mod, w_out)


def kernel(x, c, norm_g, ada_w, ada_b, hgrn_lb, even_w_in, hgrn_onorm_g, pool_w, pool_scale,
           even_w_out, odd_w_in, fox_b_f, fox_qnorm_g, fox_knorm_g, odd_w_out):
    depth = norm_g.shape[0]
    mods = _adaln_mods(c, ada_w, ada_b)
    for l in range(depth):
        j = l // 2
        if l % 2 == 0:
            x = _layer0(x, mods[l], norm_g[l], even_w_in[j], hgrn_lb, hgrn_onorm_g[j],
                        pool_w[j], pool_scale[j], even_w_out[j], layer_slot=l)
        else:
            x = _layer1(x, mods[l], norm_g[l], odd_w_in[j], fox_b_f[j], fox_qnorm_g[j],
                        fox_knorm_g[j], odd_w_out[j])
    return x
```

```python
import functools

import jax
import jax.numpy as jnp
import numpy as np
from jax import lax
from jax.experimental import pallas as pl
from jax.experimental.pallas import tpu as pltpu

F32 = jnp.float32
BF16 = jnp.bfloat16
EPS = 1e-6

HGRN_HEADS = 4
HGRN_KEY = 128
HGRN_VAL = 128
HGRN_KW = HGRN_HEADS * HGRN_KEY
HGRN_VW = HGRN_HEADS * HGRN_VAL
POOL_WINDOWS = (2, 4, 8, 16)
POOL_GROUP = 128
POOL_WIDTH = POOL_GROUP * len(POOL_WINDOWS)
POOL_HISTORY = 16
FOX_HEADS = 16
FOX_HEAD_DIM = 64
FOX_WIDTH = FOX_HEADS * FOX_HEAD_DIM

LANES = 128
SUBLANES = 8
VMEM_LIMIT_BYTES = 56 * 1024 * 1024

SEQ_TILE = 256
OUT_PROJ_TILE = 2048
OUT_PROJ_READ_STREAMS = 4
MOD_READ_STREAMS = 4
HGRN_CHUNK = 128
ATTN_Q_TILE = 512
ATTN_HEADS = 4
ATTN_TILES_PER_STEP = 2
FOX_SAFE_LOGIT = 100.0
FOX_DEAD_EXPONENT = 150.0
LOG2E = 1.4426950408889634
MASK_VALUE = -1e30
AUG_LANE = FOX_HEAD_DIM


def _sigmoid(x):
    return 0.5 * jnp.tanh(0.5 * x) + 0.5


def _silu(x):
    return x * _sigmoid(x)


def _params(*semantics):
    return pltpu.CompilerParams(dimension_semantics=semantics, vmem_limit_bytes=VMEM_LIMIT_BYTES)


def _mod_kernel(c_ref, *refs):
    *w_refs, b_ref, o_ref, cond_ref = refs

    @pl.when((pl.program_id(0) == 0) & (pl.program_id(1) == 0))
    def _():
        cond_ref[...] = _silu(c_ref[...])

    band = w_refs[0].shape[1]
    for bi in range(c_ref.shape[0]):
        acc = b_ref[0]
        for r, w_ref in enumerate(w_refs):
            w = w_ref[0]
            cond = cond_ref[bi, r * band:(r + 1) * band]
            acc = acc + jnp.concatenate(
                [jnp.sum(w[:, j:j + LANES] * cond, axis=0, keepdims=True)
                 for j in range(0, w.shape[1], LANES)], axis=1)
        o_ref[0, bi:bi + 1, :] = acc


def _adaln_mods(c, ada_w, ada_b):
    depth, d, n = ada_w.shape
    b = c.shape[0]
    tn, bands = 1024, MOD_READ_STREAMS
    band_spec = lambda r: pl.BlockSpec((1, d // bands, tn), lambda l, j: (l, r, j))
    out = pl.pallas_call(
        _mod_kernel,
        grid=(depth, n // tn),
        in_specs=[pl.BlockSpec((b, d, LANES), lambda l, j: (0, 0, 0))]
                 + [band_spec(r) for r in range(bands)]
                 + [pl.BlockSpec((1, 1, tn), lambda l, j: (l, 0, j))],
        out_specs=pl.BlockSpec((1, b, tn), lambda l, j: (l, 0, j)),
        out_shape=jax.ShapeDtypeStruct((depth, b, n), F32),
        scratch_shapes=[pltpu.VMEM((b, d, LANES), F32)],
        compiler_params=_params("arbitrary", "arbitrary"),
        name="adaln_mods",
    )(jnp.broadcast_to(c[:, :, None], (b, d, LANES)), *([ada_w] * bands),
      ada_b.reshape(depth, 1, n))
    return out.reshape(depth, b, 3, d)


def _modulated_norm(x, shift, scale, gain):
    ms = jnp.mean(x * x, axis=-1, keepdims=True)
    return x * lax.rsqrt(ms + EPS) * gain * (1.0 + scale) + shift


def _layer0_kernel(x_ref, mod_ref, ng_ref, win32_ref, lb_ref, og_ref, pw_ref, ps_ref, wout32_ref,
                   o_ref, state_ref, carry_ref, win_ref, wout_ref, *, tile, chunk, layer_slot):
    T, C, H = tile, chunk, HGRN_HEADS
    batch = range(x_ref.shape[0])
    t_idx = pl.program_id(0)

    @pl.when(t_idx == 0)
    def _():
        state_ref[...] = jnp.zeros_like(state_ref)
        carry_ref[...] = jnp.zeros_like(carry_ref)
        win_ref[...] = win32_ref[...].astype(BF16)
        wout_ref[...] = wout32_ref[...].astype(BF16)

    lbv = lb_ref[...]
    e = jnp.exp(lbv - jnp.max(lbv, axis=0, keepdims=True))
    lower = (jnp.sum(e[0:layer_slot + 1], axis=0, keepdims=True)
             / jnp.sum(e, axis=0, keepdims=True))

    xs, parts = [], []
    for bi in batch:
        x = x_ref[bi]
        h = _modulated_norm(x, mod_ref[bi, 0:1, :], mod_ref[bi, 1:2, :], ng_ref[...])
        proj = jnp.dot(h.astype(BF16), win_ref[...], preferred_element_type=F32)
        o0, widths, piece = 0, (HGRN_KW, HGRN_KW, HGRN_VW, HGRN_VW, POOL_WIDTH, POOL_WIDTH), []
        for wd in widths:
            piece.append(proj[:, o0:o0 + wd])
            o0 += wd
        xs.append(x)
        parts.append(piece)

    row = lax.broadcasted_iota(jnp.int32, (T, HGRN_KW), 0)
    ti = lax.broadcasted_iota(jnp.int32, (C, C), 0)
    si = lax.broadcasted_iota(jnp.int32, (C, C), 1)
    n_chunks = T // C
    nt_dims = (((1,), (1,)), ((), ()))
    tn_dims = (((0,), (0,)), ((), ()))

    def level_scores(qe, ke, mask, scores):
        qb = qe.astype(BF16)
        kb = ke.astype(BF16)
        out = []
        for c in range(n_chunks):
            for hh in range(H):
                rs = slice(c * C, (c + 1) * C)
                cs = slice(hh * HGRN_KEY, (hh + 1) * HGRN_KEY)
                d = lax.dot_general(qb[rs, cs], kb[rs, cs], nt_dims, preferred_element_type=F32)
                prev = scores[c * H + hh]
                out.append(jnp.where(mask, d, 0.0 if prev is None else prev))
        return out

    scores, q_dec, k_dec, total = [], [], [], []
    for bi in batch:
        q, f = parts[bi][0], parts[bi][1]
        forget = lower + (1.0 - lower) * _sigmoid(f)
        key = 1.0 - forget
        scores.append(level_scores(q, key, ti == si, [None] * (n_chunks * H)))
        q_dec.append(q * forget)
        k_dec.append(key)
        total.append(forget)
    m = 1
    while m < C:
        mask = ((ti ^ si) < 2 * m) & ((ti & m) != 0) & ((si & m) == 0)
        upper = (row & m) != 0
        for bi in batch:
            scores[bi] = level_scores(q_dec[bi], k_dec[bi], mask, scores[bi])
            t_dn = pltpu.roll(total[bi], m, 0)
            t_up = pltpu.roll(total[bi], T - m, 0)
            q_dec[bi] = q_dec[bi] * jnp.where(upper, t_dn, 1.0)
            k_dec[bi] = k_dec[bi] * jnp.where(upper, 1.0, t_up)
            total[bi] = total[bi] * jnp.where(upper, t_dn, t_up)
        m *= 2

    o_a = []
    for bi in batch:
        val, g_a = parts[bi][2], parts[bi][3]
        q_in = q_dec[bi].astype(BF16)
        k_out = k_dec[bi].astype(BF16)
        decay = total[bi]
        val_b = val.astype(BF16)
        oa_rows = []
        for c in range(n_chunks):
            rs = slice(c * C, (c + 1) * C)
            heads_out = []
            for hh in range(H):
                cs = slice(hh * HGRN_KEY, (hh + 1) * HGRN_KEY)
                st = state_ref[bi, hh]
                vb = val_b[rs, cs]
                o = jnp.dot(scores[bi][c * H + hh].astype(BF16), vb, preferred_element_type=F32)
                o = o + lax.dot_general(q_in[rs, cs], st.astype(BF16), nt_dims,
                                        preferred_element_type=F32)
                state_ref[bi, hh] = (st * decay[c * C:c * C + 1, cs]
                                     + lax.dot_general(vb, k_out[rs, cs], tn_dims,
                                                       preferred_element_type=F32))
                ms_o = jnp.mean(o * o, axis=-1, keepdims=True)
                heads_out.append(o * lax.rsqrt(ms_o + EPS))
            oa_rows.append(jnp.concatenate(heads_out, axis=1))
        o_a.append(jnp.concatenate(oa_rows, axis=0) * og_ref[...] * _silu(g_a))

    pos = lax.broadcasted_iota(jnp.int32, (T, POOL_GROUP), 0) + t_idx * T + 1
    for bi in batch:
        u, g_b = parts[bi][4], parts[bi][5]
        ub = jnp.concatenate([carry_ref[bi], u], axis=0)
        carry_ref[bi] = u[T - POOL_HISTORY:T, :]
        wins = []
        acc = ub
        sh = 1
        while sh < max(POOL_WINDOWS):
            acc = acc + pltpu.roll(acc, sh, 0)
            sh *= 2
            wins.append(acc)
        ob = []
        for gi, w in enumerate(POOL_WINDOWS):
            cs = slice(gi * POOL_GROUP, (gi + 1) * POOL_GROUP)
            win = wins[w.bit_length() - 2][POOL_HISTORY:, cs]
            cnt = jnp.minimum(pos, w).astype(F32)
            pooled = win / cnt - u[:, cs]
            ob.append(jnp.dot(pooled.astype(BF16), pw_ref[gi], preferred_element_type=F32))
        o_b = jnp.concatenate(ob, axis=1) * ps_ref[...] * _silu(g_b)
        mixed = jnp.concatenate([o_a[bi], o_b], axis=1).astype(BF16)
        y = jnp.dot(mixed, wout_ref[...], preferred_element_type=F32)
        o_ref[bi] = xs[bi] + mod_ref[bi, 2:3, :] * y


def _layer0(x, mod, norm_g, w_in, hgrn_lb, onorm_g, pool_w, pool_scale, w_out, layer_slot):
    b, s, d = x.shape
    tile = SEQ_TILE
    n_in = w_in.shape[1]
    mix = w_out.shape[0]
    const2 = lambda ti: (0, 0)
    const3 = lambda ti: (0, 0, 0)
    return pl.pallas_call(
        functools.partial(_layer0_kernel, tile=tile, chunk=HGRN_CHUNK, layer_slot=layer_slot),
        grid=(s // tile,),
        in_specs=[pl.BlockSpec((b, tile, d), lambda ti: (0, ti, 0)),
                  pl.BlockSpec((b, 3, d), const3),
                  pl.BlockSpec((1, d), const2),
                  pl.BlockSpec((d, n_in), const2, pipeline_mode=pl.Buffered(1)),
                  pl.BlockSpec(hgrn_lb.shape, const2),
                  pl.BlockSpec((1, HGRN_VW), const2),
                  pl.BlockSpec(pool_w.shape, const3),
                  pl.BlockSpec((1, POOL_WIDTH), const2),
                  pl.BlockSpec((mix, d), const2, pipeline_mode=pl.Buffered(1))],
        out_specs=pl.BlockSpec((b, tile, d), lambda ti: (0, ti, 0)),
        out_shape=jax.ShapeDtypeStruct((b, s, d), F32),
        scratch_shapes=[pltpu.VMEM((b, HGRN_HEADS, HGRN_VAL, HGRN_KEY), F32),
                        pltpu.VMEM((b, POOL_HISTORY, POOL_WIDTH), F32),
                        pltpu.VMEM((d, n_in), BF16),
                        pltpu.VMEM((mix, d), BF16)],
        compiler_params=_params("arbitrary"),
        name="layer0_hgrn_pool",
    )(x, mod, norm_g.reshape(1, d), w_in, hgrn_lb, onorm_g.reshape(1, HGRN_VW),
      pool_w.astype(BF16), pool_scale.reshape(1, POOL_WIDTH), w_out)


def _split3(c):
    hi = c.astype(BF16).astype(F32)
    r = c - hi
    mid = r.astype(BF16).astype(F32)
    lo = (r - mid).astype(BF16).astype(F32)
    return hi, mid, lo


def _carrier_selectors():
    per_tile = LANES // FOX_HEAD_DIM
    sel = np.zeros((FOX_HEADS // per_tile, LANES, per_tile * LANES), np.float32)
    for hd in range(FOX_HEADS):
        base = (hd % per_tile) * LANES + AUG_LANE
        for j in range(3):
            sel[hd // per_tile, j * FOX_HEADS + hd, base + j] = 1.0
            sel[hd // per_tile, j * FOX_HEADS + hd, base + 3 + j] = -1.0
            sel[hd // per_tile, (3 + j) * FOX_HEADS + hd, base + 6 + j] = -1.0
    return jnp.asarray(sel, BF16)


def _layer1_proj_kernel(x_ref, mod_ref, ng_ref, win_ref, bf_ref, qg_ref, kg_ref, bd_ref, sel_ref,
                        qa_ref, ka_ref, vt_ref, g_ref, cq_ref, edge_ref, carry_ref, tile_start_ref,
                        *, tile):
    T, W = tile, FOX_WIDTH
    batch = range(x_ref.shape[0])
    t_idx = pl.program_id(0)
    lane = lax.broadcasted_iota(jnp.int32, (T, LANES), 1)
    row = lax.broadcasted_iota(jnp.int32, (T, LANES), 0)
    bd = bd_ref[...]
    qk_scale = FOX_HEAD_DIM ** -0.5 * LOG2E

    hs = [_modulated_norm(x_ref[bi], mod_ref[bi, 0:1, :], mod_ref[bi, 1:2, :],
                          ng_ref[...]).astype(BF16) for bi in batch]
    projs, feats = [], []
    for bi in batch:
        projs.append(jnp.dot(hs[bi], win_ref[...], preferred_element_type=F32))
        q = projs[bi][:, 0:W]
        k = projs[bi][:, W:2 * W]
        vt_ref[bi, 0] = projs[bi][:, 2 * W:3 * W].T.astype(BF16)
        g_ref[bi] = projs[bi][:, 3 * W:4 * W]
        pairs = []
        for pair in range(FOX_HEADS // 2):
            cs = slice(pair * LANES, (pair + 1) * LANES)
            qp = q[:, cs]
            kp = k[:, cs]
            squares = jnp.concatenate([(qp * qp).astype(BF16), (kp * kp).astype(BF16)], axis=1)
            ms = jnp.dot(squares, bd, preferred_element_type=F32)
            pairs.append((qp * lax.rsqrt(ms[:, :LANES] + EPS) * (qg_ref[...] * qk_scale),
                          kp * lax.rsqrt(ms[:, LANES:] + EPS) * kg_ref[...]))
        feats.append(pairs)

    ones_q = jnp.where((lane >= AUG_LANE + 3) & (lane < AUG_LANE + 9), 1.0, 0.0)
    ones_k = jnp.where((lane >= AUG_LANE) & (lane < AUG_LANE + 3), 1.0, 0.0)
    for bi in batch:
        z = projs[bi][:, 4 * W:4 * W + LANES] + bf_ref[...]
        logf = jnp.minimum(z, 0.0) - jnp.log1p(jnp.exp(-jnp.abs(z)))
        cum = logf
        sh = 1
        while sh < T:
            cum = cum + jnp.where(row >= sh, pltpu.roll(cum, sh, 0), 0.0)
            sh *= 2
        start = jnp.where(t_idx == 0, 0.0, carry_ref[bi])
        end = start + cum[T - 1:T, :]
        carry_ref[bi] = end
        tile_start = jnp.where(t_idx % (ATTN_Q_TILE // T) == 0, start, tile_start_ref[bi])
        tile_start_ref[bi] = tile_start
        edge_ref[bi, 0] = jnp.concatenate([tile_start, end], axis=0) * LOG2E
        s_hi, s_mid, s_lo = _split3((cum + (start - tile_start)) * LOG2E)
        e_hi, e_mid, e_lo = _split3((cum - cum[T - 1:T, :]) * LOG2E)
        total_t = ((cum + start) * LOG2E).T
        for hd in range(FOX_HEADS):
            cq_ref[bi, hd] = total_t[hd:hd + 1, :]

        terms = (s_hi, s_mid, s_lo, e_hi, e_mid, e_lo)
        packed = jnp.zeros((T, LANES), F32)
        for j, term in enumerate(terms):
            moved = term if j == 0 else pltpu.roll(term, j * FOX_HEADS, 1)
            packed = jnp.where((lane >= j * FOX_HEADS) & (lane < (j + 1) * FOX_HEADS), moved,
                               packed)
        packed = packed.astype(BF16)
        for pair in range(FOX_HEADS // 2):
            qn, kn = feats[bi][pair]
            aug = jnp.dot(packed, sel_ref[pair], preferred_element_type=F32)
            for e in range(LANES // FOX_HEAD_DIM):
                hd = (LANES // FOX_HEAD_DIM) * pair + e
                carriers = aug[:, e * LANES:(e + 1) * LANES]
                q_main = qn if e == 0 else pltpu.roll(qn, FOX_HEAD_DIM, 1)
                k_main = kn if e == 0 else pltpu.roll(kn, FOX_HEAD_DIM, 1)
                q_aug = jnp.where(lane < AUG_LANE + 3, carriers, ones_q)
                k_aug = jnp.where(lane < AUG_LANE + 3, ones_k, carriers)
                qa_ref[bi, hd] = jnp.where(lane < FOX_HEAD_DIM, q_main, q_aug).astype(BF16)
                ka_ref[bi, hd] = jnp.where(lane < FOX_HEAD_DIM, k_main, k_aug).astype(BF16)


def _layer1_proj(x, mod, norm_g, w_in, b_f, qnorm_g, knorm_g):
    b, s, d = x.shape
    tile = SEQ_TILE
    nt = s // tile
    w = FOX_WIDTH
    n_pad = w_in.shape[1]
    bf_pad = jnp.zeros((1, LANES), F32).at[0, :FOX_HEADS].set(b_f)
    blk = jnp.arange(2 * LANES) // FOX_HEAD_DIM
    bd = jnp.where(blk[:, None] == blk[None, :], 1.0 / FOX_HEAD_DIM, 0.0).astype(BF16)
    qg = jnp.tile(qnorm_g, LANES // FOX_HEAD_DIM).reshape(1, LANES)
    kg = jnp.tile(knorm_g, LANES // FOX_HEAD_DIM).reshape(1, LANES)
    const2 = lambda ti: (0, 0)
    const3 = lambda ti: (0, 0, 0)
    return pl.pallas_call(
        functools.partial(_layer1_proj_kernel, tile=tile),
        grid=(nt,),
        in_specs=[pl.BlockSpec((b, tile, d), lambda ti: (0, ti, 0)),
                  pl.BlockSpec((b, 3, d), const3),
                  pl.BlockSpec((1, d), const2),
                  pl.BlockSpec((d, n_pad), const2, pipeline_mode=pl.Buffered(1)),
                  pl.BlockSpec((1, LANES), const2),
                  pl.BlockSpec((1, LANES), const2),
                  pl.BlockSpec((1, LANES), const2),
                  pl.BlockSpec((2 * LANES, 2 * LANES), const2),
                  pl.BlockSpec((FOX_HEADS // 2, LANES, 2 * LANES), const3)],
        out_specs=[pl.BlockSpec((b, FOX_HEADS, tile, LANES), lambda ti: (0, 0, ti, 0)),
                   pl.BlockSpec((b, FOX_HEADS, tile, LANES), lambda ti: (0, 0, ti, 0)),
                   pl.BlockSpec((b, 1, w, tile), lambda ti: (0, ti, 0, 0)),
                   pl.BlockSpec((b, tile, w), lambda ti: (0, ti, 0)),
                   pl.BlockSpec((b, FOX_HEADS, 1, tile), lambda ti: (0, 0, 0, ti)),
                   pl.BlockSpec((b, 1, 2, LANES), lambda ti: (0, ti, 0, 0))],
        out_shape=[jax.ShapeDtypeStruct((b, FOX_HEADS, s, LANES), BF16),
                   jax.ShapeDtypeStruct((b, FOX_HEADS, s, LANES), BF16),
                   jax.ShapeDtypeStruct((b, nt, w, tile), BF16),
                   jax.ShapeDtypeStruct((b, s, w), F32),
                   jax.ShapeDtypeStruct((b, FOX_HEADS, 1, s), F32),
                   jax.ShapeDtypeStruct((b, nt, 2, LANES), F32)],
        scratch_shapes=[pltpu.VMEM((b, 1, LANES), F32), pltpu.VMEM((b, 1, LANES), F32)],
        compiler_params=_params("arbitrary"),
        name="layer1_proj",
    )(x, mod, norm_g.reshape(1, d), w_in, bf_pad, qg, kg, bd, _carrier_selectors())


def _query_operands(q):
    slab = 2 * SUBLANES
    row = lax.broadcasted_iota(jnp.int32, (slab, q.shape[0]), 0)
    q_t = q.astype(F32).T
    carriers = q_t[AUG_LANE:AUG_LANE + slab]
    return tuple(jnp.concatenate([q_t[:AUG_LANE], jnp.where(keep, carriers, 0.0),
                                  q_t[AUG_LANE + slab:]], axis=0).astype(BF16)
                 for keep in (row < 6, row >= 6))


def _gated_output(acc_t, g_ref, o_ref, rows):
    o_ref[0, rows] = (acc_t.T * _silu(g_ref[0, rows])).astype(BF16)


def _fox_bounded_kernel(first_ref, end_ref, qa_ref, ka_ref, vt_ref, cq_ref, g_ref, o_ref,
                        p_ref, acc_ref, *, tile, heads):
    TQ, HP = tile, heads
    QT = qa_ref.shape[2] // TQ
    SB = vt_ref.shape[-1]
    R = TQ // SB
    n_blocks = ka_ref.shape[2] // SB
    b = pl.program_id(0)
    group = pl.program_id(1)
    i0 = pl.program_id(2) * QT
    chains = [(qs, e) for qs in range(QT) for e in range(HP)]
    cols = [slice(qs * TQ, (qs + 1) * TQ) for qs in range(QT)]
    operands = {(qs, e): _query_operands(qa_ref[0, e, cols[qs], :]) for qs, e in chains}
    causal = (lax.broadcasted_iota(jnp.int32, (TQ, TQ), 0)
              <= lax.broadcasted_iota(jnp.int32, (TQ, TQ), 1))

    def keys(e, n):
        return ka_ref[0, e, pl.ds(pl.multiple_of(n * TQ, TQ), TQ), :]

    def values_t(e, blk):
        return vt_ref[0, blk, pl.ds(e * FOX_HEAD_DIM, FOX_HEAD_DIM), :]

    def rest(c, blk, gate=1.0):
        qs, e = c
        end = end_ref[(b * n_blocks + blk) * FOX_HEADS + HP * group + e]
        return jnp.exp2(jnp.minimum(cq_ref[0, e][:, cols[qs]] - end, 0.0)) * gate

    def col_partial(p):
        return jnp.sum(p.reshape(p.shape[0] // SUBLANES, SUBLANES, p.shape[1]), axis=0)

    def far_scores(c, n):
        return jnp.dot(keys(c[1], n), operands[c][1], preferred_element_type=F32)

    def contract(c, n, p_bf16, gate=1.0):
        for r in range(R):
            pv = jnp.dot(values_t(c[1], n * R + r), p_bf16[r * SB:(r + 1) * SB],
                         preferred_element_type=F32)
            acc_ref[c] += rest(c, n * R + r, gate) * pv

    def weighted_partial(c, n, p, gate=1.0):
        return sum(rest(c, n * R + r, gate) * col_partial(p[r * SB:(r + 1) * SB])
                   for r in range(R))

    first = first_ref[(b * pl.num_programs(1) + group) * pl.num_programs(2) + pl.program_id(2)]
    gate0 = jnp.where(i0 > 0, 1.0, 0.0)

    l_part = {}

    def own_tile(c):
        qs, e = c
        s_t = jnp.dot(keys(e, i0 + qs), operands[c][0], preferred_element_type=F32)
        p = jnp.where(causal, jnp.exp2(s_t), 0.0)
        l_part[c] = col_partial(p)
        p = p.astype(BF16)

        def later():
            v_t = jnp.concatenate([values_t(e, (i0 + qs) * R + r) for r in range(R)], axis=1)
            acc_ref[c] = jnp.dot(v_t, p, preferred_element_type=F32)
        return later

    def earlier_tile(c, n):
        p = jnp.exp2(far_scores(c, n))
        l_part[c] = l_part[c] + weighted_partial(c, n, p)
        p = p.astype(BF16)
        return lambda: contract(c, n, p)

    def fill(c):
        p = jnp.exp2(far_scores(c, first))
        l_part[c] = l_part[c] + weighted_partial(c, first, p, gate0)
        p_ref[c] = p.astype(BF16)

    jobs = ([functools.partial(own_tile, c) for c in chains]
            + [functools.partial(earlier_tile, c, i0 + t) for c in chains for t in range(c[0])]
            + [functools.partial(fill, c) for c in chains])
    pending = None
    for job in jobs:
        later = job()
        if pending is not None:
            pending()
        pending = later
    if pending is not None:
        pending()

    def body(n, l_run):
        s_new = []
        for c in chains:
            s_new.append(far_scores(c, n))
            contract(c, n - 1, p_ref[c])
        out = []
        for c, s_t, l_c in zip(chains, s_new, l_run):
            p = jnp.exp2(s_t)
            out.append(l_c + weighted_partial(c, n, p))
            p_ref[c] = p.astype(BF16)
        return tuple(out)

    l_run = lax.fori_loop(first + 1, i0, body, tuple(l_part[c] for c in chains))
    for c in chains:
        contract(c, jnp.maximum(i0 - 1, 0), p_ref[c], gate0)
    for qs in range(QT):
        out_t = [acc_ref[qs, e] * (1.0 / jnp.sum(l_run[qs * HP + e], axis=0, keepdims=True))
                 for e in range(HP)]
        _gated_output(jnp.concatenate(out_t, axis=0), g_ref, o_ref, cols[qs])


def _fox_online_kernel(start_ref, qa_ref, ka_ref, vt_ref, g_ref, o_ref, *, tile, heads):
    TB, HP = tile, heads
    n_blocks = ka_ref.shape[2] // TB
    b = pl.program_id(0)
    group = pl.program_id(1)
    i = pl.program_id(2)
    q_t = [_query_operands(qa_ref[0, e])[0] for e in range(HP)]
    k_row = lax.broadcasted_iota(jnp.int32, (TB, TB), 0)
    q_col = lax.broadcasted_iota(jnp.int32, (TB, TB), 1)

    def step(j, carry, masked):
        out = []
        for e in range(HP):
            m_run, l_run, acc = carry[e]
            k_blk = ka_ref[0, e, pl.ds(pl.multiple_of(j * TB, TB), TB), :]
            s_t = jnp.dot(k_blk, q_t[e], preferred_element_type=F32)
            if masked:
                s_t = jnp.where(k_row <= q_col, s_t, MASK_VALUE)
            hd = HP * group + e
            delta = (start_ref[(b * n_blocks + i) * FOX_HEADS + hd]
                     - start_ref[(b * n_blocks + j) * FOX_HEADS + hd])
            m_new = jnp.maximum(m_run, jnp.max(s_t, axis=0, keepdims=True) + delta)
            p_t = jnp.exp2(s_t - (m_new - delta))
            alpha = jnp.exp2(m_run - m_new)
            l_new = alpha * l_run + jnp.sum(p_t, axis=0, keepdims=True)
            v_t = vt_ref[0, j, pl.ds(e * FOX_HEAD_DIM, FOX_HEAD_DIM), :]
            acc_new = alpha * acc + jnp.dot(v_t, p_t.astype(BF16), preferred_element_type=F32)
            out.append((m_new, l_new, acc_new))
        return tuple(out)

    init = (jnp.full((1, TB), MASK_VALUE, F32), jnp.zeros((1, TB), F32),
            jnp.zeros((FOX_HEAD_DIM, TB), F32))
    carry = lax.fori_loop(0, i, lambda j, c: step(j, c, False), (init,) * HP)
    carry = step(i, carry, True)
    _gated_output(jnp.concatenate([acc / l_run for _, l_run, acc in carry], axis=0), g_ref, o_ref,
                  slice(None))


def _fox_attention(qa, ka, vt, g, cq, edges, *, bounded):
    b, n_heads, s, _ = qa.shape
    _, nvb, w, vb = vt.shape
    if bounded:
        tile, hp, qt = ATTN_Q_TILE, ATTN_HEADS, ATTN_TILES_PER_STEP
    else:
        tile, hp, qt = vb, LANES // FOX_HEAD_DIM, 1
    gw = hp * FOX_HEAD_DIM
    rows = qt * tile
    specs = dict(
        table=pl.BlockSpec(memory_space=pltpu.SMEM),
        q=pl.BlockSpec((1, hp, rows, LANES), lambda bi, p, i: (bi, p, i, 0)),
        k=pl.BlockSpec((1, hp, s, LANES), lambda bi, p, i: (bi, p, 0, 0)),
        v=pl.BlockSpec((1, nvb, gw, vb), lambda bi, p, i: (bi, 0, p, 0)),
        cq=pl.BlockSpec((1, hp, 1, rows), lambda bi, p, i: (bi, p, 0, i)),
        g=pl.BlockSpec((1, rows, gw), lambda bi, p, i: (bi, i, p)))
    if bounded:
        body = functools.partial(_fox_bounded_kernel, tile=tile, heads=hp)
        names = ("table", "table", "q", "k", "v", "cq", "g")
        step_start = edges[:, ::rows // vb, 0, :FOX_HEADS]
        tile_end = edges[:, tile // vb - 1::tile // vb, 1, :FOX_HEADS]
        dead = (step_start[:, :, None] - tile_end[:, None]) < -FOX_DEAD_EXPONENT
        before = jnp.arange(s // rows)[:, None] * qt > jnp.arange(s // tile)[None, :]
        first_live = jnp.sum(dead & before[None, :, :, None], axis=2)
        last_before = jnp.maximum(jnp.arange(s // rows) * qt - 1, 0)
        first_live = jnp.minimum(first_live, last_before[None, :, None])
        first_live = jnp.min(first_live.reshape(b, s // rows, n_heads // hp, hp), axis=-1)
        args = (first_live.transpose(0, 2, 1).reshape(-1).astype(jnp.int32),
                edges[:, :, 1, :FOX_HEADS].reshape(-1), qa, ka, vt, cq, g)
        scratch = [pltpu.VMEM((qt, hp, tile, tile), BF16),
                   pltpu.VMEM((qt, hp, FOX_HEAD_DIM, tile), F32)]
    else:
        body = functools.partial(_fox_online_kernel, tile=tile, heads=hp)
        names = ("table", "q", "k", "v", "g")
        args = (edges[:, :, 0, :FOX_HEADS].reshape(-1), qa, ka, vt, g)
        scratch = []
    return pl.pallas_call(
        body,
        grid=(b, n_heads // hp, s // rows),
        in_specs=[specs[n] for n in names],
        out_specs=specs["g"],
        out_shape=jax.ShapeDtypeStruct((b, s, w), BF16),
        scratch_shapes=scratch,
        compiler_params=_params("arbitrary", "arbitrary", "arbitrary"),
        name="fox_attention_bounded" if bounded else "fox_attention_online",
    )(*args)


def _out_proj_kernel(a_ref, *refs):
    *x_refs, mod_ref, w_ref, o_ref = refs
    y = jnp.dot(a_ref[0], w_ref[...], preferred_element_type=F32)
    x = jnp.concatenate([x_ref[0] for x_ref in x_refs], axis=0)
    o_ref[0] = x + mod_ref[0, 2:3, :] * y


def _out_proj(a, x, mod, w_out):
    b, s, d = x.shape
    tile, bands = OUT_PROJ_TILE, OUT_PROJ_READ_STREAMS
    k = a.shape[-1]
    band_spec = lambda r: pl.BlockSpec((1, tile // bands, d),
                                       lambda bi, ti: (bi, ti * bands + r, 0))
    return pl.pallas_call(
        _out_proj_kernel,
        grid=(b, s // tile),
        in_specs=[pl.BlockSpec((1, tile, k), lambda bi, ti: (bi, ti, 0))]
                 + [band_spec(r) for r in range(bands)]
                 + [pl.BlockSpec((1, 3, d), lambda bi, ti: (bi, 0, 0)),
                    pl.BlockSpec((k, d), lambda bi, ti: (0, 0))],
        out_specs=pl.BlockSpec((1, tile, d), lambda bi, ti: (bi, ti, 0)),
        out_shape=jax.ShapeDtypeStruct((b, s, d), F32),
        compiler_params=_params("arbitrary", "arbitrary"),
        name="layer1_out_proj",
    )(a, *([x] * bands), mod, w_out.astype(BF16))


def _layer1(x, mod, norm_g, w_in, b_f, qnorm_g, knorm_g, w_out):
    d, w = x.shape[-1], FOX_WIDTH
    order = jnp.argsort(b_f)
    w_in = w_in.astype(BF16)
    w_heads = w_in[:, :4 * w].reshape(d, 4, FOX_HEADS, FOX_HEAD_DIM)[:, :, order]
    w_forget = jnp.zeros((d, LANES), BF16).at[:, :FOX_HEADS].set(
        w_in[:, 4 * w:][:, order])
    w_in = jnp.concatenate([w_heads.reshape(d, 4 * w), w_forget], axis=1)
    w_out = w_out.astype(BF16).reshape(FOX_HEADS, FOX_HEAD_DIM, d)[order].reshape(w, d)
    qa, ka, vt, g, cq, edges = _layer1_proj(x, mod, norm_g, w_in, b_f[order], qnorm_g, knorm_g)
    score_bound = (LOG2E * FOX_HEAD_DIM ** 0.5
                   * jnp.max(jnp.abs(qnorm_g)) * jnp.max(jnp.abs(knorm_g)))
    gated = lax.cond(score_bound <= FOX_SAFE_LOGIT,
                     functools.partial(_fox_attention, bounded=True),
                     functools.partial(_fox_attention, bounded=False),
                     qa, ka, vt, g, cq, edges)
    return _out_proj(gated, x, mod, w_out)


def kernel(x, c, norm_g, ada_w, ada_b, hgrn_lb, even_w_in, hgrn_onorm_g, pool_w, pool_scale,
           even_w_out, odd_w_in, fox_b_f, fox_qnorm_g, fox_knorm_g, odd_w_out):
    depth = norm_g.shape[0]
    mods = _adaln_mods(c, ada_w, ada_b)
    for l in range(depth):
        j = l // 2
        if l % 2 == 0:
            x = _layer0(x, mods[l], norm_g[l], even_w_in[j], hgrn_lb, hgrn_onorm_g[j],
                        pool_w[j], pool_scale[j], even_w_out[j], layer_slot=l)
        else:
            x = _layer1(x, mods[l], norm_g[l], odd_w_in[j], fox_b_f[j], fox_qnorm_g[j],
                        fox_knorm_g[j], odd_w_out[j])
    return x
```

```python
import functools

import jax
import jax.numpy as jnp
import numpy as np
from jax import lax
from jax.experimental import pallas as pl
from jax.experimental.pallas import tpu as pltpu

F32 = jnp.float32
BF16 = jnp.bfloat16
EPS = 1e-6

HGRN_HEADS = 4
HGRN_KEY = 128
HGRN_VAL = 128
HGRN_KW = HGRN_HEADS * HGRN_KEY
HGRN_VW = HGRN_HEADS * HGRN_VAL
POOL_WINDOWS = (2, 4, 8, 16)
POOL_GROUP = 128
POOL_WIDTH = POOL_GROUP * len(POOL_WINDOWS)
POOL_HISTORY = 16
FOX_HEADS = 16
FOX_HEAD_DIM = 64
FOX_WIDTH = FOX_HEADS * FOX_HEAD_DIM

LANES = 128
SUBLANES = 8
VMEM_LIMIT_BYTES = 56 * 1024 * 1024

SEQ_TILE = 256
OUT_PROJ_TILE = 2048
MOD_READ_STREAMS = 4
HGRN_CHUNK = 128
ATTN_Q_TILE = 512
ATTN_HEADS = 4
ATTN_TILES_PER_STEP = 2
FOX_SAFE_LOGIT = 100.0
FOX_DEAD_EXPONENT = 150.0
LOG2E = 1.4426950408889634
MASK_VALUE = -1e30
AUG_LANE = FOX_HEAD_DIM


def _sigmoid(x):
    return 0.5 * jnp.tanh(0.5 * x) + 0.5


def _silu(x):
    return x * _sigmoid(x)


def _params(*semantics):
    return pltpu.CompilerParams(dimension_semantics=semantics, vmem_limit_bytes=VMEM_LIMIT_BYTES)


def _mod_kernel(c_ref, *refs):
    *w_refs, b_ref, o_ref, cond_ref = refs

    @pl.when((pl.program_id(0) == 0) & (pl.program_id(1) == 0))
    def _():
        cond_ref[...] = _silu(c_ref[...])

    band = w_refs[0].shape[1]
    for bi in range(c_ref.shape[0]):
        acc = b_ref[0]
        for r, w_ref in enumerate(w_refs):
            w = w_ref[0]
            cond = cond_ref[bi, r * band:(r + 1) * band]
            acc = acc + jnp.concatenate(
                [jnp.sum(w[:, j:j + LANES] * cond, axis=0, keepdims=True)
                 for j in range(0, w.shape[1], LANES)], axis=1)
        o_ref[0, bi:bi + 1, :] = acc


def _adaln_mods(c, ada_w, ada_b):
    depth, d, n = ada_w.shape
    b = c.shape[0]
    tn, bands = 1024, MOD_READ_STREAMS
    band_spec = lambda r: pl.BlockSpec((1, d // bands, tn), lambda l, j: (l, r, j))
    out = pl.pallas_call(
        _mod_kernel,
        grid=(depth, n // tn),
        in_specs=[pl.BlockSpec((b, d, LANES), lambda l, j: (0, 0, 0))]
                 + [band_spec(r) for r in range(bands)]
                 + [pl.BlockSpec((1, 1, tn), lambda l, j: (l, 0, j))],
        out_specs=pl.BlockSpec((1, b, tn), lambda l, j: (l, 0, j)),
        out_shape=jax.ShapeDtypeStruct((depth, b, n), F32),
        scratch_shapes=[pltpu.VMEM((b, d, LANES), F32)],
        compiler_params=_params("arbitrary", "arbitrary"),
        name="adaln_mods",
    )(jnp.broadcast_to(c[:, :, None], (b, d, LANES)), *([ada_w] * bands),
      ada_b.reshape(depth, 1, n))
    return out.reshape(depth, b, 3, d)


def _modulated_norm(x, shift, scale, gain):
    ms = jnp.mean(x * x, axis=-1, keepdims=True)
    return x * lax.rsqrt(ms + EPS) * gain * (1.0 + scale) + shift


def _layer0_kernel(x_ref, mod_ref, ng_ref, win32_ref, lb_ref, og_ref, pw_ref, ps_ref, wout32_ref,
                   o_ref, state_ref, carry_ref, win_ref, wout_ref, *, tile, chunk, layer_slot):
    T, C, H = tile, chunk, HGRN_HEADS
    batch = range(x_ref.shape[0])
    t_idx = pl.program_id(0)

    @pl.when(t_idx == 0)
    def _():
        state_ref[...] = jnp.zeros_like(state_ref)
        carry_ref[...] = jnp.zeros_like(carry_ref)
        win_ref[...] = win32_ref[...].astype(BF16)
        wout_ref[...] = wout32_ref[...].astype(BF16)

    lbv = lb_ref[...]
    e = jnp.exp(lbv - jnp.max(lbv, axis=0, keepdims=True))
    lower = (jnp.sum(e[0:layer_slot + 1], axis=0, keepdims=True)
             / jnp.sum(e, axis=0, keepdims=True))

    xs, parts = [], []
    for bi in batch:
        x = x_ref[bi]
        h = _modulated_norm(x, mod_ref[bi, 0:1, :], mod_ref[bi, 1:2, :], ng_ref[...])
        proj = jnp.dot(h.astype(BF16), win_ref[...], preferred_element_type=F32)
        o0, widths, piece = 0, (HGRN_KW, HGRN_KW, HGRN_VW, HGRN_VW, POOL_WIDTH, POOL_WIDTH), []
        for wd in widths:
            piece.append(proj[:, o0:o0 + wd])
            o0 += wd
        xs.append(x)
        parts.append(piece)

    row = lax.broadcasted_iota(jnp.int32, (T, HGRN_KW), 0)
    ti = lax.broadcasted_iota(jnp.int32, (C, C), 0)
    si = lax.broadcasted_iota(jnp.int32, (C, C), 1)
    n_chunks = T // C
    nt_dims = (((1,), (1,)), ((), ()))
    tn_dims = (((0,), (0,)), ((), ()))

    def level_scores(qe, ke, mask, scores):
        qb = qe.astype(BF16)
        kb = ke.astype(BF16)
        out = []
        for c in range(n_chunks):
            for hh in range(H):
                rs = slice(c * C, (c + 1) * C)
                cs = slice(hh * HGRN_KEY, (hh + 1) * HGRN_KEY)
                d = lax.dot_general(qb[rs, cs], kb[rs, cs], nt_dims, preferred_element_type=F32)
                prev = scores[c * H + hh]
                out.append(jnp.where(mask, d, 0.0 if prev is None else prev))
        return out

    scores, q_dec, k_dec, total = [], [], [], []
    for bi in batch:
        q, f = parts[bi][0], parts[bi][1]
        forget = lower + (1.0 - lower) * _sigmoid(f)
        key = 1.0 - forget
        scores.append(level_scores(q, key, ti == si, [None] * (n_chunks * H)))
        q_dec.append(q * forget)
        k_dec.append(key)
        total.append(forget)
    m = 1
    while m < C:
        mask = ((ti ^ si) < 2 * m) & ((ti & m) != 0) & ((si & m) == 0)
        upper = (row & m) != 0
        for bi in batch:
            scores[bi] = level_scores(q_dec[bi], k_dec[bi], mask, scores[bi])
            t_dn = pltpu.roll(total[bi], m, 0)
            t_up = pltpu.roll(total[bi], T - m, 0)
            q_dec[bi] = q_dec[bi] * jnp.where(upper, t_dn, 1.0)
            k_dec[bi] = k_dec[bi] * jnp.where(upper, 1.0, t_up)
            total[bi] = total[bi] * jnp.where(upper, t_dn, t_up)
        m *= 2

    o_a = []
    for bi in batch:
        val, g_a = parts[bi][2], parts[bi][3]
        q_in = q_dec[bi].astype(BF16)
        k_out = k_dec[bi].astype(BF16)
        decay = total[bi]
        val_b = val.astype(BF16)
        oa_rows = []
        for c in range(n_chunks):
            rs = slice(c * C, (c + 1) * C)
            heads_out = []
            for hh in range(H):
                cs = slice(hh * HGRN_KEY, (hh + 1) * HGRN_KEY)
                st = state_ref[bi, hh]
                vb = val_b[rs, cs]
                o = jnp.dot(scores[bi][c * H + hh].astype(BF16), vb, preferred_element_type=F32)
                o = o + lax.dot_general(q_in[rs, cs], st.astype(BF16), nt_dims,
                                        preferred_element_type=F32)
                state_ref[bi, hh] = (st * decay[c * C:c * C + 1, cs]
                                     + lax.dot_general(vb, k_out[rs, cs], tn_dims,
                                                       preferred_element_type=F32))
                ms_o = jnp.mean(o * o, axis=-1, keepdims=True)
                heads_out.append(o * lax.rsqrt(ms_o + EPS))
            oa_rows.append(jnp.concatenate(heads_out, axis=1))
        o_a.append(jnp.concatenate(oa_rows, axis=0) * og_ref[...] * _silu(g_a))

    pos = lax.broadcasted_iota(jnp.int32, (T, POOL_GROUP), 0) + t_idx * T + 1
    for bi in batch:
        u, g_b = parts[bi][4], parts[bi][5]
        ub = jnp.concatenate([carry_ref[bi], u], axis=0)
        carry_ref[bi] = u[T - POOL_HISTORY:T, :]
        wins = []
        acc = ub
        sh = 1
        while sh < max(POOL_WINDOWS):
            acc = acc + pltpu.roll(acc, sh, 0)
            sh *= 2
            wins.append(acc)
        ob = []
        for gi, w in enumerate(POOL_WINDOWS):
            cs = slice(gi * POOL_GROUP, (gi + 1) * POOL_GROUP)
            win = wins[w.bit_length() - 2][POOL_HISTORY:, cs]
            cnt = jnp.minimum(pos, w).astype(F32)
            pooled = win / cnt - u[:, cs]
            ob.append(jnp.dot(pooled.astype(BF16), pw_ref[gi], preferred_element_type=F32))
        o_b = jnp.concatenate(ob, axis=1) * ps_ref[...] * _silu(g_b)
        mixed = jnp.concatenate([o_a[bi], o_b], axis=1).astype(BF16)
        y = jnp.dot(mixed, wout_ref[...], preferred_element_type=F32)
        o_ref[bi] = xs[bi] + mod_ref[bi, 2:3, :] * y


def _layer0(x, mod, norm_g, w_in, hgrn_lb, onorm_g, pool_w, pool_scale, w_out, layer_slot):
    b, s, d = x.shape
    tile = SEQ_TILE
    n_in = w_in.shape[1]
    mix = w_out.shape[0]
    const2 = lambda ti: (0, 0)
    const3 = lambda ti: (0, 0, 0)
    return pl.pallas_call(
        functools.partial(_layer0_kernel, tile=tile, chunk=HGRN_CHUNK, layer_slot=layer_slot),
        grid=(s // tile,),
        in_specs=[pl.BlockSpec((b, tile, d), lambda ti: (0, ti, 0)),
                  pl.BlockSpec((b, 3, d), const3),
                  pl.BlockSpec((1, d), const2),
                  pl.BlockSpec((d, n_in), const2, pipeline_mode=pl.Buffered(1)),
                  pl.BlockSpec(hgrn_lb.shape, const2),
                  pl.BlockSpec((1, HGRN_VW), const2),
                  pl.BlockSpec(pool_w.shape, const3),
                  pl.BlockSpec((1, POOL_WIDTH), const2),
                  pl.BlockSpec((mix, d), const2, pipeline_mode=pl.Buffered(1))],
        out_specs=pl.BlockSpec((b, tile, d), lambda ti: (0, ti, 0)),
        out_shape=jax.ShapeDtypeStruct((b, s, d), F32),
        scratch_shapes=[pltpu.VMEM((b, HGRN_HEADS, HGRN_VAL, HGRN_KEY), F32),
                        pltpu.VMEM((b, POOL_HISTORY, POOL_WIDTH), F32),
                        pltpu.VMEM((d, n_in), BF16),
                        pltpu.VMEM((mix, d), BF16)],
        compiler_params=_params("arbitrary"),
        name="layer0_hgrn_pool",
    )(x, mod, norm_g.reshape(1, d), w_in, hgrn_lb, onorm_g.reshape(1, HGRN_VW),
      pool_w.astype(BF16), pool_scale.reshape(1, POOL_WIDTH), w_out)


def _split3(c):
    hi = c.astype(BF16).astype(F32)
    r = c - hi
    mid = r.astype(BF16).astype(F32)
    lo = (r - mid).astype(BF16).astype(F32)
    return hi, mid, lo


def _carrier_selectors():
    per_tile = LANES // FOX_HEAD_DIM
    sel = np.zeros((FOX_HEADS // per_tile, LANES, per_tile * LANES), np.float32)
    for hd in range(FOX_HEADS):
        base = (hd % per_tile) * LANES + AUG_LANE
        for j in range(3):
            sel[hd // per_tile, j * FOX_HEADS + hd, base + j] = 1.0
            sel[hd // per_tile, j * FOX_HEADS + hd, base + 3 + j] = -1.0
            sel[hd // per_tile, (3 + j) * FOX_HEADS + hd, base + 6 + j] = -1.0
    return jnp.asarray(sel, BF16)


def _layer1_proj_kernel(x_ref, mod_ref, ng_ref, win_ref, bf_ref, qg_ref, kg_ref, bd_ref, sel_ref,
                        qa_ref, ka_ref, vt_ref, g_ref, cq_ref, edge_ref, carry_ref, tile_start_ref,
                        *, tile):
    T, W = tile, FOX_WIDTH
    batch = range(x_ref.shape[0])
    t_idx = pl.program_id(0)
    lane = lax.broadcasted_iota(jnp.int32, (T, LANES), 1)
    row = lax.broadcasted_iota(jnp.int32, (T, LANES), 0)
    bd = bd_ref[...]
    qk_scale = FOX_HEAD_DIM ** -0.5 * LOG2E

    hs = [_modulated_norm(x_ref[bi], mod_ref[bi, 0:1, :], mod_ref[bi, 1:2, :],
                          ng_ref[...]).astype(BF16) for bi in batch]
    projs, feats = [], []
    for bi in batch:
        projs.append(jnp.dot(hs[bi], win_ref[...], preferred_element_type=F32))
        q = projs[bi][:, 0:W]
        k = projs[bi][:, W:2 * W]
        vt_ref[bi, 0] = projs[bi][:, 2 * W:3 * W].T.astype(BF16)
        g_ref[bi] = projs[bi][:, 3 * W:4 * W]
        pairs = []
        for pair in range(FOX_HEADS // 2):
            cs = slice(pair * LANES, (pair + 1) * LANES)
            qp = q[:, cs]
            kp = k[:, cs]
            squares = jnp.concatenate([(qp * qp).astype(BF16), (kp * kp).astype(BF16)], axis=1)
            ms = jnp.dot(squares, bd, preferred_element_type=F32)
            pairs.append((qp * lax.rsqrt(ms[:, :LANES] + EPS) * (qg_ref[...] * qk_scale),
                          kp * lax.rsqrt(ms[:, LANES:] + EPS) * kg_ref[...]))
        feats.append(pairs)

    ones_q = jnp.where((lane >= AUG_LANE + 3) & (lane < AUG_LANE + 9), 1.0, 0.0)
    ones_k = jnp.where((lane >= AUG_LANE) & (lane < AUG_LANE + 3), 1.0, 0.0)
    for bi in batch:
        z = projs[bi][:, 4 * W:4 * W + LANES] + bf_ref[...]
        logf = jnp.minimum(z, 0.0) - jnp.log1p(jnp.exp(-jnp.abs(z)))
        cum = logf
        sh = 1
        while sh < T:
            cum = cum + jnp.where(row >= sh, pltpu.roll(cum, sh, 0), 0.0)
            sh *= 2
        start = jnp.where(t_idx == 0, 0.0, carry_ref[bi])
        end = start + cum[T - 1:T, :]
        carry_ref[bi] = end
        tile_start = jnp.where(t_idx % (ATTN_Q_TILE // T) == 0, start, tile_start_ref[bi])
        tile_start_ref[bi] = tile_start
        edge_ref[bi, 0] = jnp.concatenate([tile_start, end], axis=0) * LOG2E
        s_hi, s_mid, s_lo = _split3((cum + (start - tile_start)) * LOG2E)
        e_hi, e_mid, e_lo = _split3((cum - cum[T - 1:T, :]) * LOG2E)
        total_t = ((cum + start) * LOG2E).T
        for hd in range(FOX_HEADS):
            cq_ref[bi, hd] = total_t[hd:hd + 1, :]

        terms = (s_hi, s_mid, s_lo, e_hi, e_mid, e_lo)
        packed = jnp.zeros((T, LANES), F32)
        for j, term in enumerate(terms):
            moved = term if j == 0 else pltpu.roll(term, j * FOX_HEADS, 1)
            packed = jnp.where((lane >= j * FOX_HEADS) & (lane < (j + 1) * FOX_HEADS), moved,
                               packed)
        packed = packed.astype(BF16)
        for pair in range(FOX_HEADS // 2):
            qn, kn = feats[bi][pair]
            aug = jnp.dot(packed, sel_ref[pair], preferred_element_type=F32)
            for e in range(LANES // FOX_HEAD_DIM):
                hd = (LANES // FOX_HEAD_DIM) * pair + e
                carriers = aug[:, e * LANES:(e + 1) * LANES]
                q_main = qn if e == 0 else pltpu.roll(qn, FOX_HEAD_DIM, 1)
                k_main = kn if e == 0 else pltpu.roll(kn, FOX_HEAD_DIM, 1)
                q_aug = jnp.where(lane < AUG_LANE + 3, carriers, ones_q)
                k_aug = jnp.where(lane < AUG_LANE + 3, ones_k, carriers)
                qa_ref[bi, hd] = jnp.where(lane < FOX_HEAD_DIM, q_main, q_aug).astype(BF16)
                ka_ref[bi, hd] = jnp.where(lane < FOX_HEAD_DIM, k_main, k_aug).astype(BF16)


def _layer1_proj(x, mod, norm_g, w_in, b_f, qnorm_g, knorm_g):
    b, s, d = x.shape
    tile = SEQ_TILE
    nt = s // tile
    w = FOX_WIDTH
    n_pad = w_in.shape[1]
    bf_pad = jnp.zeros((1, LANES), F32).at[0, :FOX_HEADS].set(b_f)
    blk = jnp.arange(2 * LANES) // FOX_HEAD_DIM
    bd = jnp.where(blk[:, None] == blk[None, :], 1.0 / FOX_HEAD_DIM, 0.0).astype(BF16)
    qg = jnp.tile(qnorm_g, LANES // FOX_HEAD_DIM).reshape(1, LANES)
    kg = jnp.tile(knorm_g, LANES // FOX_HEAD_DIM).reshape(1, LANES)
    const2 = lambda ti: (0, 0)
    const3 = lambda ti: (0, 0, 0)
    return pl.pallas_call(
        functools.partial(_layer1_proj_kernel, tile=tile),
        grid=(nt,),
        in_specs=[pl.BlockSpec((b, tile, d), lambda ti: (0, ti, 0)),
                  pl.BlockSpec((b, 3, d), const3),
                  pl.BlockSpec((1, d), const2),
                  pl.BlockSpec((d, n_pad), const2, pipeline_mode=pl.Buffered(1)),
                  pl.BlockSpec((1, LANES), const2),
                  pl.BlockSpec((1, LANES), const2),
                  pl.BlockSpec((1, LANES), const2),
                  pl.BlockSpec((2 * LANES, 2 * LANES), const2),
                  pl.BlockSpec((FOX_HEADS // 2, LANES, 2 * LANES), const3)],
        out_specs=[pl.BlockSpec((b, FOX_HEADS, tile, LANES), lambda ti: (0, 0, ti, 0)),
                   pl.BlockSpec((b, FOX_HEADS, tile, LANES), lambda ti: (0, 0, ti, 0)),
                   pl.BlockSpec((b, 1, w, tile), lambda ti: (0, ti, 0, 0)),
                   pl.BlockSpec((b, tile, w), lambda ti: (0, ti, 0)),
                   pl.BlockSpec((b, FOX_HEADS, 1, tile), lambda ti: (0, 0, 0, ti)),
                   pl.BlockSpec((b, 1, 2, LANES), lambda ti: (0, ti, 0, 0))],
        out_shape=[jax.ShapeDtypeStruct((b, FOX_HEADS, s, LANES), BF16),
                   jax.ShapeDtypeStruct((b, FOX_HEADS, s, LANES), BF16),
                   jax.ShapeDtypeStruct((b, nt, w, tile), BF16),
                   jax.ShapeDtypeStruct((b, s, w), F32),
                   jax.ShapeDtypeStruct((b, FOX_HEADS, 1, s), F32),
                   jax.ShapeDtypeStruct((b, nt, 2, LANES), F32)],
        scratch_shapes=[pltpu.VMEM((b, 1, LANES), F32), pltpu.VMEM((b, 1, LANES), F32)],
        compiler_params=_params("arbitrary"),
        name="layer1_proj",
    )(x, mod, norm_g.reshape(1, d), w_in, bf_pad, qg, kg, bd, _carrier_selectors())


def _query_operands(q):
    slab = 2 * SUBLANES
    row = lax.broadcasted_iota(jnp.int32, (slab, q.shape[0]), 0)
    q_t = q.astype(F32).T
    carriers = q_t[AUG_LANE:AUG_LANE + slab]
    return tuple(jnp.concatenate([q_t[:AUG_LANE], jnp.where(keep, carriers, 0.0),
                                  q_t[AUG_LANE + slab:]], axis=0).astype(BF16)
                 for keep in (row < 6, row >= 6))


def _gated_output(acc_t, g_ref, o_ref, rows):
    o_ref[0, rows] = (acc_t.T * _silu(g_ref[0, rows])).astype(BF16)


def _fox_bounded_kernel(first_ref, end_ref, qa_ref, ka_ref, vt_ref, cq_ref, g_ref, o_ref,
                        p_ref, acc_ref, *, tile, heads):
    TQ, HP = tile, heads
    QT = qa_ref.shape[2] // TQ
    SB = vt_ref.shape[-1]
    R = TQ // SB
    n_blocks = ka_ref.shape[2] // SB
    b = pl.program_id(0)
    group = pl.program_id(1)
    i0 = pl.program_id(2) * QT
    chains = [(qs, e) for qs in range(QT) for e in range(HP)]
    cols = [slice(qs * TQ, (qs + 1) * TQ) for qs in range(QT)]
    operands = {(qs, e): _query_operands(qa_ref[0, e, cols[qs], :]) for qs, e in chains}
    causal = (lax.broadcasted_iota(jnp.int32, (TQ, TQ), 0)
              <= lax.broadcasted_iota(jnp.int32, (TQ, TQ), 1))

    def keys(e, n):
        return ka_ref[0, e, pl.ds(pl.multiple_of(n * TQ, TQ), TQ), :]

    def values_t(e, blk):
        return vt_ref[0, blk, pl.ds(e * FOX_HEAD_DIM, FOX_HEAD_DIM), :]

    def rest(c, blk, gate=1.0):
        qs, e = c
        end = end_ref[(b * n_blocks + blk) * FOX_HEADS + HP * group + e]
        return jnp.exp2(jnp.minimum(cq_ref[0, e][:, cols[qs]] - end, 0.0)) * gate

    def col_partial(p):
        return jnp.sum(p.reshape(p.shape[0] // SUBLANES, SUBLANES, p.shape[1]), axis=0)

    def far_scores(c, n):
        return jnp.dot(keys(c[1], n), operands[c][1], preferred_element_type=F32)

    def contract(c, n, p_bf16, gate=1.0):
        for r in range(R):
            pv = jnp.dot(values_t(c[1], n * R + r), p_bf16[r * SB:(r + 1) * SB],
                         preferred_element_type=F32)
            acc_ref[c] += rest(c, n * R + r, gate) * pv

    def weighted_partial(c, n, p, gate=1.0):
        return sum(rest(c, n * R + r, gate) * col_partial(p[r * SB:(r + 1) * SB])
                   for r in range(R))

    first = first_ref[(b * pl.num_programs(1) + group) * pl.num_programs(2) + pl.program_id(2)]
    gate0 = jnp.where(i0 > 0, 1.0, 0.0)

    l_part = {}

    def own_tile(c):
        qs, e = c
        k_own = keys(e, i0 + qs)
        pieces = []
        for r in range(R):
            n_keys, lanes = (r + 1) * SB, slice(r * SB, (r + 1) * SB)
            s_t = jnp.dot(k_own[:n_keys], operands[c][0][:, lanes], preferred_element_type=F32)
            pieces.append(jnp.where(causal[:n_keys, lanes], jnp.exp2(s_t), 0.0))
        l_part[c] = jnp.concatenate([col_partial(p) for p in pieces], axis=1)
        pieces = [p.astype(BF16) for p in pieces]

        def later():
            v_t = jnp.concatenate([values_t(e, (i0 + qs) * R + r) for r in range(R)], axis=1)
            for r, p in enumerate(pieces):
                acc_ref[qs, e, :, r * SB:(r + 1) * SB] = jnp.dot(
                    v_t[:, :(r + 1) * SB], p, preferred_element_type=F32)
        return later

    def earlier_tile(c, n):
        p = jnp.exp2(far_scores(c, n))
        l_part[c] = l_part[c] + weighted_partial(c, n, p)
        p = p.astype(BF16)
        return lambda: contract(c, n, p)

    def fill(c):
        p = jnp.exp2(far_scores(c, first))
        l_part[c] = l_part[c] + weighted_partial(c, first, p, gate0)
        p_ref[c] = p.astype(BF16)

    jobs = ([functools.partial(own_tile, c) for c in chains]
            + [functools.partial(earlier_tile, c, i0 + t) for c in chains for t in range(c[0])]
            + [functools.partial(fill, c) for c in chains])
    pending = None
    for job in jobs:
        later = job()
        if pending is not None:
            pending()
        pending = later
    if pending is not None:
        pending()

    def body(n, l_run):
        s_new = []
        for c in chains:
            s_new.append(far_scores(c, n))
            contract(c, n - 1, p_ref[c])
        out = []
        for c, s_t, l_c in zip(chains, s_new, l_run):
            p = jnp.exp2(s_t)
            out.append(l_c + weighted_partial(c, n, p))
            p_ref[c] = p.astype(BF16)
        return tuple(out)

    l_run = lax.fori_loop(first + 1, i0, body, tuple(l_part[c] for c in chains))
    for c in chains:
        contract(c, jnp.maximum(i0 - 1, 0), p_ref[c], gate0)
    for qs in range(QT):
        out_t = [acc_ref[qs, e] * (1.0 / jnp.sum(l_run[qs * HP + e], axis=0, keepdims=True))
                 for e in range(HP)]
        _gated_output(jnp.concatenate(out_t, axis=0), g_ref, o_ref, cols[qs])


def _fox_online_kernel(start_ref, qa_ref, ka_ref, vt_ref, g_ref, o_ref, *, tile, heads):
    TB, HP = tile, heads
    n_blocks = ka_ref.shape[2] // TB
    b = pl.program_id(0)
    group = pl.program_id(1)
    i = pl.program_id(2)
    q_t = [_query_operands(qa_ref[0, e])[0] for e in range(HP)]
    k_row = lax.broadcasted_iota(jnp.int32, (TB, TB), 0)
    q_col = lax.broadcasted_iota(jnp.int32, (TB, TB), 1)

    def step(j, carry, masked):
        out = []
        for e in range(HP):
            m_run, l_run, acc = carry[e]
            k_blk = ka_ref[0, e, pl.ds(pl.multiple_of(j * TB, TB), TB), :]
            s_t = jnp.dot(k_blk, q_t[e], preferred_element_type=F32)
            if masked:
                s_t = jnp.where(k_row <= q_col, s_t, MASK_VALUE)
            hd = HP * group + e
            delta = (start_ref[(b * n_blocks + i) * FOX_HEADS + hd]
                     - start_ref[(b * n_blocks + j) * FOX_HEADS + hd])
            m_new = jnp.maximum(m_run, jnp.max(s_t, axis=0, keepdims=True) + delta)
            p_t = jnp.exp2(s_t - (m_new - delta))
            alpha = jnp.exp2(m_run - m_new)
            l_new = alpha * l_run + jnp.sum(p_t, axis=0, keepdims=True)
            v_t = vt_ref[0, j, pl.ds(e * FOX_HEAD_DIM, FOX_HEAD_DIM), :]
            acc_new = alpha * acc + jnp.dot(v_t, p_t.astype(BF16), preferred_element_type=F32)
            out.append((m_new, l_new, acc_new))
        return tuple(out)

    init = (jnp.full((1, TB), MASK_VALUE, F32), jnp.zeros((1, TB), F32),
            jnp.zeros((FOX_HEAD_DIM, TB), F32))
    carry = lax.fori_loop(0, i, lambda j, c: step(j, c, False), (init,) * HP)
    carry = step(i, carry, True)
    _gated_output(jnp.concatenate([acc / l_run for _, l_run, acc in carry], axis=0), g_ref, o_ref,
                  slice(None))


def _fox_attention(qa, ka, vt, g, cq, edges, *, bounded):
    b, n_heads, s, _ = qa.shape
    _, nvb, w, vb = vt.shape
    if bounded:
        tile, hp, qt = ATTN_Q_TILE, ATTN_HEADS, ATTN_TILES_PER_STEP
    else:
        tile, hp, qt = vb, LANES // FOX_HEAD_DIM, 1
    gw = hp * FOX_HEAD_DIM
    rows = qt * tile
    specs = dict(
        table=pl.BlockSpec(memory_space=pltpu.SMEM),
        q=pl.BlockSpec((1, hp, rows, LANES), lambda bi, p, i: (bi, p, i, 0)),
        k=pl.BlockSpec((1, hp, s, LANES), lambda bi, p, i: (bi, p, 0, 0)),
        v=pl.BlockSpec((1, nvb, gw, vb), lambda bi, p, i: (bi, 0, p, 0)),
        cq=pl.BlockSpec((1, hp, 1, rows), lambda bi, p, i: (bi, p, 0, i)),
        g=pl.BlockSpec((1, rows, gw), lambda bi, p, i: (bi, i, p)))
    if bounded:
        body = functools.partial(_fox_bounded_kernel, tile=tile, heads=hp)
        names = ("table", "table", "q", "k", "v", "cq", "g")
        step_start = edges[:, ::rows // vb, 0, :FOX_HEADS]
        tile_end = edges[:, tile // vb - 1::tile // vb, 1, :FOX_HEADS]
        dead = (step_start[:, :, None] - tile_end[:, None]) < -FOX_DEAD_EXPONENT
        before = jnp.arange(s // rows)[:, None] * qt > jnp.arange(s // tile)[None, :]
        first_live = jnp.sum(dead & before[None, :, :, None], axis=2)
        last_before = jnp.maximum(jnp.arange(s // rows) * qt - 1, 0)
        first_live = jnp.minimum(first_live, last_before[None, :, None])
        first_live = jnp.min(first_live.reshape(b, s // rows, n_heads // hp, hp), axis=-1)
        args = (first_live.transpose(0, 2, 1).reshape(-1).astype(jnp.int32),
                edges[:, :, 1, :FOX_HEADS].reshape(-1), qa, ka, vt, cq, g)
        scratch = [pltpu.VMEM((qt, hp, tile, tile), BF16),
                   pltpu.VMEM((qt, hp, FOX_HEAD_DIM, tile), F32)]
    else:
        body = functools.partial(_fox_online_kernel, tile=tile, heads=hp)
        names = ("table", "q", "k", "v", "g")
        args = (edges[:, :, 0, :FOX_HEADS].reshape(-1), qa, ka, vt, g)
        scratch = []
    return pl.pallas_call(
        body,
        grid=(b, n_heads // hp, s // rows),
        in_specs=[specs[n] for n in names],
        out_specs=specs["g"],
        out_shape=jax.ShapeDtypeStruct((b, s, w), BF16),
        scratch_shapes=scratch,
        compiler_params=_params("arbitrary", "arbitrary", "arbitrary"),
        name="fox_attention_bounded" if bounded else "fox_attention_online",
    )(*args)


def _out_proj_kernel(a_ref, x_ref, mod_ref, w_ref, o_ref):
    y = jnp.dot(a_ref[0], w_ref[...], preferred_element_type=F32)
    o_ref[0] = x_ref[0] + mod_ref[0, 2:3, :] * y


def _out_proj(a, x, mod, w_out):
    b, s, d = x.shape
    tile = OUT_PROJ_TILE
    k = a.shape[-1]
    return pl.pallas_call(
        _out_proj_kernel,
        grid=(b, s // tile),
        in_specs=[pl.BlockSpec((1, tile, k), lambda bi, ti: (bi, ti, 0)),
                  pl.BlockSpec((1, tile, d), lambda bi, ti: (bi, ti, 0)),
                  pl.BlockSpec((1, 3, d), lambda bi, ti: (bi, 0, 0)),
                  pl.BlockSpec((k, d), lambda bi, ti: (0, 0))],
        out_specs=pl.BlockSpec((1, tile, d), lambda bi, ti: (bi, ti, 0)),
        out_shape=jax.ShapeDtypeStruct((b, s, d), F32),
        compiler_params=_params("arbitrary", "arbitrary"),
        name="layer1_out_proj",
    )(a, x, mod, w_out.astype(BF16))


def _layer1(x, mod, norm_g, w_in, b_f, qnorm_g, knorm_g, w_out):
    d, w = x.shape[-1], FOX_WIDTH
    order = jnp.argsort(b_f)
    w_in = w_in.astype(BF16)
    w_heads = w_in[:, :4 * w].reshape(d, 4, FOX_HEADS, FOX_HEAD_DIM)[:, :, order]
    w_forget = jnp.zeros((d, LANES), BF16).at[:, :FOX_HEADS].set(
        w_in[:, 4 * w:][:, order])
    w_in = jnp.concatenate([w_heads.reshape(d, 4 * w), w_forget], axis=1)
    w_out = w_out.astype(BF16).reshape(FOX_HEADS, FOX_HEAD_DIM, d)[order].reshape(w, d)
    qa, ka, vt, g, cq, edges = _layer1_proj(x, mod, norm_g, w_in, b_f[order], qnorm_g, knorm_g)
    score_bound = (LOG2E * FOX_HEAD_DIM ** 0.5
                   * jnp.max(jnp.abs(qnorm_g)) * jnp.max(jnp.abs(knorm_g)))
    gated = lax.cond(score_bound <= FOX_SAFE_LOGIT,
                     functools.partial(_fox_attention, bounded=True),
                     functools.partial(_fox_attention, bounded=False),
                     qa, ka, vt, g, cq, edges)
    return _out_proj(gated, x, mod, w_out)


def kernel(x, c, norm_g, ada_w, ada_b, hgrn_lb, even_w_in, hgrn_onorm_g, pool_w, pool_scale,
           even_w_out, odd_w_in, fox_b_f, fox_qnorm_g, fox_knorm_g, odd_w_out):
    depth = norm_g.shape[0]
    mods = _adaln_mods(c, ada_w, ada_b)
    for l in range(depth):
        j = l // 2
        if l % 2 == 0:
            x = _layer0(x, mods[l], norm_g[l], even_w_in[j], hgrn_lb, hgrn_onorm_g[j],
                        pool_w[j], pool_scale[j], even_w_out[j], layer_slot=l)
        else:
            x = _layer1(x, mods[l], norm_g[l], odd_w_in[j], fox_b_f[j], fox_qnorm_g[j],
                        fox_knorm_g[j], odd_w_out[j])
    return x
```

```python
import functools

import jax
import jax.numpy as jnp
import numpy as np
from jax import lax
from jax.experimental import pallas as pl
from jax.experimental.pallas import tpu as pltpu

F32 = jnp.float32
BF16 = jnp.bfloat16
EPS = 1e-6

HGRN_HEADS = 4
HGRN_KEY = 128
HGRN_VAL = 128
HGRN_KW = HGRN_HEADS * HGRN_KEY
HGRN_VW = HGRN_HEADS * HGRN_VAL
POOL_WINDOWS = (2, 4, 8, 16)
POOL_GROUP = 128
POOL_WIDTH = POOL_GROUP * len(POOL_WINDOWS)
POOL_HISTORY = 16
FOX_HEADS = 16
FOX_HEAD_DIM = 64
FOX_WIDTH = FOX_HEADS * FOX_HEAD_DIM

LANES = 128
SUBLANES = 8
VMEM_LIMIT_BYTES = 56 * 1024 * 1024

SEQ_TILE = 256
OUT_PROJ_TILE = 2048
MOD_READ_STREAMS = 4
HGRN_CHUNK = 128
ATTN_Q_TILE = 512
ATTN_HEADS = 4
ATTN_TILES_PER_STEP = 2
FOX_SAFE_LOGIT = 100.0
FOX_DEAD_EXPONENT = 150.0
LOG2E = 1.4426950408889634
MASK_VALUE = -1e30
AUG_LANE = FOX_HEAD_DIM


def _sigmoid(x):
    return 0.5 * jnp.tanh(0.5 * x) + 0.5


def _silu(x):
    return x * _sigmoid(x)


def _params(*semantics):
    return pltpu.CompilerParams(dimension_semantics=semantics, vmem_limit_bytes=VMEM_LIMIT_BYTES)


def _mod_kernel(c_ref, *refs):
    *w_refs, b_ref, o_ref, cond_ref = refs

    @pl.when((pl.program_id(0) == 0) & (pl.program_id(1) == 0))
    def _():
        cond_ref[...] = _silu(c_ref[...])

    band = w_refs[0].shape[1]
    for bi in range(c_ref.shape[0]):
        acc = b_ref[0]
        for r, w_ref in enumerate(w_refs):
            w = w_ref[0]
            cond = cond_ref[bi, r * band:(r + 1) * band]
            acc = acc + jnp.concatenate(
                [jnp.sum(w[:, j:j + LANES] * cond, axis=0, keepdims=True)
                 for j in range(0, w.shape[1], LANES)], axis=1)
        o_ref[0, bi:bi + 1, :] = acc


def _adaln_mods(c, ada_w, ada_b):
    depth, d, n = ada_w.shape
    b = c.shape[0]
    tn, bands = 1024, MOD_READ_STREAMS
    band_spec = lambda r: pl.BlockSpec((1, d // bands, tn), lambda l, j: (l, r, j))
    out = pl.pallas_call(
        _mod_kernel,
        grid=(depth, n // tn),
        in_specs=[pl.BlockSpec((b, d, LANES), lambda l, j: (0, 0, 0))]
                 + [band_spec(r) for r in range(bands)]
                 + [pl.BlockSpec((1, 1, tn), lambda l, j: (l, 0, j))],
        out_specs=pl.BlockSpec((1, b, tn), lambda l, j: (l, 0, j)),
        out_shape=jax.ShapeDtypeStruct((depth, b, n), F32),
        scratch_shapes=[pltpu.VMEM((b, d, LANES), F32)],
        compiler_params=_params("arbitrary", "arbitrary"),
        name="adaln_mods",
    )(jnp.broadcast_to(c[:, :, None], (b, d, LANES)), *([ada_w] * bands),
      ada_b.reshape(depth, 1, n))
    return out.reshape(depth, b, 3, d)


def _modulated_norm(x, shift, scale, gain):
    ms = jnp.mean(x * x, axis=-1, keepdims=True)
    return x * lax.rsqrt(ms + EPS) * gain * (1.0 + scale) + shift


def _layer0_kernel(x_ref, mod_ref, ng_ref, win32_ref, lb_ref, og_ref, pw_ref, ps_ref, wout32_ref,
                   o_ref, state_ref, carry_ref, win_ref, wout_ref, *, tile, chunk, layer_slot):
    T, C, H = tile, chunk, HGRN_HEADS
    batch = range(x_ref.shape[0])
    t_idx = pl.program_id(0)

    @pl.when(t_idx == 0)
    def _():
        state_ref[...] = jnp.zeros_like(state_ref)
        carry_ref[...] = jnp.zeros_like(carry_ref)
        win_ref[...] = win32_ref[...].astype(BF16)
        wout_ref[...] = wout32_ref[...].astype(BF16)

    lbv = lb_ref[...]
    e = jnp.exp(lbv - jnp.max(lbv, axis=0, keepdims=True))
    lower = (jnp.sum(e[0:layer_slot + 1], axis=0, keepdims=True)
             / jnp.sum(e, axis=0, keepdims=True))

    xs, parts = [], []
    for bi in batch:
        x = x_ref[bi]
        h = _modulated_norm(x, mod_ref[bi, 0:1, :], mod_ref[bi, 1:2, :], ng_ref[...])
        proj = jnp.dot(h.astype(BF16), win_ref[...], preferred_element_type=F32)
        o0, widths, piece = 0, (HGRN_KW, HGRN_KW, HGRN_VW, HGRN_VW, POOL_WIDTH, POOL_WIDTH), []
        for wd in widths:
            piece.append(proj[:, o0:o0 + wd])
            o0 += wd
        xs.append(x)
        parts.append(piece)

    row = lax.broadcasted_iota(jnp.int32, (T, HGRN_KW), 0)
    ti = lax.broadcasted_iota(jnp.int32, (C, C), 0)
    si = lax.broadcasted_iota(jnp.int32, (C, C), 1)
    n_chunks = T // C
    nt_dims = (((1,), (1,)), ((), ()))
    tn_dims = (((0,), (0,)), ((), ()))

    def level_scores(qe, ke, mask, scores):
        qb = qe.astype(BF16)
        kb = ke.astype(BF16)
        out = []
        for c in range(n_chunks):
            for hh in range(H):
                rs = slice(c * C, (c + 1) * C)
                cs = slice(hh * HGRN_KEY, (hh + 1) * HGRN_KEY)
                d = lax.dot_general(qb[rs, cs], kb[rs, cs], nt_dims, preferred_element_type=F32)
                prev = scores[c * H + hh]
                out.append(jnp.where(mask, d, 0.0 if prev is None else prev))
        return out

    scores, q_dec, k_dec, total = [], [], [], []
    for bi in batch:
        q, f = parts[bi][0], parts[bi][1]
        forget = lower + (1.0 - lower) * _sigmoid(f)
        key = 1.0 - forget
        scores.append(level_scores(q, key, ti == si, [None] * (n_chunks * H)))
        q_dec.append(q * forget)
        k_dec.append(key)
        total.append(forget)
    m = 1
    while m < C:
        mask = ((ti ^ si) < 2 * m) & ((ti & m) != 0) & ((si & m) == 0)
        upper = (row & m) != 0
        for bi in batch:
            scores[bi] = level_scores(q_dec[bi], k_dec[bi], mask, scores[bi])
            t_dn = pltpu.roll(total[bi], m, 0)
            t_up = pltpu.roll(total[bi], T - m, 0)
            q_dec[bi] = q_dec[bi] * jnp.where(upper, t_dn, 1.0)
            k_dec[bi] = k_dec[bi] * jnp.where(upper, 1.0, t_up)
            total[bi] = total[bi] * jnp.where(upper, t_dn, t_up)
        m *= 2

    o_a = []
    for bi in batch:
        val, g_a = parts[bi][2], parts[bi][3]
        q_in = q_dec[bi].astype(BF16)
        k_out = k_dec[bi].astype(BF16)
        decay = total[bi]
        val_b = val.astype(BF16)
        oa_rows = []
        for c in range(n_chunks):
            rs = slice(c * C, (c + 1) * C)
            heads_out = []
            for hh in range(H):
                cs = slice(hh * HGRN_KEY, (hh + 1) * HGRN_KEY)
                st = state_ref[bi, hh]
                vb = val_b[rs, cs]
                o = jnp.dot(scores[bi][c * H + hh].astype(BF16), vb, preferred_element_type=F32)
                o = o + lax.dot_general(q_in[rs, cs], st.astype(BF16), nt_dims,
                                        preferred_element_type=F32)
                state_ref[bi, hh] = (st * decay[c * C:c * C + 1, cs]
                                     + lax.dot_general(vb, k_out[rs, cs], tn_dims,
                                                       preferred_element_type=F32))
                ms_o = jnp.mean(o * o, axis=-1, keepdims=True)
                heads_out.append(o * lax.rsqrt(ms_o + EPS))
            oa_rows.append(jnp.concatenate(heads_out, axis=1))
        o_a.append(jnp.concatenate(oa_rows, axis=0) * og_ref[...] * _silu(g_a))

    pos = lax.broadcasted_iota(jnp.int32, (T, POOL_GROUP), 0) + t_idx * T + 1
    for bi in batch:
        u, g_b = parts[bi][4], parts[bi][5]
        ub = jnp.concatenate([carry_ref[bi], u], axis=0)
        carry_ref[bi] = u[T - POOL_HISTORY:T, :]
        wins = []
        acc = ub
        sh = 1
        while sh < max(POOL_WINDOWS):
            acc = acc + pltpu.roll(acc, sh, 0)
            sh *= 2
            wins.append(acc)
        ob = []
        for gi, w in enumerate(POOL_WINDOWS):
            cs = slice(gi * POOL_GROUP, (gi + 1) * POOL_GROUP)
            win = wins[w.bit_length() - 2][POOL_HISTORY:, cs]
            cnt = jnp.minimum(pos, w).astype(F32)
            pooled = win / cnt - u[:, cs]
            ob.append(jnp.dot(pooled.astype(BF16), pw_ref[gi], preferred_element_type=F32))
        o_b = jnp.concatenate(ob, axis=1) * ps_ref[...] * _silu(g_b)
        mixed = jnp.concatenate([o_a[bi], o_b], axis=1).astype(BF16)
        y = jnp.dot(mixed, wout_ref[...], preferred_element_type=F32)
        o_ref[bi] = xs[bi] + mod_ref[bi, 2:3, :] * y


def _layer0(x, mod, norm_g, w_in, hgrn_lb, onorm_g, pool_w, pool_scale, w_out, layer_slot):
    b, s, d = x.shape
    tile = SEQ_TILE
    assert s % tile == 0 and tile % HGRN_CHUNK == 0, (s, tile)
    n_in = w_in.shape[1]
    mix = w_out.shape[0]
    const2 = lambda ti: (0, 0)
    const3 = lambda ti: (0, 0, 0)
    return pl.pallas_call(
        functools.partial(_layer0_kernel, tile=tile, chunk=HGRN_CHUNK, layer_slot=layer_slot),
        grid=(s // tile,),
        in_specs=[pl.BlockSpec((b, tile, d), lambda ti: (0, ti, 0)),
                  pl.BlockSpec((b, 3, d), const3),
                  pl.BlockSpec((1, d), const2),
                  pl.BlockSpec((d, n_in), const2, pipeline_mode=pl.Buffered(1)),
                  pl.BlockSpec(hgrn_lb.shape, const2),
                  pl.BlockSpec((1, HGRN_VW), const2),
                  pl.BlockSpec(pool_w.shape, const3),
                  pl.BlockSpec((1, POOL_WIDTH), const2),
                  pl.BlockSpec((mix, d), const2, pipeline_mode=pl.Buffered(1))],
        out_specs=pl.BlockSpec((b, tile, d), lambda ti: (0, ti, 0)),
        out_shape=jax.ShapeDtypeStruct((b, s, d), F32),
        scratch_shapes=[pltpu.VMEM((b, HGRN_HEADS, HGRN_VAL, HGRN_KEY), F32),
                        pltpu.VMEM((b, POOL_HISTORY, POOL_WIDTH), F32),
                        pltpu.VMEM((d, n_in), BF16),
                        pltpu.VMEM((mix, d), BF16)],
        compiler_params=_params("arbitrary"),
        name="layer0_hgrn_pool",
    )(x, mod, norm_g.reshape(1, d), w_in, hgrn_lb, onorm_g.reshape(1, HGRN_VW),
      pool_w.astype(BF16), pool_scale.reshape(1, POOL_WIDTH), w_out)


def _split3(c):
    hi = c.astype(BF16).astype(F32)
    r = c - hi
    mid = r.astype(BF16).astype(F32)
    lo = (r - mid).astype(BF16).astype(F32)
    return hi, mid, lo


def _carrier_selectors():
    per_tile = LANES // FOX_HEAD_DIM
    sel = np.zeros((FOX_HEADS // per_tile, LANES, per_tile * LANES), np.float32)
    for hd in range(FOX_HEADS):
        base = (hd % per_tile) * LANES + AUG_LANE
        for j in range(3):
            sel[hd // per_tile, j * FOX_HEADS + hd, base + j] = 1.0
            sel[hd // per_tile, j * FOX_HEADS + hd, base + 3 + j] = -1.0
            sel[hd // per_tile, (3 + j) * FOX_HEADS + hd, base + 6 + j] = -1.0
    return jnp.asarray(sel, BF16)


def _layer1_proj_kernel(x_ref, mod_ref, ng_ref, win_ref, bf_ref, qg_ref, kg_ref, bd_ref, sel_ref,
                        qa_ref, ka_ref, vt_ref, g_ref, cq_ref, edge_ref, carry_ref, tile_start_ref,
                        *, tile):
    T, W = tile, FOX_WIDTH
    batch = range(x_ref.shape[0])
    t_idx = pl.program_id(0)
    lane = lax.broadcasted_iota(jnp.int32, (T, LANES), 1)
    row = lax.broadcasted_iota(jnp.int32, (T, LANES), 0)
    bd = bd_ref[...]
    qk_scale = FOX_HEAD_DIM ** -0.5 * LOG2E

    hs = [_modulated_norm(x_ref[bi], mod_ref[bi, 0:1, :], mod_ref[bi, 1:2, :],
                          ng_ref[...]).astype(BF16) for bi in batch]
    ones_q = jnp.where((lane >= AUG_LANE + 3) & (lane < AUG_LANE + 9), 1.0, 0.0)
    ones_k = jnp.where((lane >= AUG_LANE) & (lane < AUG_LANE + 3), 1.0, 0.0)
    forget_logits = [jnp.dot(hs[bi], win_ref[:, 4 * W:], preferred_element_type=F32)
                     for bi in batch]

    def big_product(bi):
        return jnp.dot(hs[bi], win_ref[:, :4 * W], preferred_element_type=F32)

    projs, carriers_packed = [], []
    for bi in batch:
        z = forget_logits[bi] + bf_ref[...]
        logf = jnp.minimum(z, 0.0) - jnp.log1p(jnp.exp(-jnp.abs(z)))
        cum = logf
        sh = 1
        while sh < T:
            cum = cum + jnp.where(row >= sh, pltpu.roll(cum, sh, 0), 0.0)
            sh *= 2
        start = jnp.where(t_idx == 0, 0.0, carry_ref[bi])
        end = start + cum[T - 1:T, :]
        carry_ref[bi] = end
        tile_start = jnp.where(t_idx % (ATTN_Q_TILE // T) == 0, start, tile_start_ref[bi])
        tile_start_ref[bi] = tile_start
        edge_ref[bi, 0] = jnp.concatenate([tile_start, end], axis=0) * LOG2E
        s_hi, s_mid, s_lo = _split3((cum + (start - tile_start)) * LOG2E)
        e_hi, e_mid, e_lo = _split3((cum - cum[T - 1:T, :]) * LOG2E)
        total_t = ((cum + start) * LOG2E).T
        for hd in range(FOX_HEADS):
            cq_ref[bi, hd] = total_t[hd:hd + 1, :]

        terms = (s_hi, s_mid, s_lo, e_hi, e_mid, e_lo)
        packed = jnp.zeros((T, LANES), F32)
        for j, term in enumerate(terms):
            moved = term if j == 0 else pltpu.roll(term, j * FOX_HEADS, 1)
            packed = jnp.where((lane >= j * FOX_HEADS) & (lane < (j + 1) * FOX_HEADS), moved,
                               packed)
        carriers_packed.append(packed.astype(BF16))

    def pair_inputs(bi, pair):
        cs = slice(pair * LANES, (pair + 1) * LANES)
        return projs[bi][:, 0:W][:, cs], projs[bi][:, W:2 * W][:, cs]

    def small_products(bi):
        mean_squares, placed = [], []
        for pair in range(FOX_HEADS // 2):
            qp, kp = pair_inputs(bi, pair)
            squares = jnp.concatenate([(qp * qp).astype(BF16), (kp * kp).astype(BF16)], axis=1)
            mean_squares.append(jnp.dot(squares, bd, preferred_element_type=F32))
            placed.append(jnp.dot(carriers_packed[bi], sel_ref[pair],
                                  preferred_element_type=F32))
        return mean_squares, placed

    def assemble(bi, mean_squares, placed):
        vt_ref[bi, 0] = projs[bi][:, 2 * W:3 * W].T.astype(BF16)
        g_ref[bi] = projs[bi][:, 3 * W:4 * W]
        for pair in range(FOX_HEADS // 2):
            qp, kp = pair_inputs(bi, pair)
            ms, aug = mean_squares[pair], placed[pair]
            qn = qp * lax.rsqrt(ms[:, :LANES] + EPS) * (qg_ref[...] * qk_scale)
            kn = kp * lax.rsqrt(ms[:, LANES:] + EPS) * kg_ref[...]
            for e in range(LANES // FOX_HEAD_DIM):
                hd = (LANES // FOX_HEAD_DIM) * pair + e
                carriers = aug[:, e * LANES:(e + 1) * LANES]
                q_main = qn if e == 0 else pltpu.roll(qn, FOX_HEAD_DIM, 1)
                k_main = kn if e == 0 else pltpu.roll(kn, FOX_HEAD_DIM, 1)
                q_aug = jnp.where(lane < AUG_LANE + 3, carriers, ones_q)
                k_aug = jnp.where(lane < AUG_LANE + 3, ones_k, carriers)
                qa_ref[bi, hd] = jnp.where(lane < FOX_HEAD_DIM, q_main, q_aug).astype(BF16)
                ka_ref[bi, hd] = jnp.where(lane < FOX_HEAD_DIM, k_main, k_aug).astype(BF16)

    for bi in batch:
        projs.append(big_product(bi))
        assemble(bi, *small_products(bi))


def _layer1_proj(x, mod, norm_g, w_in, b_f, qnorm_g, knorm_g):
    b, s, d = x.shape
    tile = SEQ_TILE
    assert s % tile == 0 and ATTN_Q_TILE % tile == 0, (s, tile)
    nt = s // tile
    w = FOX_WIDTH
    n_pad = w_in.shape[1]
    bf_pad = jnp.zeros((1, LANES), F32).at[0, :FOX_HEADS].set(b_f)
    blk = jnp.arange(2 * LANES) // FOX_HEAD_DIM
    bd = jnp.where(blk[:, None] == blk[None, :], 1.0 / FOX_HEAD_DIM, 0.0).astype(BF16)
    qg = jnp.tile(qnorm_g, LANES // FOX_HEAD_DIM).reshape(1, LANES)
    kg = jnp.tile(knorm_g, LANES // FOX_HEAD_DIM).reshape(1, LANES)
    const2 = lambda ti: (0, 0)
    const3 = lambda ti: (0, 0, 0)
    return pl.pallas_call(
        functools.partial(_layer1_proj_kernel, tile=tile),
        grid=(nt,),
        in_specs=[pl.BlockSpec((b, tile, d), lambda ti: (0, ti, 0)),
                  pl.BlockSpec((b, 3, d), const3),
                  pl.BlockSpec((1, d), const2),
                  pl.BlockSpec((d, n_pad), const2, pipeline_mode=pl.Buffered(1)),
                  pl.BlockSpec((1, LANES), const2),
                  pl.BlockSpec((1, LANES), const2),
                  pl.BlockSpec((1, LANES), const2),
                  pl.BlockSpec((2 * LANES, 2 * LANES), const2),
                  pl.BlockSpec((FOX_HEADS // 2, LANES, 2 * LANES), const3)],
        out_specs=[pl.BlockSpec((b, FOX_HEADS, tile, LANES), lambda ti: (0, 0, ti, 0)),
                   pl.BlockSpec((b, FOX_HEADS, tile, LANES), lambda ti: (0, 0, ti, 0)),
                   pl.BlockSpec((b, 1, w, tile), lambda ti: (0, ti, 0, 0)),
                   pl.BlockSpec((b, tile, w), lambda ti: (0, ti, 0)),
                   pl.BlockSpec((b, FOX_HEADS, 1, tile), lambda ti: (0, 0, 0, ti)),
                   pl.BlockSpec((b, 1, 2, LANES), lambda ti: (0, ti, 0, 0))],
        out_shape=[jax.ShapeDtypeStruct((b, FOX_HEADS, s, LANES), BF16),
                   jax.ShapeDtypeStruct((b, FOX_HEADS, s, LANES), BF16),
                   jax.ShapeDtypeStruct((b, nt, w, tile), BF16),
                   jax.ShapeDtypeStruct((b, s, w), F32),
                   jax.ShapeDtypeStruct((b, FOX_HEADS, 1, s), F32),
                   jax.ShapeDtypeStruct((b, nt, 2, LANES), F32)],
        scratch_shapes=[pltpu.VMEM((b, 1, LANES), F32), pltpu.VMEM((b, 1, LANES), F32)],
        compiler_params=_params("arbitrary"),
        name="layer1_proj",
    )(x, mod, norm_g.reshape(1, d), w_in, bf_pad, qg, kg, bd, _carrier_selectors())


def _query_operands(q):
    slab = 2 * SUBLANES
    row = lax.broadcasted_iota(jnp.int32, (slab, q.shape[0]), 0)
    q_t = q.astype(F32).T
    carriers = q_t[AUG_LANE:AUG_LANE + slab]
    return tuple(jnp.concatenate([q_t[:AUG_LANE], jnp.where(keep, carriers, 0.0),
                                  q_t[AUG_LANE + slab:]], axis=0).astype(BF16)
                 for keep in (row < 6, row >= 6))


def _gated_output(acc_t, g_ref, o_ref, rows):
    o_ref[0, rows] = (acc_t.T * _silu(g_ref[0, rows])).astype(BF16)


def _fox_bounded_kernel(first_ref, end_ref, qa_ref, ka_ref, vt_ref, cq_ref, g_ref, o_ref,
                        p_ref, acc_ref, *, tile, heads):
    TQ, HP = tile, heads
    QT = qa_ref.shape[2] // TQ
    SB = vt_ref.shape[-1]
    R = TQ // SB
    n_blocks = ka_ref.shape[2] // SB
    b = pl.program_id(0)
    group = pl.program_id(1)
    i0 = pl.program_id(2) * QT
    chains = [(qs, e) for qs in range(QT) for e in range(HP)]
    cols = [slice(qs * TQ, (qs + 1) * TQ) for qs in range(QT)]
    operands = {(qs, e): _query_operands(qa_ref[0, e, cols[qs], :]) for qs, e in chains}
    causal = (lax.broadcasted_iota(jnp.int32, (TQ, TQ), 0)
              <= lax.broadcasted_iota(jnp.int32, (TQ, TQ), 1))

    def keys(e, n):
        return ka_ref[0, e, pl.ds(pl.multiple_of(n * TQ, TQ), TQ), :]

    def values_t(e, blk):
        return vt_ref[0, blk, pl.ds(e * FOX_HEAD_DIM, FOX_HEAD_DIM), :]

    def rest(c, blk, gate=1.0):
        qs, e = c
        end = end_ref[(b * n_blocks + blk) * FOX_HEADS + HP * group + e]
        return jnp.exp2(jnp.minimum(cq_ref[0, e][:, cols[qs]] - end, 0.0)) * gate

    def col_partial(p):
        return jnp.sum(p.reshape(p.shape[0] // SUBLANES, SUBLANES, p.shape[1]), axis=0)

    def far_scores(c, n):
        return jnp.dot(keys(c[1], n), operands[c][1], preferred_element_type=F32)

    def contract(c, n, p_bf16, gate=1.0):
        for r in range(R):
            pv = jnp.dot(values_t(c[1], n * R + r), p_bf16[r * SB:(r + 1) * SB],
                         preferred_element_type=F32)
            acc_ref[c] += rest(c, n * R + r, gate) * pv

    def weighted_partial(c, n, p, gate=1.0):
        return sum(rest(c, n * R + r, gate) * col_partial(p[r * SB:(r + 1) * SB])
                   for r in range(R))

    first = first_ref[(b * pl.num_programs(1) + group) * pl.num_programs(2) + pl.program_id(2)]
    gate0 = jnp.where(i0 > 0, 1.0, 0.0)

    l_part = {}

    def own_tile(c):
        qs, e = c
        k_own = keys(e, i0 + qs)
        pieces = []
        for r in range(R):
            n_keys, lanes = (r + 1) * SB, slice(r * SB, (r + 1) * SB)
            s_t = jnp.dot(k_own[:n_keys], operands[c][0][:, lanes], preferred_element_type=F32)
            pieces.append(jnp.where(causal[:n_keys, lanes], jnp.exp2(s_t), 0.0))
        l_part[c] = jnp.concatenate([col_partial(p) for p in pieces], axis=1)
        pieces = [p.astype(BF16) for p in pieces]

        def later():
            v_t = jnp.concatenate([values_t(e, (i0 + qs) * R + r) for r in range(R)], axis=1)
            for r, p in enumerate(pieces):
                acc_ref[qs, e, :, r * SB:(r + 1) * SB] = jnp.dot(
                    v_t[:, :(r + 1) * SB], p, preferred_element_type=F32)
        return later

    def earlier_tile(c, n):
        p = jnp.exp2(far_scores(c, n))
        l_part[c] = l_part[c] + weighted_partial(c, n, p)
        p = p.astype(BF16)
        return lambda: contract(c, n, p)

    def fill(c):
        p = jnp.exp2(far_scores(c, first))
        l_part[c] = l_part[c] + weighted_partial(c, first, p, gate0)
        p_ref[c] = p.astype(BF16)

    jobs = ([functools.partial(own_tile, c) for c in chains]
            + [functools.partial(earlier_tile, c, i0 + t) for c in chains for t in range(c[0])]
            + [functools.partial(fill, c) for c in chains])
    pending = None
    for job in jobs:
        later = job()
        if pending is not None:
            pending()
        pending = later
    if pending is not None:
        pending()

    def body(n, l_run):
        s_new = []
        for c in chains:
            s_new.append(far_scores(c, n))
            contract(c, n - 1, p_ref[c])
        out = []
        for c, s_t, l_c in zip(chains, s_new, l_run):
            p = jnp.exp2(s_t)
            out.append(l_c + weighted_partial(c, n, p))
            p_ref[c] = p.astype(BF16)
        return tuple(out)

    l_run = lax.fori_loop(first + 1, i0, body, tuple(l_part[c] for c in chains))
    for c in chains:
        contract(c, jnp.maximum(i0 - 1, 0), p_ref[c], gate0)
    for qs in range(QT):
        out_t = [acc_ref[qs, e] * (1.0 / jnp.sum(l_run[qs * HP + e], axis=0, keepdims=True))
                 for e in range(HP)]
        _gated_output(jnp.concatenate(out_t, axis=0), g_ref, o_ref, cols[qs])


def _fox_online_kernel(start_ref, qa_ref, ka_ref, vt_ref, g_ref, o_ref, *, tile, heads):
    TB, HP = tile, heads
    n_blocks = ka_ref.shape[2] // TB
    b = pl.program_id(0)
    group = pl.program_id(1)
    i = pl.program_id(2)
    q_t = [_query_operands(qa_ref[0, e])[0] for e in range(HP)]
    k_row = lax.broadcasted_iota(jnp.int32, (TB, TB), 0)
    q_col = lax.broadcasted_iota(jnp.int32, (TB, TB), 1)

    def step(j, carry, masked):
        out = []
        for e in range(HP):
            m_run, l_run, acc = carry[e]
            k_blk = ka_ref[0, e, pl.ds(pl.multiple_of(j * TB, TB), TB), :]
            s_t = jnp.dot(k_blk, q_t[e], preferred_element_type=F32)
            if masked:
                s_t = jnp.where(k_row <= q_col, s_t, MASK_VALUE)
            hd = HP * group + e
            delta = (start_ref[(b * n_blocks + i) * FOX_HEADS + hd]
                     - start_ref[(b * n_blocks + j) * FOX_HEADS + hd])
            m_new = jnp.maximum(m_run, jnp.max(s_t, axis=0, keepdims=True) + delta)
            p_t = jnp.exp2(s_t - (m_new - delta))
            alpha = jnp.exp2(m_run - m_new)
            l_new = alpha * l_run + jnp.sum(p_t, axis=0, keepdims=True)
            v_t = vt_ref[0, j, pl.ds(e * FOX_HEAD_DIM, FOX_HEAD_DIM), :]
            acc_new = alpha * acc + jnp.dot(v_t, p_t.astype(BF16), preferred_element_type=F32)
            out.append((m_new, l_new, acc_new))
        return tuple(out)

    init = (jnp.full((1, TB), MASK_VALUE, F32), jnp.zeros((1, TB), F32),
            jnp.zeros((FOX_HEAD_DIM, TB), F32))
    carry = lax.fori_loop(0, i, lambda j, c: step(j, c, False), (init,) * HP)
    carry = step(i, carry, True)
    _gated_output(jnp.concatenate([acc / l_run for _, l_run, acc in carry], axis=0), g_ref, o_ref,
                  slice(None))


def _fox_attention(qa, ka, vt, g, cq, edges, *, bounded):
    b, n_heads, s, _ = qa.shape
    _, nvb, w, vb = vt.shape
    if bounded:
        tile, hp, qt = ATTN_Q_TILE, ATTN_HEADS, ATTN_TILES_PER_STEP
    else:
        tile, hp, qt = vb, LANES // FOX_HEAD_DIM, 1
    gw = hp * FOX_HEAD_DIM
    rows = qt * tile
    assert s % rows == 0 and n_heads % hp == 0 and tile % vb == 0, (s, rows, n_heads, hp, tile, vb)
    specs = dict(
        table=pl.BlockSpec(memory_space=pltpu.SMEM),
        q=pl.BlockSpec((1, hp, rows, LANES), lambda bi, p, i: (bi, p, i, 0)),
        k=pl.BlockSpec((1, hp, s, LANES), lambda bi, p, i: (bi, p, 0, 0)),
        v=pl.BlockSpec((1, nvb, gw, vb), lambda bi, p, i: (bi, 0, p, 0)),
        cq=pl.BlockSpec((1, hp, 1, rows), lambda bi, p, i: (bi, p, 0, i)),
        g=pl.BlockSpec((1, rows, gw), lambda bi, p, i: (bi, i, p)))
    if bounded:
        body = functools.partial(_fox_bounded_kernel, tile=tile, heads=hp)
        names = ("table", "table", "q", "k", "v", "cq", "g")
        step_start = edges[:, ::rows // vb, 0, :FOX_HEADS]
        tile_end = edges[:, tile // vb - 1::tile // vb, 1, :FOX_HEADS]
        dead = (step_start[:, :, None] - tile_end[:, None]) < -FOX_DEAD_EXPONENT
        before = jnp.arange(s // rows)[:, None] * qt > jnp.arange(s // tile)[None, :]
        first_live = jnp.sum(dead & before[None, :, :, None], axis=2)
        last_before = jnp.maximum(jnp.arange(s // rows) * qt - 1, 0)
        first_live = jnp.minimum(first_live, last_before[None, :, None])
        first_live = jnp.min(first_live.reshape(b, s // rows, n_heads // hp, hp), axis=-1)
        args = (first_live.transpose(0, 2, 1).reshape(-1).astype(jnp.int32),
                edges[:, :, 1, :FOX_HEADS].reshape(-1), qa, ka, vt, cq, g)
        scratch = [pltpu.VMEM((qt, hp, tile, tile), BF16),
                   pltpu.VMEM((qt, hp, FOX_HEAD_DIM, tile), F32)]
    else:
        body = functools.partial(_fox_online_kernel, tile=tile, heads=hp)
        names = ("table", "q", "k", "v", "g")
        args = (edges[:, :, 0, :FOX_HEADS].reshape(-1), qa, ka, vt, g)
        scratch = []
    return pl.pallas_call(
        body,
        grid=(b, n_heads // hp, s // rows),
        in_specs=[specs[n] for n in names],
        out_specs=specs["g"],
        out_shape=jax.ShapeDtypeStruct((b, s, w), BF16),
        scratch_shapes=scratch,
        compiler_params=_params("arbitrary", "arbitrary", "arbitrary"),
        name="fox_attention_bounded" if bounded else "fox_attention_online",
    )(*args)


def _out_proj_kernel(a_ref, x_ref, mod_ref, w_ref, o_ref):
    y = jnp.dot(a_ref[0], w_ref[...], preferred_element_type=F32)
    o_ref[0] = x_ref[0] + mod_ref[0, 2:3, :] * y


def _out_proj(a, x, mod, w_out):
    b, s, d = x.shape
    tile = OUT_PROJ_TILE
    assert s % tile == 0, (s, tile)
    k = a.shape[-1]
    return pl.pallas_call(
        _out_proj_kernel,
        grid=(b, s // tile),
        in_specs=[pl.BlockSpec((1, tile, k), lambda bi, ti: (bi, ti, 0)),
                  pl.BlockSpec((1, tile, d), lambda bi, ti: (bi, ti, 0)),
                  pl.BlockSpec((1, 3, d), lambda bi, ti: (bi, 0, 0)),
                  pl.BlockSpec((k, d), lambda bi, ti: (0, 0))],
        out_specs=pl.BlockSpec((1, tile, d), lambda bi, ti: (bi, ti, 0)),
        out_shape=jax.ShapeDtypeStruct((b, s, d), F32),
        compiler_params=_params("arbitrary", "arbitrary"),
        name="layer1_out_proj",
    )(a, x, mod, w_out.astype(BF16))


def _layer1(x, mod, norm_g, w_in, b_f, qnorm_g, knorm_g, w_out):
    d, w = x.shape[-1], FOX_WIDTH
    order = jnp.argsort(b_f)
    w_in = w_in.astype(BF16)
    w_heads = w_in[:, :4 * w].reshape(d, 4, FOX_HEADS, FOX_HEAD_DIM)[:, :, order]
    w_forget = jnp.zeros((d, LANES), BF16).at[:, :FOX_HEADS].set(
        w_in[:, 4 * w:][:, order])
    w_in = jnp.concatenate([w_heads.reshape(d, 4 * w), w_forget], axis=1)
    w_out = w_out.astype(BF16).reshape(FOX_HEADS, FOX_HEAD_DIM, d)[order].reshape(w, d)
    qa, ka, vt, g, cq, edges = _layer1_proj(x, mod, norm_g, w_in, b_f[order], qnorm_g, knorm_g)
    score_bound = (LOG2E * FOX_HEAD_DIM ** 0.5
                   * jnp.max(jnp.abs(qnorm_g)) * jnp.max(jnp.abs(knorm_g)))
    gated = lax.cond(score_bound <= FOX_SAFE_LOGIT,
                     functools.partial(_fox_attention, bounded=True),
                     functools.partial(_fox_attention, bounded=False),
                     qa, ka, vt, g, cq, edges)
    return _out_proj(gated, x, mod, w_out)


def kernel(x, c, norm_g, ada_w, ada_b, hgrn_lb, even_w_in, hgrn_onorm_g, pool_w, pool_scale,
           even_w_out, odd_w_in, fox_b_f, fox_qnorm_g, fox_knorm_g, odd_w_out):
    depth = norm_g.shape[0]
    mods = _adaln_mods(c, ada_w, ada_b)
    for l in range(depth):
        j = l // 2
        if l % 2 == 0:
            x = _layer0(x, mods[l], norm_g[l], even_w_in[j], hgrn_lb, hgrn_onorm_g[j],
                        pool_w[j], pool_scale[j], even_w_out[j], layer_slot=l)
        else:
            x = _layer1(x, mods[l], norm_g[l], odd_w_in[j], fox_b_f[j], fox_qnorm_g[j],
                        fox_knorm_g[j], odd_w_out[j])
    return x
```

```python
import functools

import jax
import jax.numpy as jnp
import numpy as np
from jax import lax
from jax.experimental import pallas as pl
from jax.experimental.pallas import tpu as pltpu

F32 = jnp.float32
BF16 = jnp.bfloat16
EPS = 1e-6

HGRN_HEADS = 4
HGRN_KEY = 128
HGRN_VAL = 128
HGRN_KW = HGRN_HEADS * HGRN_KEY
HGRN_VW = HGRN_HEADS * HGRN_VAL
POOL_WINDOWS = (2, 4, 8, 16)
POOL_GROUP = 128
POOL_WIDTH = POOL_GROUP * len(POOL_WINDOWS)
POOL_HISTORY = 16
FOX_HEADS = 16
FOX_HEAD_DIM = 64
FOX_WIDTH = FOX_HEADS * FOX_HEAD_DIM

LANES = 128
SUBLANES = 8
VMEM_LIMIT_BYTES = 56 * 1024 * 1024

SEQ_TILE = 256
OUT_PROJ_TILE = 2048
MOD_READ_STREAMS = 4
HGRN_CHUNK = 128
ATTN_Q_TILE = 512
ATTN_HEADS = 4
ATTN_TILES_PER_STEP = 2
FOX_SAFE_LOGIT = 100.0
FOX_DEAD_EXPONENT = 150.0
LOG2E = 1.4426950408889634
MASK_VALUE = -1e30
AUG_LANE = FOX_HEAD_DIM


def _sigmoid(x):
    return 0.5 * jnp.tanh(0.5 * x) + 0.5


def _silu(x):
    return x * _sigmoid(x)


def _params(*semantics):
    return pltpu.CompilerParams(dimension_semantics=semantics, vmem_limit_bytes=VMEM_LIMIT_BYTES)


def _mod_kernel(c_ref, *refs):
    *w_refs, b_ref, o_ref, cond_ref = refs

    @pl.when((pl.program_id(0) == 0) & (pl.program_id(1) == 0))
    def _():
        cond_ref[...] = _silu(c_ref[...])

    band = w_refs[0].shape[1]
    for bi in range(c_ref.shape[0]):
        acc = b_ref[0]
        for r, w_ref in enumerate(w_refs):
            w = w_ref[0]
            cond = cond_ref[bi, r * band:(r + 1) * band]
            acc = acc + jnp.concatenate(
                [jnp.sum(w[:, j:j + LANES] * cond, axis=0, keepdims=True)
                 for j in range(0, w.shape[1], LANES)], axis=1)
        o_ref[0, bi:bi + 1, :] = acc


def _adaln_mods(c, ada_w, ada_b):
    depth, d, n = ada_w.shape
    b = c.shape[0]
    tn, bands = 1024, MOD_READ_STREAMS
    band_spec = lambda r: pl.BlockSpec((1, d // bands, tn), lambda l, j: (l, r, j))
    out = pl.pallas_call(
        _mod_kernel,
        grid=(depth, n // tn),
        in_specs=[pl.BlockSpec((b, d, LANES), lambda l, j: (0, 0, 0))]
                 + [band_spec(r) for r in range(bands)]
                 + [pl.BlockSpec((1, 1, tn), lambda l, j: (l, 0, j))],
        out_specs=pl.BlockSpec((1, b, tn), lambda l, j: (l, 0, j)),
        out_shape=jax.ShapeDtypeStruct((depth, b, n), F32),
        scratch_shapes=[pltpu.VMEM((b, d, LANES), F32)],
        compiler_params=_params("arbitrary", "arbitrary"),
        name="adaln_mods",
    )(jnp.broadcast_to(c[:, :, None], (b, d, LANES)), *([ada_w] * bands),
      ada_b.reshape(depth, 1, n))
    return out.reshape(depth, b, 3, d)


def _modulated_norm(x, shift, scale, gain):
    ms = jnp.mean(x * x, axis=-1, keepdims=True)
    return x * lax.rsqrt(ms + EPS) * gain * (1.0 + scale) + shift


def _layer0_kernel(x_ref, mod_ref, ng_ref, win32_ref, lb_ref, og_ref, pw_ref, ps_ref, wout32_ref,
                   o_ref, state_ref, carry_ref, win_ref, wout_ref, *, tile, chunk, layer_slot):
    T, C, H = tile, chunk, HGRN_HEADS
    batch = range(x_ref.shape[0])
    t_idx = pl.program_id(0)

    @pl.when(t_idx == 0)
    def _():
        state_ref[...] = jnp.zeros_like(state_ref)
        carry_ref[...] = jnp.zeros_like(carry_ref)
        win_ref[...] = win32_ref[...].astype(BF16)
        wout_ref[...] = wout32_ref[...].astype(BF16)

    lbv = lb_ref[...]
    e = jnp.exp(lbv - jnp.max(lbv, axis=0, keepdims=True))
    lower = (jnp.sum(e[0:layer_slot + 1], axis=0, keepdims=True)
             / jnp.sum(e, axis=0, keepdims=True))

    xs, parts = [], []
    for bi in batch:
        x = x_ref[bi]
        h = _modulated_norm(x, mod_ref[bi, 0:1, :], mod_ref[bi, 1:2, :], ng_ref[...])
        proj = jnp.dot(h.astype(BF16), win_ref[...], preferred_element_type=F32)
        o0, widths, piece = 0, (HGRN_KW, HGRN_KW, HGRN_VW, HGRN_VW, POOL_WIDTH, POOL_WIDTH), []
        for wd in widths:
            piece.append(proj[:, o0:o0 + wd])
            o0 += wd
        xs.append(x)
        parts.append(piece)

    row = lax.broadcasted_iota(jnp.int32, (T, HGRN_KW), 0)
    ti = lax.broadcasted_iota(jnp.int32, (C, C), 0)
    si = lax.broadcasted_iota(jnp.int32, (C, C), 1)
    n_chunks = T // C
    nt_dims = (((1,), (1,)), ((), ()))
    tn_dims = (((0,), (0,)), ((), ()))

    def level_scores(qe, ke, mask, scores):
        qb = qe.astype(BF16)
        kb = ke.astype(BF16)
        out = []
        for c in range(n_chunks):
            for hh in range(H):
                rs = slice(c * C, (c + 1) * C)
                cs = slice(hh * HGRN_KEY, (hh + 1) * HGRN_KEY)
                d = lax.dot_general(qb[rs, cs], kb[rs, cs], nt_dims, preferred_element_type=F32)
                prev = scores[c * H + hh]
                out.append(jnp.where(mask, d, 0.0 if prev is None else prev))
        return out

    scores, q_dec, k_dec, total = [], [], [], []
    for bi in batch:
        q, f = parts[bi][0], parts[bi][1]
        forget = lower + (1.0 - lower) * _sigmoid(f)
        key = 1.0 - forget
        scores.append(level_scores(q, key, ti == si, [None] * (n_chunks * H)))
        q_dec.append(q * forget)
        k_dec.append(key)
        total.append(forget)
    m = 1
    while m < C:
        mask = ((ti ^ si) < 2 * m) & ((ti & m) != 0) & ((si & m) == 0)
        upper = (row & m) != 0
        for bi in batch:
            scores[bi] = level_scores(q_dec[bi], k_dec[bi], mask, scores[bi])
            t_dn = pltpu.roll(total[bi], m, 0)
            t_up = pltpu.roll(total[bi], T - m, 0)
            q_dec[bi] = q_dec[bi] * jnp.where(upper, t_dn, 1.0)
            k_dec[bi] = k_dec[bi] * jnp.where(upper, 1.0, t_up)
            total[bi] = total[bi] * jnp.where(upper, t_dn, t_up)
        m *= 2

    o_a = []
    for bi in batch:
        val, g_a = parts[bi][2], parts[bi][3]
        q_in = q_dec[bi].astype(BF16)
        k_out = k_dec[bi].astype(BF16)
        decay = total[bi]
        val_b = val.astype(BF16)
        oa_rows = []
        for c in range(n_chunks):
            rs = slice(c * C, (c + 1) * C)
            heads_out = []
            for hh in range(H):
                cs = slice(hh * HGRN_KEY, (hh + 1) * HGRN_KEY)
                st = state_ref[bi, hh]
                vb = val_b[rs, cs]
                o = jnp.dot(scores[bi][c * H + hh].astype(BF16), vb, preferred_element_type=F32)
                o = o + lax.dot_general(q_in[rs, cs], st.astype(BF16), nt_dims,
                                        preferred_element_type=F32)
                state_ref[bi, hh] = (st * decay[c * C:c * C + 1, cs]
                                     + lax.dot_general(vb, k_out[rs, cs], tn_dims,
                                                       preferred_element_type=F32))
                ms_o = jnp.mean(o * o, axis=-1, keepdims=True)
                heads_out.append(o * lax.rsqrt(ms_o + EPS))
            oa_rows.append(jnp.concatenate(heads_out, axis=1))
        o_a.append(jnp.concatenate(oa_rows, axis=0) * og_ref[...] * _silu(g_a))

    pos = lax.broadcasted_iota(jnp.int32, (T, POOL_GROUP), 0) + t_idx * T + 1
    for bi in batch:
        u, g_b = parts[bi][4], parts[bi][5]
        ub = jnp.concatenate([carry_ref[bi], u], axis=0)
        carry_ref[bi] = u[T - POOL_HISTORY:T, :]
        wins = []
        acc = ub
        sh = 1
        while sh < max(POOL_WINDOWS):
            acc = acc + pltpu.roll(acc, sh, 0)
            sh *= 2
            wins.append(acc)
        ob = []
        for gi, w in enumerate(POOL_WINDOWS):
            cs = slice(gi * POOL_GROUP, (gi + 1) * POOL_GROUP)
            win = wins[w.bit_length() - 2][POOL_HISTORY:, cs]
            cnt = jnp.minimum(pos, w).astype(F32)
            pooled = win / cnt - u[:, cs]
            ob.append(jnp.dot(pooled.astype(BF16), pw_ref[gi], preferred_element_type=F32))
        o_b = jnp.concatenate(ob, axis=1) * ps_ref[...] * _silu(g_b)
        mixed = jnp.concatenate([o_a[bi], o_b], axis=1).astype(BF16)
        y = jnp.dot(mixed, wout_ref[...], preferred_element_type=F32)
        o_ref[bi] = xs[bi] + mod_ref[bi, 2:3, :] * y


def _layer0(x, mod, norm_g, w_in, hgrn_lb, onorm_g, pool_w, pool_scale, w_out, layer_slot):
    b, s, d = x.shape
    tile = SEQ_TILE
    assert s % tile == 0 and tile % HGRN_CHUNK == 0, (s, tile)
    n_in = w_in.shape[1]
    mix = w_out.shape[0]
    const2 = lambda ti: (0, 0)
    const3 = lambda ti: (0, 0, 0)
    return pl.pallas_call(
        functools.partial(_layer0_kernel, tile=tile, chunk=HGRN_CHUNK, layer_slot=layer_slot),
        grid=(s // tile,),
        in_specs=[pl.BlockSpec((b, tile, d), lambda ti: (0, ti, 0)),
                  pl.BlockSpec((b, 3, d), const3),
                  pl.BlockSpec((1, d), const2),
                  pl.BlockSpec((d, n_in), const2, pipeline_mode=pl.Buffered(1)),
                  pl.BlockSpec(hgrn_lb.shape, const2),
                  pl.BlockSpec((1, HGRN_VW), const2),
                  pl.BlockSpec(pool_w.shape, const3),
                  pl.BlockSpec((1, POOL_WIDTH), const2),
                  pl.BlockSpec((mix, d), const2, pipeline_mode=pl.Buffered(1))],
        out_specs=pl.BlockSpec((b, tile, d), lambda ti: (0, ti, 0)),
        out_shape=jax.ShapeDtypeStruct((b, s, d), F32),
        scratch_shapes=[pltpu.VMEM((b, HGRN_HEADS, HGRN_VAL, HGRN_KEY), F32),
                        pltpu.VMEM((b, POOL_HISTORY, POOL_WIDTH), F32),
                        pltpu.VMEM((d, n_in), BF16),
                        pltpu.VMEM((mix, d), BF16)],
        compiler_params=_params("arbitrary"),
        name="layer0_hgrn_pool",
    )(x, mod, norm_g.reshape(1, d), w_in, hgrn_lb, onorm_g.reshape(1, HGRN_VW),
      pool_w.astype(BF16), pool_scale.reshape(1, POOL_WIDTH), w_out)


def _split3(c):
    hi = c.astype(BF16).astype(F32)
    r = c - hi
    mid = r.astype(BF16).astype(F32)
    lo = (r - mid).astype(BF16).astype(F32)
    return hi, mid, lo


def _carrier_selectors():
    per_tile = LANES // FOX_HEAD_DIM
    sel = np.zeros((FOX_HEADS // per_tile, LANES, per_tile * LANES), np.float32)
    for hd in range(FOX_HEADS):
        base = (hd % per_tile) * LANES + AUG_LANE
        for j in range(3):
            sel[hd // per_tile, j * FOX_HEADS + hd, base + j] = 1.0
            sel[hd // per_tile, j * FOX_HEADS + hd, base + 3 + j] = -1.0
            sel[hd // per_tile, (3 + j) * FOX_HEADS + hd, base + 6 + j] = -1.0
    return jnp.asarray(sel, BF16)


def _layer1_proj_kernel(x_ref, mod_ref, ng_ref, win_ref, bf_ref, qg_ref, kg_ref, bd_ref, sel_ref,
                        qa_ref, ka_ref, vt_ref, g_ref, cq_ref, edge_ref, carry_ref, tile_start_ref,
                        *, tile):
    T, W = tile, FOX_WIDTH
    batch = range(x_ref.shape[0])
    t_idx = pl.program_id(0)
    lane = lax.broadcasted_iota(jnp.int32, (T, LANES), 1)
    row = lax.broadcasted_iota(jnp.int32, (T, LANES), 0)
    bd = bd_ref[...]
    qk_scale = FOX_HEAD_DIM ** -0.5 * LOG2E

    hs = [_modulated_norm(x_ref[bi], mod_ref[bi, 0:1, :], mod_ref[bi, 1:2, :],
                          ng_ref[...]).astype(BF16) for bi in batch]
    ones_q = jnp.where((lane >= AUG_LANE + 3) & (lane < AUG_LANE + 9), 1.0, 0.0)
    ones_k = jnp.where((lane >= AUG_LANE) & (lane < AUG_LANE + 3), 1.0, 0.0)
    forget_logits = [jnp.dot(hs[bi], win_ref[:, 4 * W:], preferred_element_type=F32)
                     for bi in batch]

    def big_product(bi):
        return jnp.dot(hs[bi], win_ref[:, :4 * W], preferred_element_type=F32)

    projs, carriers_packed = [], []
    for bi in batch:
        z = forget_logits[bi] + bf_ref[...]
        logf = jnp.minimum(z, 0.0) - jnp.log1p(jnp.exp(-jnp.abs(z)))
        cum = logf
        sh = 1
        while sh < T:
            cum = cum + jnp.where(row >= sh, pltpu.roll(cum, sh, 0), 0.0)
            sh *= 2
        start = jnp.where(t_idx == 0, 0.0, carry_ref[bi])
        end = start + cum[T - 1:T, :]
        carry_ref[bi] = end
        tile_start = jnp.where(t_idx % (ATTN_Q_TILE // T) == 0, start, tile_start_ref[bi])
        tile_start_ref[bi] = tile_start
        edge_ref[bi, 0] = jnp.concatenate([tile_start, end], axis=0) * LOG2E
        s_hi, s_mid, s_lo = _split3((cum + (start - tile_start)) * LOG2E)
        e_hi, e_mid, e_lo = _split3((cum - cum[T - 1:T, :]) * LOG2E)
        total_t = ((cum + start) * LOG2E).T
        for hd in range(FOX_HEADS):
            cq_ref[bi, hd] = total_t[hd:hd + 1, :]

        terms = (s_hi, s_mid, s_lo, e_hi, e_mid, e_lo)
        packed = jnp.zeros((T, LANES), F32)
        for j, term in enumerate(terms):
            moved = term if j == 0 else pltpu.roll(term, j * FOX_HEADS, 1)
            packed = jnp.where((lane >= j * FOX_HEADS) & (lane < (j + 1) * FOX_HEADS), moved,
                               packed)
        carriers_packed.append(packed.astype(BF16))

    def pair_inputs(bi, pair):
        cs = slice(pair * LANES, (pair + 1) * LANES)
        return projs[bi][:, 0:W][:, cs], projs[bi][:, W:2 * W][:, cs]

    def small_products(bi):
        mean_squares, placed = [], []
        for pair in range(FOX_HEADS // 2):
            qp, kp = pair_inputs(bi, pair)
            squares = jnp.concatenate([(qp * qp).astype(BF16), (kp * kp).astype(BF16)], axis=1)
            mean_squares.append(jnp.dot(squares, bd, preferred_element_type=F32))
            placed.append(jnp.dot(carriers_packed[bi], sel_ref[pair],
                                  preferred_element_type=F32))
        return mean_squares, placed

    def assemble(bi, mean_squares, placed):
        vt_ref[bi, 0] = projs[bi][:, 2 * W:3 * W].T.astype(BF16)
        g_ref[bi, 0] = projs[bi][:, 3 * W:4 * W].T
        for pair in range(FOX_HEADS // 2):
            qp, kp = pair_inputs(bi, pair)
            ms, aug = mean_squares[pair], placed[pair]
            qn = qp * lax.rsqrt(ms[:, :LANES] + EPS) * (qg_ref[...] * qk_scale)
            kn = kp * lax.rsqrt(ms[:, LANES:] + EPS) * kg_ref[...]
            for e in range(LANES // FOX_HEAD_DIM):
                hd = (LANES // FOX_HEAD_DIM) * pair + e
                carriers = aug[:, e * LANES:(e + 1) * LANES]
                q_main = qn if e == 0 else pltpu.roll(qn, FOX_HEAD_DIM, 1)
                k_main = kn if e == 0 else pltpu.roll(kn, FOX_HEAD_DIM, 1)
                q_aug = jnp.where(lane < AUG_LANE + 3, carriers, ones_q)
                k_aug = jnp.where(lane < AUG_LANE + 3, ones_k, carriers)
                qa_ref[bi, hd] = jnp.where(lane < FOX_HEAD_DIM, q_main, q_aug).astype(BF16)
                ka_ref[bi, hd] = jnp.where(lane < FOX_HEAD_DIM, k_main, k_aug).astype(BF16)

    for bi in batch:
        projs.append(big_product(bi))
        assemble(bi, *small_products(bi))


def _layer1_proj(x, mod, norm_g, w_in, b_f, qnorm_g, knorm_g):
    b, s, d = x.shape
    tile = SEQ_TILE
    assert s % tile == 0 and ATTN_Q_TILE % tile == 0, (s, tile)
    nt = s // tile
    w = FOX_WIDTH
    n_pad = w_in.shape[1]
    bf_pad = jnp.zeros((1, LANES), F32).at[0, :FOX_HEADS].set(b_f)
    blk = jnp.arange(2 * LANES) // FOX_HEAD_DIM
    bd = jnp.where(blk[:, None] == blk[None, :], 1.0 / FOX_HEAD_DIM, 0.0).astype(BF16)
    qg = jnp.tile(qnorm_g, LANES // FOX_HEAD_DIM).reshape(1, LANES)
    kg = jnp.tile(knorm_g, LANES // FOX_HEAD_DIM).reshape(1, LANES)
    const2 = lambda ti: (0, 0)
    const3 = lambda ti: (0, 0, 0)
    return pl.pallas_call(
        functools.partial(_layer1_proj_kernel, tile=tile),
        grid=(nt,),
        in_specs=[pl.BlockSpec((b, tile, d), lambda ti: (0, ti, 0)),
                  pl.BlockSpec((b, 3, d), const3),
                  pl.BlockSpec((1, d), const2),
                  pl.BlockSpec((d, n_pad), const2, pipeline_mode=pl.Buffered(1)),
                  pl.BlockSpec((1, LANES), const2),
                  pl.BlockSpec((1, LANES), const2),
                  pl.BlockSpec((1, LANES), const2),
                  pl.BlockSpec((2 * LANES, 2 * LANES), const2),
                  pl.BlockSpec((FOX_HEADS // 2, LANES, 2 * LANES), const3)],
        out_specs=[pl.BlockSpec((b, FOX_HEADS, tile, LANES), lambda ti: (0, 0, ti, 0)),
                   pl.BlockSpec((b, FOX_HEADS, tile, LANES), lambda ti: (0, 0, ti, 0)),
                   pl.BlockSpec((b, 1, w, tile), lambda ti: (0, ti, 0, 0)),
                   pl.BlockSpec((b, 1, w, tile), lambda ti: (0, ti, 0, 0)),
                   pl.BlockSpec((b, FOX_HEADS, 1, tile), lambda ti: (0, 0, 0, ti)),
                   pl.BlockSpec((b, 1, 2, LANES), lambda ti: (0, ti, 0, 0))],
        out_shape=[jax.ShapeDtypeStruct((b, FOX_HEADS, s, LANES), BF16),
                   jax.ShapeDtypeStruct((b, FOX_HEADS, s, LANES), BF16),
                   jax.ShapeDtypeStruct((b, nt, w, tile), BF16),
                   jax.ShapeDtypeStruct((b, nt, w, tile), F32),
                   jax.ShapeDtypeStruct((b, FOX_HEADS, 1, s), F32),
                   jax.ShapeDtypeStruct((b, nt, 2, LANES), F32)],
        scratch_shapes=[pltpu.VMEM((b, 1, LANES), F32), pltpu.VMEM((b, 1, LANES), F32)],
        compiler_params=_params("arbitrary"),
        name="layer1_proj",
    )(x, mod, norm_g.reshape(1, d), w_in, bf_pad, qg, kg, bd, _carrier_selectors())


def _query_operands(q):
    slab = 2 * SUBLANES
    row = lax.broadcasted_iota(jnp.int32, (slab, q.shape[0]), 0)
    q_t = q.astype(F32).T
    carriers = q_t[AUG_LANE:AUG_LANE + slab]
    return tuple(jnp.concatenate([q_t[:AUG_LANE], jnp.where(keep, carriers, 0.0),
                                  q_t[AUG_LANE + slab:]], axis=0).astype(BF16)
                 for keep in (row < 6, row >= 6))


def _gated_output(acc_t, g_ref, o_ref, first_block, lanes):
    n_blocks = acc_t.shape[1] // g_ref.shape[-1]
    gate = jnp.concatenate([g_ref[0, first_block + r] for r in range(n_blocks)], axis=1)
    o_ref[0, :, lanes] = (acc_t * _silu(gate)).astype(BF16)


def _fox_bounded_kernel(first_ref, end_ref, qa_ref, ka_ref, vt_ref, cq_ref, g_ref, o_ref,
                        p_ref, acc_ref, *, tile, heads):
    TQ, HP = tile, heads
    QT = qa_ref.shape[2] // TQ
    SB = vt_ref.shape[-1]
    R = TQ // SB
    n_blocks = ka_ref.shape[2] // SB
    b = pl.program_id(0)
    group = pl.program_id(1)
    i0 = pl.program_id(2) * QT
    chains = [(qs, e) for qs in range(QT) for e in range(HP)]
    cols = [slice(qs * TQ, (qs + 1) * TQ) for qs in range(QT)]
    operands = {(qs, e): _query_operands(qa_ref[0, e, cols[qs], :]) for qs, e in chains}
    causal = (lax.broadcasted_iota(jnp.int32, (TQ, TQ), 0)
              <= lax.broadcasted_iota(jnp.int32, (TQ, TQ), 1))

    def keys(e, n):
        return ka_ref[0, e, pl.ds(pl.multiple_of(n * TQ, TQ), TQ), :]

    def values_t(e, blk):
        return vt_ref[0, blk, pl.ds(e * FOX_HEAD_DIM, FOX_HEAD_DIM), :]

    def rest(c, blk, gate=1.0):
        qs, e = c
        end = end_ref[(b * n_blocks + blk) * FOX_HEADS + HP * group + e]
        return jnp.exp2(jnp.minimum(cq_ref[0, e][:, cols[qs]] - end, 0.0)) * gate

    def col_partial(p):
        return jnp.sum(p.reshape(p.shape[0] // SUBLANES, SUBLANES, p.shape[1]), axis=0)

    def far_scores(c, n):
        return jnp.dot(keys(c[1], n), operands[c][1], preferred_element_type=F32)

    def contract(c, n, p_bf16, gate=1.0):
        for r in range(R):
            pv = jnp.dot(values_t(c[1], n * R + r), p_bf16[r * SB:(r + 1) * SB],
                         preferred_element_type=F32)
            acc_ref[c] += rest(c, n * R + r, gate) * pv

    def weighted_partial(c, n, p, gate=1.0):
        return sum(rest(c, n * R + r, gate) * col_partial(p[r * SB:(r + 1) * SB])
                   for r in range(R))

    first = first_ref[(b * pl.num_programs(1) + group) * pl.num_programs(2) + pl.program_id(2)]
    gate0 = jnp.where(i0 > 0, 1.0, 0.0)

    l_part = {}

    def own_tile(c):
        qs, e = c
        k_own = keys(e, i0 + qs)
        pieces = []
        for r in range(R):
            n_keys, lanes = (r + 1) * SB, slice(r * SB, (r + 1) * SB)
            s_t = jnp.dot(k_own[:n_keys], operands[c][0][:, lanes], preferred_element_type=F32)
            pieces.append(jnp.where(causal[:n_keys, lanes], jnp.exp2(s_t), 0.0))
        l_part[c] = jnp.concatenate([col_partial(p) for p in pieces], axis=1)
        pieces = [p.astype(BF16) for p in pieces]

        def later():
            v_t = jnp.concatenate([values_t(e, (i0 + qs) * R + r) for r in range(R)], axis=1)
            for r, p in enumerate(pieces):
                acc_ref[qs, e, :, r * SB:(r + 1) * SB] = jnp.dot(
                    v_t[:, :(r + 1) * SB], p, preferred_element_type=F32)
        return later

    def earlier_tile(c, n):
        p = jnp.exp2(far_scores(c, n))
        l_part[c] = l_part[c] + weighted_partial(c, n, p)
        p = p.astype(BF16)
        return lambda: contract(c, n, p)

    def fill(c):
        p = jnp.exp2(far_scores(c, first))
        l_part[c] = l_part[c] + weighted_partial(c, first, p, gate0)
        p_ref[c] = p.astype(BF16)

    jobs = ([functools.partial(own_tile, c) for c in chains]
            + [functools.partial(earlier_tile, c, i0 + t) for c in chains for t in range(c[0])]
            + [functools.partial(fill, c) for c in chains])
    pending = None
    for job in jobs:
        later = job()
        if pending is not None:
            pending()
        pending = later
    if pending is not None:
        pending()

    def body(n, l_run):
        s_new = []
        for c in chains:
            s_new.append(far_scores(c, n))
            contract(c, n - 1, p_ref[c])
        out = []
        for c, s_t, l_c in zip(chains, s_new, l_run):
            p = jnp.exp2(s_t)
            out.append(l_c + weighted_partial(c, n, p))
            p_ref[c] = p.astype(BF16)
        return tuple(out)

    l_run = lax.fori_loop(first + 1, i0, body, tuple(l_part[c] for c in chains))
    for c in chains:
        contract(c, jnp.maximum(i0 - 1, 0), p_ref[c], gate0)
    for qs in range(QT):
        out_t = [acc_ref[qs, e] * (1.0 / jnp.sum(l_run[qs * HP + e], axis=0, keepdims=True))
                 for e in range(HP)]
        _gated_output(jnp.concatenate(out_t, axis=0), g_ref, o_ref, qs * R, cols[qs])


def _fox_online_kernel(start_ref, qa_ref, ka_ref, vt_ref, g_ref, o_ref, *, tile, heads):
    TB, HP = tile, heads
    n_blocks = ka_ref.shape[2] // TB
    b = pl.program_id(0)
    group = pl.program_id(1)
    i = pl.program_id(2)
    q_t = [_query_operands(qa_ref[0, e])[0] for e in range(HP)]
    k_row = lax.broadcasted_iota(jnp.int32, (TB, TB), 0)
    q_col = lax.broadcasted_iota(jnp.int32, (TB, TB), 1)

    def step(j, carry, masked):
        out = []
        for e in range(HP):
            m_run, l_run, acc = carry[e]
            k_blk = ka_ref[0, e, pl.ds(pl.multiple_of(j * TB, TB), TB), :]
            s_t = jnp.dot(k_blk, q_t[e], preferred_element_type=F32)
            if masked:
                s_t = jnp.where(k_row <= q_col, s_t, MASK_VALUE)
            hd = HP * group + e
            delta = (start_ref[(b * n_blocks + i) * FOX_HEADS + hd]
                     - start_ref[(b * n_blocks + j) * FOX_HEADS + hd])
            m_new = jnp.maximum(m_run, jnp.max(s_t, axis=0, keepdims=True) + delta)
            p_t = jnp.exp2(s_t - (m_new - delta))
            alpha = jnp.exp2(m_run - m_new)
            l_new = alpha * l_run + jnp.sum(p_t, axis=0, keepdims=True)
            v_t = vt_ref[0, j, pl.ds(e * FOX_HEAD_DIM, FOX_HEAD_DIM), :]
            acc_new = alpha * acc + jnp.dot(v_t, p_t.astype(BF16), preferred_element_type=F32)
            out.append((m_new, l_new, acc_new))
        return tuple(out)

    init = (jnp.full((1, TB), MASK_VALUE, F32), jnp.zeros((1, TB), F32),
            jnp.zeros((FOX_HEAD_DIM, TB), F32))
    carry = lax.fori_loop(0, i, lambda j, c: step(j, c, False), (init,) * HP)
    carry = step(i, carry, True)
    _gated_output(jnp.concatenate([acc / l_run for _, l_run, acc in carry], axis=0), g_ref, o_ref,
                  0, slice(None))


def _fox_attention(qa, ka, vt, g, cq, edges, *, bounded):
    b, n_heads, s, _ = qa.shape
    _, nvb, w, vb = vt.shape
    if bounded:
        tile, hp, qt = ATTN_Q_TILE, ATTN_HEADS, ATTN_TILES_PER_STEP
    else:
        tile, hp, qt = vb, LANES // FOX_HEAD_DIM, 1
    gw = hp * FOX_HEAD_DIM
    rows = qt * tile
    assert s % rows == 0 and n_heads % hp == 0 and tile % vb == 0, (s, rows, n_heads, hp, tile, vb)
    specs = dict(
        table=pl.BlockSpec(memory_space=pltpu.SMEM),
        q=pl.BlockSpec((1, hp, rows, LANES), lambda bi, p, i: (bi, p, i, 0)),
        k=pl.BlockSpec((1, hp, s, LANES), lambda bi, p, i: (bi, p, 0, 0)),
        v=pl.BlockSpec((1, nvb, gw, vb), lambda bi, p, i: (bi, 0, p, 0)),
        cq=pl.BlockSpec((1, hp, 1, rows), lambda bi, p, i: (bi, p, 0, i)),
        g=pl.BlockSpec((1, rows // vb, gw, vb), lambda bi, p, i: (bi, i, p, 0)),
        o=pl.BlockSpec((1, gw, rows), lambda bi, p, i: (bi, p, i)))
    if bounded:
        body = functools.partial(_fox_bounded_kernel, tile=tile, heads=hp)
        names = ("table", "table", "q", "k", "v", "cq", "g")
        step_start = edges[:, ::rows // vb, 0, :FOX_HEADS]
        tile_end = edges[:, tile // vb - 1::tile // vb, 1, :FOX_HEADS]
        dead = (step_start[:, :, None] - tile_end[:, None]) < -FOX_DEAD_EXPONENT
        before = jnp.arange(s // rows)[:, None] * qt > jnp.arange(s // tile)[None, :]
        first_live = jnp.sum(dead & before[None, :, :, None], axis=2)
        last_before = jnp.maximum(jnp.arange(s // rows) * qt - 1, 0)
        first_live = jnp.minimum(first_live, last_before[None, :, None])
        first_live = jnp.min(first_live.reshape(b, s // rows, n_heads // hp, hp), axis=-1)
        args = (first_live.transpose(0, 2, 1).reshape(-1).astype(jnp.int32),
                edges[:, :, 1, :FOX_HEADS].reshape(-1), qa, ka, vt, cq, g)
        scratch = [pltpu.VMEM((qt, hp, tile, tile), BF16),
                   pltpu.VMEM((qt, hp, FOX_HEAD_DIM, tile), F32)]
    else:
        body = functools.partial(_fox_online_kernel, tile=tile, heads=hp)
        names = ("table", "q", "k", "v", "g")
        args = (edges[:, :, 0, :FOX_HEADS].reshape(-1), qa, ka, vt, g)
        scratch = []
    return pl.pallas_call(
        body,
        grid=(b, n_heads // hp, s // rows),
        in_specs=[specs[n] for n in names],
        out_specs=specs["o"],
        out_shape=jax.ShapeDtypeStruct((b, w, s), BF16),
        scratch_shapes=scratch,
        compiler_params=_params("arbitrary", "arbitrary", "arbitrary"),
        name="fox_attention_bounded" if bounded else "fox_attention_online",
    )(*args)


def _out_proj_kernel(a_ref, x_ref, mod_ref, w_ref, o_ref):
    y = lax.dot_general(a_ref[0], w_ref[...], (((0,), (0,)), ((), ())),
                        preferred_element_type=F32)
    o_ref[0] = x_ref[0] + mod_ref[0, 2:3, :] * y


def _out_proj(a, x, mod, w_out):
    b, s, d = x.shape
    tile = OUT_PROJ_TILE
    assert s % tile == 0, (s, tile)
    k = a.shape[1]
    return pl.pallas_call(
        _out_proj_kernel,
        grid=(b, s // tile),
        in_specs=[pl.BlockSpec((1, k, tile), lambda bi, ti: (bi, 0, ti)),
                  pl.BlockSpec((1, tile, d), lambda bi, ti: (bi, ti, 0)),
                  pl.BlockSpec((1, 3, d), lambda bi, ti: (bi, 0, 0)),
                  pl.BlockSpec((k, d), lambda bi, ti: (0, 0))],
        out_specs=pl.BlockSpec((1, tile, d), lambda bi, ti: (bi, ti, 0)),
        out_shape=jax.ShapeDtypeStruct((b, s, d), F32),
        compiler_params=_params("arbitrary", "arbitrary"),
        name="layer1_out_proj",
    )(a, x, mod, w_out.astype(BF16))


def _layer1(x, mod, norm_g, w_in, b_f, qnorm_g, knorm_g, w_out):
    d, w = x.shape[-1], FOX_WIDTH
    order = jnp.argsort(b_f)
    w_in = w_in.astype(BF16)
    w_heads = w_in[:, :4 * w].reshape(d, 4, FOX_HEADS, FOX_HEAD_DIM)[:, :, order]
    w_forget = jnp.zeros((d, LANES), BF16).at[:, :FOX_HEADS].set(
        w_in[:, 4 * w:][:, order])
    w_in = jnp.concatenate([w_heads.reshape(d, 4 * w), w_forget], axis=1)
    w_out = w_out.astype(BF16).reshape(FOX_HEADS, FOX_HEAD_DIM, d)[order].reshape(w, d)
    qa, ka, vt, g, cq, edges = _layer1_proj(x, mod, norm_g, w_in, b_f[order], qnorm_g, knorm_g)
    score_bound = (LOG2E * FOX_HEAD_DIM ** 0.5
                   * jnp.max(jnp.abs(qnorm_g)) * jnp.max(jnp.abs(knorm_g)))
    gated = lax.cond(score_bound <= FOX_SAFE_LOGIT,
                     functools.partial(_fox_attention, bounded=True),
                     functools.partial(_fox_attention, bounded=False),
                     qa, ka, vt, g, cq, edges)
    return _out_proj(gated, x, mod, w_out)


def kernel(x, c, norm_g, ada_w, ada_b, hgrn_lb, even_w_in, hgrn_onorm_g, pool_w, pool_scale,
           even_w_out, odd_w_in, fox_b_f, fox_qnorm_g, fox_knorm_g, odd_w_out):
    depth = norm_g.shape[0]
    mods = _adaln_mods(c, ada_w, ada_b)
    for l in range(depth):
        j = l // 2
        if l % 2 == 0:
            x = _layer0(x, mods[l], norm_g[l], even_w_in[j], hgrn_lb, hgrn_onorm_g[j],
                        pool_w[j], pool_scale[j], even_w_out[j], layer_slot=l)
        else:
            x = _layer1(x, mods[l], norm_g[l], odd_w_in[j], fox_b_f[j], fox_qnorm_g[j],
                        fox_knorm_g[j], odd_w_out[j])
    return x
```

```python
import functools

import jax
import jax.numpy as jnp
import numpy as np
from jax import lax
from jax.experimental import pallas as pl
from jax.experimental.pallas import tpu as pltpu

F32 = jnp.float32
BF16 = jnp.bfloat16
EPS = 1e-6

HGRN_HEADS = 4
HGRN_KEY = 128
HGRN_VAL = 128
HGRN_KW = HGRN_HEADS * HGRN_KEY
HGRN_VW = HGRN_HEADS * HGRN_VAL
POOL_WINDOWS = (2, 4, 8, 16)
POOL_GROUP = 128
POOL_WIDTH = POOL_GROUP * len(POOL_WINDOWS)
POOL_HISTORY = 16
FOX_HEADS = 16
FOX_HEAD_DIM = 64
FOX_WIDTH = FOX_HEADS * FOX_HEAD_DIM

LANES = 128
SUBLANES = 8
VMEM_LIMIT_BYTES = 56 * 1024 * 1024

SEQ_TILE = 256
OUT_PROJ_TILE = 2048
MOD_READ_STREAMS = 4
HGRN_CHUNK = 128
ATTN_Q_TILE = 512
ATTN_HEADS = 4
ATTN_TILES_PER_STEP = 2
FOX_SAFE_LOGIT = 100.0
FOX_DEAD_EXPONENT = 150.0
LOG2E = 1.4426950408889634
MASK_VALUE = -1e30
AUG_LANE = FOX_HEAD_DIM


def _sigmoid(x):
    return 0.5 * jnp.tanh(0.5 * x) + 0.5


def _silu(x):
    return x * _sigmoid(x)


def _params(*semantics):
    return pltpu.CompilerParams(dimension_semantics=semantics, vmem_limit_bytes=VMEM_LIMIT_BYTES)


def _mod_kernel(c_ref, *refs):
    *w_refs, b_ref, o_ref, cond_ref = refs

    @pl.when((pl.program_id(0) == 0) & (pl.program_id(1) == 0))
    def _():
        cond_ref[...] = _silu(c_ref[...])

    band = w_refs[0].shape[1]
    for bi in range(c_ref.shape[0]):
        acc = b_ref[0]
        for r, w_ref in enumerate(w_refs):
            w = w_ref[0]
            cond = cond_ref[bi, r * band:(r + 1) * band]
            acc = acc + jnp.concatenate(
                [jnp.sum(w[:, j:j + LANES] * cond, axis=0, keepdims=True)
                 for j in range(0, w.shape[1], LANES)], axis=1)
        o_ref[0, bi:bi + 1, :] = acc


def _adaln_mods(c, ada_w, ada_b):
    depth, d, n = ada_w.shape
    b = c.shape[0]
    tn, bands = 1024, MOD_READ_STREAMS
    band_spec = lambda r: pl.BlockSpec((1, d // bands, tn), lambda l, j: (l, r, j))
    out = pl.pallas_call(
        _mod_kernel,
        grid=(depth, n // tn),
        in_specs=[pl.BlockSpec((b, d, LANES), lambda l, j: (0, 0, 0))]
                 + [band_spec(r) for r in range(bands)]
                 + [pl.BlockSpec((1, 1, tn), lambda l, j: (l, 0, j))],
        out_specs=pl.BlockSpec((1, b, tn), lambda l, j: (l, 0, j)),
        out_shape=jax.ShapeDtypeStruct((depth, b, n), F32),
        scratch_shapes=[pltpu.VMEM((b, d, LANES), F32)],
        compiler_params=_params("arbitrary", "arbitrary"),
        name="adaln_mods",
    )(jnp.broadcast_to(c[:, :, None], (b, d, LANES)), *([ada_w] * bands),
      ada_b.reshape(depth, 1, n))
    return out.reshape(depth, b, 3, d)


def _modulated_norm(x, shift, scale, gain):
    ms = jnp.mean(x * x, axis=-1, keepdims=True)
    return x * lax.rsqrt(ms + EPS) * gain * (1.0 + scale) + shift


def _layer0_kernel(x_ref, mod_ref, ng_ref, win32_ref, lb_ref, og_ref, pw_ref, ps_ref, wout32_ref,
                   o_ref, state_ref, carry_ref, win_ref, wout_ref, *, tile, chunk, layer_slot):
    T, C, H = tile, chunk, HGRN_HEADS
    batch = range(x_ref.shape[0])
    t_idx = pl.program_id(0)

    @pl.when(t_idx == 0)
    def _():
        state_ref[...] = jnp.zeros_like(state_ref)
        carry_ref[...] = jnp.zeros_like(carry_ref)
        win_ref[...] = win32_ref[...].astype(BF16)
        wout_ref[...] = wout32_ref[...].astype(BF16)

    lbv = lb_ref[...]
    e = jnp.exp(lbv - jnp.max(lbv, axis=0, keepdims=True))
    lower = (jnp.sum(e[0:layer_slot + 1], axis=0, keepdims=True)
             / jnp.sum(e, axis=0, keepdims=True))

    xs, parts = [], []
    for bi in batch:
        x = x_ref[bi]
        h = _modulated_norm(x, mod_ref[bi, 0:1, :], mod_ref[bi, 1:2, :], ng_ref[...])
        proj = jnp.dot(h.astype(BF16), win_ref[...], preferred_element_type=F32)
        o0, widths, piece = 0, (HGRN_KW, HGRN_KW, HGRN_VW, HGRN_VW, POOL_WIDTH, POOL_WIDTH), []
        for wd in widths:
            piece.append(proj[:, o0:o0 + wd])
            o0 += wd
        xs.append(x)
        parts.append(piece)

    row = lax.broadcasted_iota(jnp.int32, (T, HGRN_KW), 0)
    ti = lax.broadcasted_iota(jnp.int32, (C, C), 0)
    si = lax.broadcasted_iota(jnp.int32, (C, C), 1)
    n_chunks = T // C
    nt_dims = (((1,), (1,)), ((), ()))
    tn_dims = (((0,), (0,)), ((), ()))

    def level_scores(qe, ke, mask, scores):
        qb = qe.astype(BF16)
        kb = ke.astype(BF16)
        out = []
        for c in range(n_chunks):
            for hh in range(H):
                rs = slice(c * C, (c + 1) * C)
                cs = slice(hh * HGRN_KEY, (hh + 1) * HGRN_KEY)
                d = lax.dot_general(qb[rs, cs], kb[rs, cs], nt_dims, preferred_element_type=F32)
                prev = scores[c * H + hh]
                out.append(jnp.where(mask, d, 0.0 if prev is None else prev))
        return out

    scores, q_dec, k_dec, total = [], [], [], []
    for bi in batch:
        q, f = parts[bi][0], parts[bi][1]
        forget = lower + (1.0 - lower) * _sigmoid(f)
        key = 1.0 - forget
        scores.append(level_scores(q, key, ti == si, [None] * (n_chunks * H)))
        q_dec.append(q * forget)
        k_dec.append(key)
        total.append(forget)
    m = 1
    while m < C:
        mask = ((ti ^ si) < 2 * m) & ((ti & m) != 0) & ((si & m) == 0)
        upper = (row & m) != 0
        for bi in batch:
            scores[bi] = level_scores(q_dec[bi], k_dec[bi], mask, scores[bi])
            t_dn = pltpu.roll(total[bi], m, 0)
            t_up = pltpu.roll(total[bi], T - m, 0)
            q_dec[bi] = q_dec[bi] * jnp.where(upper, t_dn, 1.0)
            k_dec[bi] = k_dec[bi] * jnp.where(upper, 1.0, t_up)
            total[bi] = total[bi] * jnp.where(upper, t_dn, t_up)
        m *= 2

    o_a = []
    for bi in batch:
        val, g_a = parts[bi][2], parts[bi][3]
        q_in = q_dec[bi].astype(BF16)
        k_out = k_dec[bi].astype(BF16)
        decay = total[bi]
        val_b = val.astype(BF16)
        oa_rows = []
        for c in range(n_chunks):
            rs = slice(c * C, (c + 1) * C)
            heads_out = []
            for hh in range(H):
                cs = slice(hh * HGRN_KEY, (hh + 1) * HGRN_KEY)
                st = state_ref[bi, hh]
                vb = val_b[rs, cs]
                o = jnp.dot(scores[bi][c * H + hh].astype(BF16), vb, preferred_element_type=F32)
                o = o + lax.dot_general(q_in[rs, cs], st.astype(BF16), nt_dims,
                                        preferred_element_type=F32)
                state_ref[bi, hh] = (st * decay[c * C:c * C + 1, cs]
                                     + lax.dot_general(vb, k_out[rs, cs], tn_dims,
                                                       preferred_element_type=F32))
                ms_o = jnp.mean(o * o, axis=-1, keepdims=True)
                heads_out.append(o * lax.rsqrt(ms_o + EPS))
            oa_rows.append(jnp.concatenate(heads_out, axis=1))
        o_a.append(jnp.concatenate(oa_rows, axis=0) * og_ref[...] * _silu(g_a))

    pos = lax.broadcasted_iota(jnp.int32, (T, POOL_GROUP), 0) + t_idx * T + 1
    for bi in batch:
        u, g_b = parts[bi][4], parts[bi][5]
        ub = jnp.concatenate([carry_ref[bi], u], axis=0)
        carry_ref[bi] = u[T - POOL_HISTORY:T, :]
        wins = []
        acc = ub
        sh = 1
        while sh < max(POOL_WINDOWS):
            acc = acc + pltpu.roll(acc, sh, 0)
            sh *= 2
            wins.append(acc)
        ob = []
        for gi, w in enumerate(POOL_WINDOWS):
            cs = slice(gi * POOL_GROUP, (gi + 1) * POOL_GROUP)
            win = wins[w.bit_length() - 2][POOL_HISTORY:, cs]
            cnt = jnp.minimum(pos, w).astype(F32)
            pooled = win / cnt - u[:, cs]
            ob.append(jnp.dot(pooled.astype(BF16), pw_ref[gi], preferred_element_type=F32))
        o_b = jnp.concatenate(ob, axis=1) * ps_ref[...] * _silu(g_b)
        mixed = jnp.concatenate([o_a[bi], o_b], axis=1).astype(BF16)
        y = jnp.dot(mixed, wout_ref[...], preferred_element_type=F32)
        o_ref[bi] = xs[bi] + mod_ref[bi, 2:3, :] * y


def _layer0(x, mod, norm_g, w_in, hgrn_lb, onorm_g, pool_w, pool_scale, w_out, layer_slot):
    b, s, d = x.shape
    tile = SEQ_TILE
    assert s % tile == 0 and tile % HGRN_CHUNK == 0, (s, tile)
    n_in = w_in.shape[1]
    mix = w_out.shape[0]
    const2 = lambda ti: (0, 0)
    const3 = lambda ti: (0, 0, 0)
    return pl.pallas_call(
        functools.partial(_layer0_kernel, tile=tile, chunk=HGRN_CHUNK, layer_slot=layer_slot),
        grid=(s // tile,),
        in_specs=[pl.BlockSpec((b, tile, d), lambda ti: (0, ti, 0)),
                  pl.BlockSpec((b, 3, d), const3),
                  pl.BlockSpec((1, d), const2),
                  pl.BlockSpec((d, n_in), const2, pipeline_mode=pl.Buffered(1)),
                  pl.BlockSpec(hgrn_lb.shape, const2),
                  pl.BlockSpec((1, HGRN_VW), const2),
                  pl.BlockSpec(pool_w.shape, const3),
                  pl.BlockSpec((1, POOL_WIDTH), const2),
                  pl.BlockSpec((mix, d), const2, pipeline_mode=pl.Buffered(1))],
        out_specs=pl.BlockSpec((b, tile, d), lambda ti: (0, ti, 0)),
        out_shape=jax.ShapeDtypeStruct((b, s, d), F32),
        scratch_shapes=[pltpu.VMEM((b, HGRN_HEADS, HGRN_VAL, HGRN_KEY), F32),
                        pltpu.VMEM((b, POOL_HISTORY, POOL_WIDTH), F32),
                        pltpu.VMEM((d, n_in), BF16),
                        pltpu.VMEM((mix, d), BF16)],
        compiler_params=_params("arbitrary"),
        name="layer0_hgrn_pool",
    )(x, mod, norm_g.reshape(1, d), w_in, hgrn_lb, onorm_g.reshape(1, HGRN_VW),
      pool_w.astype(BF16), pool_scale.reshape(1, POOL_WIDTH), w_out)


def _split3(c):
    hi = c.astype(BF16).astype(F32)
    r = c - hi
    mid = r.astype(BF16).astype(F32)
    lo = (r - mid).astype(BF16).astype(F32)
    return hi, mid, lo


def _carrier_selectors():
    per_tile = LANES // FOX_HEAD_DIM
    sel = np.zeros((FOX_HEADS // per_tile, LANES, per_tile * LANES), np.float32)
    for hd in range(FOX_HEADS):
        base = (hd % per_tile) * LANES + AUG_LANE
        for j in range(3):
            sel[hd // per_tile, j * FOX_HEADS + hd, base + j] = 1.0
            sel[hd // per_tile, j * FOX_HEADS + hd, base + 3 + j] = -1.0
            sel[hd // per_tile, (3 + j) * FOX_HEADS + hd, base + 6 + j] = -1.0
    return jnp.asarray(sel, BF16)


def _layer1_proj_kernel(x_ref, mod_ref, ng_ref, win_ref, bf_ref, qg_ref, kg_ref, bd_ref, sel_ref,
                        qa_ref, ka_ref, vt_ref, g_ref, cq_ref, edge_ref, carry_ref, tile_start_ref,
                        *, tile):
    T, W = tile, FOX_WIDTH
    batch = range(x_ref.shape[0])
    t_idx = pl.program_id(0)
    lane = lax.broadcasted_iota(jnp.int32, (T, LANES), 1)
    row = lax.broadcasted_iota(jnp.int32, (T, LANES), 0)
    bd = bd_ref[...]
    qk_scale = FOX_HEAD_DIM ** -0.5 * LOG2E

    hs = [_modulated_norm(x_ref[bi], mod_ref[bi, 0:1, :], mod_ref[bi, 1:2, :],
                          ng_ref[...]).astype(BF16) for bi in batch]
    ones_q = jnp.where((lane >= AUG_LANE + 3) & (lane < AUG_LANE + 9), 1.0, 0.0)
    ones_k = jnp.where((lane >= AUG_LANE) & (lane < AUG_LANE + 3), 1.0, 0.0)
    forget_logits = [jnp.dot(hs[bi], win_ref[:, 4 * W:], preferred_element_type=F32)
                     for bi in batch]

    def big_product(bi):
        return jnp.dot(hs[bi], win_ref[:, :4 * W], preferred_element_type=F32)

    projs, carriers_packed = [], []
    for bi in batch:
        z = forget_logits[bi] + bf_ref[...]
        logf = jnp.minimum(z, 0.0) - jnp.log1p(jnp.exp(-jnp.abs(z)))
        cum = logf
        sh = 1
        while sh < T:
            cum = cum + jnp.where(row >= sh, pltpu.roll(cum, sh, 0), 0.0)
            sh *= 2
        start = jnp.where(t_idx == 0, 0.0, carry_ref[bi])
        end = start + cum[T - 1:T, :]
        carry_ref[bi] = end
        tile_start = jnp.where(t_idx % (ATTN_Q_TILE // T) == 0, start, tile_start_ref[bi])
        tile_start_ref[bi] = tile_start
        edge_ref[bi, 0] = jnp.concatenate([tile_start, end], axis=0) * LOG2E
        s_hi, s_mid, s_lo = _split3((cum + (start - tile_start)) * LOG2E)
        e_hi, e_mid, e_lo = _split3((cum - cum[T - 1:T, :]) * LOG2E)
        total_t = ((cum + start) * LOG2E).T
        for hd in range(FOX_HEADS):
            cq_ref[bi, hd] = total_t[hd:hd + 1, :]

        terms = (s_hi, s_mid, s_lo, e_hi, e_mid, e_lo)
        packed = jnp.zeros((T, LANES), F32)
        for j, term in enumerate(terms):
            moved = term if j == 0 else pltpu.roll(term, j * FOX_HEADS, 1)
            packed = jnp.where((lane >= j * FOX_HEADS) & (lane < (j + 1) * FOX_HEADS), moved,
                               packed)
        carriers_packed.append(packed.astype(BF16))

    def pair_inputs(bi, pair):
        cs = slice(pair * LANES, (pair + 1) * LANES)
        return projs[bi][:, 0:W][:, cs], projs[bi][:, W:2 * W][:, cs]

    def small_products(bi):
        mean_squares, placed = [], []
        for pair in range(FOX_HEADS // 2):
            qp, kp = pair_inputs(bi, pair)
            squares = jnp.concatenate([(qp * qp).astype(BF16), (kp * kp).astype(BF16)], axis=1)
            mean_squares.append(jnp.dot(squares, bd, preferred_element_type=F32))
            placed.append(jnp.dot(carriers_packed[bi], sel_ref[pair],
                                  preferred_element_type=F32))
        return mean_squares, placed

    def assemble(bi, mean_squares, placed):
        vt_ref[bi, 0] = projs[bi][:, 2 * W:3 * W].T.astype(BF16)
        g_ref[bi, 0] = projs[bi][:, 3 * W:4 * W].T
        for pair in range(FOX_HEADS // 2):
            qp, kp = pair_inputs(bi, pair)
            ms, aug = mean_squares[pair], placed[pair]
            qn = qp * lax.rsqrt(ms[:, :LANES] + EPS) * (qg_ref[...] * qk_scale)
            kn = kp * lax.rsqrt(ms[:, LANES:] + EPS) * kg_ref[...]
            for e in range(LANES // FOX_HEAD_DIM):
                hd = (LANES // FOX_HEAD_DIM) * pair + e
                carriers = aug[:, e * LANES:(e + 1) * LANES]
                q_main = qn if e == 0 else pltpu.roll(qn, FOX_HEAD_DIM, 1)
                k_main = kn if e == 0 else pltpu.roll(kn, FOX_HEAD_DIM, 1)
                q_aug = jnp.where(lane < AUG_LANE + 3, carriers, ones_q)
                k_aug = jnp.where(lane < AUG_LANE + 3, ones_k, carriers)
                qa_ref[bi, hd] = jnp.where(lane < FOX_HEAD_DIM, q_main, q_aug).astype(BF16)
                ka_ref[bi, hd] = jnp.where(lane < FOX_HEAD_DIM, k_main, k_aug).astype(BF16)

    for bi in batch:
        projs.append(big_product(bi))
        assemble(bi, *small_products(bi))


def _layer1_proj(x, mod, norm_g, w_in, b_f, qnorm_g, knorm_g):
    b, s, d = x.shape
    tile = SEQ_TILE
    assert s % tile == 0 and ATTN_Q_TILE % tile == 0, (s, tile)
    nt = s // tile
    w = FOX_WIDTH
    n_pad = w_in.shape[1]
    bf_pad = jnp.zeros((1, LANES), F32).at[0, :FOX_HEADS].set(b_f)
    blk = jnp.arange(2 * LANES) // FOX_HEAD_DIM
    bd = jnp.where(blk[:, None] == blk[None, :], 1.0 / FOX_HEAD_DIM, 0.0).astype(BF16)
    qg = jnp.tile(qnorm_g, LANES // FOX_HEAD_DIM).reshape(1, LANES)
    kg = jnp.tile(knorm_g, LANES // FOX_HEAD_DIM).reshape(1, LANES)
    const2 = lambda ti: (0, 0)
    const3 = lambda ti: (0, 0, 0)
    return pl.pallas_call(
        functools.partial(_layer1_proj_kernel, tile=tile),
        grid=(nt,),
        in_specs=[pl.BlockSpec((b, tile, d), lambda ti: (0, ti, 0)),
                  pl.BlockSpec((b, 3, d), const3),
                  pl.BlockSpec((1, d), const2),
                  pl.BlockSpec((d, n_pad), const2, pipeline_mode=pl.Buffered(1)),
                  pl.BlockSpec((1, LANES), const2),
                  pl.BlockSpec((1, LANES), const2),
                  pl.BlockSpec((1, LANES), const2),
                  pl.BlockSpec((2 * LANES, 2 * LANES), const2),
                  pl.BlockSpec((FOX_HEADS // 2, LANES, 2 * LANES), const3)],
        out_specs=[pl.BlockSpec((b, FOX_HEADS, tile, LANES), lambda ti: (0, 0, ti, 0)),
                   pl.BlockSpec((b, FOX_HEADS, tile, LANES), lambda ti: (0, 0, ti, 0)),
                   pl.BlockSpec((b, 1, w, tile), lambda ti: (0, ti, 0, 0)),
                   pl.BlockSpec((b, 1, w, tile), lambda ti: (0, ti, 0, 0)),
                   pl.BlockSpec((b, FOX_HEADS, 1, tile), lambda ti: (0, 0, 0, ti)),
                   pl.BlockSpec((b, 1, 2, LANES), lambda ti: (0, ti, 0, 0))],
        out_shape=[jax.ShapeDtypeStruct((b, FOX_HEADS, s, LANES), BF16),
                   jax.ShapeDtypeStruct((b, FOX_HEADS, s, LANES), BF16),
                   jax.ShapeDtypeStruct((b, nt, w, tile), BF16),
                   jax.ShapeDtypeStruct((b, nt, w, tile), F32),
                   jax.ShapeDtypeStruct((b, FOX_HEADS, 1, s), F32),
                   jax.ShapeDtypeStruct((b, nt, 2, LANES), F32)],
        scratch_shapes=[pltpu.VMEM((b, 1, LANES), F32), pltpu.VMEM((b, 1, LANES), F32)],
        compiler_params=_params("arbitrary"),
        name="layer1_proj",
    )(x, mod, norm_g.reshape(1, d), w_in, bf_pad, qg, kg, bd, _carrier_selectors())


def _query_operands(q):
    slab = 2 * SUBLANES
    row = lax.broadcasted_iota(jnp.int32, (slab, q.shape[0]), 0)
    q_t = q.astype(F32).T
    carriers = q_t[AUG_LANE:AUG_LANE + slab]
    return tuple(jnp.concatenate([q_t[:AUG_LANE], jnp.where(keep, carriers, 0.0),
                                  q_t[AUG_LANE + slab:]], axis=0).astype(BF16)
                 for keep in (row < 6, row >= 6))


def _gated_output(acc_t, g_ref, o_ref, first_block, lanes):
    n_blocks = acc_t.shape[1] // g_ref.shape[-1]
    gate = jnp.concatenate([g_ref[0, first_block + r] for r in range(n_blocks)], axis=1)
    o_ref[0, :, lanes] = (acc_t * _silu(gate)).astype(BF16)


def _fox_bounded_kernel(first_ref, end_ref, qa_ref, ka_ref, vt_ref, cq_ref, g_ref, o_ref,
                        p_ref, acc_ref, *, tile, heads):
    TQ, HP = tile, heads
    QT = qa_ref.shape[2] // TQ
    SB = vt_ref.shape[-1]
    R = TQ // SB
    n_blocks = ka_ref.shape[2] // SB
    b = pl.program_id(0)
    group = pl.program_id(1)
    i0 = pl.program_id(2) * QT
    chains = [(qs, e) for qs in range(QT) for e in range(HP)]
    cols = [slice(qs * TQ, (qs + 1) * TQ) for qs in range(QT)]
    operands = {(qs, e): _query_operands(qa_ref[0, e, cols[qs], :]) for qs, e in chains}
    causal = (lax.broadcasted_iota(jnp.int32, (TQ, TQ), 0)
              <= lax.broadcasted_iota(jnp.int32, (TQ, TQ), 1))

    def keys(e, n):
        return ka_ref[0, e, pl.ds(pl.multiple_of(n * TQ, TQ), TQ), :]

    def values_t(e, blk):
        return vt_ref[0, blk, pl.ds(e * FOX_HEAD_DIM, FOX_HEAD_DIM), :]

    def rest(c, blk, gate=1.0):
        qs, e = c
        end = end_ref[(b * n_blocks + blk) * FOX_HEADS + HP * group + e]
        return jnp.exp2(jnp.minimum(cq_ref[0, e][:, cols[qs]] - end, 0.0)) * gate

    def col_partial(p):
        return jnp.sum(p.reshape(p.shape[0] // SUBLANES, SUBLANES, p.shape[1]), axis=0)

    def far_scores(c, n):
        return jnp.dot(keys(c[1], n), operands[c][1], preferred_element_type=F32)

    def contract(c, n, p_bf16, gate=1.0):
        for r in range(R):
            pv = jnp.dot(values_t(c[1], n * R + r), p_bf16[r * SB:(r + 1) * SB],
                         preferred_element_type=F32)
            acc_ref[c] += rest(c, n * R + r, gate) * pv

    def weighted_partial(c, n, p, gate=1.0):
        return sum(rest(c, n * R + r, gate) * col_partial(p[r * SB:(r + 1) * SB])
                   for r in range(R))

    first = first_ref[(b * pl.num_programs(1) + group) * pl.num_programs(2) + pl.program_id(2)]
    gate0 = jnp.where(i0 > 0, 1.0, 0.0)

    l_part = {}

    def own_tile(c):
        qs, e = c
        k_own = keys(e, i0 + qs)
        pieces = []
        for r in range(R):
            n_keys, lanes = (r + 1) * SB, slice(r * SB, (r + 1) * SB)
            s_t = jnp.dot(k_own[:n_keys], operands[c][0][:, lanes], preferred_element_type=F32)
            pieces.append(jnp.where(causal[:n_keys, lanes], jnp.exp2(s_t), 0.0))
        l_part[c] = jnp.concatenate([col_partial(p) for p in pieces], axis=1)
        pieces = [p.astype(BF16) for p in pieces]

        def later():
            v_t = jnp.concatenate([values_t(e, (i0 + qs) * R + r) for r in range(R)], axis=1)
            for r, p in enumerate(pieces):
                acc_ref[qs, e, :, r * SB:(r + 1) * SB] = jnp.dot(
                    v_t[:, :(r + 1) * SB], p, preferred_element_type=F32)
        return later

    def earlier_tile(c, n):
        p = jnp.exp2(far_scores(c, n))
        l_part[c] = l_part[c] + weighted_partial(c, n, p)
        p = p.astype(BF16)
        return lambda: contract(c, n, p)

    def fill(c):
        p = jnp.exp2(far_scores(c, first))
        l_part[c] = l_part[c] + weighted_partial(c, first, p, gate0)
        p_ref[c] = p.astype(BF16)

    jobs = ([functools.partial(own_tile, c) for c in chains]
            + [functools.partial(earlier_tile, c, i0 + t) for c in chains for t in range(c[0])]
            + [functools.partial(fill, c) for c in chains])
    pending = None
    for job in jobs:
        later = job()
        if pending is not None:
            pending()
        pending = later
    if pending is not None:
        pending()

    def body(n, l_run):
        s_new = []
        for c in chains:
            s_new.append(far_scores(c, n))
            contract(c, n - 1, p_ref[c])
        out = []
        for c, s_t, l_c in zip(chains, s_new, l_run):
            p = jnp.exp2(s_t)
            out.append(l_c + weighted_partial(c, n, p))
            p_ref[c] = p.astype(BF16)
        return tuple(out)

    l_run = lax.fori_loop(first + 1, i0, body, tuple(l_part[c] for c in chains))
    for c in chains:
        contract(c, jnp.maximum(i0 - 1, 0), p_ref[c], gate0)
    for qs in range(QT):
        out_t = [acc_ref[qs, e] * (1.0 / jnp.sum(l_run[qs * HP + e], axis=0, keepdims=True))
                 for e in range(HP)]
        _gated_output(jnp.concatenate(out_t, axis=0), g_ref, o_ref, qs * R, cols[qs])


def _fox_online_kernel(start_ref, qa_ref, ka_ref, vt_ref, g_ref, o_ref, *, tile, heads):
    TB, HP = tile, heads
    n_blocks = ka_ref.shape[2] // TB
    b = pl.program_id(0)
    group = pl.program_id(1)
    i = pl.program_id(2)
    q_t = [_query_operands(qa_ref[0, e])[0] for e in range(HP)]
    k_row = lax.broadcasted_iota(jnp.int32, (TB, TB), 0)
    q_col = lax.broadcasted_iota(jnp.int32, (TB, TB), 1)

    def step(j, carry, masked):
        out = []
        for e in range(HP):
            m_run, l_run, acc = carry[e]
            k_blk = ka_ref[0, e, pl.ds(pl.multiple_of(j * TB, TB), TB), :]
            s_t = jnp.dot(k_blk, q_t[e], preferred_element_type=F32)
            if masked:
                s_t = jnp.where(k_row <= q_col, s_t, MASK_VALUE)
            hd = HP * group + e
            delta = (start_ref[(b * n_blocks + i) * FOX_HEADS + hd]
                     - start_ref[(b * n_blocks + j) * FOX_HEADS + hd])
            m_new = jnp.maximum(m_run, jnp.max(s_t, axis=0, keepdims=True) + delta)
            p_t = jnp.exp2(s_t - (m_new - delta))
            alpha = jnp.exp2(m_run - m_new)
            l_new = alpha * l_run + jnp.sum(p_t, axis=0, keepdims=True)
            v_t = vt_ref[0, j, pl.ds(e * FOX_HEAD_DIM, FOX_HEAD_DIM), :]
            acc_new = alpha * acc + jnp.dot(v_t, p_t.astype(BF16), preferred_element_type=F32)
            out.append((m_new, l_new, acc_new))
        return tuple(out)

    init = (jnp.full((1, TB), MASK_VALUE, F32), jnp.zeros((1, TB), F32),
            jnp.zeros((FOX_HEAD_DIM, TB), F32))
    carry = lax.fori_loop(0, i, lambda j, c: step(j, c, False), (init,) * HP)
    carry = step(i, carry, True)
    _gated_output(jnp.concatenate([acc / l_run for _, l_run, acc in carry], axis=0), g_ref, o_ref,
                  0, slice(None))


def _fox_attention(qa, ka, vt, g, cq, edges, *, bounded):
    b, n_heads, s, _ = qa.shape
    _, nvb, w, vb = vt.shape
    if bounded:
        tile, hp, qt = ATTN_Q_TILE, ATTN_HEADS, ATTN_TILES_PER_STEP
    else:
        tile, hp, qt = vb, LANES // FOX_HEAD_DIM, 1
    gw = hp * FOX_HEAD_DIM
    rows = qt * tile
    assert s % rows == 0 and n_heads % hp == 0 and tile % vb == 0, (s, rows, n_heads, hp, tile, vb)
    specs = dict(
        table=pl.BlockSpec(memory_space=pltpu.SMEM),
        q=pl.BlockSpec((1, hp, rows, LANES), lambda bi, p, i: (bi, p, i, 0)),
        k=pl.BlockSpec((1, hp, s, LANES), lambda bi, p, i: (bi, p, 0, 0)),
        v=pl.BlockSpec((1, nvb, gw, vb), lambda bi, p, i: (bi, 0, p, 0)),
        cq=pl.BlockSpec((1, hp, 1, rows), lambda bi, p, i: (bi, p, 0, i)),
        g=pl.BlockSpec((1, rows // vb, gw, vb), lambda bi, p, i: (bi, i, p, 0)),
        o=pl.BlockSpec((1, gw, rows), lambda bi, p, i: (bi, p, i)))
    if bounded:
        body = functools.partial(_fox_bounded_kernel, tile=tile, heads=hp)
        names = ("table", "table", "q", "k", "v", "cq", "g")
        step_start = edges[:, ::rows // vb, 0, :FOX_HEADS]
        tile_end = edges[:, tile // vb - 1::tile // vb, 1, :FOX_HEADS]
        dead = (step_start[:, :, None] - tile_end[:, None]) < -FOX_DEAD_EXPONENT
        before = jnp.arange(s // rows)[:, None] * qt > jnp.arange(s // tile)[None, :]
        first_live = jnp.sum(dead & before[None, :, :, None], axis=2)
        last_before = jnp.maximum(jnp.arange(s // rows) * qt - 1, 0)
        first_live = jnp.minimum(first_live, last_before[None, :, None])
        first_live = jnp.min(first_live.reshape(b, s // rows, n_heads // hp, hp), axis=-1)
        args = (first_live.transpose(0, 2, 1).reshape(-1).astype(jnp.int32),
                edges[:, :, 1, :FOX_HEADS].reshape(-1), qa, ka, vt, cq, g)
        scratch = [pltpu.VMEM((qt, hp, tile, tile), BF16),
                   pltpu.VMEM((qt, hp, FOX_HEAD_DIM, tile), F32)]
    else:
        body = functools.partial(_fox_online_kernel, tile=tile, heads=hp)
        names = ("table", "q", "k", "v", "g")
        args = (edges[:, :, 0, :FOX_HEADS].reshape(-1), qa, ka, vt, g)
        scratch = []
    return pl.pallas_call(
        body,
        grid=(b, n_heads // hp, s // rows),
        in_specs=[specs[n] for n in names],
        out_specs=specs["o"],
        out_shape=jax.ShapeDtypeStruct((b, w, s), BF16),
        scratch_shapes=scratch,
        compiler_params=_params("arbitrary", "arbitrary", "arbitrary"),
        name="fox_attention_bounded" if bounded else "fox_attention_online",
    )(*args)


def _out_proj_kernel(a_ref, x_ref, mod_ref, w_ref, o_ref):
    y = lax.dot_general(a_ref[0], w_ref[...], (((0,), (0,)), ((), ())),
                        preferred_element_type=F32)
    o_ref[0] = x_ref[0] + mod_ref[0, 2:3, :] * y


def _out_proj(a, x, mod, w_out):
    b, s, d = x.shape
    tile = OUT_PROJ_TILE
    assert s % tile == 0, (s, tile)
    k = a.shape[1]
    return pl.pallas_call(
        _out_proj_kernel,
        grid=(b, s // tile),
        in_specs=[pl.BlockSpec((1, k, tile), lambda bi, ti: (bi, 0, ti)),
                  pl.BlockSpec((1, tile, d), lambda bi, ti: (bi, ti, 0)),
                  pl.BlockSpec((1, 3, d), lambda bi, ti: (bi, 0, 0)),
                  pl.BlockSpec((k, d), lambda bi, ti: (0, 0))],
        out_specs=pl.BlockSpec((1, tile, d), lambda bi, ti: (bi, ti, 0)),
        out_shape=jax.ShapeDtypeStruct((b, s, d), F32),
        compiler_params=_params("arbitrary", "arbitrary"),
        name="layer1_out_proj",
    )(a, x, mod, w_out.astype(BF16))


def _permute_in_kernel(w_ref, p_ref, pf_ref, o_ref):
    w = FOX_WIDTH
    for grp in range(4):
        cols = slice(grp * w, (grp + 1) * w)
        o_ref[:, cols] = jnp.dot(w_ref[:, cols].astype(BF16), p_ref[...],
                                 preferred_element_type=F32).astype(BF16)
    o_ref[:, 4 * w:] = jnp.dot(w_ref[:, 4 * w:].astype(BF16), pf_ref[...],
                               preferred_element_type=F32).astype(BF16)


def _permute_out_kernel(p_ref, w_ref, o_ref):
    o_ref[...] = lax.dot_general(p_ref[...], w_ref[...].astype(BF16), (((0,), (0,)), ((), ())),
                                 preferred_element_type=F32).astype(BF16)


def _permute_weights(w_in, w_out, order):
    d, w = w_in.shape[0], FOX_WIDTH
    band = 256
    assert d % band == 0 and w_out.shape == (w, d), (w_in.shape, w_out.shape)
    onehot = (jnp.arange(FOX_HEADS)[:, None] == order[None, :]).astype(BF16)
    perm = jnp.kron(onehot, jnp.eye(FOX_HEAD_DIM, dtype=BF16))
    perm_f = jnp.zeros((FOX_HEADS, LANES), BF16).at[:, :FOX_HEADS].set(onehot)
    w_in_p = pl.pallas_call(
        _permute_in_kernel,
        grid=(d // band,),
        in_specs=[pl.BlockSpec((band, w_in.shape[1]), lambda i: (i, 0)),
                  pl.BlockSpec((w, w), lambda i: (0, 0)),
                  pl.BlockSpec((FOX_HEADS, LANES), lambda i: (0, 0))],
        out_specs=pl.BlockSpec((band, 4 * w + LANES), lambda i: (i, 0)),
        out_shape=jax.ShapeDtypeStruct((d, 4 * w + LANES), BF16),
        compiler_params=_params("arbitrary"),
        name="layer1_permute_w_in",
    )(w_in, perm, perm_f)
    w_out_p = pl.pallas_call(
        _permute_out_kernel,
        grid=(d // band,),
        in_specs=[pl.BlockSpec((w, w), lambda i: (0, 0)),
                  pl.BlockSpec((w, band), lambda i: (0, i))],
        out_specs=pl.BlockSpec((w, band), lambda i: (0, i)),
        out_shape=jax.ShapeDtypeStruct((w, d), BF16),
        compiler_params=_params("arbitrary"),
        name="layer1_permute_w_out",
    )(perm, w_out)
    return w_in_p, w_out_p


def _layer1(x, mod, norm_g, w_in, b_f, qnorm_g, knorm_g, w_out):
    order = jnp.argsort(b_f)
    w_in, w_out = _permute_weights(w_in, w_out, order)
    qa, ka, vt, g, cq, edges = _layer1_proj(x, mod, norm_g, w_in, b_f[order], qnorm_g, knorm_g)
    score_bound = (LOG2E * FOX_HEAD_DIM ** 0.5
                   * jnp.max(jnp.abs(qnorm_g)) * jnp.max(jnp.abs(knorm_g)))
    gated = lax.cond(score_bound <= FOX_SAFE_LOGIT,
                     functools.partial(_fox_attention, bounded=True),
                     functools.partial(_fox_attention, bounded=False),
                     qa, ka, vt, g, cq, edges)
    return _out_proj(gated, x, mod, w_out)


def kernel(x, c, norm_g, ada_w, ada_b, hgrn_lb, even_w_in, hgrn_onorm_g, pool_w, pool_scale,
           even_w_out, odd_w_in, fox_b_f, fox_qnorm_g, fox_knorm_g, odd_w_out):
    depth = norm_g.shape[0]
    mods = _adaln_mods(c, ada_w, ada_b)
    for l in range(depth):
        j = l // 2
        if l % 2 == 0:
            x = _layer0(x, mods[l], norm_g[l], even_w_in[j], hgrn_lb, hgrn_onorm_g[j],
                        pool_w[j], pool_scale[j], even_w_out[j], layer_slot=l)
        else:
            x = _layer1(x, mods[l], norm_g[l], odd_w_in[j], fox_b_f[j], fox_qnorm_g[j],
                        fox_knorm_g[j], odd_w_out[j])
    return x
```

```python
import functools

import jax
import jax.numpy as jnp
import numpy as np
from jax import lax
from jax.experimental import pallas as pl
from jax.experimental.pallas import tpu as pltpu

F32 = jnp.float32
BF16 = jnp.bfloat16
EPS = 1e-6

HGRN_HEADS = 4
HGRN_KEY = 128
HGRN_VAL = 128
HGRN_KW = HGRN_HEADS * HGRN_KEY
HGRN_VW = HGRN_HEADS * HGRN_VAL
POOL_WINDOWS = (2, 4, 8, 16)
POOL_GROUP = 128
POOL_WIDTH = POOL_GROUP * len(POOL_WINDOWS)
POOL_HISTORY = 16
FOX_HEADS = 16
FOX_HEAD_DIM = 64
FOX_WIDTH = FOX_HEADS * FOX_HEAD_DIM

LANES = 128
SUBLANES = 8
VMEM_LIMIT_BYTES = 56 * 1024 * 1024

SEQ_TILE = 256
OUT_PROJ_TILE = 2048
MOD_READ_STREAMS = 4
HGRN_CHUNK = 128
ATTN_Q_TILE = 512
ATTN_HEADS = 4
ATTN_TILES_PER_STEP = 2
FOX_SAFE_LOGIT = 100.0
FOX_DEAD_EXPONENT = 150.0
LOG2E = 1.4426950408889634
MASK_VALUE = -1e30
AUG_LANE = FOX_HEAD_DIM


def _sigmoid(x):
    return 0.5 * jnp.tanh(0.5 * x) + 0.5


def _silu(x):
    return x * _sigmoid(x)


def _params(*semantics):
    return pltpu.CompilerParams(dimension_semantics=semantics, vmem_limit_bytes=VMEM_LIMIT_BYTES)


def _mod_kernel(c_ref, *refs):
    *w_refs, b_ref, o_ref, cond_ref = refs

    @pl.when((pl.program_id(0) == 0) & (pl.program_id(1) == 0))
    def _():
        cond_ref[...] = _silu(c_ref[...])

    band = w_refs[0].shape[1]
    for bi in range(c_ref.shape[0]):
        acc = b_ref[0]
        for r, w_ref in enumerate(w_refs):
            w = w_ref[0]
            cond = cond_ref[bi, r * band:(r + 1) * band]
            acc = acc + jnp.concatenate(
                [jnp.sum(w[:, j:j + LANES] * cond, axis=0, keepdims=True)
                 for j in range(0, w.shape[1], LANES)], axis=1)
        o_ref[0, bi:bi + 1, :] = acc


def _adaln_mods(c, ada_w, ada_b):
    depth, d, n = ada_w.shape
    b = c.shape[0]
    tn, bands = 1024, MOD_READ_STREAMS
    band_spec = lambda r: pl.BlockSpec((1, d // bands, tn), lambda l, j: (l, r, j))
    out = pl.pallas_call(
        _mod_kernel,
        grid=(depth, n // tn),
        in_specs=[pl.BlockSpec((b, d, LANES), lambda l, j: (0, 0, 0))]
                 + [band_spec(r) for r in range(bands)]
                 + [pl.BlockSpec((1, 1, tn), lambda l, j: (l, 0, j))],
        out_specs=pl.BlockSpec((1, b, tn), lambda l, j: (l, 0, j)),
        out_shape=jax.ShapeDtypeStruct((depth, b, n), F32),
        scratch_shapes=[pltpu.VMEM((b, d, LANES), F32)],
        compiler_params=_params("arbitrary", "arbitrary"),
        name="adaln_mods",
    )(jnp.broadcast_to(c[:, :, None], (b, d, LANES)), *([ada_w] * bands),
      ada_b.reshape(depth, 1, n))
    return out.reshape(depth, b, 3, d)


def _modulated_norm(x, shift, scale, gain):
    ms = jnp.mean(x * x, axis=-1, keepdims=True)
    return x * lax.rsqrt(ms + EPS) * gain * (1.0 + scale) + shift


def _layer0_kernel(x_ref, mod_ref, ng_ref, win32_ref, lb_ref, og_ref, pw_ref, ps_ref, wout32_ref,
                   o_ref, state_ref, carry_ref, win_ref, wout_ref, *, tile, chunk, layer_slot):
    T, C, H = tile, chunk, HGRN_HEADS
    batch = range(x_ref.shape[0])
    t_idx = pl.program_id(0)

    @pl.when(t_idx == 0)
    def _():
        state_ref[...] = jnp.zeros_like(state_ref)
        carry_ref[...] = jnp.zeros_like(carry_ref)
        win_ref[...] = win32_ref[...].astype(BF16)
        wout_ref[...] = wout32_ref[...].astype(BF16)

    lbv = lb_ref[...]
    e = jnp.exp(lbv - jnp.max(lbv, axis=0, keepdims=True))
    lower = (jnp.sum(e[0:layer_slot + 1], axis=0, keepdims=True)
             / jnp.sum(e, axis=0, keepdims=True))

    xs, parts = [], []
    for bi in batch:
        x = x_ref[bi]
        h = _modulated_norm(x, mod_ref[bi, 0:1, :], mod_ref[bi, 1:2, :], ng_ref[...])
        proj = jnp.dot(h.astype(BF16), win_ref[...], preferred_element_type=F32)
        o0, widths, piece = 0, (HGRN_KW, HGRN_KW, HGRN_VW, HGRN_VW, POOL_WIDTH, POOL_WIDTH), []
        for wd in widths:
            piece.append(proj[:, o0:o0 + wd])
            o0 += wd
        xs.append(x)
        parts.append(piece)

    row = lax.broadcasted_iota(jnp.int32, (T, HGRN_KW), 0)
    ti = lax.broadcasted_iota(jnp.int32, (C, C), 0)
    si = lax.broadcasted_iota(jnp.int32, (C, C), 1)
    n_chunks = T // C
    nt_dims = (((1,), (1,)), ((), ()))
    tn_dims = (((0,), (0,)), ((), ()))

    def level_scores(qe, ke, mask, scores):
        qb = qe.astype(BF16)
        kb = ke.astype(BF16)
        out = []
        for c in range(n_chunks):
            for hh in range(H):
                rs = slice(c * C, (c + 1) * C)
                cs = slice(hh * HGRN_KEY, (hh + 1) * HGRN_KEY)
                d = lax.dot_general(qb[rs, cs], kb[rs, cs], nt_dims, preferred_element_type=F32)
                prev = scores[c * H + hh]
                out.append(jnp.where(mask, d, 0.0 if prev is None else prev))
        return out

    scores, q_dec, k_dec, total = [], [], [], []
    for bi in batch:
        q, f = parts[bi][0], parts[bi][1]
        forget = lower + (1.0 - lower) * _sigmoid(f)
        key = 1.0 - forget
        scores.append(level_scores(q, key, ti == si, [None] * (n_chunks * H)))
        q_dec.append(q * forget)
        k_dec.append(key)
        total.append(forget)
    m = 1
    while m < C:
        mask = ((ti ^ si) < 2 * m) & ((ti & m) != 0) & ((si & m) == 0)
        upper = (row & m) != 0
        for bi in batch:
            scores[bi] = level_scores(q_dec[bi], k_dec[bi], mask, scores[bi])
            t_dn = pltpu.roll(total[bi], m, 0)
            t_up = pltpu.roll(total[bi], T - m, 0)
            q_dec[bi] = q_dec[bi] * jnp.where(upper, t_dn, 1.0)
            k_dec[bi] = k_dec[bi] * jnp.where(upper, 1.0, t_up)
            total[bi] = total[bi] * jnp.where(upper, t_dn, t_up)
        m *= 2

    o_a = []
    for bi in batch:
        val, g_a = parts[bi][2], parts[bi][3]
        q_in = q_dec[bi].astype(BF16)
        k_out = k_dec[bi].astype(BF16)
        decay = total[bi]
        val_b = val.astype(BF16)
        oa_rows = []
        for c in range(n_chunks):
            rs = slice(c * C, (c + 1) * C)
            heads_out = []
            for hh in range(H):
                cs = slice(hh * HGRN_KEY, (hh + 1) * HGRN_KEY)
                st = state_ref[bi, hh]
                vb = val_b[rs, cs]
                o = jnp.dot(scores[bi][c * H + hh].astype(BF16), vb, preferred_element_type=F32)
                o = o + lax.dot_general(q_in[rs, cs], st.astype(BF16), nt_dims,
                                        preferred_element_type=F32)
                state_ref[bi, hh] = (st * decay[c * C:c * C + 1, cs]
                                     + lax.dot_general(vb, k_out[rs, cs], tn_dims,
                                                       preferred_element_type=F32))
                ms_o = jnp.mean(o * o, axis=-1, keepdims=True)
                heads_out.append(o * lax.rsqrt(ms_o + EPS))
            oa_rows.append(jnp.concatenate(heads_out, axis=1))
        o_a.append(jnp.concatenate(oa_rows, axis=0) * og_ref[...] * _silu(g_a))

    pos = lax.broadcasted_iota(jnp.int32, (T, POOL_GROUP), 0) + t_idx * T + 1
    for bi in batch:
        u, g_b = parts[bi][4], parts[bi][5]
        ub = jnp.concatenate([carry_ref[bi], u], axis=0)
        carry_ref[bi] = u[T - POOL_HISTORY:T, :]
        wins = []
        acc = ub
        sh = 1
        while sh < max(POOL_WINDOWS):
            acc = acc + pltpu.roll(acc, sh, 0)
            sh *= 2
            wins.append(acc)
        ob = []
        for gi, w in enumerate(POOL_WINDOWS):
            cs = slice(gi * POOL_GROUP, (gi + 1) * POOL_GROUP)
            win = wins[w.bit_length() - 2][POOL_HISTORY:, cs]
            cnt = jnp.minimum(pos, w).astype(F32)
            pooled = win / cnt - u[:, cs]
            ob.append(jnp.dot(pooled.astype(BF16), pw_ref[gi], preferred_element_type=F32))
        o_b = jnp.concatenate(ob, axis=1) * ps_ref[...] * _silu(g_b)
        mixed = jnp.concatenate([o_a[bi], o_b], axis=1).astype(BF16)
        y = jnp.dot(mixed, wout_ref[...], preferred_element_type=F32)
        o_ref[bi] = xs[bi] + mod_ref[bi, 2:3, :] * y


def _layer0(x, mod, norm_g, w_in, hgrn_lb, onorm_g, pool_w, pool_scale, w_out, layer_slot):
    b, s, d = x.shape
    tile = SEQ_TILE
    assert s % tile == 0 and tile % HGRN_CHUNK == 0, (s, tile)
    n_in = w_in.shape[1]
    mix = w_out.shape[0]
    const2 = lambda ti: (0, 0)
    const3 = lambda ti: (0, 0, 0)
    return pl.pallas_call(
        functools.partial(_layer0_kernel, tile=tile, chunk=HGRN_CHUNK, layer_slot=layer_slot),
        grid=(s // tile,),
        in_specs=[pl.BlockSpec((b, tile, d), lambda ti: (0, ti, 0)),
                  pl.BlockSpec((b, 3, d), const3),
                  pl.BlockSpec((1, d), const2),
                  pl.BlockSpec((d, n_in), const2, pipeline_mode=pl.Buffered(1)),
                  pl.BlockSpec(hgrn_lb.shape, const2),
                  pl.BlockSpec((1, HGRN_VW), const2),
                  pl.BlockSpec(pool_w.shape, const3),
                  pl.BlockSpec((1, POOL_WIDTH), const2),
                  pl.BlockSpec((mix, d), const2, pipeline_mode=pl.Buffered(1))],
        out_specs=pl.BlockSpec((b, tile, d), lambda ti: (0, ti, 0)),
        out_shape=jax.ShapeDtypeStruct((b, s, d), F32),
        scratch_shapes=[pltpu.VMEM((b, HGRN_HEADS, HGRN_VAL, HGRN_KEY), F32),
                        pltpu.VMEM((b, POOL_HISTORY, POOL_WIDTH), F32),
                        pltpu.VMEM((d, n_in), BF16),
                        pltpu.VMEM((mix, d), BF16)],
        compiler_params=_params("arbitrary"),
        name="layer0_hgrn_pool",
    )(x, mod, norm_g.reshape(1, d), w_in, hgrn_lb, onorm_g.reshape(1, HGRN_VW),
      pool_w.astype(BF16), pool_scale.reshape(1, POOL_WIDTH), w_out)


def _split3(c):
    hi = c.astype(BF16).astype(F32)
    r = c - hi
    mid = r.astype(BF16).astype(F32)
    lo = (r - mid).astype(BF16).astype(F32)
    return hi, mid, lo


def _carrier_selectors():
    per_tile = LANES // FOX_HEAD_DIM
    sel = np.zeros((FOX_HEADS // per_tile, LANES, per_tile * LANES), np.float32)
    for hd in range(FOX_HEADS):
        base = (hd % per_tile) * LANES + AUG_LANE
        for j in range(3):
            sel[hd // per_tile, j * FOX_HEADS + hd, base + j] = 1.0
            sel[hd // per_tile, j * FOX_HEADS + hd, base + 3 + j] = -1.0
            sel[hd // per_tile, (3 + j) * FOX_HEADS + hd, base + 6 + j] = -1.0
    return jnp.asarray(sel, BF16)


def _layer1_proj_kernel(x_ref, mod_ref, ng_ref, win_ref, bf_ref, qg_ref, kg_ref, bd_ref, sel_ref,
                        qa_ref, ka_ref, vt_ref, g_ref, cq_ref, edge_ref, carry_ref, tile_start_ref,
                        *, tile):
    T, W = tile, FOX_WIDTH
    batch = range(x_ref.shape[0])
    t_idx = pl.program_id(0)
    lane = lax.broadcasted_iota(jnp.int32, (T, LANES), 1)
    row = lax.broadcasted_iota(jnp.int32, (T, LANES), 0)
    bd = bd_ref[...]
    qk_scale = FOX_HEAD_DIM ** -0.5 * LOG2E

    hs = [_modulated_norm(x_ref[bi], mod_ref[bi, 0:1, :], mod_ref[bi, 1:2, :],
                          ng_ref[...]).astype(BF16) for bi in batch]
    ones_q = jnp.where((lane >= AUG_LANE + 3) & (lane < AUG_LANE + 9), 1.0, 0.0)
    ones_k = jnp.where((lane >= AUG_LANE) & (lane < AUG_LANE + 3), 1.0, 0.0)
    forget_logits = [jnp.dot(hs[bi], win_ref[:, 4 * W:], preferred_element_type=F32)
                     for bi in batch]

    def big_product(bi):
        return jnp.dot(hs[bi], win_ref[:, :4 * W], preferred_element_type=F32)

    projs, carriers_packed = [], []
    for bi in batch:
        z = forget_logits[bi] + bf_ref[...]
        logf = jnp.minimum(z, 0.0) - jnp.log1p(jnp.exp(-jnp.abs(z)))
        cum = logf
        sh = 1
        while sh < T:
            cum = cum + jnp.where(row >= sh, pltpu.roll(cum, sh, 0), 0.0)
            sh *= 2
        start = jnp.where(t_idx == 0, 0.0, carry_ref[bi])
        end = start + cum[T - 1:T, :]
        carry_ref[bi] = end
        tile_start = jnp.where(t_idx % (ATTN_Q_TILE // T) == 0, start, tile_start_ref[bi])
        tile_start_ref[bi] = tile_start
        edge_ref[bi, 0] = jnp.concatenate([tile_start, end], axis=0) * LOG2E
        s_hi, s_mid, s_lo = _split3((cum + (start - tile_start)) * LOG2E)
        e_hi, e_mid, e_lo = _split3((cum - cum[T - 1:T, :]) * LOG2E)
        total_t = ((cum + start) * LOG2E).T
        for hd in range(FOX_HEADS):
            cq_ref[bi, hd] = total_t[hd:hd + 1, :]

        terms = (s_hi, s_mid, s_lo, e_hi, e_mid, e_lo)
        packed = jnp.zeros((T, LANES), F32)
        for j, term in enumerate(terms):
            moved = term if j == 0 else pltpu.roll(term, j * FOX_HEADS, 1)
            packed = jnp.where((lane >= j * FOX_HEADS) & (lane < (j + 1) * FOX_HEADS), moved,
                               packed)
        carriers_packed.append(packed.astype(BF16))

    def pair_inputs(bi, pair):
        cs = slice(pair * LANES, (pair + 1) * LANES)
        return projs[bi][:, 0:W][:, cs], projs[bi][:, W:2 * W][:, cs]

    def small_products(bi):
        mean_squares, placed = [], []
        for pair in range(FOX_HEADS // 2):
            qp, kp = pair_inputs(bi, pair)
            squares = jnp.concatenate([(qp * qp).astype(BF16), (kp * kp).astype(BF16)], axis=1)
            mean_squares.append(jnp.dot(squares, bd, preferred_element_type=F32))
            placed.append(jnp.dot(carriers_packed[bi], sel_ref[pair],
                                  preferred_element_type=F32))
        return mean_squares, placed

    def assemble(bi, mean_squares, placed):
        vt_ref[bi, 0] = projs[bi][:, 2 * W:3 * W].T.astype(BF16)
        g_ref[bi, 0] = projs[bi][:, 3 * W:4 * W].T
        for pair in range(FOX_HEADS // 2):
            qp, kp = pair_inputs(bi, pair)
            ms, aug = mean_squares[pair], placed[pair]
            qn = qp * lax.rsqrt(ms[:, :LANES] + EPS) * (qg_ref[...] * qk_scale)
            kn = kp * lax.rsqrt(ms[:, LANES:] + EPS) * kg_ref[...]
            for e in range(LANES // FOX_HEAD_DIM):
                hd = (LANES // FOX_HEAD_DIM) * pair + e
                carriers = aug[:, e * LANES:(e + 1) * LANES]
                q_main = qn if e == 0 else pltpu.roll(qn, FOX_HEAD_DIM, 1)
                k_main = kn if e == 0 else pltpu.roll(kn, FOX_HEAD_DIM, 1)
                q_aug = jnp.where(lane < AUG_LANE + 3, carriers, ones_q)
                k_aug = jnp.where(lane < AUG_LANE + 3, ones_k, carriers)
                qa_ref[bi, hd] = jnp.where(lane < FOX_HEAD_DIM, q_main, q_aug).astype(BF16)
                ka_ref[bi, hd] = jnp.where(lane < FOX_HEAD_DIM, k_main, k_aug).astype(BF16)

    for bi in batch:
        projs.append(big_product(bi))
        assemble(bi, *small_products(bi))


def _layer1_proj(x, mod, norm_g, w_in, b_f, qnorm_g, knorm_g):
    b, s, d = x.shape
    tile = SEQ_TILE
    assert s % tile == 0 and ATTN_Q_TILE % tile == 0, (s, tile)
    nt = s // tile
    w = FOX_WIDTH
    n_pad = w_in.shape[1]
    bf_pad = jnp.zeros((1, LANES), F32).at[0, :FOX_HEADS].set(b_f)
    blk = jnp.arange(2 * LANES) // FOX_HEAD_DIM
    bd = jnp.where(blk[:, None] == blk[None, :], 1.0 / FOX_HEAD_DIM, 0.0).astype(BF16)
    qg = jnp.tile(qnorm_g, LANES // FOX_HEAD_DIM).reshape(1, LANES)
    kg = jnp.tile(knorm_g, LANES // FOX_HEAD_DIM).reshape(1, LANES)
    const2 = lambda ti: (0, 0)
    const3 = lambda ti: (0, 0, 0)
    return pl.pallas_call(
        functools.partial(_layer1_proj_kernel, tile=tile),
        grid=(nt,),
        in_specs=[pl.BlockSpec((b, tile, d), lambda ti: (0, ti, 0)),
                  pl.BlockSpec((b, 3, d), const3),
                  pl.BlockSpec((1, d), const2),
                  pl.BlockSpec((d, n_pad), const2, pipeline_mode=pl.Buffered(1)),
                  pl.BlockSpec((1, LANES), const2),
                  pl.BlockSpec((1, LANES), const2),
                  pl.BlockSpec((1, LANES), const2),
                  pl.BlockSpec((2 * LANES, 2 * LANES), const2),
                  pl.BlockSpec((FOX_HEADS // 2, LANES, 2 * LANES), const3)],
        out_specs=[pl.BlockSpec((b, FOX_HEADS, tile, LANES), lambda ti: (0, 0, ti, 0)),
                   pl.BlockSpec((b, FOX_HEADS, tile, LANES), lambda ti: (0, 0, ti, 0)),
                   pl.BlockSpec((b, 1, w, tile), lambda ti: (0, ti, 0, 0)),
                   pl.BlockSpec((b, 1, w, tile), lambda ti: (0, ti, 0, 0)),
                   pl.BlockSpec((b, FOX_HEADS, 1, tile), lambda ti: (0, 0, 0, ti)),
                   pl.BlockSpec((b, 1, 2, LANES), lambda ti: (0, ti, 0, 0))],
        out_shape=[jax.ShapeDtypeStruct((b, FOX_HEADS, s, LANES), BF16),
                   jax.ShapeDtypeStruct((b, FOX_HEADS, s, LANES), BF16),
                   jax.ShapeDtypeStruct((b, nt, w, tile), BF16),
                   jax.ShapeDtypeStruct((b, nt, w, tile), F32),
                   jax.ShapeDtypeStruct((b, FOX_HEADS, 1, s), F32),
                   jax.ShapeDtypeStruct((b, nt, 2, LANES), F32)],
        scratch_shapes=[pltpu.VMEM((b, 1, LANES), F32), pltpu.VMEM((b, 1, LANES), F32)],
        compiler_params=_params("arbitrary"),
        name="layer1_proj",
    )(x, mod, norm_g.reshape(1, d), w_in, bf_pad, qg, kg, bd, _carrier_selectors())


def _query_operands(q):
    slab = 2 * SUBLANES
    row = lax.broadcasted_iota(jnp.int32, (slab, q.shape[0]), 0)
    q_t = q.astype(F32).T
    carriers = q_t[AUG_LANE:AUG_LANE + slab]
    return tuple(jnp.concatenate([q_t[:AUG_LANE], jnp.where(keep, carriers, 0.0),
                                  q_t[AUG_LANE + slab:]], axis=0).astype(BF16)
                 for keep in (row < 6, row >= 6))


def _gated_output(acc_t, g_ref, o_ref, first_block, lanes):
    n_blocks = acc_t.shape[1] // g_ref.shape[-1]
    gate = jnp.concatenate([g_ref[0, first_block + r] for r in range(n_blocks)], axis=1)
    o_ref[0, :, lanes] = (acc_t * _silu(gate)).astype(BF16)


def _fox_bounded_kernel(first_ref, end_ref, qa_ref, ka_ref, vt_ref, cq_ref, g_ref, o_ref,
                        p_ref, acc_ref, *, tile, heads):
    TQ, HP = tile, heads
    QT = qa_ref.shape[2] // TQ
    SB = vt_ref.shape[-1]
    R = TQ // SB
    n_blocks = ka_ref.shape[2] // SB
    b = pl.program_id(0)
    group = pl.program_id(1)
    i0 = pl.program_id(2) * QT
    chains = [(qs, e) for qs in range(QT) for e in range(HP)]
    cols = [slice(qs * TQ, (qs + 1) * TQ) for qs in range(QT)]
    operands = {(qs, e): _query_operands(qa_ref[0, e, cols[qs], :]) for qs, e in chains}
    causal = (lax.broadcasted_iota(jnp.int32, (TQ, TQ), 0)
              <= lax.broadcasted_iota(jnp.int32, (TQ, TQ), 1))

    def keys(e, n):
        return ka_ref[0, e, pl.ds(pl.multiple_of(n * TQ, TQ), TQ), :]

    def values_t(e, blk):
        return vt_ref[0, blk, pl.ds(e * FOX_HEAD_DIM, FOX_HEAD_DIM), :]

    def rest(c, blk, gate=1.0):
        qs, e = c
        end = end_ref[(b * n_blocks + blk) * FOX_HEADS + HP * group + e]
        return jnp.exp2(jnp.minimum(cq_ref[0, e][:, cols[qs]] - end, 0.0)) * gate

    def col_partial(p):
        return jnp.sum(p.reshape(p.shape[0] // SUBLANES, SUBLANES, p.shape[1]), axis=0)

    def far_scores(c, n):
        return jnp.dot(keys(c[1], n), operands[c][1], preferred_element_type=F32)

    def contract(c, n, p_bf16, gate=1.0):
        for r in range(R):
            pv = jnp.dot(values_t(c[1], n * R + r), p_bf16[r * SB:(r + 1) * SB],
                         preferred_element_type=F32)
            acc_ref[c] += rest(c, n * R + r, gate) * pv

    def weighted_partial(c, n, p, gate=1.0):
        return sum(rest(c, n * R + r, gate) * col_partial(p[r * SB:(r + 1) * SB])
                   for r in range(R))

    first = first_ref[(b * pl.num_programs(1) + group) * pl.num_programs(2) + pl.program_id(2)]
    gate0 = jnp.where(i0 > 0, 1.0, 0.0)

    l_part = {}

    def own_tile(c):
        qs, e = c
        k_own = keys(e, i0 + qs)
        pieces = []
        for r in range(R):
            n_keys, lanes = (r + 1) * SB, slice(r * SB, (r + 1) * SB)
            s_t = jnp.dot(k_own[:n_keys], operands[c][0][:, lanes], preferred_element_type=F32)
            pieces.append(jnp.where(causal[:n_keys, lanes], jnp.exp2(s_t), 0.0))
        l_part[c] = jnp.concatenate([col_partial(p) for p in pieces], axis=1)
        pieces = [p.astype(BF16) for p in pieces]

        def later():
            v_t = jnp.concatenate([values_t(e, (i0 + qs) * R + r) for r in range(R)], axis=1)
            for r, p in enumerate(pieces):
                acc_ref[qs, e, :, r * SB:(r + 1) * SB] = jnp.dot(
                    v_t[:, :(r + 1) * SB], p, preferred_element_type=F32)
        return later

    def earlier_tile(c, n):
        p = jnp.exp2(far_scores(c, n))
        l_part[c] = l_part[c] + weighted_partial(c, n, p)
        p = p.astype(BF16)
        return lambda: contract(c, n, p)

    def fill(c):
        p = jnp.exp2(far_scores(c, first))
        l_part[c] = l_part[c] + weighted_partial(c, first, p, gate0)
        p_ref[c] = p.astype(BF16)

    jobs = ([functools.partial(own_tile, c) for c in chains]
            + [functools.partial(earlier_tile, c, i0 + t) for c in chains for t in range(c[0])]
            + [functools.partial(fill, c) for c in chains])
    pending = None
    for job in jobs:
        later = job()
        if pending is not None:
            pending()
        pending = later
    if pending is not None:
        pending()

    def body(n, l_run):
        s_new = []
        for c in chains:
            s_new.append(far_scores(c, n))
            contract(c, n - 1, p_ref[c])
        out = []
        for c, s_t, l_c in zip(chains, s_new, l_run):
            p = jnp.exp2(s_t)
            out.append(l_c + weighted_partial(c, n, p))
            p_ref[c] = p.astype(BF16)
        return tuple(out)

    l_run = lax.fori_loop(first + 1, i0, body, tuple(l_part[c] for c in chains))
    for c in chains:
        contract(c, jnp.maximum(i0 - 1, 0), p_ref[c], gate0)
    for qs in range(QT):
        out_t = [acc_ref[qs, e] * (1.0 / jnp.sum(l_run[qs * HP + e], axis=0, keepdims=True))
                 for e in range(HP)]
        _gated_output(jnp.concatenate(out_t, axis=0), g_ref, o_ref, qs * R, cols[qs])


def _fox_online_kernel(start_ref, qa_ref, ka_ref, vt_ref, g_ref, o_ref, *, tile, heads):
    TB, HP = tile, heads
    n_blocks = ka_ref.shape[2] // TB
    b = pl.program_id(0)
    group = pl.program_id(1)
    i = pl.program_id(2)
    q_t = [_query_operands(qa_ref[0, e])[0] for e in range(HP)]
    k_row = lax.broadcasted_iota(jnp.int32, (TB, TB), 0)
    q_col = lax.broadcasted_iota(jnp.int32, (TB, TB), 1)

    def step(j, carry, masked):
        out = []
        for e in range(HP):
            m_run, l_run, acc = carry[e]
            k_blk = ka_ref[0, e, pl.ds(pl.multiple_of(j * TB, TB), TB), :]
            s_t = jnp.dot(k_blk, q_t[e], preferred_element_type=F32)
            if masked:
                s_t = jnp.where(k_row <= q_col, s_t, MASK_VALUE)
            hd = HP * group + e
            delta = (start_ref[(b * n_blocks + i) * FOX_HEADS + hd]
                     - start_ref[(b * n_blocks + j) * FOX_HEADS + hd])
            m_new = jnp.maximum(m_run, jnp.max(s_t, axis=0, keepdims=True) + delta)
            p_t = jnp.exp2(s_t - (m_new - delta))
            alpha = jnp.exp2(m_run - m_new)
            l_new = alpha * l_run + jnp.sum(p_t, axis=0, keepdims=True)
            v_t = vt_ref[0, j, pl.ds(e * FOX_HEAD_DIM, FOX_HEAD_DIM), :]
            acc_new = alpha * acc + jnp.dot(v_t, p_t.astype(BF16), preferred_element_type=F32)
            out.append((m_new, l_new, acc_new))
        return tuple(out)

    init = (jnp.full((1, TB), MASK_VALUE, F32), jnp.zeros((1, TB), F32),
            jnp.zeros((FOX_HEAD_DIM, TB), F32))
    carry = lax.fori_loop(0, i, lambda j, c: step(j, c, False), (init,) * HP)
    carry = step(i, carry, True)
    _gated_output(jnp.concatenate([acc / l_run for _, l_run, acc in carry], axis=0), g_ref, o_ref,
                  0, slice(None))


def _fox_attention(qa, ka, vt, g, cq, edges, *, bounded):
    b, n_heads, s, _ = qa.shape
    _, nvb, w, vb = vt.shape
    if bounded:
        tile, hp, qt = ATTN_Q_TILE, ATTN_HEADS, ATTN_TILES_PER_STEP
    else:
        tile, hp, qt = vb, LANES // FOX_HEAD_DIM, 1
    gw = hp * FOX_HEAD_DIM
    rows = qt * tile
    assert s % rows == 0 and n_heads % hp == 0 and tile % vb == 0, (s, rows, n_heads, hp, tile, vb)
    specs = dict(
        table=pl.BlockSpec(memory_space=pltpu.SMEM),
        q=pl.BlockSpec((1, hp, rows, LANES), lambda bi, p, i: (bi, p, i, 0)),
        k=pl.BlockSpec((1, hp, s, LANES), lambda bi, p, i: (bi, p, 0, 0)),
        v=pl.BlockSpec((1, nvb, gw, vb), lambda bi, p, i: (bi, 0, p, 0)),
        cq=pl.BlockSpec((1, hp, 1, rows), lambda bi, p, i: (bi, p, 0, i)),
        g=pl.BlockSpec((1, rows // vb, gw, vb), lambda bi, p, i: (bi, i, p, 0)),
        o=pl.BlockSpec((1, gw, rows), lambda bi, p, i: (bi, p, i)))
    if bounded:
        body = functools.partial(_fox_bounded_kernel, tile=tile, heads=hp)
        names = ("table", "table", "q", "k", "v", "cq", "g")
        step_start = edges[:, ::rows // vb, 0, :FOX_HEADS]
        tile_end = edges[:, tile // vb - 1::tile // vb, 1, :FOX_HEADS]
        dead = (step_start[:, :, None] - tile_end[:, None]) < -FOX_DEAD_EXPONENT
        before = jnp.arange(s // rows)[:, None] * qt > jnp.arange(s // tile)[None, :]
        first_live = jnp.sum(dead & before[None, :, :, None], axis=2)
        last_before = jnp.maximum(jnp.arange(s // rows) * qt - 1, 0)
        first_live = jnp.minimum(first_live, last_before[None, :, None])
        first_live = jnp.min(first_live.reshape(b, s // rows, n_heads // hp, hp), axis=-1)
        args = (first_live.transpose(0, 2, 1).reshape(-1).astype(jnp.int32),
                edges[:, :, 1, :FOX_HEADS].reshape(-1), qa, ka, vt, cq, g)
        scratch = [pltpu.VMEM((qt, hp, tile, tile), BF16),
                   pltpu.VMEM((qt, hp, FOX_HEAD_DIM, tile), F32)]
    else:
        body = functools.partial(_fox_online_kernel, tile=tile, heads=hp)
        names = ("table", "q", "k", "v", "g")
        args = (edges[:, :, 0, :FOX_HEADS].reshape(-1), qa, ka, vt, g)
        scratch = []
    return pl.pallas_call(
        body,
        grid=(b, n_heads // hp, s // rows),
        in_specs=[specs[n] for n in names],
        out_specs=specs["o"],
        out_shape=jax.ShapeDtypeStruct((b, w, s), BF16),
        scratch_shapes=scratch,
        compiler_params=_params("arbitrary", "arbitrary", "arbitrary"),
        name="fox_attention_bounded" if bounded else "fox_attention_online",
    )(*args)


def _out_proj_kernel(a_ref, x_ref, mod_ref, w_ref, o_ref):
    y = lax.dot_general(a_ref[0], w_ref[...], (((0,), (0,)), ((), ())),
                        preferred_element_type=F32)
    o_ref[0] = x_ref[0] + mod_ref[0, 2:3, :] * y


def _out_proj(a, x, mod, w_out):
    b, s, d = x.shape
    tile = OUT_PROJ_TILE
    assert s % tile == 0, (s, tile)
    k = a.shape[1]
    return pl.pallas_call(
        _out_proj_kernel,
        grid=(b, s // tile),
        in_specs=[pl.BlockSpec((1, k, tile), lambda bi, ti: (bi, 0, ti)),
                  pl.BlockSpec((1, tile, d), lambda bi, ti: (bi, ti, 0)),
                  pl.BlockSpec((1, 3, d), lambda bi, ti: (bi, 0, 0)),
                  pl.BlockSpec((k, d), lambda bi, ti: (0, 0))],
        out_specs=pl.BlockSpec((1, tile, d), lambda bi, ti: (bi, ti, 0)),
        out_shape=jax.ShapeDtypeStruct((b, s, d), F32),
        compiler_params=_params("arbitrary", "arbitrary"),
        name="layer1_out_proj",
    )(a, x, mod, w_out.astype(BF16))


def _permute_in_kernel(w_ref, p_ref, pf_ref, o_ref):
    w = FOX_WIDTH
    lead = (((0,), (0,)), ((), ()))
    for grp in range(4):
        cols = slice(grp * w, (grp + 1) * w)
        o_ref[:, cols] = lax.dot_general(w_ref[cols, :].astype(BF16), p_ref[...], lead,
                                         preferred_element_type=F32).astype(BF16)
    o_ref[:, 4 * w:] = lax.dot_general(w_ref[4 * w:, :].astype(BF16), pf_ref[...], lead,
                                       preferred_element_type=F32).astype(BF16)


def _permute_out_kernel(p_ref, w_ref, o_ref):
    o_ref[...] = lax.dot_general(p_ref[...], w_ref[...].astype(BF16), (((0,), (0,)), ((), ())),
                                 preferred_element_type=F32).astype(BF16)


def _permute_weights(w_in, w_out, order):
    d, w = w_in.shape[0], FOX_WIDTH
    band = 256
    assert d % band == 0 and w_out.shape == (w, d), (w_in.shape, w_out.shape)
    onehot = (jnp.arange(FOX_HEADS)[:, None] == order[None, :]).astype(BF16)
    head, lane = jnp.arange(w) // FOX_HEAD_DIM, jnp.arange(w) % FOX_HEAD_DIM
    perm = ((head[:, None] == order[head][None, :])
            & (lane[:, None] == lane[None, :])).astype(BF16)
    perm_f = jnp.zeros((FOX_HEADS, LANES), BF16).at[:, :FOX_HEADS].set(onehot)
    w_in_p = pl.pallas_call(
        _permute_in_kernel,
        grid=(d // band,),
        in_specs=[pl.BlockSpec((w_in.shape[1], band), lambda i: (0, i)),
                  pl.BlockSpec((w, w), lambda i: (0, 0)),
                  pl.BlockSpec((FOX_HEADS, LANES), lambda i: (0, 0))],
        out_specs=pl.BlockSpec((band, 4 * w + LANES), lambda i: (i, 0)),
        out_shape=jax.ShapeDtypeStruct((d, 4 * w + LANES), BF16),
        compiler_params=_params("arbitrary"),
        name="layer1_permute_w_in",
    )(w_in.T, perm, perm_f)
    w_out_p = pl.pallas_call(
        _permute_out_kernel,
        grid=(d // band,),
        in_specs=[pl.BlockSpec((w, w), lambda i: (0, 0)),
                  pl.BlockSpec((w, band), lambda i: (0, i))],
        out_specs=pl.BlockSpec((w, band), lambda i: (0, i)),
        out_shape=jax.ShapeDtypeStruct((w, d), BF16),
        compiler_params=_params("arbitrary"),
        name="layer1_permute_w_out",
    )(perm, w_out)
    return w_in_p, w_out_p


def _layer1(x, mod, norm_g, w_in, b_f, qnorm_g, knorm_g, w_out):
    order = jnp.argsort(b_f)
    w_in, w_out = _permute_weights(w_in, w_out, order)
    qa, ka, vt, g, cq, edges = _layer1_proj(x, mod, norm_g, w_in, b_f[order], qnorm_g, knorm_g)
    score_bound = (LOG2E * FOX_HEAD_DIM ** 0.5
                   * jnp.max(jnp.abs(qnorm_g)) * jnp.max(jnp.abs(knorm_g)))
    gated = lax.cond(score_bound <= FOX_SAFE_LOGIT,
                     functools.partial(_fox_attention, bounded=True),
                     functools.partial(_fox_attention, bounded=False),
                     qa, ka, vt, g, cq, edges)
    return _out_proj(gated, x, mod, w_out)


def kernel(x, c, norm_g, ada_w, ada_b, hgrn_lb, even_w_in, hgrn_onorm_g, pool_w, pool_scale,
           even_w_out, odd_w_in, fox_b_f, fox_qnorm_g, fox_knorm_g, odd_w_out):
    depth = norm_g.shape[0]
    mods = _adaln_mods(c, ada_w, ada_b)
    for l in range(depth):
        j = l // 2
        if l % 2 == 0:
            x = _layer0(x, mods[l], norm_g[l], even_w_in[j], hgrn_lb, hgrn_onorm_g[j],
                        pool_w[j], pool_scale[j], even_w_out[j], layer_slot=l)
        else:
            x = _layer1(x, mods[l], norm_g[l], odd_w_in[j], fox_b_f[j], fox_qnorm_g[j],
                        fox_knorm_g[j], odd_w_out[j])
    return x
```

```python
import functools

import jax
import jax.numpy as jnp
import numpy as np
from jax import lax
from jax.experimental import pallas as pl
from jax.experimental.pallas import tpu as pltpu

F32 = jnp.float32
BF16 = jnp.bfloat16
EPS = 1e-6

HGRN_HEADS = 4
HGRN_KEY = 128
HGRN_VAL = 128
HGRN_KW = HGRN_HEADS * HGRN_KEY
HGRN_VW = HGRN_HEADS * HGRN_VAL
POOL_WINDOWS = (2, 4, 8, 16)
POOL_GROUP = 128
POOL_WIDTH = POOL_GROUP * len(POOL_WINDOWS)
POOL_HISTORY = 16
FOX_HEADS = 16
FOX_HEAD_DIM = 64
FOX_WIDTH = FOX_HEADS * FOX_HEAD_DIM

LANES = 128
SUBLANES = 8
VMEM_LIMIT_BYTES = 56 * 1024 * 1024

SEQ_TILE = 256
OUT_PROJ_TILE = 2048
MOD_READ_STREAMS = 4
HGRN_CHUNK = 128
ATTN_Q_TILE = 512
ATTN_HEADS = 4
ATTN_TILES_PER_STEP = 2
FOX_SAFE_LOGIT = 100.0
FOX_DEAD_EXPONENT = 150.0
LOG2E = 1.4426950408889634
MASK_VALUE = -1e30
AUG_LANE = FOX_HEAD_DIM


def _sigmoid(x):
    return 0.5 * jnp.tanh(0.5 * x) + 0.5


def _silu(x):
    return x * _sigmoid(x)


def _params(*semantics):
    return pltpu.CompilerParams(dimension_semantics=semantics, vmem_limit_bytes=VMEM_LIMIT_BYTES)


def _mod_kernel(c_ref, *refs):
    *w_refs, b_ref, o_ref, cond_ref = refs

    @pl.when((pl.program_id(0) == 0) & (pl.program_id(1) == 0))
    def _():
        cond_ref[...] = _silu(c_ref[...])

    band = w_refs[0].shape[1]
    for bi in range(c_ref.shape[0]):
        acc = b_ref[0]
        for r, w_ref in enumerate(w_refs):
            w = w_ref[0]
            cond = cond_ref[bi, r * band:(r + 1) * band]
            acc = acc + jnp.concatenate(
                [jnp.sum(w[:, j:j + LANES] * cond, axis=0, keepdims=True)
                 for j in range(0, w.shape[1], LANES)], axis=1)
        o_ref[0, bi:bi + 1, :] = acc


def _adaln_mods(c, ada_w, ada_b):
    depth, d, n = ada_w.shape
    b = c.shape[0]
    tn, bands = 1024, MOD_READ_STREAMS
    band_spec = lambda r: pl.BlockSpec((1, d // bands, tn), lambda l, j: (l, r, j))
    out = pl.pallas_call(
        _mod_kernel,
        grid=(depth, n // tn),
        in_specs=[pl.BlockSpec((b, d, LANES), lambda l, j: (0, 0, 0))]
                 + [band_spec(r) for r in range(bands)]
                 + [pl.BlockSpec((1, 1, tn), lambda l, j: (l, 0, j))],
        out_specs=pl.BlockSpec((1, b, tn), lambda l, j: (l, 0, j)),
        out_shape=jax.ShapeDtypeStruct((depth, b, n), F32),
        scratch_shapes=[pltpu.VMEM((b, d, LANES), F32)],
        compiler_params=_params("arbitrary", "arbitrary"),
        name="adaln_mods",
    )(jnp.broadcast_to(c[:, :, None], (b, d, LANES)), *([ada_w] * bands),
      ada_b.reshape(depth, 1, n))
    return out.reshape(depth, b, 3, d)


def _modulated_norm(x, shift, scale, gain):
    ms = jnp.mean(x * x, axis=-1, keepdims=True)
    return x * lax.rsqrt(ms + EPS) * gain * (1.0 + scale) + shift


def _layer0_kernel(x_ref, mod_ref, ng_ref, win32_ref, lb_ref, og_ref, pw_ref, ps_ref, wout32_ref,
                   o_ref, state_ref, carry_ref, win_ref, wout_ref, *, tile, chunk, layer_slot):
    T, C, H = tile, chunk, HGRN_HEADS
    batch = range(x_ref.shape[0])
    t_idx = pl.program_id(0)

    @pl.when(t_idx == 0)
    def _():
        state_ref[...] = jnp.zeros_like(state_ref)
        carry_ref[...] = jnp.zeros_like(carry_ref)
        win_ref[...] = win32_ref[...].astype(BF16)
        wout_ref[...] = wout32_ref[...].astype(BF16)

    lbv = lb_ref[...]
    e = jnp.exp(lbv - jnp.max(lbv, axis=0, keepdims=True))
    lower = (jnp.sum(e[0:layer_slot + 1], axis=0, keepdims=True)
             / jnp.sum(e, axis=0, keepdims=True))

    xs, parts = [], []
    for bi in batch:
        x = x_ref[bi]
        h = _modulated_norm(x, mod_ref[bi, 0:1, :], mod_ref[bi, 1:2, :], ng_ref[...])
        proj = jnp.dot(h.astype(BF16), win_ref[...], preferred_element_type=F32)
        o0, widths, piece = 0, (HGRN_KW, HGRN_KW, HGRN_VW, HGRN_VW, POOL_WIDTH, POOL_WIDTH), []
        for wd in widths:
            piece.append(proj[:, o0:o0 + wd])
            o0 += wd
        xs.append(x)
        parts.append(piece)

    row = lax.broadcasted_iota(jnp.int32, (T, HGRN_KW), 0)
    ti = lax.broadcasted_iota(jnp.int32, (C, C), 0)
    si = lax.broadcasted_iota(jnp.int32, (C, C), 1)
    n_chunks = T // C
    nt_dims = (((1,), (1,)), ((), ()))
    tn_dims = (((0,), (0,)), ((), ()))

    def level_scores(qe, ke, mask, scores):
        qb = qe.astype(BF16)
        kb = ke.astype(BF16)
        out = []
        for c in range(n_chunks):
            for hh in range(H):
                rs = slice(c * C, (c + 1) * C)
                cs = slice(hh * HGRN_KEY, (hh + 1) * HGRN_KEY)
                d = lax.dot_general(qb[rs, cs], kb[rs, cs], nt_dims, preferred_element_type=F32)
                prev = scores[c * H + hh]
                out.append(jnp.where(mask, d, 0.0 if prev is None else prev))
        return out

    scores, q_dec, k_dec, total = [], [], [], []
    for bi in batch:
        q, f = parts[bi][0], parts[bi][1]
        forget = lower + (1.0 - lower) * _sigmoid(f)
        key = 1.0 - forget
        scores.append(level_scores(q, key, ti == si, [None] * (n_chunks * H)))
        q_dec.append(q * forget)
        k_dec.append(key)
        total.append(forget)
    m = 1
    while m < C:
        mask = ((ti ^ si) < 2 * m) & ((ti & m) != 0) & ((si & m) == 0)
        upper = (row & m) != 0
        for bi in batch:
            scores[bi] = level_scores(q_dec[bi], k_dec[bi], mask, scores[bi])
            t_dn = pltpu.roll(total[bi], m, 0)
            t_up = pltpu.roll(total[bi], T - m, 0)
            q_dec[bi] = q_dec[bi] * jnp.where(upper, t_dn, 1.0)
            k_dec[bi] = k_dec[bi] * jnp.where(upper, 1.0, t_up)
            total[bi] = total[bi] * jnp.where(upper, t_dn, t_up)
        m *= 2

    o_a = []
    for bi in batch:
        val, g_a = parts[bi][2], parts[bi][3]
        q_in = q_dec[bi].astype(BF16)
        k_out = k_dec[bi].astype(BF16)
        decay = total[bi]
        val_b = val.astype(BF16)
        oa_rows = []
        for c in range(n_chunks):
            rs = slice(c * C, (c + 1) * C)
            heads_out = []
            for hh in range(H):
                cs = slice(hh * HGRN_KEY, (hh + 1) * HGRN_KEY)
                st = state_ref[bi, hh]
                vb = val_b[rs, cs]
                o = jnp.dot(scores[bi][c * H + hh].astype(BF16), vb, preferred_element_type=F32)
                o = o + lax.dot_general(q_in[rs, cs], st.astype(BF16), nt_dims,
                                        preferred_element_type=F32)
                state_ref[bi, hh] = (st * decay[c * C:c * C + 1, cs]
                                     + lax.dot_general(vb, k_out[rs, cs], tn_dims,
                                                       preferred_element_type=F32))
                ms_o = jnp.mean(o * o, axis=-1, keepdims=True)
                heads_out.append(o * lax.rsqrt(ms_o + EPS))
            oa_rows.append(jnp.concatenate(heads_out, axis=1))
        o_a.append(jnp.concatenate(oa_rows, axis=0) * og_ref[...] * _silu(g_a))

    pos = lax.broadcasted_iota(jnp.int32, (T, POOL_GROUP), 0) + t_idx * T + 1
    for bi in batch:
        u, g_b = parts[bi][4], parts[bi][5]
        ub = jnp.concatenate([carry_ref[bi], u], axis=0)
        carry_ref[bi] = u[T - POOL_HISTORY:T, :]
        wins = []
        acc = ub
        sh = 1
        while sh < max(POOL_WINDOWS):
            acc = acc + pltpu.roll(acc, sh, 0)
            sh *= 2
            wins.append(acc)
        ob = []
        for gi, w in enumerate(POOL_WINDOWS):
            cs = slice(gi * POOL_GROUP, (gi + 1) * POOL_GROUP)
            win = wins[w.bit_length() - 2][POOL_HISTORY:, cs]
            cnt = jnp.minimum(pos, w).astype(F32)
            pooled = win / cnt - u[:, cs]
            ob.append(jnp.dot(pooled.astype(BF16), pw_ref[gi], preferred_element_type=F32))
        o_b = jnp.concatenate(ob, axis=1) * ps_ref[...] * _silu(g_b)
        mixed = jnp.concatenate([o_a[bi], o_b], axis=1).astype(BF16)
        y = jnp.dot(mixed, wout_ref[...], preferred_element_type=F32)
        o_ref[bi] = xs[bi] + mod_ref[bi, 2:3, :] * y


def _layer0(x, mod, norm_g, w_in, hgrn_lb, onorm_g, pool_w, pool_scale, w_out, layer_slot):
    b, s, d = x.shape
    tile = SEQ_TILE
    assert s % tile == 0 and tile % HGRN_CHUNK == 0, (s, tile)
    n_in = w_in.shape[1]
    mix = w_out.shape[0]
    const2 = lambda ti: (0, 0)
    const3 = lambda ti: (0, 0, 0)
    return pl.pallas_call(
        functools.partial(_layer0_kernel, tile=tile, chunk=HGRN_CHUNK, layer_slot=layer_slot),
        grid=(s // tile,),
        in_specs=[pl.BlockSpec((b, tile, d), lambda ti: (0, ti, 0)),
                  pl.BlockSpec((b, 3, d), const3),
                  pl.BlockSpec((1, d), const2),
                  pl.BlockSpec((d, n_in), const2, pipeline_mode=pl.Buffered(1)),
                  pl.BlockSpec(hgrn_lb.shape, const2),
                  pl.BlockSpec((1, HGRN_VW), const2),
                  pl.BlockSpec(pool_w.shape, const3),
                  pl.BlockSpec((1, POOL_WIDTH), const2),
                  pl.BlockSpec((mix, d), const2, pipeline_mode=pl.Buffered(1))],
        out_specs=pl.BlockSpec((b, tile, d), lambda ti: (0, ti, 0)),
        out_shape=jax.ShapeDtypeStruct((b, s, d), F32),
        scratch_shapes=[pltpu.VMEM((b, HGRN_HEADS, HGRN_VAL, HGRN_KEY), F32),
                        pltpu.VMEM((b, POOL_HISTORY, POOL_WIDTH), F32),
                        pltpu.VMEM((d, n_in), BF16),
                        pltpu.VMEM((mix, d), BF16)],
        compiler_params=_params("arbitrary"),
        name="layer0_hgrn_pool",
    )(x, mod, norm_g.reshape(1, d), w_in, hgrn_lb, onorm_g.reshape(1, HGRN_VW),
      pool_w.astype(BF16), pool_scale.reshape(1, POOL_WIDTH), w_out)


def _split3(c):
    hi = c.astype(BF16).astype(F32)
    r = c - hi
    mid = r.astype(BF16).astype(F32)
    lo = (r - mid).astype(BF16).astype(F32)
    return hi, mid, lo


def _carrier_selectors():
    per_tile = LANES // FOX_HEAD_DIM
    sel = np.zeros((FOX_HEADS // per_tile, LANES, per_tile * LANES), np.float32)
    for hd in range(FOX_HEADS):
        base = (hd % per_tile) * LANES + AUG_LANE
        for j in range(3):
            sel[hd // per_tile, j * FOX_HEADS + hd, base + j] = 1.0
            sel[hd // per_tile, j * FOX_HEADS + hd, base + 3 + j] = -1.0
            sel[hd // per_tile, (3 + j) * FOX_HEADS + hd, base + 6 + j] = -1.0
    return jnp.asarray(sel, BF16)


def _layer1_proj_kernel(x_ref, mod_ref, ng_ref, win_ref, bf_ref, qg_ref, kg_ref, bd_ref, sel_ref,
                        qa_ref, ka_ref, vt_ref, g_ref, cq_ref, edge_ref, carry_ref, tile_start_ref,
                        *, tile):
    T, W = tile, FOX_WIDTH
    batch = range(x_ref.shape[0])
    t_idx = pl.program_id(0)
    lane = lax.broadcasted_iota(jnp.int32, (T, LANES), 1)
    row = lax.broadcasted_iota(jnp.int32, (T, LANES), 0)
    bd = bd_ref[...]
    qk_scale = FOX_HEAD_DIM ** -0.5 * LOG2E

    hs = [_modulated_norm(x_ref[bi], mod_ref[bi, 0:1, :], mod_ref[bi, 1:2, :],
                          ng_ref[...]).astype(BF16) for bi in batch]
    ones_q = jnp.where((lane >= AUG_LANE + 3) & (lane < AUG_LANE + 9), 1.0, 0.0)
    ones_k = jnp.where((lane >= AUG_LANE) & (lane < AUG_LANE + 3), 1.0, 0.0)
    forget_logits = [jnp.dot(hs[bi], win_ref[:, 4 * W:], preferred_element_type=F32)
                     for bi in batch]

    def big_product(bi):
        return jnp.dot(hs[bi], win_ref[:, :4 * W], preferred_element_type=F32)

    projs, carriers_packed = [], []
    for bi in batch:
        z = forget_logits[bi] + bf_ref[...]
        logf = jnp.minimum(z, 0.0) - jnp.log1p(jnp.exp(-jnp.abs(z)))
        cum = logf
        sh = 1
        while sh < T:
            cum = cum + jnp.where(row >= sh, pltpu.roll(cum, sh, 0), 0.0)
            sh *= 2
        start = jnp.where(t_idx == 0, 0.0, carry_ref[bi])
        end = start + cum[T - 1:T, :]
        carry_ref[bi] = end
        tile_start = jnp.where(t_idx % (ATTN_Q_TILE // T) == 0, start, tile_start_ref[bi])
        tile_start_ref[bi] = tile_start
        edge_ref[bi, 0] = jnp.concatenate([tile_start, end], axis=0) * LOG2E
        s_hi, s_mid, s_lo = _split3((cum + (start - tile_start)) * LOG2E)
        e_hi, e_mid, e_lo = _split3((cum - cum[T - 1:T, :]) * LOG2E)
        total_t = ((cum + start) * LOG2E).T
        for hd in range(FOX_HEADS):
            cq_ref[bi, hd] = total_t[hd:hd + 1, :]

        terms = (s_hi, s_mid, s_lo, e_hi, e_mid, e_lo)
        packed = jnp.zeros((T, LANES), F32)
        for j, term in enumerate(terms):
            moved = term if j == 0 else pltpu.roll(term, j * FOX_HEADS, 1)
            packed = jnp.where((lane >= j * FOX_HEADS) & (lane < (j + 1) * FOX_HEADS), moved,
                               packed)
        carriers_packed.append(packed.astype(BF16))

    def pair_inputs(bi, pair):
        cs = slice(pair * LANES, (pair + 1) * LANES)
        return projs[bi][:, 0:W][:, cs], projs[bi][:, W:2 * W][:, cs]

    def small_products(bi):
        mean_squares, placed = [], []
        for pair in range(FOX_HEADS // 2):
            qp, kp = pair_inputs(bi, pair)
            squares = jnp.concatenate([(qp * qp).astype(BF16), (kp * kp).astype(BF16)], axis=1)
            mean_squares.append(jnp.dot(squares, bd, preferred_element_type=F32))
            placed.append(jnp.dot(carriers_packed[bi], sel_ref[pair],
                                  preferred_element_type=F32))
        return mean_squares, placed

    def assemble(bi, mean_squares, placed):
        vt_ref[bi, 0] = projs[bi][:, 2 * W:3 * W].T.astype(BF16)
        g_ref[bi, 0] = projs[bi][:, 3 * W:4 * W].T
        for pair in range(FOX_HEADS // 2):
            qp, kp = pair_inputs(bi, pair)
            ms, aug = mean_squares[pair], placed[pair]
            qn = qp * lax.rsqrt(ms[:, :LANES] + EPS) * (qg_ref[...] * qk_scale)
            kn = kp * lax.rsqrt(ms[:, LANES:] + EPS) * kg_ref[...]
            for e in range(LANES // FOX_HEAD_DIM):
                hd = (LANES // FOX_HEAD_DIM) * pair + e
                carriers = aug[:, e * LANES:(e + 1) * LANES]
                q_main = qn if e == 0 else pltpu.roll(qn, FOX_HEAD_DIM, 1)
                k_main = kn if e == 0 else pltpu.roll(kn, FOX_HEAD_DIM, 1)
                q_aug = jnp.where(lane < AUG_LANE + 3, carriers, ones_q)
                k_aug = jnp.where(lane < AUG_LANE + 3, ones_k, carriers)
                qa_ref[bi, hd] = jnp.where(lane < FOX_HEAD_DIM, q_main, q_aug).astype(BF16)
                ka_ref[bi, hd] = jnp.where(lane < FOX_HEAD_DIM, k_main, k_aug).astype(BF16)

    for bi in batch:
        projs.append(big_product(bi))
        assemble(bi, *small_products(bi))


def _layer1_proj(x, mod, norm_g, w_in, b_f, qnorm_g, knorm_g):
    b, s, d = x.shape
    tile = SEQ_TILE
    assert s % tile == 0 and ATTN_Q_TILE % tile == 0, (s, tile)
    nt = s // tile
    w = FOX_WIDTH
    n_pad = w_in.shape[1]
    bf_pad = jnp.zeros((1, LANES), F32).at[0, :FOX_HEADS].set(b_f)
    blk = jnp.arange(2 * LANES) // FOX_HEAD_DIM
    bd = jnp.where(blk[:, None] == blk[None, :], 1.0 / FOX_HEAD_DIM, 0.0).astype(BF16)
    qg = jnp.tile(qnorm_g, LANES // FOX_HEAD_DIM).reshape(1, LANES)
    kg = jnp.tile(knorm_g, LANES // FOX_HEAD_DIM).reshape(1, LANES)
    const2 = lambda ti: (0, 0)
    const3 = lambda ti: (0, 0, 0)
    return pl.pallas_call(
        functools.partial(_layer1_proj_kernel, tile=tile),
        grid=(nt,),
        in_specs=[pl.BlockSpec((b, tile, d), lambda ti: (0, ti, 0)),
                  pl.BlockSpec((b, 3, d), const3),
                  pl.BlockSpec((1, d), const2),
                  pl.BlockSpec((d, n_pad), const2, pipeline_mode=pl.Buffered(1)),
                  pl.BlockSpec((1, LANES), const2),
                  pl.BlockSpec((1, LANES), const2),
                  pl.BlockSpec((1, LANES), const2),
                  pl.BlockSpec((2 * LANES, 2 * LANES), const2),
                  pl.BlockSpec((FOX_HEADS // 2, LANES, 2 * LANES), const3)],
        out_specs=[pl.BlockSpec((b, FOX_HEADS, tile, LANES), lambda ti: (0, 0, ti, 0)),
                   pl.BlockSpec((b, FOX_HEADS, tile, LANES), lambda ti: (0, 0, ti, 0)),
                   pl.BlockSpec((b, 1, w, tile), lambda ti: (0, ti, 0, 0)),
                   pl.BlockSpec((b, 1, w, tile), lambda ti: (0, ti, 0, 0)),
                   pl.BlockSpec((b, FOX_HEADS, 1, tile), lambda ti: (0, 0, 0, ti)),
                   pl.BlockSpec((b, 1, 2, LANES), lambda ti: (0, ti, 0, 0))],
        out_shape=[jax.ShapeDtypeStruct((b, FOX_HEADS, s, LANES), BF16),
                   jax.ShapeDtypeStruct((b, FOX_HEADS, s, LANES), BF16),
                   jax.ShapeDtypeStruct((b, nt, w, tile), BF16),
                   jax.ShapeDtypeStruct((b, nt, w, tile), F32),
                   jax.ShapeDtypeStruct((b, FOX_HEADS, 1, s), F32),
                   jax.ShapeDtypeStruct((b, nt, 2, LANES), F32)],
        scratch_shapes=[pltpu.VMEM((b, 1, LANES), F32), pltpu.VMEM((b, 1, LANES), F32)],
        compiler_params=_params("arbitrary"),
        name="layer1_proj",
    )(x, mod, norm_g.reshape(1, d), w_in, bf_pad, qg, kg, bd, _carrier_selectors())


def _query_operands(q):
    slab = 2 * SUBLANES
    row = lax.broadcasted_iota(jnp.int32, (slab, q.shape[0]), 0)
    q_t = q.astype(F32).T
    carriers = q_t[AUG_LANE:AUG_LANE + slab]
    return tuple(jnp.concatenate([q_t[:AUG_LANE], jnp.where(keep, carriers, 0.0),
                                  q_t[AUG_LANE + slab:]], axis=0).astype(BF16)
                 for keep in (row < 6, row >= 6))


def _gated_output(acc_t, g_ref, o_ref, first_block, lanes):
    n_blocks = acc_t.shape[1] // g_ref.shape[-1]
    gate = jnp.concatenate([g_ref[0, first_block + r] for r in range(n_blocks)], axis=1)
    o_ref[0, :, lanes] = (acc_t * _silu(gate)).astype(BF16)


def _fox_bounded_kernel(first_ref, end_ref, qa_ref, ka_ref, vt_ref, cq_ref, g_ref, o_ref,
                        p_ref, acc_ref, *, tile, heads):
    TQ, HP = tile, heads
    QT = qa_ref.shape[2] // TQ
    SB = vt_ref.shape[-1]
    R = TQ // SB
    n_blocks = ka_ref.shape[2] // SB
    b = pl.program_id(0)
    group = pl.program_id(1)
    i0 = pl.program_id(2) * QT
    chains = [(qs, e) for qs in range(QT) for e in range(HP)]
    cols = [slice(qs * TQ, (qs + 1) * TQ) for qs in range(QT)]
    operands = {(qs, e): _query_operands(qa_ref[0, e, cols[qs], :]) for qs, e in chains}
    causal = (lax.broadcasted_iota(jnp.int32, (TQ, TQ), 0)
              <= lax.broadcasted_iota(jnp.int32, (TQ, TQ), 1))

    def keys(e, n):
        return ka_ref[0, e, pl.ds(pl.multiple_of(n * TQ, TQ), TQ), :]

    def values_t(e, blk):
        return vt_ref[0, blk, pl.ds(e * FOX_HEAD_DIM, FOX_HEAD_DIM), :]

    def rest(c, blk, gate=1.0):
        qs, e = c
        end = end_ref[(b * n_blocks + blk) * FOX_HEADS + HP * group + e]
        return jnp.exp2(jnp.minimum(cq_ref[0, e][:, cols[qs]] - end, 0.0)) * gate

    def col_partial(p):
        return jnp.sum(p.reshape(p.shape[0] // SUBLANES, SUBLANES, p.shape[1]), axis=0)

    def far_scores(c, n):
        return jnp.dot(keys(c[1], n), operands[c][1], preferred_element_type=F32)

    def contract(c, n, p_bf16, gate=1.0):
        for r in range(R):
            pv = jnp.dot(values_t(c[1], n * R + r), p_bf16[r * SB:(r + 1) * SB],
                         preferred_element_type=F32)
            acc_ref[c] += rest(c, n * R + r, gate) * pv

    def weighted_partial(c, n, p, gate=1.0):
        return sum(rest(c, n * R + r, gate) * col_partial(p[r * SB:(r + 1) * SB])
                   for r in range(R))

    first = first_ref[(b * pl.num_programs(1) + group) * pl.num_programs(2) + pl.program_id(2)]
    gate0 = jnp.where(i0 > 0, 1.0, 0.0)

    l_part = {}

    def own_tile(c):
        qs, e = c
        k_own = keys(e, i0 + qs)
        pieces = []
        for r in range(R):
            n_keys, lanes = (r + 1) * SB, slice(r * SB, (r + 1) * SB)
            s_t = jnp.dot(k_own[:n_keys], operands[c][0][:, lanes], preferred_element_type=F32)
            pieces.append(jnp.where(causal[:n_keys, lanes], jnp.exp2(s_t), 0.0))
        l_part[c] = jnp.concatenate([col_partial(p) for p in pieces], axis=1)
        pieces = [p.astype(BF16) for p in pieces]

        def later():
            v_t = jnp.concatenate([values_t(e, (i0 + qs) * R + r) for r in range(R)], axis=1)
            for r, p in enumerate(pieces):
                acc_ref[qs, e, :, r * SB:(r + 1) * SB] = jnp.dot(
                    v_t[:, :(r + 1) * SB], p, preferred_element_type=F32)
        return later

    def earlier_tile(c, n):
        p = jnp.exp2(far_scores(c, n))
        l_part[c] = l_part[c] + weighted_partial(c, n, p)
        p = p.astype(BF16)
        return lambda: contract(c, n, p)

    def fill(c):
        p = jnp.exp2(far_scores(c, first))
        l_part[c] = l_part[c] + weighted_partial(c, first, p, gate0)
        p_ref[c] = p.astype(BF16)

    jobs = ([functools.partial(own_tile, c) for c in chains]
            + [functools.partial(earlier_tile, c, i0 + t) for c in chains for t in range(c[0])]
            + [functools.partial(fill, c) for c in chains])
    pending = None
    for job in jobs:
        later = job()
        if pending is not None:
            pending()
        pending = later
    if pending is not None:
        pending()

    def body(n, l_run):
        s_new = []
        for c in chains:
            s_new.append(far_scores(c, n))
            contract(c, n - 1, p_ref[c])
        out = []
        for c, s_t, l_c in zip(chains, s_new, l_run):
            p = jnp.exp2(s_t)
            out.append(l_c + weighted_partial(c, n, p))
            p_ref[c] = p.astype(BF16)
        return tuple(out)

    l_run = lax.fori_loop(first + 1, i0, body, tuple(l_part[c] for c in chains))
    for c in chains:
        contract(c, jnp.maximum(i0 - 1, 0), p_ref[c], gate0)
    for qs in range(QT):
        out_t = [acc_ref[qs, e] * (1.0 / jnp.sum(l_run[qs * HP + e], axis=0, keepdims=True))
                 for e in range(HP)]
        _gated_output(jnp.concatenate(out_t, axis=0), g_ref, o_ref, qs * R, cols[qs])


def _fox_online_kernel(start_ref, qa_ref, ka_ref, vt_ref, g_ref, o_ref, *, tile, heads):
    TB, HP = tile, heads
    n_blocks = ka_ref.shape[2] // TB
    b = pl.program_id(0)
    group = pl.program_id(1)
    i = pl.program_id(2)
    q_t = [_query_operands(qa_ref[0, e])[0] for e in range(HP)]
    k_row = lax.broadcasted_iota(jnp.int32, (TB, TB), 0)
    q_col = lax.broadcasted_iota(jnp.int32, (TB, TB), 1)

    def step(j, carry, masked):
        out = []
        for e in range(HP):
            m_run, l_run, acc = carry[e]
            k_blk = ka_ref[0, e, pl.ds(pl.multiple_of(j * TB, TB), TB), :]
            s_t = jnp.dot(k_blk, q_t[e], preferred_element_type=F32)
            if masked:
                s_t = jnp.where(k_row <= q_col, s_t, MASK_VALUE)
            hd = HP * group + e
            delta = (start_ref[(b * n_blocks + i) * FOX_HEADS + hd]
                     - start_ref[(b * n_blocks + j) * FOX_HEADS + hd])
            m_new = jnp.maximum(m_run, jnp.max(s_t, axis=0, keepdims=True) + delta)
            p_t = jnp.exp2(s_t - (m_new - delta))
            alpha = jnp.exp2(m_run - m_new)
            l_new = alpha * l_run + jnp.sum(p_t, axis=0, keepdims=True)
            v_t = vt_ref[0, j, pl.ds(e * FOX_HEAD_DIM, FOX_HEAD_DIM), :]
            acc_new = alpha * acc + jnp.dot(v_t, p_t.astype(BF16), preferred_element_type=F32)
            out.append((m_new, l_new, acc_new))
        return tuple(out)

    init = (jnp.full((1, TB), MASK_VALUE, F32), jnp.zeros((1, TB), F32),
            jnp.zeros((FOX_HEAD_DIM, TB), F32))
    carry = lax.fori_loop(0, i, lambda j, c: step(j, c, False), (init,) * HP)
    carry = step(i, carry, True)
    _gated_output(jnp.concatenate([acc / l_run for _, l_run, acc in carry], axis=0), g_ref, o_ref,
                  0, slice(None))


def _fox_attention(qa, ka, vt, g, cq, edges, *, bounded):
    b, n_heads, s, _ = qa.shape
    _, nvb, w, vb = vt.shape
    if bounded:
        tile, hp, qt = ATTN_Q_TILE, ATTN_HEADS, ATTN_TILES_PER_STEP
    else:
        tile, hp, qt = vb, LANES // FOX_HEAD_DIM, 1
    gw = hp * FOX_HEAD_DIM
    rows = qt * tile
    assert s % rows == 0 and n_heads % hp == 0 and tile % vb == 0, (s, rows, n_heads, hp, tile, vb)
    specs = dict(
        table=pl.BlockSpec(memory_space=pltpu.SMEM),
        q=pl.BlockSpec((1, hp, rows, LANES), lambda bi, p, i: (bi, p, i, 0)),
        k=pl.BlockSpec((1, hp, s, LANES), lambda bi, p, i: (bi, p, 0, 0)),
        v=pl.BlockSpec((1, nvb, gw, vb), lambda bi, p, i: (bi, 0, p, 0)),
        cq=pl.BlockSpec((1, hp, 1, rows), lambda bi, p, i: (bi, p, 0, i)),
        g=pl.BlockSpec((1, rows // vb, gw, vb), lambda bi, p, i: (bi, i, p, 0)),
        o=pl.BlockSpec((1, gw, rows), lambda bi, p, i: (bi, p, i)))
    if bounded:
        body = functools.partial(_fox_bounded_kernel, tile=tile, heads=hp)
        names = ("table", "table", "q", "k", "v", "cq", "g")
        step_start = edges[:, ::rows // vb, 0, :FOX_HEADS]
        tile_end = edges[:, tile // vb - 1::tile // vb, 1, :FOX_HEADS]
        dead = (step_start[:, :, None] - tile_end[:, None]) < -FOX_DEAD_EXPONENT
        before = jnp.arange(s // rows)[:, None] * qt > jnp.arange(s // tile)[None, :]
        first_live = jnp.sum(dead & before[None, :, :, None], axis=2)
        last_before = jnp.maximum(jnp.arange(s // rows) * qt - 1, 0)
        first_live = jnp.minimum(first_live, last_before[None, :, None])
        first_live = jnp.min(first_live.reshape(b, s // rows, n_heads // hp, hp), axis=-1)
        args = (first_live.transpose(0, 2, 1).reshape(-1).astype(jnp.int32),
                edges[:, :, 1, :FOX_HEADS].reshape(-1), qa, ka, vt, cq, g)
        scratch = [pltpu.VMEM((qt, hp, tile, tile), BF16),
                   pltpu.VMEM((qt, hp, FOX_HEAD_DIM, tile), F32)]
    else:
        body = functools.partial(_fox_online_kernel, tile=tile, heads=hp)
        names = ("table", "q", "k", "v", "g")
        args = (edges[:, :, 0, :FOX_HEADS].reshape(-1), qa, ka, vt, g)
        scratch = []
    return pl.pallas_call(
        body,
        grid=(b, n_heads // hp, s // rows),
        in_specs=[specs[n] for n in names],
        out_specs=specs["o"],
        out_shape=jax.ShapeDtypeStruct((b, w, s), BF16),
        scratch_shapes=scratch,
        compiler_params=_params("arbitrary", "arbitrary", "arbitrary"),
        name="fox_attention_bounded" if bounded else "fox_attention_online",
    )(*args)


def _out_proj_kernel(a_ref, x_ref, mod_ref, w_ref, o_ref):
    y = lax.dot_general(a_ref[0], w_ref[...], (((0,), (0,)), ((), ())),
                        preferred_element_type=F32)
    o_ref[0] = x_ref[0] + mod_ref[0, 2:3, :] * y


def _out_proj(a, x, mod, w_out):
    b, s, d = x.shape
    tile = OUT_PROJ_TILE
    assert s % tile == 0, (s, tile)
    k = a.shape[1]
    return pl.pallas_call(
        _out_proj_kernel,
        grid=(b, s // tile),
        in_specs=[pl.BlockSpec((1, k, tile), lambda bi, ti: (bi, 0, ti)),
                  pl.BlockSpec((1, tile, d), lambda bi, ti: (bi, ti, 0)),
                  pl.BlockSpec((1, 3, d), lambda bi, ti: (bi, 0, 0)),
                  pl.BlockSpec((k, d), lambda bi, ti: (0, 0))],
        out_specs=pl.BlockSpec((1, tile, d), lambda bi, ti: (bi, ti, 0)),
        out_shape=jax.ShapeDtypeStruct((b, s, d), F32),
        compiler_params=_params("arbitrary", "arbitrary"),
        name="layer1_out_proj",
    )(a, x, mod, w_out.astype(BF16))


def _permute_in_kernel(order_ref, w_ref, eye_ref, pf_ref, o_ref, slab_ref):
    w, hd = FOX_WIDTH, FOX_HEAD_DIM
    lead = (((0,), (0,)), ((), ()))
    for grp in range(4):
        for j in range(FOX_HEADS):
            src = pl.multiple_of(grp * w + order_ref[j] * hd, hd)
            slab_ref[j * hd:(j + 1) * hd, :] = w_ref[pl.ds(src, hd), :].astype(BF16)
        o_ref[:, grp * w:(grp + 1) * w] = lax.dot_general(
            slab_ref[...], eye_ref[...], lead, preferred_element_type=F32).astype(BF16)
    o_ref[:, 4 * w:] = lax.dot_general(w_ref[4 * w:, :].astype(BF16), pf_ref[...], lead,
                                       preferred_element_type=F32).astype(BF16)


def _permute_out_kernel(order_ref, w_ref, o_ref):
    hd = FOX_HEAD_DIM
    for j in range(FOX_HEADS):
        src = pl.multiple_of(order_ref[j] * hd, hd)
        o_ref[j * hd:(j + 1) * hd, :] = w_ref[pl.ds(src, hd), :].astype(BF16)


def _permute_weights(w_in, w_out, order):
    d, w = w_in.shape[0], FOX_WIDTH
    band = 256
    assert d % band == 0 and w_out.shape == (w, d), (w_in.shape, w_out.shape)
    order = order.astype(jnp.int32)
    onehot = (jnp.arange(FOX_HEADS)[:, None] == order[None, :]).astype(BF16)
    perm_f = jnp.zeros((FOX_HEADS, LANES), BF16).at[:, :FOX_HEADS].set(onehot)
    table = pl.BlockSpec(memory_space=pltpu.SMEM)
    w_in_p = pl.pallas_call(
        _permute_in_kernel,
        grid=(d // band,),
        in_specs=[table,
                  pl.BlockSpec((w_in.shape[1], band), lambda i: (0, i)),
                  pl.BlockSpec((w, w), lambda i: (0, 0)),
                  pl.BlockSpec((FOX_HEADS, LANES), lambda i: (0, 0))],
        out_specs=pl.BlockSpec((band, 4 * w + LANES), lambda i: (i, 0)),
        out_shape=jax.ShapeDtypeStruct((d, 4 * w + LANES), BF16),
        scratch_shapes=[pltpu.VMEM((w, band), BF16)],
        compiler_params=_params("arbitrary"),
        name="layer1_permute_w_in",
    )(order, w_in.T, jnp.eye(w, dtype=BF16), perm_f)
    w_out_p = pl.pallas_call(
        _permute_out_kernel,
        grid=(d // band,),
        in_specs=[table, pl.BlockSpec((w, band), lambda i: (0, i))],
        out_specs=pl.BlockSpec((w, band), lambda i: (0, i)),
        out_shape=jax.ShapeDtypeStruct((w, d), BF16),
        compiler_params=_params("arbitrary"),
        name="layer1_permute_w_out",
    )(order, w_out)
    return w_in_p, w_out_p


def _layer1(x, mod, norm_g, w_in, b_f, qnorm_g, knorm_g, w_out):
    order = jnp.argsort(b_f)
    w_in, w_out = _permute_weights(w_in, w_out, order)
    qa, ka, vt, g, cq, edges = _layer1_proj(x, mod, norm_g, w_in, b_f[order], qnorm_g, knorm_g)
    score_bound = (LOG2E * FOX_HEAD_DIM ** 0.5
                   * jnp.max(jnp.abs(qnorm_g)) * jnp.max(jnp.abs(knorm_g)))
    gated = lax.cond(score_bound <= FOX_SAFE_LOGIT,
                     functools.partial(_fox_attention, bounded=True),
                     functools.partial(_fox_attention, bounded=False),
                     qa, ka, vt, g, cq, edges)
    return _out_proj(gated, x, mod, w_out)


def kernel(x, c, norm_g, ada_w, ada_b, hgrn_lb, even_w_in, hgrn_onorm_g, pool_w, pool_scale,
           even_w_out, odd_w_in, fox_b_f, fox_qnorm_g, fox_knorm_g, odd_w_out):
    depth = norm_g.shape[0]
    mods = _adaln_mods(c, ada_w, ada_b)
    for l in range(depth):
        j = l // 2
        if l % 2 == 0:
            x = _layer0(x, mods[l], norm_g[l], even_w_in[j], hgrn_lb, hgrn_onorm_g[j],
                        pool_w[j], pool_scale[j], even_w_out[j], layer_slot=l)
        else:
            x = _layer1(x, mods[l], norm_g[l], odd_w_in[j], fox_b_f[j], fox_qnorm_g[j],
                        fox_knorm_g[j], odd_w_out[j])
    return x
```
